```python
import math, functools
import jax, jax.numpy as jnp
from jax import lax
import numpy as np

D_MODEL = 1024
BATCH = 4
SEQ = 8192
DEPTH = 4

HEAD_DIM = 64
ROPE_THETA = 500000.0
ROPE_DIM = HEAD_DIM // 4
Q_BLOCK = 128
NORM_EPS = 1e-5
D_FF = -(-8 * D_MODEL // (3 * 256)) * 256

MOBA_HEADS = 8
MOBA_BLOCK = 256
MOBA_TOPK = 3
MOBA_Q = 64
DIFF_HEADS = 4
DIFF_V = 2 * HEAD_DIM
NSA_HEADS = 8
NSA_GROUPS = 2
NSA_CMP_LEN = 32
NSA_CMP_STRIDE = 16
NSA_CMP_HIDDEN = 256
NSA_SEL_BLOCK = 64
NSA_SEL_TOPK = 16
NSA_WINDOW = 512
NSA_FORCE_BONUS = 1e3
MLA_HEADS = 8
MLA_Q_RANK = 256
MLA_KV_RANK = 128
MLA_NOPE = 64
MLA_ROPE = 32
MLA_V = 64

EV_SIZES = [MOBA_HEADS * HEAD_DIM] * 3 + [DIFF_HEADS * 2 * HEAD_DIM] * 2 + [DIFF_HEADS * DIFF_V]
EV_IN = sum(EV_SIZES)
EV_OUT = MOBA_HEADS * HEAD_DIM + DIFF_HEADS * DIFF_V
OD_SIZES = ([NSA_HEADS * HEAD_DIM] + [NSA_GROUPS * HEAD_DIM] * 6 + [NSA_HEADS * 3]
            + [MLA_Q_RANK, MLA_KV_RANK, MLA_ROPE])
OD_IN = sum(OD_SIZES)
OD_OUT = NSA_HEADS * HEAD_DIM + MLA_HEADS * MLA_V

kernel_name = "hybrid_moba_diff_nsa_mla_trunk"


def rmsnorm(x, g):
    xf = x.astype(jnp.float32)
    y = xf * lax.rsqrt(jnp.mean(xf * xf, axis=-1, keepdims=True) + NORM_EPS)
    return (y * g.astype(jnp.float32)).astype(x.dtype)


def rope_tables(positions, dim):
    inv = 1.0 / (ROPE_THETA ** (jnp.arange(0, dim, 2, dtype=jnp.float32) / dim))
    ang = positions.astype(jnp.float32)[..., None] * inv
    return jnp.cos(ang), jnp.sin(ang)


def apply_rope(x, cos, sin):
    r = cos.shape[-1]
    cos, sin = cos.astype(x.dtype), sin.astype(x.dtype)
    x1, x2, xp = x[..., :r], x[..., r:2 * r], x[..., 2 * r:]
    return jnp.concatenate([x1 * cos - x2 * sin, x2 * cos + x1 * sin, xp], axis=-1)


def masked_softmax(s, mask):
    s = jnp.where(mask, s.astype(jnp.float32), -1e30)
    m = jnp.max(s, axis=-1, keepdims=True)
    e = jnp.where(mask, jnp.exp(s - m), 0.0)
    return e / jnp.maximum(jnp.sum(e, axis=-1, keepdims=True), 1e-30)


def split_cols(x, sizes):
    return jnp.split(x, [int(i) for i in np.cumsum(sizes)[:-1]], axis=-1)


def heads(x, n):
    B, S, _ = x.shape
    return x.reshape(B, S, n, -1).transpose(0, 2, 1, 3)


def merge_heads(o):
    B, n, S, d = o.shape
    return o.transpose(0, 2, 1, 3).reshape(B, S, n * d)


def causal_attention(q, k, v, scale):
    B, H, S, _ = q.shape
    kpos = jnp.arange(S)

    def chunk(c):
        q0 = c * Q_BLOCK
        qc = lax.dynamic_slice_in_dim(q, q0, Q_BLOCK, axis=2)
        tq = q0 + jnp.arange(Q_BLOCK)
        p = masked_softmax(jnp.einsum('bhqd,bhkd->bhqk', qc, k) * scale, kpos[None, :] <= tq[:, None])
        return jnp.einsum('bhqk,bhkd->bhqd', p.astype(v.dtype), v)

    o = lax.map(chunk, jnp.arange(S // Q_BLOCK))
    return o.transpose(1, 2, 0, 3, 4).reshape(B, H, S, v.shape[-1])


def moba_attention(q, k, v):
    B, H, S, d = q.shape
    Sp = -(-S // MOBA_BLOCK) * MOBA_BLOCK
    pad = ((0, 0), (0, 0), (0, Sp - S), (0, 0))
    q, k, v = jnp.pad(q, pad), jnp.pad(k, pad), jnp.pad(v, pad)
    nb = Sp // MOBA_BLOCK
    topk = min(MOBA_TOPK, nb)
    kb = k.reshape(B, H, nb, MOBA_BLOCK, d)
    vb = v.reshape(B, H, nb, MOBA_BLOCK, d)
    kmean = jnp.mean(kb.astype(jnp.float32), axis=3).astype(k.dtype)
    scale = d ** -0.5
    bi = jnp.arange(B)[:, None, None, None]
    hi = jnp.arange(H)[None, :, None, None]
    blk = jnp.arange(nb)

    def chunk(c):
        q0 = c * MOBA_Q
        qc = lax.dynamic_slice_in_dim(q, q0, MOBA_Q, axis=2)
        tq = q0 + jnp.arange(MOBA_Q)
        own = q0 // MOBA_BLOCK
        gate = jnp.einsum('bhqd,bhnd->bhqn', qc, kmean).astype(jnp.float32)
        gate = jnp.where(blk < own, gate, -jnp.inf)
        gval, sel = lax.top_k(gate, topk)
        ok = jnp.isfinite(gval)
        ks = kb[bi, hi, sel]
        vs = vb[bi, hi, sel]
        s_sel = (jnp.einsum('bhqd,bhqkjd->bhqkj', qc, ks) * scale).reshape(B, H, MOBA_Q, topk * MOBA_BLOCK)
        m_sel = jnp.broadcast_to(ok[..., None], (B, H, MOBA_Q, topk, MOBA_BLOCK)).reshape(B, H, MOBA_Q, topk * MOBA_BLOCK)
        ko = lax.dynamic_slice_in_dim(k, own * MOBA_BLOCK, MOBA_BLOCK, axis=2)
        vo = lax.dynamic_slice_in_dim(v, own * MOBA_BLOCK, MOBA_BLOCK, axis=2)
        s_own = jnp.einsum('bhqd,bhjd->bhqj', qc, ko) * scale
        m_own = (own * MOBA_BLOCK + jnp.arange(MOBA_BLOCK))[None, :] <= tq[:, None]
        m_own = jnp.broadcast_to(m_own, (B, H, MOBA_Q, MOBA_BLOCK))
        p = masked_softmax(jnp.concatenate([s_sel, s_own], -1), jnp.concatenate([m_sel, m_own], -1)).astype(v.dtype)
        p_sel = p[..., :topk * MOBA_BLOCK].reshape(B, H, MOBA_Q, topk, MOBA_BLOCK)
        return (jnp.einsum('bhqkj,bhqkjd->bhqd', p_sel, vs)
                + jnp.einsum('bhqj,bhjd->bhqd', p[..., topk * MOBA_BLOCK:], vo))

    out = lax.map(chunk, jnp.arange(Sp // MOBA_Q))
    return out.transpose(1, 2, 0, 3, 4).reshape(B, H, Sp, d)[:, :, :S]


def diff_attention(q, k, v, lam, lambda_init, subln_g):
    B, H, _, S, d = q.shape
    scale = d ** -0.5
    kpos = jnp.arange(S)

    def chunk(c):
        q0 = c * Q_BLOCK
        qc = lax.dynamic_slice_in_dim(q, q0, Q_BLOCK, axis=3)
        tq = q0 + jnp.arange(Q_BLOCK)
        p = masked_softmax(jnp.einsum('bhmqd,bhmkd->bhmqk', qc, k) * scale, kpos[None, :] <= tq[:, None])
        attn = (p[:, :, 0] - lam * p[:, :, 1]).astype(v.dtype)
        return jnp.einsum('bhqk,bhkd->bhqd', attn, v)

    o = lax.map(chunk, jnp.arange(S // Q_BLOCK))
    o = o.transpose(1, 2, 0, 3, 4).reshape(B, H, S, v.shape[-1])
    return rmsnorm(o, subln_g) * (1.0 - lambda_init)


def nsa_compress(x, pe, w1, w2):
    B, G, S, d = x.shape
    xh = x.reshape(B, G, S // NSA_CMP_STRIDE, NSA_CMP_STRIDE, d)
    blocks = jnp.concatenate([xh[:, :, :-1], xh[:, :, 1:]], axis=3) + pe
    flat = blocks.reshape(B, G, blocks.shape[2], NSA_CMP_LEN * d)
    return jax.nn.gelu(flat @ w1) @ w2


def nsa_attention(q, k_cmp, v_cmp, k_slc, v_slc, k_win, v_win, gates, positions, pe_k, pe_v, k_w1, k_w2, v_w1, v_w2):
    B, G, R, S, d = q.shape
    scale = d ** -0.5
    kc = nsa_compress(k_cmp, pe_k, k_w1, k_w2)
    vc = nsa_compress(v_cmp, pe_v, v_w1, v_w2)
    nc = kc.shape[2]
    cos_c, sin_c = rope_tables(positions[:, NSA_CMP_LEN - 1::NSA_CMP_STRIDE], ROPE_DIM)
    kc = apply_rope(kc, cos_c[:, None], sin_c[:, None])
    cmp_start = jnp.arange(nc) * NSA_CMP_STRIDE
    cmp_end = cmp_start + NSA_CMP_LEN - 1
    ns = S // NSA_SEL_BLOCK
    topk = min(NSA_SEL_TOPK, ns)
    sel_start = jnp.arange(ns) * NSA_SEL_BLOCK
    overlap = ((cmp_start[:, None] < sel_start[None, :] + NSA_SEL_BLOCK)
               & (cmp_start[:, None] + NSA_CMP_LEN > sel_start[None, :])).astype(jnp.float32)
    ksb = k_slc.reshape(B, G, ns, NSA_SEL_BLOCK, d)
    vsb = v_slc.reshape(B, G, ns, NSA_SEL_BLOCK, d)
    wpad = ((0, 0), (0, 0), (NSA_WINDOW, 0), (0, 0))
    kw_pad, vw_pad = jnp.pad(k_win, wpad), jnp.pad(v_win, wpad)
    bi = jnp.arange(B)[:, None, None, None]
    gi = jnp.arange(G)[None, :, None, None]
    blk = jnp.arange(ns)

    def chunk(c):
        q0 = c * Q_BLOCK
        qc = lax.dynamic_slice_in_dim(q, q0, Q_BLOCK, axis=3)
        tq = q0 + jnp.arange(Q_BLOCK)
        p_c = masked_softmax(jnp.einsum('bgrqd,bgnd->bgrqn', qc, kc) * scale, cmp_end[None, :] <= tq[:, None])
        o_c = jnp.einsum('bgrqn,bgnd->bgrqd', p_c.astype(vc.dtype), vc)
        imp = jnp.einsum('bgqn,ns->bgqs', p_c.sum(axis=2), overlap)
        qb = (tq // NSA_SEL_BLOCK)[:, None]
        forced = (blk[None, :] == 0) | (blk[None, :] == qb) | (blk[None, :] == qb - 1)
        imp = jnp.where(forced, imp + NSA_FORCE_BONUS, imp)
        imp = jnp.where(blk[None, :] <= qb, imp, -jnp.inf)
        ival, sel = lax.top_k(imp, topk)
        ok = jnp.isfinite(ival)
        ks = ksb[bi, gi, sel]
        vs = vsb[bi, gi, sel]
        kpos = sel[..., None] * NSA_SEL_BLOCK + jnp.arange(NSA_SEL_BLOCK)
        m_s = (ok[..., None] & (kpos <= tq[:, None, None])).reshape(B, G, 1, Q_BLOCK, topk * NSA_SEL_BLOCK)
        s_s = (jnp.einsum('bgrqd,bgqnjd->bgrqnj', qc, ks) * scale).reshape(B, G, R, Q_BLOCK, topk * NSA_SEL_BLOCK)
        p_s = masked_softmax(s_s, m_s).astype(vs.dtype).reshape(B, G, R, Q_BLOCK, topk, NSA_SEL_BLOCK)
        o_s = jnp.einsum('bgrqnj,bgqnjd->bgrqd', p_s, vs)
        kw = lax.dynamic_slice_in_dim(kw_pad, q0, NSA_WINDOW + Q_BLOCK, axis=2)
        vw = lax.dynamic_slice_in_dim(vw_pad, q0, NSA_WINDOW + Q_BLOCK, axis=2)
        wpos = q0 - NSA_WINDOW + jnp.arange(NSA_WINDOW + Q_BLOCK)
        m_w = ((wpos[None, :] <= tq[:, None]) & (wpos[None, :] > tq[:, None] - NSA_WINDOW) & (wpos[None, :] >= 0))
        p_w = masked_softmax(jnp.einsum('bgrqd,bgkd->bgrqk', qc, kw) * scale, m_w)
        o_w = jnp.einsum('bgrqk,bgkd->bgrqd', p_w.astype(vw.dtype), vw)
        g = lax.dynamic_slice_in_dim(gates, q0, Q_BLOCK, axis=3)
        return g[..., 0:1] * o_c + g[..., 1:2] * o_s + g[..., 2:3] * o_w

    o = lax.map(chunk, jnp.arange(S // Q_BLOCK))
    return o.transpose(1, 2, 3, 0, 4, 5).reshape(B, G, R, S, d)


def mla_attention(c_q, c_kv, k_pe, q_norm, w_uq, kv_norm, w_ukv, cos, sin):
    q = heads(rmsnorm(c_q, q_norm) @ w_uq, MLA_HEADS)
    kv = heads(rmsnorm(c_kv, kv_norm) @ w_ukv, MLA_HEADS)
    q_nope, q_pe = q[..., :MLA_NOPE], q[..., MLA_NOPE:]
    k_nope, v = kv[..., :MLA_NOPE], kv[..., MLA_NOPE:]
    q_pe = apply_rope(q_pe, cos[:, None], sin[:, None])
    k_pe = apply_rope(k_pe[:, None], cos[:, None], sin[:, None])
    qf = jnp.concatenate([q_nope, q_pe], -1)
    kf = jnp.concatenate([k_nope, jnp.broadcast_to(k_pe, k_nope.shape[:3] + (MLA_ROPE,))], -1)
    return causal_attention(qf, kf, v, (MLA_NOPE + MLA_ROPE) ** -0.5)


def even_mixer(h, layer_idx, cos16, sin16, w_in, w_out, lq1, lk1, lq2, lk2, subln):
    B, S, _ = h.shape
    q_a, k_a, v_a, q_b, k_b, v_b = split_cols(h @ w_in, EV_SIZES)
    cA, sA = cos16[:, None], sin16[:, None]
    q_a, k_a, v_a = apply_rope(heads(q_a, MOBA_HEADS), cA, sA), apply_rope(heads(k_a, MOBA_HEADS), cA, sA), heads(v_a, MOBA_HEADS)
    o_a = moba_attention(q_a, k_a, v_a)
    cB, sB = cos16[:, None, None], sin16[:, None, None]
    q_b = apply_rope(q_b.reshape(B, S, DIFF_HEADS, 2, HEAD_DIM).transpose(0, 2, 3, 1, 4), cB, sB)
    k_b = apply_rope(k_b.reshape(B, S, DIFF_HEADS, 2, HEAD_DIM).transpose(0, 2, 3, 1, 4), cB, sB)
    v_b = heads(v_b, DIFF_HEADS)
    lambda_init = 0.8 - 0.6 * math.exp(-0.3 * layer_idx)
    lam = (jnp.exp(jnp.sum(lq1.astype(jnp.float32) * lk1.astype(jnp.float32)))
           - jnp.exp(jnp.sum(lq2.astype(jnp.float32) * lk2.astype(jnp.float32))) + lambda_init)
    o_b = diff_attention(q_b, k_b, v_b, lam, lambda_init, subln)
    return jnp.concatenate([merge_heads(o_a), merge_heads(o_b)], -1) @ w_out


def odd_mixer(h, positions, cos16, sin16, cos32, sin32, w_in, w_out, gate_b, pe_k, pe_v,
              k_w1, k_w2, v_w1, v_w2, q_norm, w_uq, kv_norm, w_ukv):
    B, S, _ = h.shape
    G, R = NSA_GROUPS, NSA_HEADS // NSA_GROUPS
    (q_c, k_cmp, v_cmp, k_slc, v_slc, k_win, v_win, g_c, c_q, c_kv, k_pe) = split_cols(h @ w_in, OD_SIZES)
    q_c = apply_rope(q_c.reshape(B, S, G, R, HEAD_DIM).transpose(0, 2, 3, 1, 4), cos16[:, None, None], sin16[:, None, None])
    k_slc = apply_rope(heads(k_slc, G), cos16[:, None], sin16[:, None])
    k_win = apply_rope(heads(k_win, G), cos16[:, None], sin16[:, None])
    gates = jax.nn.sigmoid(g_c.astype(jnp.float32) + gate_b.astype(jnp.float32))
    gates = gates.reshape(B, S, G, R, 3).transpose(0, 2, 3, 1, 4).astype(h.dtype)
    o_c = nsa_attention(q_c, heads(k_cmp, G), heads(v_cmp, G), k_slc, heads(v_slc, G), k_win, heads(v_win, G),
                        gates, positions, pe_k, pe_v, k_w1, k_w2, v_w1, v_w2)
    o_c = o_c.transpose(0, 3, 1, 2, 4).reshape(B, S, NSA_HEADS * HEAD_DIM)
    o_d = mla_attention(c_q, c_kv, k_pe, q_norm, w_uq, kv_norm, w_ukv, cos32, sin32)
    return jnp.concatenate([o_c, merge_heads(o_d)], -1) @ w_out


def swiglu(h, wg, wu, wd):
    return (jax.nn.silu(h @ wg) * (h @ wu)) @ wd


def setup_inputs(seed: int = 0) -> dict:
    key = jax.random.key(seed)
    keys = iter(jax.random.split(key, 32))
    NE, NO = (DEPTH + 1) // 2, DEPTH // 2

    def nrm(shape, scale):
        return scale * jax.random.normal(next(keys), shape, jnp.float32)

    def gain(shape):
        return 1.0 + 0.05 * jax.random.normal(next(keys), shape, jnp.float32)

    flat_cmp = NSA_CMP_LEN * HEAD_DIM
    return {
        "x": jax.random.normal(next(keys), (BATCH, SEQ, D_MODEL), jnp.float32),
        "positions": jnp.tile(jnp.arange(SEQ, dtype=jnp.int32)[None, :], (BATCH, 1)),
        "attn_norm": gain((DEPTH, D_MODEL)),
        "ffn_norm": gain((DEPTH, D_MODEL)),
        "final_norm": gain((D_MODEL,)),
        "ffn_w_gate": nrm((DEPTH, D_MODEL, D_FF), D_MODEL ** -0.5),
        "ffn_w_up": nrm((DEPTH, D_MODEL, D_FF), D_MODEL ** -0.5),
        "ffn_w_down": nrm((DEPTH, D_FF, D_MODEL), D_FF ** -0.5),
        "ev_w_in": nrm((NE, D_MODEL, EV_IN), D_MODEL ** -0.5),
        "ev_w_out": nrm((NE, EV_OUT, D_MODEL), EV_OUT ** -0.5),
        "diff_lambda_q1": nrm((NE, HEAD_DIM), 0.1),
        "diff_lambda_k1": nrm((NE, HEAD_DIM), 0.1),
        "diff_lambda_q2": nrm((NE, HEAD_DIM), 0.1),
        "diff_lambda_k2": nrm((NE, HEAD_DIM), 0.1),
        "diff_subln": gain((NE, DIFF_V)),
        "od_w_in": nrm((NO, D_MODEL, OD_IN), D_MODEL ** -0.5),
        "od_w_out": nrm((NO, OD_OUT, D_MODEL), OD_OUT ** -0.5),
        "nsa_gate_b": nrm((NO, NSA_HEADS * 3), 0.1),
        "nsa_pe_k": nrm((NO, NSA_CMP_LEN, HEAD_DIM), 0.1),
        "nsa_pe_v": nrm((NO, NSA_CMP_LEN, HEAD_DIM), 0.1),
        "nsa_k_w1": nrm((NO, flat_cmp, NSA_CMP_HIDDEN), flat_cmp ** -0.5),
        "nsa_k_w2": nrm((NO, NSA_CMP_HIDDEN, HEAD_DIM), NSA_CMP_HIDDEN ** -0.5),
        "nsa_v_w1": nrm((NO, flat_cmp, NSA_CMP_HIDDEN), flat_cmp ** -0.5),
        "nsa_v_w2": nrm((NO, NSA_CMP_HIDDEN, HEAD_DIM), NSA_CMP_HIDDEN ** -0.5),
        "mla_q_norm": gain((NO, MLA_Q_RANK)),
        "mla_w_uq": nrm((NO, MLA_Q_RANK, MLA_HEADS * (MLA_NOPE + MLA_ROPE)), MLA_Q_RANK ** -0.5),
        "mla_kv_norm": gain((NO, MLA_KV_RANK)),
        "mla_w_ukv": nrm((NO, MLA_KV_RANK, MLA_HEADS * (MLA_NOPE + MLA_V)), MLA_KV_RANK ** -0.5),
    }


def reference(x, positions, attn_norm, ffn_norm, final_norm, ffn_w_gate, ffn_w_up, ffn_w_down,
              ev_w_in, ev_w_out, diff_lambda_q1, diff_lambda_k1, diff_lambda_q2, diff_lambda_k2, diff_subln,
              od_w_in, od_w_out, nsa_gate_b, nsa_pe_k, nsa_pe_v, nsa_k_w1, nsa_k_w2, nsa_v_w1, nsa_v_w2,
              mla_q_norm, mla_w_uq, mla_kv_norm, mla_w_ukv):
    cos16, sin16 = rope_tables(positions, ROPE_DIM)
    cos32, sin32 = rope_tables(positions, MLA_ROPE)
    h = x
    for l in range(DEPTH):
        i = l // 2
        hn = rmsnorm(h, attn_norm[l])
        if l % 2 == 0:
            h = h + even_mixer(hn, l, cos16, sin16, ev_w_in[i], ev_w_out[i], diff_lambda_q1[i], diff_lambda_k1[i],
                               diff_lambda_q2[i], diff_lambda_k2[i], diff_subln[i])
        else:
            h = h + odd_mixer(hn, positions, cos16, sin16, cos32, sin32, od_w_in[i], od_w_out[i], nsa_gate_b[i],
                              nsa_pe_k[i], nsa_pe_v[i], nsa_k_w1[i], nsa_k_w2[i], nsa_v_w1[i], nsa_v_w2[i],
                              mla_q_norm[i], mla_w_uq[i], mla_kv_norm[i], mla_w_ukv[i])
        hn = rmsnorm(h, ffn_norm[l])
        h = h + swiglu(hn, ffn_w_gate[l], ffn_w_up[l], ffn_w_down[l])
    return rmsnorm(h, final_norm)
```

```python
import functools
import math

import jax
import jax.numpy as jnp
from jax import lax
from jax.experimental import pallas as pl
from jax.experimental.pallas import tpu as pltpu

F32 = jnp.float32
BF16 = jnp.bfloat16

D_MODEL = 1024
HEAD_DIM = 64
ROPE_THETA = 500000.0
ROPE_DIM = HEAD_DIM // 4
NORM_EPS = 1e-5
D_FF = 2816

MOBA_HEADS = 8
MOBA_BLOCK = 256
MOBA_TOPK = 3
DIFF_HEADS = 4
DIFF_V = 2 * HEAD_DIM
NSA_HEADS = 8
NSA_GROUPS = 2
NSA_REP = NSA_HEADS // NSA_GROUPS
NSA_CMP_LEN = 32
NSA_CMP_STRIDE = 16
NSA_CMP_HIDDEN = 256
NSA_SEL_BLOCK = 64
NSA_SEL_TOPK = 16
NSA_WINDOW = 512
NSA_FORCE_BONUS = 1e3
MLA_HEADS = 8
MLA_Q_RANK = 256
MLA_KV_RANK = 128
MLA_NOPE = 64
MLA_ROPE = 32
MLA_V = 64

LANES = 128
NEG = -1e30
M_INIT = -1e37
VMEM_LIMIT = 56 * 1024 * 1024

_NT = (((1,), (1,)), ((), ()))


def _cparams(sem):
    return pltpu.CompilerParams(dimension_semantics=sem, vmem_limit_bytes=VMEM_LIMIT)


def _rope_tables(positions, dim, period):
    r = dim // 2
    inv = 1.0 / (ROPE_THETA ** (jnp.arange(0, dim, 2, dtype=F32) / dim))
    ang = positions.astype(F32)[..., None] * inv
    cos, sin = jnp.cos(ang), jnp.sin(ang)
    rest = period - 2 * r
    ones = jnp.ones(ang.shape[:-1] + (rest,), F32)
    zeros = jnp.zeros(ang.shape[:-1] + (rest,), F32)
    zr = jnp.zeros_like(sin)
    c = jnp.concatenate([cos, cos, ones], -1)
    s1 = jnp.concatenate([-sin, zr, zeros], -1)
    s2 = jnp.concatenate([zr, sin, zeros], -1)
    reps = LANES // period
    tile = lambda t: jnp.tile(t, (1,) * (t.ndim - 1) + (reps,)).reshape(-1, LANES)
    return tile(c), tile(s1), tile(s2)


def _apply_rope(y, c, s1, s2, shift):
    return y * c + pltpu.roll(y, LANES - shift, 1) * s1 + pltpu.roll(y, shift, 1) * s2


def _proj_body(*refs, has_norm, has_rope, has_res, rope_cols, shift, chunk):
    it = iter(refs)
    x_ref = next(it)
    g_ref = next(it) if has_norm else None
    w_ref = next(it)
    if has_rope:
        c_ref, s1_ref, s2_ref = next(it), next(it), next(it)
    res_ref = next(it) if has_res else None
    o_ref = next(it)
    n = w_ref.shape[1]
    if has_norm:
        xf = x_ref[...].astype(F32)
        y = xf * lax.rsqrt(jnp.mean(xf * xf, axis=-1, keepdims=True) + NORM_EPS)
        xb = (y * g_ref[...]).astype(BF16)
    else:
        xb = x_ref[...].astype(BF16)
    for c0 in range(0, n, chunk):
        cw = min(chunk, n - c0)
        y = jnp.dot(xb, w_ref[:, c0:c0 + cw], preferred_element_type=F32)
        if has_res:
            y = y + res_ref[:, c0:c0 + cw]
        if has_rope and c0 < rope_cols:
            for k0 in range(0, cw, LANES):
                ys = y[:, k0:k0 + LANES]
                if c0 + k0 < rope_cols:
                    ys = _apply_rope(ys, c_ref[...], s1_ref[...], s2_ref[...], shift)
                o_ref[:, c0 + k0:c0 + k0 + LANES] = ys.astype(o_ref.dtype)
        else:
            o_ref[:, c0:c0 + cw] = y.astype(o_ref.dtype)


def _proj(x, w, *, gain=None, rope=None, rope_cols=0, shift=0, res=None, out_dtype=F32, tm=512, chunk=512):
    t, k = x.shape
    n = w.shape[1]
    assert t % tm == 0 and n % LANES == 0 and rope_cols % LANES == 0
    has_norm, has_rope, has_res = gain is not None, rope is not None, res is not None
    args, specs = [x], [pl.BlockSpec((tm, k), lambda i: (i, 0))]
    if has_norm:
        args.append(gain.reshape(1, k).astype(F32))
        specs.append(pl.BlockSpec((1, k), lambda i: (0, 0)))
    args.append(w)
    specs.append(pl.BlockSpec((k, n), lambda i: (0, 0)))
    if has_rope:
        for tb in rope:
            args.append(tb)
            specs.append(pl.BlockSpec((tm, LANES), lambda i: (i, 0)))
    if has_res:
        args.append(res)
        specs.append(pl.BlockSpec((tm, n), lambda i: (i, 0)))
    body = functools.partial(_proj_body, has_norm=has_norm, has_rope=has_rope, has_res=has_res,
                             rope_cols=rope_cols, shift=shift, chunk=chunk)
    return pl.pallas_call(
        body, grid=(t // tm,), in_specs=specs,
        out_specs=pl.BlockSpec((tm, n), lambda i: (i, 0)),
        out_shape=jax.ShapeDtypeStruct((t, n), out_dtype),
        compiler_params=_cparams(("parallel",)),
    )(*args)


def _ffn_body(x_ref, g_ref, wg_ref, wu_ref, wd_ref, fg_ref, o_ref, xn_ref, acc_ref, *, final_norm):
    j = pl.program_id(1)

    @pl.when(j == 0)
    def _():
        xf = x_ref[...]
        y = xf * lax.rsqrt(jnp.mean(xf * xf, axis=-1, keepdims=True) + NORM_EPS)
        xn_ref[...] = (y * g_ref[...]).astype(BF16)
        acc_ref[...] = jnp.zeros_like(acc_ref)

    xn = xn_ref[...]
    g = jnp.dot(xn, wg_ref[...], preferred_element_type=F32)
    u = jnp.dot(xn, wu_ref[...], preferred_element_type=F32)
    a = (jax.nn.silu(g) * u).astype(BF16)
    acc_ref[...] += jnp.dot(a, wd_ref[...], preferred_element_type=F32)

    @pl.when(j == pl.num_programs(1) - 1)
    def _():
        h = x_ref[...] + acc_ref[...]
        if final_norm:
            y = h * lax.rsqrt(jnp.mean(h * h, axis=-1, keepdims=True) + NORM_EPS)
            h = y * fg_ref[...]
        o_ref[...] = h


def _ffn(x, gain, wg, wu, wd, final_gain, *, final_norm, tm=512, tf=1408):
    t, d = x.shape
    f = wg.shape[1]
    assert t % tm == 0 and f % tf == 0
    return pl.pallas_call(
        functools.partial(_ffn_body, final_norm=final_norm),
        grid=(t // tm, f // tf),
        in_specs=[
            pl.BlockSpec((tm, d), lambda i, j: (i, 0)),
            pl.BlockSpec((1, d), lambda i, j: (0, 0)),
            pl.BlockSpec((d, tf), lambda i, j: (0, j)),
            pl.BlockSpec((d, tf), lambda i, j: (0, j)),
            pl.BlockSpec((tf, d), lambda i, j: (j, 0)),
            pl.BlockSpec((1, d), lambda i, j: (0, 0)),
        ],
        out_specs=pl.BlockSpec((tm, d), lambda i, j: (i, 0)),
        out_shape=jax.ShapeDtypeStruct((t, d), F32),
        scratch_shapes=[pltpu.VMEM((tm, d), BF16), pltpu.VMEM((tm, d), F32)],
        compiler_params=_cparams(("parallel", "arbitrary")),
    )(x, gain.reshape(1, d).astype(F32), wg, wu, wd, final_gain.reshape(1, d).astype(F32))


def _flash_body(q_ref, k_ref, v_ref, o_ref, *, rep, tq, tk, scale, window, merged):
    qi = pl.program_id(1)
    q0 = qi * tq
    dk = q_ref.shape[-1]
    dv = v_ref.shape[-1]
    rows = rep * tq
    q = q_ref[0].reshape(rows, dk)

    def step(j, carry, masked):
        m, l, acc = carry
        ks = pl.multiple_of(j * tk, tk)
        kb = k_ref[0, pl.ds(ks, tk), :]
        vb = v_ref[0, pl.ds(ks, tk), :]
        s = lax.dot_general(q, kb, _NT, preferred_element_type=F32)
        if scale != 1.0:
            s = s * scale
        if masked:
            rpos = q0 + lax.rem(lax.broadcasted_iota(jnp.int32, (rows, tk), 0), tq)
            cpos = ks + lax.broadcasted_iota(jnp.int32, (rows, tk), 1)
            ok = cpos <= rpos
            if window is not None:
                ok = jnp.logical_and(ok, cpos > rpos - window)
            s = jnp.where(ok, s, NEG)
        m_new = jnp.maximum(m, jnp.max(s, axis=-1, keepdims=True))
        alpha = jnp.exp(m - m_new)
        p = jnp.exp(s - m_new)
        l = alpha * l + jnp.sum(p, axis=-1, keepdims=True)
        acc = alpha * acc + jnp.dot(p.astype(BF16), vb, preferred_element_type=F32)
        return m_new, l, acc

    init = (jnp.full((rows, 1), M_INIT, F32), jnp.zeros((rows, 1), F32), jnp.zeros((rows, dv), F32))
    if window is None:
        carry = lax.fori_loop(0, qi, lambda j, c: step(j, c, False), init)
        carry = step(qi, carry, True)
    else:
        j_lo = jnp.maximum(q0 - window + 1, 0) // tk
        j_hi = (q0 + tq - 1) // tk
        carry = lax.fori_loop(j_lo, j_hi + 1, lambda j, c: step(j, c, True), init)
    _, l, acc = carry
    o = acc / jnp.maximum(l, 1e-30)
    if merged:
        for r in range(rep):
            o_ref[0, :, r * dv:(r + 1) * dv] = o[r * tq:(r + 1) * tq].astype(o_ref.dtype)
    else:
        o_ref[0] = o.reshape(rep, tq, dv).astype(o_ref.dtype)


def _flash(q, k, v, *, tq, scale=1.0, window=None, v_div=1, merged_groups=None, out_dtype=F32):
    g, rep, s, dk = q.shape
    dv = v.shape[-1]
    tk = tq
    assert s % tq == 0
    in_specs = [
        pl.BlockSpec((1, rep, tq, dk), lambda i, j: (i, 0, j, 0)),
        pl.BlockSpec((1, s, dk), lambda i, j: (i, 0, 0)),
        pl.BlockSpec((1, s, dv), lambda i, j: (i // v_div, 0, 0)),
    ]
    if merged_groups is None:
        out_spec = pl.BlockSpec((1, rep, tq, dv), lambda i, j: (i, 0, j, 0))
        out_shape = jax.ShapeDtypeStruct((g, rep, s, dv), out_dtype)
    else:
        n = merged_groups
        out_spec = pl.BlockSpec((1, tq, rep * dv), lambda i, j: (i // n, j, i % n))
        out_shape = jax.ShapeDtypeStruct((g // n, s, n * rep * dv), out_dtype)
    body = functools.partial(_flash_body, rep=rep, tq=tq, tk=tk, scale=scale, window=window,
                             merged=merged_groups is not None)
    return pl.pallas_call(
        body, grid=(g, s // tq), in_specs=in_specs, out_specs=out_spec, out_shape=out_shape,
        compiler_params=_cparams(("parallel", "arbitrary")),
    )(q, k, v)


def _topk_mask(score, lane, k):
    sel = None
    for _ in range(k):
        mx = jnp.max(score, axis=-1, keepdims=True)
        idx = jnp.min(jnp.where(score == mx, lane, LANES), axis=-1, keepdims=True)
        hit = lane == idx
        pick = jnp.logical_and(hit, mx > -jnp.inf)
        sel = pick if sel is None else jnp.logical_or(sel, pick)
        score = jnp.where(hit, -jnp.inf, score)
    return sel


def _moba_gate_body(q_ref, k_ref, qa_ref, ka_ref, kmean_ref, *, tq, seq):
    qi = pl.program_id(1)

    @pl.when(qi == 0)
    def _():
        rowblk = lax.broadcasted_iota(jnp.int32, (LANES, seq), 0) - HEAD_DIM
        colblk = lax.broadcasted_iota(jnp.int32, (LANES, seq), 1) // MOBA_BLOCK
        ind = jnp.where(rowblk == colblk, 1.0, 0.0).astype(BF16)
        ksum = jnp.dot(ind, k_ref[0], preferred_element_type=F32)
        kmean_ref[...] = ksum * (1.0 / MOBA_BLOCK)

    q = q_ref[0]
    km = kmean_ref[...]
    km_hi = km.astype(BF16)
    km_lo = (km - km_hi.astype(F32)).astype(BF16)
    gate = (lax.dot_general(q, km_hi, _NT, preferred_element_type=F32)
            + lax.dot_general(q, km_lo, _NT, preferred_element_type=F32))
    lane = lax.broadcasted_iota(jnp.int32, (tq, LANES), 1)
    blk = lane - HEAD_DIM
    own = qi * tq // MOBA_BLOCK
    gate = jnp.where(jnp.logical_and(blk >= 0, blk < own), gate, -jnp.inf)
    sel = _topk_mask(gate, lane, MOBA_TOPK)
    sel = jnp.logical_or(sel, blk == own)
    bias = jnp.where(jnp.logical_or(sel, blk < 0), 0.0, NEG)
    eye = (lax.broadcasted_iota(jnp.int32, (HEAD_DIM, LANES), 0)
           == lax.broadcasted_iota(jnp.int32, (HEAD_DIM, LANES), 1))
    place_q = jnp.where(eye, HEAD_DIM ** -0.5, 0.0).astype(BF16)
    place_k = jnp.where(eye, 1.0, 0.0).astype(BF16)
    qa_ref[0] = (jnp.dot(q, place_q, preferred_element_type=F32) + bias).astype(BF16)
    q0 = pl.multiple_of(qi * tq, tq)
    onehot = jnp.where(blk == own, 1.0, 0.0)
    ka_ref[0] = (jnp.dot(k_ref[0, pl.ds(q0, tq), :], place_k, preferred_element_type=F32) + onehot).astype(BF16)


def _moba_gate(q, k, *, tq=MOBA_BLOCK):
    g, s, d = q.shape
    assert s % MOBA_BLOCK == 0 and s // MOBA_BLOCK <= LANES - HEAD_DIM and tq == MOBA_BLOCK
    out = jax.ShapeDtypeStruct((g, s, LANES), BF16)
    return pl.pallas_call(
        functools.partial(_moba_gate_body, tq=tq, seq=s),
        grid=(g, s // tq),
        in_specs=[pl.BlockSpec((1, tq, d), lambda i, j: (i, j, 0)),
                  pl.BlockSpec((1, s, d), lambda i, j: (i, 0, 0))],
        out_specs=[pl.BlockSpec((1, tq, LANES), lambda i, j: (i, j, 0)),
                   pl.BlockSpec((1, tq, LANES), lambda i, j: (i, j, 0))],
        out_shape=[out, out],
        scratch_shapes=[pltpu.VMEM((LANES, d), F32)],
        compiler_params=_cparams(("parallel", "arbitrary")),
    )(q, k)


def _diff_combine_body(o1_ref, o2_ref, lq1_ref, lk1_ref, lq2_ref, lk2_ref, sg_ref, o_ref, *, lambda_init):
    lam = (jnp.exp(jnp.sum(lq1_ref[...] * lk1_ref[...], axis=-1, keepdims=True))
           - jnp.exp(jnp.sum(lq2_ref[...] * lk2_ref[...], axis=-1, keepdims=True)) + lambda_init)
    d = o1_ref[0, 0] - lam * o2_ref[0, 0]
    y = d * lax.rsqrt(jnp.mean(d * d, axis=-1, keepdims=True) + NORM_EPS)
    o_ref[0] = ((y * sg_ref[...]) * (1.0 - lambda_init)).astype(o_ref.dtype)


def _diff_combine(o, lq1, lk1, lq2, lk2, subln, lambda_init, *, batch, tq=512):
    g, _, s, dv = o.shape
    vec = lambda a: a.reshape(1, -1).astype(F32)
    vspec = lambda n: pl.BlockSpec((1, n), lambda i, j: (0, 0))
    return pl.pallas_call(
        functools.partial(_diff_combine_body, lambda_init=lambda_init),
        grid=(g, s // tq),
        in_specs=[pl.BlockSpec((1, 1, tq, dv), lambda i, j: (i, 0, j, 0)),
                  pl.BlockSpec((1, 1, tq, dv), lambda i, j: (i, 1, j, 0)),
                  vspec(HEAD_DIM), vspec(HEAD_DIM), vspec(HEAD_DIM), vspec(HEAD_DIM), vspec(dv)],
        out_specs=pl.BlockSpec((1, tq, dv), lambda i, j: (i // DIFF_HEADS, j, i % DIFF_HEADS)),
        out_shape=jax.ShapeDtypeStruct((batch, s, DIFF_HEADS * dv), BF16),
        compiler_params=_cparams(("parallel", "parallel")),
    )(o, o, vec(lq1), vec(lk1), vec(lq2), vec(lk2), vec(subln))


def _nsa_compress_body(x_ref, pe_ref, w1_ref, w2_ref, c_ref, s1_ref, s2_ref, o_ref, *, rope):
    half = w1_ref.shape[0] // 2
    x = x_ref[0].astype(F32)
    lo = (x + pe_ref[:, 0:half]).astype(BF16)
    hi = (x + pe_ref[:, half:2 * half]).astype(BF16)
    a = jnp.dot(lo, w1_ref[0:half, :], preferred_element_type=F32)
    b = jnp.dot(hi, w1_ref[half:2 * half, :], preferred_element_type=F32)
    n = a.shape[0]
    h1 = a + pltpu.roll(b, n - 1, 0)
    y = jnp.dot(jax.nn.gelu(h1).astype(BF16), w2_ref[...], preferred_element_type=F32)
    if rope:
        y = _apply_rope(y, c_ref[0], s1_ref[0], s2_ref[0], ROPE_DIM // 2)
    o_ref[0] = y[:, 0:HEAD_DIM].astype(o_ref.dtype)


def _nsa_compress(xr, pe, w1, w2, tables, *, rope, groups):
    g, n, w = xr.shape
    hid = w1.shape[1]
    w2p = jnp.pad(w2, ((0, 0), (0, LANES - w2.shape[1])))
    tspec = pl.BlockSpec((1, n, LANES), lambda i: (i // groups, 0, 0))
    return pl.pallas_call(
        functools.partial(_nsa_compress_body, rope=rope),
        grid=(g,),
        in_specs=[pl.BlockSpec((1, n, w), lambda i: (i, 0, 0)),
                  pl.BlockSpec((1, 2 * w), lambda i: (0, 0)),
                  pl.BlockSpec((2 * w, hid), lambda i: (0, 0)),
                  pl.BlockSpec((hid, LANES), lambda i: (0, 0)),
                  tspec, tspec, tspec],
        out_specs=pl.BlockSpec((1, n, HEAD_DIM), lambda i: (i, 0, 0)),
        out_shape=jax.ShapeDtypeStruct((g, n, HEAD_DIM), BF16),
        compiler_params=_cparams(("parallel",)),
    )(xr, pe.reshape(1, 2 * w).astype(F32), w1, w2p, *tables)


def _nsa_cmp_body(q_ref, kc_ref, vc_ref, oc_ref, mb_ref, *, tq, rep):
    qi = pl.program_id(1)
    q0 = qi * tq
    nc = kc_ref.shape[1]
    kc = kc_ref[0]
    vc = vc_ref[0]
    tpos = q0 + lax.broadcasted_iota(jnp.int32, (tq, nc), 0)
    cend = lax.broadcasted_iota(jnp.int32, (tq, nc), 1) * NSA_CMP_STRIDE + (NSA_CMP_LEN - 1)
    ok = cend <= tpos
    psum = jnp.zeros((tq, nc), F32)
    for r in range(rep):
        s = lax.dot_general(q_ref[0, r], kc, _NT, preferred_element_type=F32)
        s = jnp.where(ok, s, NEG)
        m = jnp.max(s, axis=-1, keepdims=True)
        e = jnp.where(ok, jnp.exp(s - m), 0.0)
        p = e / jnp.maximum(jnp.sum(e, axis=-1, keepdims=True), 1e-30)
        oc_ref[0, :, r * HEAD_DIM:(r + 1) * HEAD_DIM] = jnp.dot(
            p.astype(BF16), vc, preferred_element_type=F32).astype(oc_ref.dtype)
        psum = psum + p
    cstart = lax.broadcasted_iota(jnp.int32, (nc, LANES), 0) * NSA_CMP_STRIDE
    sstart = lax.broadcasted_iota(jnp.int32, (nc, LANES), 1) * NSA_SEL_BLOCK
    ov = jnp.where(jnp.logical_and(cstart < sstart + NSA_SEL_BLOCK, cstart + NSA_CMP_LEN > sstart),
                   1.0, 0.0).astype(BF16)
    p_hi = psum.astype(BF16)
    r1 = psum - p_hi.astype(F32)
    p_mid = r1.astype(BF16)
    p_lo = (r1 - p_mid.astype(F32)).astype(BF16)
    imp = (jnp.dot(p_hi, ov, preferred_element_type=F32) + jnp.dot(p_mid, ov, preferred_element_type=F32)
           + jnp.dot(p_lo, ov, preferred_element_type=F32))
    lane = lax.broadcasted_iota(jnp.int32, (tq, LANES), 1)
    qb = (q0 + lax.broadcasted_iota(jnp.int32, (tq, LANES), 0)) // NSA_SEL_BLOCK
    forced = jnp.logical_or(lane == 0, jnp.logical_or(lane == qb, lane == qb - 1))
    imp = jnp.where(forced, imp + NSA_FORCE_BONUS, imp)
    imp = jnp.where(lane <= qb, imp, -jnp.inf)
    sel = _topk_mask(imp, lane, NSA_SEL_TOPK)
    mb_ref[0] = jnp.where(sel, 0.0, NEG).astype(mb_ref.dtype)


def _nsa_cmp(q, kc, vc, *, tq=128):
    g, rep, s, d = q.shape
    nc = kc.shape[1]
    assert s // NSA_SEL_BLOCK <= LANES
    n = NSA_GROUPS
    return pl.pallas_call(
        functools.partial(_nsa_cmp_body, tq=tq, rep=rep),
        grid=(g, s // tq),
        in_specs=[pl.BlockSpec((1, rep, tq, d), lambda i, j: (i, 0, j, 0)),
                  pl.BlockSpec((1, nc, d), lambda i, j: (i, 0, 0)),
                  pl.BlockSpec((1, nc, d), lambda i, j: (i, 0, 0))],
        out_specs=[pl.BlockSpec((1, tq, rep * d), lambda i, j: (i // n, j, i % n)),
                   pl.BlockSpec((1, tq, LANES), lambda i, j: (i, j, 0))],
        out_shape=[jax.ShapeDtypeStruct((g // n, s, n * rep * d), F32),
                   jax.ShapeDtypeStruct((g, s, LANES), BF16)],
        compiler_params=_cparams(("parallel", "parallel")),
    )(q, kc, vc)


def _nsa_combine_body(oc_ref, os_ref, ow_ref, g_ref, b_ref, e_ref, o_ref):
    gs = jax.nn.sigmoid(g_ref[...] + b_ref[...])
    g_hi = gs.astype(BF16)
    g_lo = (gs - g_hi.astype(F32)).astype(BF16)
    out = None
    for i, ref in enumerate((oc_ref, os_ref, ow_ref)):
        w = (jnp.dot(g_hi, e_ref[i], preferred_element_type=F32)
             + jnp.dot(g_lo, e_ref[i], preferred_element_type=F32))
        term = w * ref[...]
        out = term if out is None else out + term
    o_ref[...] = out.astype(o_ref.dtype)


def _nsa_combine(oc, osel, ow, g_raw, gate_b, *, tm=512):
    t, n = oc.shape
    nh = NSA_HEADS * 3
    gp = jnp.pad(g_raw, ((0, 0), (0, LANES - nh)))
    bp = jnp.pad(gate_b.reshape(1, nh).astype(F32), ((0, 0), (0, LANES - nh)))
    row = jnp.arange(LANES)[:, None]
    col = jnp.arange(n)[None, :] // HEAD_DIM
    expand = jnp.stack([(row == col * 3 + i) for i in range(3)]).astype(BF16)
    tok = pl.BlockSpec((tm, n), lambda i: (i, 0))
    return pl.pallas_call(
        _nsa_combine_body, grid=(t // tm,),
        in_specs=[tok, tok, tok, pl.BlockSpec((tm, LANES), lambda i: (i, 0)),
                  pl.BlockSpec((1, LANES), lambda i: (0, 0)),
                  pl.BlockSpec((3, LANES, n), lambda i: (0, 0, 0))],
        out_specs=tok,
        out_shape=jax.ShapeDtypeStruct((t, n), BF16),
        compiler_params=_cparams(("parallel",)),
    )(oc, osel, ow, gp, bp, expand)


def _to_heads(x, b, s, n):
    return x.reshape(b, s, n, -1).transpose(0, 2, 1, 3).reshape(b * n, s, -1)


def _from_heads(o, b, n):
    _, s, d = o.shape
    return o.reshape(b, n, s, d).transpose(0, 2, 1, 3).reshape(b * s, n * d)


def _even_mixer(h, gain, layer_idx, rope16, w_in, w_out, lq1, lk1, lq2, lk2, subln, b, s):
    na = MOBA_HEADS * HEAD_DIM
    nb = DIFF_HEADS * 2 * HEAD_DIM
    w_rope = jnp.concatenate([w_in[:, 0:2 * na], w_in[:, 3 * na:3 * na + 2 * nb]], axis=1)
    w_rest = jnp.concatenate([w_in[:, 2 * na:3 * na], w_in[:, 3 * na + 2 * nb:]], axis=1)
    w = jnp.concatenate([w_rope, w_rest], axis=1).astype(BF16)
    y = _proj(h, w, gain=gain, rope=rope16, rope_cols=w_rope.shape[1], shift=ROPE_DIM // 2, out_dtype=BF16)
    q_a, k_a, q_b, k_b, v_a, v_b = jnp.split(y, [na, 2 * na, 2 * na + nb, 2 * na + 2 * nb, 3 * na + 2 * nb], axis=1)
    qa, ka = _moba_gate(_to_heads(q_a, b, s, MOBA_HEADS), _to_heads(k_a, b, s, MOBA_HEADS))
    o_a = _flash(qa[:, None], ka, _to_heads(v_a, b, s, MOBA_HEADS), tq=MOBA_BLOCK, out_dtype=BF16)
    o_a = _from_heads(o_a[:, 0], b, MOBA_HEADS)
    nm = DIFF_HEADS * 2
    o_b = _flash(_to_heads(q_b, b, s, nm)[:, None], _to_heads(k_b, b, s, nm), _to_heads(v_b, b, s, DIFF_HEADS),
                 tq=256, scale=HEAD_DIM ** -0.5, v_div=2, out_dtype=F32)
    lambda_init = 0.8 - 0.6 * math.exp(-0.3 * layer_idx)
    o_b = _diff_combine(o_b.reshape(b * DIFF_HEADS, 2, s, DIFF_V), lq1, lk1, lq2, lk2, subln, lambda_init, batch=b)
    o = jnp.concatenate([o_a, o_b.reshape(b * s, DIFF_HEADS * DIFF_V)], axis=1)
    return _proj(o, w_out.astype(BF16), res=h, out_dtype=F32)


def _odd_mixer(h, gain, positions, rope16, rope32, w_in, w_out, gate_b, pe_k, pe_v, k_w1, k_w2, v_w1, v_w2,
               q_norm, w_uq, kv_norm, w_ukv, b, s):
    G, R, d = NSA_GROUPS, NSA_REP, HEAD_DIM
    sizes = [NSA_HEADS * d] + [G * d] * 6 + [NSA_HEADS * 3, MLA_Q_RANK, MLA_KV_RANK, MLA_ROPE]
    offs = [0]
    for z in sizes:
        offs.append(offs[-1] + z)
    col = lambda i: w_in[:, offs[i]:offs[i + 1]]
    w_r = jnp.concatenate([col(0), col(3), col(5)], axis=1).astype(BF16)
    yr = _proj(h, w_r, gain=gain, rope=rope16, rope_cols=w_r.shape[1], shift=ROPE_DIM // 2, out_dtype=BF16)
    q_c, k_slc, k_win = jnp.split(yr, [NSA_HEADS * d, NSA_HEADS * d + G * d], axis=1)
    w_p = jnp.concatenate([col(1), col(2), col(4), col(6), col(8), col(9), col(7)], axis=1)
    pad = (-w_p.shape[1]) % LANES
    w_p = jnp.pad(w_p, ((0, 0), (0, pad))).astype(BF16)
    yp = _proj(h, w_p, gain=gain, out_dtype=F32)
    k_cmp, v_cmp, v_slc, v_win, c_q, c_kv, g_raw = jnp.split(
        yp[:, :w_p.shape[1] - pad],
        [G * d, 2 * G * d, 3 * G * d, 4 * G * d, 4 * G * d + MLA_Q_RANK, 4 * G * d + MLA_Q_RANK + MLA_KV_RANK], axis=1)
    w_pe = jnp.pad(col(10), ((0, 0), (0, LANES - MLA_ROPE))).astype(BF16)
    k_pe = _proj(h, w_pe, gain=gain, rope=rope32, rope_cols=LANES, shift=MLA_ROPE // 2, out_dtype=BF16)[:, :MLA_ROPE]

    scale = d ** -0.5
    qh = (q_c * scale).reshape(b, s, G, R, d).transpose(0, 2, 3, 1, 4).reshape(b * G, R, s, d)
    nc = s // NSA_CMP_STRIDE
    cpos = jnp.concatenate([positions[:, NSA_CMP_LEN - 1::NSA_CMP_STRIDE], positions[:, -1:]], axis=1)
    ctab = [t.reshape(b, nc, LANES) for t in _rope_tables(cpos, ROPE_DIM, HEAD_DIM)]
    xk = _to_heads(k_cmp, b, s, G).reshape(b * G, nc, NSA_CMP_STRIDE * d)
    xv = _to_heads(v_cmp, b, s, G).reshape(b * G, nc, NSA_CMP_STRIDE * d)
    kc = _nsa_compress(xk, pe_k, k_w1.astype(BF16), k_w2.astype(BF16), ctab, rope=True, groups=G)
    vc = _nsa_compress(xv, pe_v, v_w1.astype(BF16), v_w2.astype(BF16), ctab, rope=False, groups=G)
    o_c, mbias = _nsa_cmp(qh, kc, vc)
    ns_onehot = jax.nn.one_hot(jnp.arange(s) // NSA_SEL_BLOCK, LANES, dtype=BF16)
    zq = jnp.zeros((b * G, R, s, LANES - d), BF16)
    q_aug = jnp.concatenate([jnp.broadcast_to(mbias[:, None], (b * G, R, s, LANES)), qh, zq], axis=-1)
    ks = _to_heads(k_slc, b, s, G)
    k_aug = jnp.concatenate([jnp.broadcast_to(ns_onehot[None], (b * G, s, LANES)), ks,
                             jnp.zeros((b * G, s, LANES - d), BF16)], axis=-1)
    o_s = _flash(q_aug, k_aug, _to_heads(v_slc.astype(BF16), b, s, G), tq=128, merged_groups=G, out_dtype=F32)
    o_w = _flash(qh, _to_heads(k_win, b, s, G), _to_heads(v_win.astype(BF16), b, s, G), tq=256,
                 window=NSA_WINDOW, merged_groups=G, out_dtype=F32)
    t = b * s
    o_nsa = _nsa_combine(o_c.reshape(t, -1), o_s.reshape(t, -1), o_w.reshape(t, -1), g_raw, gate_b)

    hq = MLA_NOPE + MLA_ROPE
    wq = w_uq.reshape(MLA_Q_RANK, MLA_HEADS, hq)
    wq = jnp.concatenate([wq[:, :, MLA_NOPE:].reshape(MLA_Q_RANK, -1), wq[:, :, :MLA_NOPE].reshape(MLA_Q_RANK, -1)],
                         axis=1).astype(BF16)
    q = _proj(c_q, wq, gain=q_norm, rope=rope32, rope_cols=MLA_HEADS * MLA_ROPE, shift=MLA_ROPE // 2,
              out_dtype=BF16, chunk=256)
    q_pe = q[:, :MLA_HEADS * MLA_ROPE].reshape(b, s, MLA_HEADS, MLA_ROPE)
    q_nope = q[:, MLA_HEADS * MLA_ROPE:].reshape(b, s, MLA_HEADS, MLA_NOPE)
    kv = _proj(c_kv, w_ukv.astype(BF16), gain=kv_norm, out_dtype=BF16).reshape(b, s, MLA_HEADS, MLA_NOPE + MLA_V)
    zpad = jnp.zeros((b, s, MLA_HEADS, LANES - hq), BF16)
    qf = jnp.concatenate([q_nope, q_pe, zpad], axis=-1).transpose(0, 2, 1, 3).reshape(b * MLA_HEADS, 1, s, LANES)
    kpe = jnp.broadcast_to(k_pe.reshape(b, s, 1, MLA_ROPE), (b, s, MLA_HEADS, MLA_ROPE))
    kf = jnp.concatenate([kv[..., :MLA_NOPE], kpe, zpad], axis=-1).transpose(0, 2, 1, 3).reshape(b * MLA_HEADS, s, LANES)
    vf = kv[..., MLA_NOPE:].transpose(0, 2, 1, 3).reshape(b * MLA_HEADS, s, MLA_V)
    o_d = _flash(qf, kf, vf, tq=256, scale=hq ** -0.5, out_dtype=BF16)
    o_d = _from_heads(o_d[:, 0], b, MLA_HEADS)

    o = jnp.concatenate([o_nsa, o_d], axis=1)
    return _proj(o, w_out.astype(BF16), res=h, out_dtype=F32)


def kernel(x, positions, attn_norm, ffn_norm, final_norm, ffn_w_gate, ffn_w_up, ffn_w_down, ev_w_in, ev_w_out, diff_lambda_q1, diff_lambda_k1, diff_lambda_q2, diff_lambda_k2, diff_subln, od_w_in, od_w_out, nsa_gate_b, nsa_pe_k, nsa_pe_v, nsa_k_w1, nsa_k_w2, nsa_v_w1, nsa_v_w2, mla_q_norm, mla_w_uq, mla_kv_norm, mla_w_ukv):
    b, s, d = x.shape
    depth = attn_norm.shape[0]
    rope16 = _rope_tables(positions, ROPE_DIM, HEAD_DIM)
    rope32 = _rope_tables(positions, MLA_ROPE, MLA_ROPE)
    h = x.reshape(b * s, d)
    for l in range(depth):
        i = l // 2
        if l % 2 == 0:
            h = _even_mixer(h, attn_norm[l], l, rope16, ev_w_in[i], ev_w_out[i], diff_lambda_q1[i],
                            diff_lambda_k1[i], diff_lambda_q2[i], diff_lambda_k2[i], diff_subln[i], b, s)
        else:
            h = _odd_mixer(h, attn_norm[l], positions, rope16, rope32, od_w_in[i], od_w_out[i], nsa_gate_b[i],
                           nsa_pe_k[i], nsa_pe_v[i], nsa_k_w1[i], nsa_k_w2[i], nsa_v_w1[i], nsa_v_w2[i],
                           mla_q_norm[i], mla_w_uq[i], mla_kv_norm[i], mla_w_ukv[i], b, s)
        h = _ffn(h, ffn_norm[l], ffn_w_gate[l].astype(BF16), ffn_w_up[l].astype(BF16), ffn_w_down[l].astype(BF16),
                 final_norm, final_norm=(l == depth - 1))
    return h.reshape(b, s, d)
```

```python
import functools
import math

import jax
import jax.numpy as jnp
from jax import lax
from jax.experimental import pallas as pl
from jax.experimental.pallas import tpu as pltpu

F32 = jnp.float32
BF16 = jnp.bfloat16

D_MODEL = 1024
HEAD_DIM = 64
ROPE_THETA = 500000.0
ROPE_DIM = HEAD_DIM // 4
NORM_EPS = 1e-5
D_FF = 2816

MOBA_HEADS = 8
MOBA_BLOCK = 256
MOBA_TOPK = 3
DIFF_HEADS = 4
DIFF_V = 2 * HEAD_DIM
NSA_HEADS = 8
NSA_GROUPS = 2
NSA_REP = NSA_HEADS // NSA_GROUPS
NSA_CMP_LEN = 32
NSA_CMP_STRIDE = 16
NSA_CMP_HIDDEN = 256
NSA_SEL_BLOCK = 64
NSA_SEL_TOPK = 16
NSA_WINDOW = 512
NSA_FORCE_BONUS = 1e3
MLA_HEADS = 8
MLA_Q_RANK = 256
MLA_KV_RANK = 128
MLA_NOPE = 64
MLA_ROPE = 32
MLA_V = 64

LANES = 128
NEG = -1e30
M_INIT = -1e37
VMEM_LIMIT = 56 * 1024 * 1024

_NT = (((1,), (1,)), ((), ()))


def _cparams(sem):
    return pltpu.CompilerParams(dimension_semantics=sem, vmem_limit_bytes=VMEM_LIMIT)


def _rope_tables(positions, dim, period):
    r = dim // 2
    inv = 1.0 / (ROPE_THETA ** (jnp.arange(0, dim, 2, dtype=F32) / dim))
    ang = positions.astype(F32)[..., None] * inv
    cos, sin = jnp.cos(ang), jnp.sin(ang)
    rest = period - 2 * r
    ones = jnp.ones(ang.shape[:-1] + (rest,), F32)
    zeros = jnp.zeros(ang.shape[:-1] + (rest,), F32)
    zr = jnp.zeros_like(sin)
    c = jnp.concatenate([cos, cos, ones], -1)
    s1 = jnp.concatenate([-sin, zr, zeros], -1)
    s2 = jnp.concatenate([zr, sin, zeros], -1)
    reps = LANES // period
    tile = lambda t: jnp.tile(t, (1,) * (t.ndim - 1) + (reps,)).reshape(-1, LANES)
    return tile(c), tile(s1), tile(s2)


def _apply_rope(y, c, s1, s2, shift):
    return y * c + pltpu.roll(y, LANES - shift, 1) * s1 + pltpu.roll(y, shift, 1) * s2


def _proj_body(*refs, has_norm, has_rope, has_res, rope_cols, shift, chunk):
    it = iter(refs)
    x_ref = next(it)
    g_ref = next(it) if has_norm else None
    w_ref = next(it)
    if has_rope:
        c_ref, s1_ref, s2_ref = next(it), next(it), next(it)
    res_ref = next(it) if has_res else None
    o_ref = next(it)
    n = w_ref.shape[1]
    if has_norm:
        xf = x_ref[...].astype(F32)
        y = xf * lax.rsqrt(jnp.mean(xf * xf, axis=-1, keepdims=True) + NORM_EPS)
        xb = (y * g_ref[...]).astype(BF16)
    else:
        xb = x_ref[...].astype(BF16)
    for c0 in range(0, n, chunk):
        cw = min(chunk, n - c0)
        y = jnp.dot(xb, w_ref[:, c0:c0 + cw], preferred_element_type=F32)
        if has_res:
            y = y + res_ref[:, c0:c0 + cw]
        if has_rope and c0 < rope_cols:
            for k0 in range(0, cw, LANES):
                ys = y[:, k0:k0 + LANES]
                if c0 + k0 < rope_cols:
                    ys = _apply_rope(ys, c_ref[...], s1_ref[...], s2_ref[...], shift)
                o_ref[:, c0 + k0:c0 + k0 + LANES] = ys.astype(o_ref.dtype)
        else:
            o_ref[:, c0:c0 + cw] = y.astype(o_ref.dtype)


def _proj(x, w, *, name, gain=None, rope=None, rope_cols=0, shift=0, res=None, out_dtype=F32, tm=512, chunk=512):
    t, k = x.shape
    n = w.shape[1]
    assert t % tm == 0 and n % LANES == 0 and rope_cols % LANES == 0
    has_norm, has_rope, has_res = gain is not None, rope is not None, res is not None
    args, specs = [x], [pl.BlockSpec((tm, k), lambda i: (i, 0))]
    if has_norm:
        args.append(gain.reshape(1, k).astype(F32))
        specs.append(pl.BlockSpec((1, k), lambda i: (0, 0)))
    args.append(w)
    specs.append(pl.BlockSpec((k, n), lambda i: (0, 0)))
    if has_rope:
        for tb in rope:
            args.append(tb)
            specs.append(pl.BlockSpec((tm, LANES), lambda i: (i, 0)))
    if has_res:
        args.append(res)
        specs.append(pl.BlockSpec((tm, n), lambda i: (i, 0)))
    body = functools.partial(_proj_body, has_norm=has_norm, has_rope=has_rope, has_res=has_res,
                             rope_cols=rope_cols, shift=shift, chunk=chunk)
    return pl.pallas_call(
        body, name=name, grid=(t // tm,), in_specs=specs,
        out_specs=pl.BlockSpec((tm, n), lambda i: (i, 0)),
        out_shape=jax.ShapeDtypeStruct((t, n), out_dtype),
        compiler_params=_cparams(("parallel",)),
    )(*args)


def _ffn_body(x_ref, g_ref, wg_ref, wu_ref, wd_ref, fg_ref, o_ref, xn_ref, acc_ref, *, final_norm):
    j = pl.program_id(1)

    @pl.when(j == 0)
    def _():
        xf = x_ref[...]
        y = xf * lax.rsqrt(jnp.mean(xf * xf, axis=-1, keepdims=True) + NORM_EPS)
        xn_ref[...] = (y * g_ref[...]).astype(BF16)
        acc_ref[...] = jnp.zeros_like(acc_ref)

    xn = xn_ref[...]
    g = jnp.dot(xn, wg_ref[...], preferred_element_type=F32)
    u = jnp.dot(xn, wu_ref[...], preferred_element_type=F32)
    a = (jax.nn.silu(g) * u).astype(BF16)
    acc_ref[...] += jnp.dot(a, wd_ref[...], preferred_element_type=F32)

    @pl.when(j == pl.num_programs(1) - 1)
    def _():
        h = x_ref[...] + acc_ref[...]
        if final_norm:
            y = h * lax.rsqrt(jnp.mean(h * h, axis=-1, keepdims=True) + NORM_EPS)
            h = y * fg_ref[...]
        o_ref[...] = h


def _ffn(x, gain, wg, wu, wd, final_gain, *, final_norm, tm=512, tf=1408):
    t, d = x.shape
    f = wg.shape[1]
    assert t % tm == 0 and f % tf == 0
    return pl.pallas_call(
        functools.partial(_ffn_body, final_norm=final_norm), name="ffn",
        grid=(t // tm, f // tf),
        in_specs=[
            pl.BlockSpec((tm, d), lambda i, j: (i, 0)),
            pl.BlockSpec((1, d), lambda i, j: (0, 0)),
            pl.BlockSpec((d, tf), lambda i, j: (0, j)),
            pl.BlockSpec((d, tf), lambda i, j: (0, j)),
            pl.BlockSpec((tf, d), lambda i, j: (j, 0)),
            pl.BlockSpec((1, d), lambda i, j: (0, 0)),
        ],
        out_specs=pl.BlockSpec((tm, d), lambda i, j: (i, 0)),
        out_shape=jax.ShapeDtypeStruct((t, d), F32),
        scratch_shapes=[pltpu.VMEM((tm, d), BF16), pltpu.VMEM((tm, d), F32)],
        compiler_params=_cparams(("parallel", "arbitrary")),
    )(x, gain.reshape(1, d).astype(F32), wg, wu, wd, final_gain.reshape(1, d).astype(F32))


def _flash_body(q_ref, k_ref, vt_ref, o_ref, s_ref, *, rep, tq, tk, scale, window, merged):
    qi = pl.program_id(1)
    q0 = qi * tq
    dk = q_ref.shape[-1]
    dv = vt_ref.shape[2]
    cols = rep * tq
    q = q_ref[0].reshape(cols, dk)

    def fold8(x, op):
        return op(x.reshape(tk // 8, 8, cols), axis=0)

    def score_step(j, slot, mrun, masked):
        ks = pl.multiple_of(j * tk, tk)
        kb = k_ref[0, pl.ds(ks, tk), :]
        s = lax.dot_general(kb, q, _NT, preferred_element_type=F32)
        if scale != 1.0:
            s = s * scale
        if masked:
            kpos = ks + lax.broadcasted_iota(jnp.int32, (tk, cols), 0)
            qpos = q0 + lax.rem(lax.broadcasted_iota(jnp.int32, (tk, cols), 1), tq)
            ok = kpos <= qpos
            if window is not None:
                ok = jnp.logical_and(ok, kpos > qpos - window)
            s = jnp.where(ok, s, NEG)
        s_ref[slot] = s
        return jnp.maximum(mrun, fold8(s, jnp.max))

    def pv_step(j, slot, carry, m):
        l8, acc = carry
        p = jnp.exp(s_ref[slot] - m)
        l8 = l8 + fold8(p, jnp.sum)
        acc = acc + jnp.dot(vt_ref[0, j], p.astype(BF16), preferred_element_type=F32)
        return l8, acc

    m_init = jnp.full((8, cols), M_INIT, F32)
    acc_init = (jnp.zeros((8, cols), F32), jnp.zeros((dv, cols), F32))
    if window is None:
        n_full = q0 // tk
        pairs = n_full // 2
        odd = n_full - 2 * pairs

        def a2(i, mrun):
            mrun = score_step(2 * i, 2 * i, mrun, False)
            return score_step(2 * i + 1, 2 * i + 1, mrun, False)

        mrun = lax.fori_loop(0, pairs, a2, m_init)
        mrun = lax.cond(odd == 1, lambda t: score_step(n_full - 1, n_full - 1, t, False), lambda t: t, mrun)
        mrun = score_step(n_full, n_full, mrun, True)
        m = jnp.max(mrun, axis=0, keepdims=True)

        def b2(i, carry):
            carry = pv_step(2 * i, 2 * i, carry, m)
            return pv_step(2 * i + 1, 2 * i + 1, carry, m)

        carry = lax.fori_loop(0, pairs, b2, acc_init)
        carry = lax.cond(odd == 1, lambda t: pv_step(n_full - 1, n_full - 1, t, m), lambda t: t, carry)
        l8, acc = pv_step(n_full, n_full, carry, m)
    else:
        j_lo = jnp.maximum(q0 - window + 1, 0) // tk
        j_hi = (q0 + tq - 1) // tk
        mrun = lax.fori_loop(j_lo, j_hi + 1, lambda j, t: score_step(j, j - j_lo, t, True), m_init)
        m = jnp.max(mrun, axis=0, keepdims=True)
        l8, acc = lax.fori_loop(j_lo, j_hi + 1, lambda j, t: pv_step(j, j - j_lo, t, m), acc_init)
    l = jnp.sum(l8, axis=0, keepdims=True)
    o = (acc / jnp.maximum(l, 1e-30)).T
    if merged:
        for r in range(rep):
            o_ref[0, :, r * dv:(r + 1) * dv] = o[r * tq:(r + 1) * tq].astype(o_ref.dtype)
    else:
        o_ref[0] = o.reshape(rep, tq, dv).astype(o_ref.dtype)


def _flash(q, k, v, *, name, tq, tk=512, scale=1.0, window=None, v_div=1, merged_groups=None, out_dtype=F32):
    g, rep, s, dk = q.shape
    gv, _, dv = v.shape
    assert s % tk == 0 and tk % tq == 0
    vt = v.reshape(gv, s // tk, tk, dv).transpose(0, 1, 3, 2)
    n_slots = s // tk if window is None else (window + tq - 2) // tk + 2
    in_specs = [
        pl.BlockSpec((1, rep, tq, dk), lambda i, j: (i, 0, j, 0)),
        pl.BlockSpec((1, s, dk), lambda i, j: (i, 0, 0)),
        pl.BlockSpec((1, s // tk, dv, tk), lambda i, j: (i // v_div, 0, 0, 0)),
    ]
    if merged_groups is None:
        out_spec = pl.BlockSpec((1, rep, tq, dv), lambda i, j: (i, 0, j, 0))
        out_shape = jax.ShapeDtypeStruct((g, rep, s, dv), out_dtype)
    else:
        n = merged_groups
        out_spec = pl.BlockSpec((1, tq, rep * dv), lambda i, j: (i // n, j, i % n))
        out_shape = jax.ShapeDtypeStruct((g // n, s, n * rep * dv), out_dtype)
    body = functools.partial(_flash_body, rep=rep, tq=tq, tk=tk, scale=scale, window=window,
                             merged=merged_groups is not None)
    return pl.pallas_call(
        body, name=name, grid=(g, s // tq), in_specs=in_specs, out_specs=out_spec, out_shape=out_shape,
        scratch_shapes=[pltpu.VMEM((n_slots, tk, rep * tq), F32)],
        compiler_params=_cparams(("parallel", "arbitrary")),
    )(q, k, vt)


def _topk_mask(score, lane, k):
    sel = None
    for _ in range(k):
        mx = jnp.max(score, axis=-1, keepdims=True)
        idx = jnp.min(jnp.where(score == mx, lane, LANES), axis=-1, keepdims=True)
        hit = lane == idx
        pick = jnp.logical_and(hit, mx > -jnp.inf)
        sel = pick if sel is None else jnp.logical_or(sel, pick)
        score = jnp.where(hit, -jnp.inf, score)
    return sel


def _moba_gate_body(q_ref, k_ref, qa_ref, ka_ref, kmean_ref, *, tq, seq):
    qi = pl.program_id(1)

    @pl.when(qi == 0)
    def _():
        rowblk = lax.broadcasted_iota(jnp.int32, (LANES, seq), 0) - HEAD_DIM
        colblk = lax.broadcasted_iota(jnp.int32, (LANES, seq), 1) // MOBA_BLOCK
        ind = jnp.where(rowblk == colblk, 1.0, 0.0).astype(BF16)
        ksum = jnp.dot(ind, k_ref[0], preferred_element_type=F32)
        kmean_ref[...] = ksum * (1.0 / MOBA_BLOCK)

    q = q_ref[0]
    km = kmean_ref[...]
    km_hi = km.astype(BF16)
    km_lo = (km - km_hi.astype(F32)).astype(BF16)
    gate = (lax.dot_general(q, km_hi, _NT, preferred_element_type=F32)
            + lax.dot_general(q, km_lo, _NT, preferred_element_type=F32))
    lane = lax.broadcasted_iota(jnp.int32, (tq, LANES), 1)
    blk = lane - HEAD_DIM
    own = qi * tq // MOBA_BLOCK
    gate = jnp.where(jnp.logical_and(blk >= 0, blk < own), gate, -jnp.inf)
    sel = _topk_mask(gate, lane, MOBA_TOPK)
    sel = jnp.logical_or(sel, blk == own)
    bias = jnp.where(jnp.logical_or(sel, blk < 0), 0.0, NEG)
    eye = (lax.broadcasted_iota(jnp.int32, (HEAD_DIM, LANES), 0)
           == lax.broadcasted_iota(jnp.int32, (HEAD_DIM, LANES), 1))
    place_q = jnp.where(eye, HEAD_DIM ** -0.5, 0.0).astype(BF16)
    place_k = jnp.where(eye, 1.0, 0.0).astype(BF16)
    qa_ref[0] = (jnp.dot(q, place_q, preferred_element_type=F32) + bias).astype(BF16)
    q0 = pl.multiple_of(qi * tq, tq)
    onehot = jnp.where(blk == own, 1.0, 0.0)
    ka_ref[0] = (jnp.dot(k_ref[0, pl.ds(q0, tq), :], place_k, preferred_element_type=F32) + onehot).astype(BF16)


def _moba_gate(q, k, *, tq=MOBA_BLOCK):
    g, s, d = q.shape
    assert s % MOBA_BLOCK == 0 and s // MOBA_BLOCK <= LANES - HEAD_DIM and tq == MOBA_BLOCK
    out = jax.ShapeDtypeStruct((g, s, LANES), BF16)
    return pl.pallas_call(
        functools.partial(_moba_gate_body, tq=tq, seq=s), name="moba_gate",
        grid=(g, s // tq),
        in_specs=[pl.BlockSpec((1, tq, d), lambda i, j: (i, j, 0)),
                  pl.BlockSpec((1, s, d), lambda i, j: (i, 0, 0))],
        out_specs=[pl.BlockSpec((1, tq, LANES), lambda i, j: (i, j, 0)),
                   pl.BlockSpec((1, tq, LANES), lambda i, j: (i, j, 0))],
        out_shape=[out, out],
        scratch_shapes=[pltpu.VMEM((LANES, d), F32)],
        compiler_params=_cparams(("parallel", "arbitrary")),
    )(q, k)


def _diff_combine_body(o1_ref, o2_ref, lq1_ref, lk1_ref, lq2_ref, lk2_ref, sg_ref, o_ref, *, lambda_init):
    lam = (jnp.exp(jnp.sum(lq1_ref[...] * lk1_ref[...], axis=-1, keepdims=True))
           - jnp.exp(jnp.sum(lq2_ref[...] * lk2_ref[...], axis=-1, keepdims=True)) + lambda_init)
    d = o1_ref[0, 0] - lam * o2_ref[0, 0]
    y = d * lax.rsqrt(jnp.mean(d * d, axis=-1, keepdims=True) + NORM_EPS)
    o_ref[0] = ((y * sg_ref[...]) * (1.0 - lambda_init)).astype(o_ref.dtype)


def _diff_combine(o, lq1, lk1, lq2, lk2, subln, lambda_init, *, batch, tq=512):
    g, _, s, dv = o.shape
    vec = lambda a: a.reshape(1, -1).astype(F32)
    vspec = lambda n: pl.BlockSpec((1, n), lambda i, j: (0, 0))
    return pl.pallas_call(
        functools.partial(_diff_combine_body, lambda_init=lambda_init), name="diff_combine",
        grid=(g, s // tq),
        in_specs=[pl.BlockSpec((1, 1, tq, dv), lambda i, j: (i, 0, j, 0)),
                  pl.BlockSpec((1, 1, tq, dv), lambda i, j: (i, 1, j, 0)),
                  vspec(HEAD_DIM), vspec(HEAD_DIM), vspec(HEAD_DIM), vspec(HEAD_DIM), vspec(dv)],
        out_specs=pl.BlockSpec((1, tq, dv), lambda i, j: (i // DIFF_HEADS, j, i % DIFF_HEADS)),
        out_shape=jax.ShapeDtypeStruct((batch, s, DIFF_HEADS * dv), BF16),
        compiler_params=_cparams(("parallel", "parallel")),
    )(o, o, vec(lq1), vec(lk1), vec(lq2), vec(lk2), vec(subln))


def _nsa_compress_body(x_ref, pe_ref, w1_ref, w2_ref, c_ref, s1_ref, s2_ref, o_ref, *, rope):
    half = w1_ref.shape[0] // 2
    x = x_ref[0].astype(F32)
    lo = (x + pe_ref[:, 0:half]).astype(BF16)
    hi = (x + pe_ref[:, half:2 * half]).astype(BF16)
    a = jnp.dot(lo, w1_ref[0:half, :], preferred_element_type=F32)
    b = jnp.dot(hi, w1_ref[half:2 * half, :], preferred_element_type=F32)
    n = a.shape[0]
    h1 = a + pltpu.roll(b, n - 1, 0)
    y = jnp.dot(jax.nn.gelu(h1).astype(BF16), w2_ref[...], preferred_element_type=F32)
    if rope:
        y = _apply_rope(y, c_ref[0], s1_ref[0], s2_ref[0], ROPE_DIM // 2)
    o_ref[0] = y[:, 0:HEAD_DIM].astype(o_ref.dtype)


def _nsa_compress(xr, pe, w1, w2, tables, *, rope, groups):
    g, n, w = xr.shape
    hid = w1.shape[1]
    w2p = jnp.pad(w2, ((0, 0), (0, LANES - w2.shape[1])))
    tspec = pl.BlockSpec((1, n, LANES), lambda i: (i // groups, 0, 0))
    return pl.pallas_call(
        functools.partial(_nsa_compress_body, rope=rope), name="nsa_compress",
        grid=(g,),
        in_specs=[pl.BlockSpec((1, n, w), lambda i: (i, 0, 0)),
                  pl.BlockSpec((1, 2 * w), lambda i: (0, 0)),
                  pl.BlockSpec((2 * w, hid), lambda i: (0, 0)),
                  pl.BlockSpec((hid, LANES), lambda i: (0, 0)),
                  tspec, tspec, tspec],
        out_specs=pl.BlockSpec((1, n, HEAD_DIM), lambda i: (i, 0, 0)),
        out_shape=jax.ShapeDtypeStruct((g, n, HEAD_DIM), BF16),
        compiler_params=_cparams(("parallel",)),
    )(xr, pe.reshape(1, 2 * w).astype(F32), w1, w2p, *tables)


def _nsa_cmp_body(q_ref, kc_ref, vc_ref, oc_ref, mb_ref, *, tq, rep):
    qi = pl.program_id(1)
    q0 = qi * tq
    nc = kc_ref.shape[1]
    kc = kc_ref[0]
    vc = vc_ref[0]
    tpos = q0 + lax.broadcasted_iota(jnp.int32, (tq, nc), 0)
    cend = lax.broadcasted_iota(jnp.int32, (tq, nc), 1) * NSA_CMP_STRIDE + (NSA_CMP_LEN - 1)
    ok = cend <= tpos
    psum = jnp.zeros((tq, nc), F32)
    for r in range(rep):
        s = lax.dot_general(q_ref[0, r], kc, _NT, preferred_element_type=F32)
        s = jnp.where(ok, s, NEG)
        m = jnp.max(s, axis=-1, keepdims=True)
        e = jnp.where(ok, jnp.exp(s - m), 0.0)
        p = e / jnp.maximum(jnp.sum(e, axis=-1, keepdims=True), 1e-30)
        oc_ref[0, :, r * HEAD_DIM:(r + 1) * HEAD_DIM] = jnp.dot(
            p.astype(BF16), vc, preferred_element_type=F32).astype(oc_ref.dtype)
        psum = psum + p
    cstart = lax.broadcasted_iota(jnp.int32, (nc, LANES), 0) * NSA_CMP_STRIDE
    sstart = lax.broadcasted_iota(jnp.int32, (nc, LANES), 1) * NSA_SEL_BLOCK
    ov = jnp.where(jnp.logical_and(cstart < sstart + NSA_SEL_BLOCK, cstart + NSA_CMP_LEN > sstart),
                   1.0, 0.0).astype(BF16)
    p_hi = psum.astype(BF16)
    r1 = psum - p_hi.astype(F32)
    p_mid = r1.astype(BF16)
    p_lo = (r1 - p_mid.astype(F32)).astype(BF16)
    imp = (jnp.dot(p_hi, ov, preferred_element_type=F32) + jnp.dot(p_mid, ov, preferred_element_type=F32)
           + jnp.dot(p_lo, ov, preferred_element_type=F32))
    lane = lax.broadcasted_iota(jnp.int32, (tq, LANES), 1)
    qb = (q0 + lax.broadcasted_iota(jnp.int32, (tq, LANES), 0)) // NSA_SEL_BLOCK
    forced = jnp.logical_or(lane == 0, jnp.logical_or(lane == qb, lane == qb - 1))
    imp = jnp.where(forced, imp + NSA_FORCE_BONUS, imp)
    imp = jnp.where(lane <= qb, imp, -jnp.inf)
    sel = _topk_mask(imp, lane, NSA_SEL_TOPK)
    mb_ref[0] = jnp.where(sel, 0.0, NEG).astype(mb_ref.dtype)


def _nsa_cmp(q, kc, vc, *, tq=128):
    g, rep, s, d = q.shape
    nc = kc.shape[1]
    assert s // NSA_SEL_BLOCK <= LANES
    n = NSA_GROUPS
    return pl.pallas_call(
        functools.partial(_nsa_cmp_body, tq=tq, rep=rep), name="nsa_cmp",
        grid=(g, s // tq),
        in_specs=[pl.BlockSpec((1, rep, tq, d), lambda i, j: (i, 0, j, 0)),
                  pl.BlockSpec((1, nc, d), lambda i, j: (i, 0, 0)),
                  pl.BlockSpec((1, nc, d), lambda i, j: (i, 0, 0))],
        out_specs=[pl.BlockSpec((1, tq, rep * d), lambda i, j: (i // n, j, i % n)),
                   pl.BlockSpec((1, tq, LANES), lambda i, j: (i, j, 0))],
        out_shape=[jax.ShapeDtypeStruct((g // n, s, n * rep * d), F32),
                   jax.ShapeDtypeStruct((g, s, LANES), BF16)],
        compiler_params=_cparams(("parallel", "parallel")),
    )(q, kc, vc)


def _nsa_combine_body(oc_ref, os_ref, ow_ref, g_ref, b_ref, e_ref, o_ref):
    gs = jax.nn.sigmoid(g_ref[...] + b_ref[...])
    g_hi = gs.astype(BF16)
    g_lo = (gs - g_hi.astype(F32)).astype(BF16)
    out = None
    for i, ref in enumerate((oc_ref, os_ref, ow_ref)):
        w = (jnp.dot(g_hi, e_ref[i], preferred_element_type=F32)
             + jnp.dot(g_lo, e_ref[i], preferred_element_type=F32))
        term = w * ref[...]
        out = term if out is None else out + term
    o_ref[...] = out.astype(o_ref.dtype)


def _nsa_combine(oc, osel, ow, g_raw, gate_b, *, tm=512):
    t, n = oc.shape
    nh = NSA_HEADS * 3
    gp = jnp.pad(g_raw, ((0, 0), (0, LANES - nh)))
    bp = jnp.pad(gate_b.reshape(1, nh).astype(F32), ((0, 0), (0, LANES - nh)))
    row = jnp.arange(LANES)[:, None]
    col = jnp.arange(n)[None, :] // HEAD_DIM
    expand = jnp.stack([(row == col * 3 + i) for i in range(3)]).astype(BF16)
    tok = pl.BlockSpec((tm, n), lambda i: (i, 0))
    return pl.pallas_call(
        _nsa_combine_body, name="nsa_combine", grid=(t // tm,),
        in_specs=[tok, tok, tok, pl.BlockSpec((tm, LANES), lambda i: (i, 0)),
                  pl.BlockSpec((1, LANES), lambda i: (0, 0)),
                  pl.BlockSpec((3, LANES, n), lambda i: (0, 0, 0))],
        out_specs=tok,
        out_shape=jax.ShapeDtypeStruct((t, n), BF16),
        compiler_params=_cparams(("parallel",)),
    )(oc, osel, ow, gp, bp, expand)


def _to_heads(x, b, s, n):
    return x.reshape(b, s, n, -1).transpose(0, 2, 1, 3).reshape(b * n, s, -1)


def _from_heads(o, b, n):
    _, s, d = o.shape
    return o.reshape(b, n, s, d).transpose(0, 2, 1, 3).reshape(b * s, n * d)


def _even_mixer(h, gain, layer_idx, rope16, w_in, w_out, lq1, lk1, lq2, lk2, subln, b, s):
    na = MOBA_HEADS * HEAD_DIM
    nb = DIFF_HEADS * 2 * HEAD_DIM
    w_rope = jnp.concatenate([w_in[:, 0:2 * na], w_in[:, 3 * na:3 * na + 2 * nb]], axis=1)
    w_rest = jnp.concatenate([w_in[:, 2 * na:3 * na], w_in[:, 3 * na + 2 * nb:]], axis=1)
    w = jnp.concatenate([w_rope, w_rest], axis=1).astype(BF16)
    y = _proj(h, w, name="proj_even_in", gain=gain, rope=rope16, rope_cols=w_rope.shape[1], shift=ROPE_DIM // 2, out_dtype=BF16)
    q_a, k_a, q_b, k_b, v_a, v_b = jnp.split(y, [na, 2 * na, 2 * na + nb, 2 * na + 2 * nb, 3 * na + 2 * nb], axis=1)
    qa, ka = _moba_gate(_to_heads(q_a, b, s, MOBA_HEADS), _to_heads(k_a, b, s, MOBA_HEADS))
    o_a = _flash(qa[:, None], ka, _to_heads(v_a, b, s, MOBA_HEADS), name="flash_moba", tq=512, out_dtype=BF16)
    o_a = _from_heads(o_a[:, 0], b, MOBA_HEADS)
    nm = DIFF_HEADS * 2
    o_b = _flash(_to_heads(q_b, b, s, nm)[:, None], _to_heads(k_b, b, s, nm), _to_heads(v_b, b, s, DIFF_HEADS),
                 name="flash_diff", tq=512, scale=HEAD_DIM ** -0.5, v_div=2, out_dtype=F32)
    lambda_init = 0.8 - 0.6 * math.exp(-0.3 * layer_idx)
    o_b = _diff_combine(o_b.reshape(b * DIFF_HEADS, 2, s, DIFF_V), lq1, lk1, lq2, lk2, subln, lambda_init, batch=b)
    o = jnp.concatenate([o_a, o_b.reshape(b * s, DIFF_HEADS * DIFF_V)], axis=1)
    return _proj(o, w_out.astype(BF16), name="proj_out", res=h, out_dtype=F32)


def _odd_mixer(h, gain, positions, rope16, rope32, w_in, w_out, gate_b, pe_k, pe_v, k_w1, k_w2, v_w1, v_w2,
               q_norm, w_uq, kv_norm, w_ukv, b, s):
    G, R, d = NSA_GROUPS, NSA_REP, HEAD_DIM
    sizes = [NSA_HEADS * d] + [G * d] * 6 + [NSA_HEADS * 3, MLA_Q_RANK, MLA_KV_RANK, MLA_ROPE]
    offs = [0]
    for z in sizes:
        offs.append(offs[-1] + z)
    col = lambda i: w_in[:, offs[i]:offs[i + 1]]
    w_r = jnp.concatenate([col(0), col(3), col(5)], axis=1).astype(BF16)
    yr = _proj(h, w_r, name="proj_odd_rope", gain=gain, rope=rope16, rope_cols=w_r.shape[1], shift=ROPE_DIM // 2, out_dtype=BF16)
    q_c, k_slc, k_win = jnp.split(yr, [NSA_HEADS * d, NSA_HEADS * d + G * d], axis=1)
    w_p = jnp.concatenate([col(1), col(2), col(4), col(6), col(8), col(9), col(7)], axis=1)
    pad = (-w_p.shape[1]) % LANES
    w_p = jnp.pad(w_p, ((0, 0), (0, pad))).astype(BF16)
    yp = _proj(h, w_p, name="proj_odd_plain", gain=gain, out_dtype=F32)
    k_cmp, v_cmp, v_slc, v_win, c_q, c_kv, g_raw = jnp.split(
        yp[:, :w_p.shape[1] - pad],
        [G * d, 2 * G * d, 3 * G * d, 4 * G * d, 4 * G * d + MLA_Q_RANK, 4 * G * d + MLA_Q_RANK + MLA_KV_RANK], axis=1)
    w_pe = jnp.pad(col(10), ((0, 0), (0, LANES - MLA_ROPE))).astype(BF16)
    k_pe = _proj(h, w_pe, name="proj_mla_kpe", gain=gain, rope=rope32, rope_cols=LANES, shift=MLA_ROPE // 2, out_dtype=BF16)[:, :MLA_ROPE]

    scale = d ** -0.5
    qh = (q_c * scale).reshape(b, s, G, R, d).transpose(0, 2, 3, 1, 4).reshape(b * G, R, s, d)
    nc = s // NSA_CMP_STRIDE
    cpos = jnp.concatenate([positions[:, NSA_CMP_LEN - 1::NSA_CMP_STRIDE], positions[:, -1:]], axis=1)
    ctab = [t.reshape(b, nc, LANES) for t in _rope_tables(cpos, ROPE_DIM, HEAD_DIM)]
    xk = _to_heads(k_cmp, b, s, G).reshape(b * G, nc, NSA_CMP_STRIDE * d)
    xv = _to_heads(v_cmp, b, s, G).reshape(b * G, nc, NSA_CMP_STRIDE * d)
    kc = _nsa_compress(xk, pe_k, k_w1.astype(BF16), k_w2.astype(BF16), ctab, rope=True, groups=G)
    vc = _nsa_compress(xv, pe_v, v_w1.astype(BF16), v_w2.astype(BF16), ctab, rope=False, groups=G)
    o_c, mbias = _nsa_cmp(qh, kc, vc)
    ns_onehot = jax.nn.one_hot(jnp.arange(s) // NSA_SEL_BLOCK, LANES, dtype=BF16)
    zq = jnp.zeros((b * G, R, s, LANES - d), BF16)
    q_aug = jnp.concatenate([jnp.broadcast_to(mbias[:, None], (b * G, R, s, LANES)), qh, zq], axis=-1)
    ks = _to_heads(k_slc, b, s, G)
    k_aug = jnp.concatenate([jnp.broadcast_to(ns_onehot[None], (b * G, s, LANES)), ks,
                             jnp.zeros((b * G, s, LANES - d), BF16)], axis=-1)
    o_s = _flash(q_aug, k_aug, _to_heads(v_slc.astype(BF16), b, s, G), name="flash_sel", tq=128,
                 merged_groups=G, out_dtype=F32)
    o_w = _flash(qh, _to_heads(k_win, b, s, G), _to_heads(v_win.astype(BF16), b, s, G), name="flash_win", tq=128,
                 window=NSA_WINDOW, merged_groups=G, out_dtype=F32)
    t = b * s
    o_nsa = _nsa_combine(o_c.reshape(t, -1), o_s.reshape(t, -1), o_w.reshape(t, -1), g_raw, gate_b)

    hq = MLA_NOPE + MLA_ROPE
    wq = w_uq.reshape(MLA_Q_RANK, MLA_HEADS, hq)
    wq = jnp.concatenate([wq[:, :, MLA_NOPE:].reshape(MLA_Q_RANK, -1), wq[:, :, :MLA_NOPE].reshape(MLA_Q_RANK, -1)],
                         axis=1).astype(BF16)
    q = _proj(c_q, wq, name="proj_mla_q", gain=q_norm, rope=rope32, rope_cols=MLA_HEADS * MLA_ROPE, shift=MLA_ROPE // 2,
              out_dtype=BF16, chunk=256)
    q_pe = q[:, :MLA_HEADS * MLA_ROPE].reshape(b, s, MLA_HEADS, MLA_ROPE)
    q_nope = q[:, MLA_HEADS * MLA_ROPE:].reshape(b, s, MLA_HEADS, MLA_NOPE)
    kv = _proj(c_kv, w_ukv.astype(BF16), name="proj_mla_kv", gain=kv_norm, out_dtype=BF16).reshape(b, s, MLA_HEADS, MLA_NOPE + MLA_V)
    zpad = jnp.zeros((b, s, MLA_HEADS, LANES - hq), BF16)
    qf = jnp.concatenate([q_nope, q_pe, zpad], axis=-1).transpose(0, 2, 1, 3).reshape(b * MLA_HEADS, 1, s, LANES)
    kpe = jnp.broadcast_to(k_pe.reshape(b, s, 1, MLA_ROPE), (b, s, MLA_HEADS, MLA_ROPE))
    kf = jnp.concatenate([kv[..., :MLA_NOPE], kpe, zpad], axis=-1).transpose(0, 2, 1, 3).reshape(b * MLA_HEADS, s, LANES)
    vf = kv[..., MLA_NOPE:].transpose(0, 2, 1, 3).reshape(b * MLA_HEADS, s, MLA_V)
    o_d = _flash(qf, kf, vf, name="flash_mla", tq=512, scale=hq ** -0.5, out_dtype=BF16)
    o_d = _from_heads(o_d[:, 0], b, MLA_HEADS)

    o = jnp.concatenate([o_nsa, o_d], axis=1)
    return _proj(o, w_out.astype(BF16), name="proj_out", res=h, out_dtype=F32)


def kernel(x, positions, attn_norm, ffn_norm, final_norm, ffn_w_gate, ffn_w_up, ffn_w_down, ev_w_in, ev_w_out, diff_lambda_q1, diff_lambda_k1, diff_lambda_q2, diff_lambda_k2, diff_subln, od_w_in, od_w_out, nsa_gate_b, nsa_pe_k, nsa_pe_v, nsa_k_w1, nsa_k_w2, nsa_v_w1, nsa_v_w2, mla_q_norm, mla_w_uq, mla_kv_norm, mla_w_ukv):
    b, s, d = x.shape
    depth = attn_norm.shape[0]
    rope16 = _rope_tables(positions, ROPE_DIM, HEAD_DIM)
    rope32 = _rope_tables(positions, MLA_ROPE, MLA_ROPE)
    h = x.reshape(b * s, d)
    for l in range(depth):
        i = l // 2
        if l % 2 == 0:
            h = _even_mixer(h, attn_norm[l], l, rope16, ev_w_in[i], ev_w_out[i], diff_lambda_q1[i],
                            diff_lambda_k1[i], diff_lambda_q2[i], diff_lambda_k2[i], diff_subln[i], b, s)
        else:
            h = _odd_mixer(h, attn_norm[l], positions, rope16, rope32, od_w_in[i], od_w_out[i], nsa_gate_b[i],
                           nsa_pe_k[i], nsa_pe_v[i], nsa_k_w1[i], nsa_k_w2[i], nsa_v_w1[i], nsa_v_w2[i],
                           mla_q_norm[i], mla_w_uq[i], mla_kv_norm[i], mla_w_ukv[i], b, s)
        h = _ffn(h, ffn_norm[l], ffn_w_gate[l].astype(BF16), ffn_w_up[l].astype(BF16), ffn_w_down[l].astype(BF16),
                 final_norm, final_norm=(l == depth - 1))
    return h.reshape(b, s, d)
```

```python
import functools
import math

import jax
import jax.numpy as jnp
from jax import lax
from jax.experimental import pallas as pl
from jax.experimental.pallas import tpu as pltpu

F32 = jnp.float32
BF16 = jnp.bfloat16

D_MODEL = 1024
HEAD_DIM = 64
ROPE_THETA = 500000.0
ROPE_DIM = HEAD_DIM // 4
NORM_EPS = 1e-5
D_FF = 2816

MOBA_HEADS = 8
MOBA_BLOCK = 256
MOBA_TOPK = 3
DIFF_HEADS = 4
DIFF_V = 2 * HEAD_DIM
NSA_HEADS = 8
NSA_GROUPS = 2
NSA_REP = NSA_HEADS // NSA_GROUPS
NSA_CMP_LEN = 32
NSA_CMP_STRIDE = 16
NSA_CMP_HIDDEN = 256
NSA_SEL_BLOCK = 64
NSA_SEL_TOPK = 16
NSA_WINDOW = 512
NSA_FORCE_BONUS = 1e3
MLA_HEADS = 8
MLA_Q_RANK = 256
MLA_KV_RANK = 128
MLA_NOPE = 64
MLA_ROPE = 32
MLA_V = 64

LANES = 128
NEG = -1e30
M_INIT = -1e37
LOG2E = 1.4426950408889634
UNROLL = 4
VMEM_LIMIT = 56 * 1024 * 1024

_NT = (((1,), (1,)), ((), ()))


def _cparams(sem):
    return pltpu.CompilerParams(dimension_semantics=sem, vmem_limit_bytes=VMEM_LIMIT)


def _rope_tables(positions, dim, period):
    r = dim // 2
    inv = 1.0 / (ROPE_THETA ** (jnp.arange(0, dim, 2, dtype=F32) / dim))
    ang = positions.astype(F32)[..., None] * inv
    cos, sin = jnp.cos(ang), jnp.sin(ang)
    rest = period - 2 * r
    ones = jnp.ones(ang.shape[:-1] + (rest,), F32)
    zeros = jnp.zeros(ang.shape[:-1] + (rest,), F32)
    zr = jnp.zeros_like(sin)
    c = jnp.concatenate([cos, cos, ones], -1)
    s1 = jnp.concatenate([-sin, zr, zeros], -1)
    s2 = jnp.concatenate([zr, sin, zeros], -1)
    reps = LANES // period
    tile = lambda t: jnp.tile(t, (1,) * (t.ndim - 1) + (reps,)).reshape(-1, LANES)
    return tile(c), tile(s1), tile(s2)


def _apply_rope(y, c, s1, s2, shift):
    return y * c + pltpu.roll(y, LANES - shift, 1) * s1 + pltpu.roll(y, shift, 1) * s2


def _proj_body(*refs, has_norm, has_rope, has_res, rope_cols, shift, chunk):
    it = iter(refs)
    x_ref = next(it)
    g_ref = next(it) if has_norm else None
    w_ref = next(it)
    if has_rope:
        c_ref, s1_ref, s2_ref = next(it), next(it), next(it)
    res_ref = next(it) if has_res else None
    o_ref = next(it)
    n = w_ref.shape[1]
    if has_norm:
        xf = x_ref[...].astype(F32)
        y = xf * lax.rsqrt(jnp.mean(xf * xf, axis=-1, keepdims=True) + NORM_EPS)
        xb = (y * g_ref[...]).astype(BF16)
    else:
        xb = x_ref[...].astype(BF16)
    for c0 in range(0, n, chunk):
        cw = min(chunk, n - c0)
        y = jnp.dot(xb, w_ref[:, c0:c0 + cw], preferred_element_type=F32)
        if has_res:
            y = y + res_ref[:, c0:c0 + cw]
        if has_rope and c0 < rope_cols:
            for k0 in range(0, cw, LANES):
                ys = y[:, k0:k0 + LANES]
                if c0 + k0 < rope_cols:
                    ys = _apply_rope(ys, c_ref[...], s1_ref[...], s2_ref[...], shift)
                o_ref[:, c0 + k0:c0 + k0 + LANES] = ys.astype(o_ref.dtype)
        else:
            o_ref[:, c0:c0 + cw] = y.astype(o_ref.dtype)


def _proj(x, w, *, name, gain=None, rope=None, rope_cols=0, shift=0, res=None, out_dtype=F32, tm=512, chunk=512):
    t, k = x.shape
    n = w.shape[1]
    assert t % tm == 0 and n % LANES == 0 and rope_cols % LANES == 0
    has_norm, has_rope, has_res = gain is not None, rope is not None, res is not None
    args, specs = [x], [pl.BlockSpec((tm, k), lambda i: (i, 0))]
    if has_norm:
        args.append(gain.reshape(1, k).astype(F32))
        specs.append(pl.BlockSpec((1, k), lambda i: (0, 0)))
    args.append(w)
    specs.append(pl.BlockSpec((k, n), lambda i: (0, 0)))
    if has_rope:
        for tb in rope:
            args.append(tb)
            specs.append(pl.BlockSpec((tm, LANES), lambda i: (i, 0)))
    if has_res:
        args.append(res)
        specs.append(pl.BlockSpec((tm, n), lambda i: (i, 0)))
    body = functools.partial(_proj_body, has_norm=has_norm, has_rope=has_rope, has_res=has_res,
                             rope_cols=rope_cols, shift=shift, chunk=chunk)
    return pl.pallas_call(
        body, name=name, grid=(t // tm,), in_specs=specs,
        out_specs=pl.BlockSpec((tm, n), lambda i: (i, 0)),
        out_shape=jax.ShapeDtypeStruct((t, n), out_dtype),
        compiler_params=_cparams(("parallel",)),
    )(*args)


def _ffn_body(x_ref, g_ref, wg_ref, wu_ref, wd_ref, fg_ref, o_ref, xn_ref, acc_ref, *, final_norm):
    j = pl.program_id(1)

    @pl.when(j == 0)
    def _():
        xf = x_ref[...]
        y = xf * lax.rsqrt(jnp.mean(xf * xf, axis=-1, keepdims=True) + NORM_EPS)
        xn_ref[...] = (y * g_ref[...]).astype(BF16)
        acc_ref[...] = jnp.zeros_like(acc_ref)

    xn = xn_ref[...]
    g = jnp.dot(xn, wg_ref[...], preferred_element_type=F32)
    u = jnp.dot(xn, wu_ref[...], preferred_element_type=F32)
    a = (jax.nn.silu(g) * u).astype(BF16)
    acc_ref[...] += jnp.dot(a, wd_ref[...], preferred_element_type=F32)

    @pl.when(j == pl.num_programs(1) - 1)
    def _():
        h = x_ref[...] + acc_ref[...]
        if final_norm:
            y = h * lax.rsqrt(jnp.mean(h * h, axis=-1, keepdims=True) + NORM_EPS)
            h = y * fg_ref[...]
        o_ref[...] = h


def _ffn(x, gain, wg, wu, wd, final_gain, *, final_norm, tm=512, tf=1408):
    t, d = x.shape
    f = wg.shape[1]
    assert t % tm == 0 and f % tf == 0
    return pl.pallas_call(
        functools.partial(_ffn_body, final_norm=final_norm), name="ffn",
        grid=(t // tm, f // tf),
        in_specs=[
            pl.BlockSpec((tm, d), lambda i, j: (i, 0)),
            pl.BlockSpec((1, d), lambda i, j: (0, 0)),
            pl.BlockSpec((d, tf), lambda i, j: (0, j)),
            pl.BlockSpec((d, tf), lambda i, j: (0, j)),
            pl.BlockSpec((tf, d), lambda i, j: (j, 0)),
            pl.BlockSpec((1, d), lambda i, j: (0, 0)),
        ],
        out_specs=pl.BlockSpec((tm, d), lambda i, j: (i, 0)),
        out_shape=jax.ShapeDtypeStruct((t, d), F32),
        scratch_shapes=[pltpu.VMEM((tm, d), BF16), pltpu.VMEM((tm, d), F32)],
        compiler_params=_cparams(("parallel", "arbitrary")),
    )(x, gain.reshape(1, d).astype(F32), wg, wu, wd, final_gain.reshape(1, d).astype(F32))


def _flash_body(q_ref, k_ref, vt_ref, o_ref, s_ref, *, rep, tq, tk, scale, window, merged):
    qi = pl.program_id(1)
    q0 = qi * tq
    dk = q_ref.shape[-1]
    dv = vt_ref.shape[2]
    cols = rep * tq
    q = q_ref[0].reshape(cols, dk)

    def fold8(x, op):
        return op(x.reshape(tk // 8, 8, cols), axis=0)

    def score_step(j, slot, mrun, masked):
        ks = pl.multiple_of(j * tk, tk)
        kb = k_ref[0, pl.ds(ks, tk), :]
        s = lax.dot_general(kb, q, _NT, preferred_element_type=F32) * (scale * LOG2E)
        if masked:
            kpos = ks + lax.broadcasted_iota(jnp.int32, (tk, cols), 0)
            qpos = q0 + lax.rem(lax.broadcasted_iota(jnp.int32, (tk, cols), 1), tq)
            ok = kpos <= qpos
            if window is not None:
                ok = jnp.logical_and(ok, kpos > qpos - window)
            s = jnp.where(ok, s, NEG)
        s_ref[slot] = s
        return jnp.maximum(mrun, fold8(s, jnp.max))

    def pv_step(j, slot, carry, m):
        l8, acc = carry
        p = jnp.exp2(s_ref[slot] - m)
        l8 = l8 + fold8(p, jnp.sum)
        acc = acc + jnp.dot(vt_ref[0, j], p.astype(BF16), preferred_element_type=F32)
        return l8, acc

    m_init = jnp.full((8, cols), M_INIT, F32)
    acc_init = (jnp.zeros((8, cols), F32), jnp.zeros((dv, cols), F32))
    if window is None:
        n_full = q0 // tk
        groups = n_full // UNROLL

        def score_group(i, mrun):
            for u in range(UNROLL):
                mrun = score_step(UNROLL * i + u, UNROLL * i + u, mrun, False)
            return mrun

        mrun = lax.fori_loop(0, groups, score_group, m_init)
        mrun = lax.fori_loop(groups * UNROLL, n_full, lambda j, t: score_step(j, j, t, False), mrun)
        mrun = score_step(n_full, n_full, mrun, True)
        m = jnp.max(mrun, axis=0, keepdims=True)

        def pv_group(i, carry):
            for u in range(UNROLL):
                carry = pv_step(UNROLL * i + u, UNROLL * i + u, carry, m)
            return carry

        carry = lax.fori_loop(0, groups, pv_group, acc_init)
        carry = lax.fori_loop(groups * UNROLL, n_full, lambda j, t: pv_step(j, j, t, m), carry)
        l8, acc = pv_step(n_full, n_full, carry, m)
    else:
        j_lo = jnp.maximum(q0 - window + 1, 0) // tk
        j_hi = (q0 + tq - 1) // tk
        mrun = lax.fori_loop(j_lo, j_hi + 1, lambda j, t: score_step(j, j - j_lo, t, True), m_init)
        m = jnp.max(mrun, axis=0, keepdims=True)
        l8, acc = lax.fori_loop(j_lo, j_hi + 1, lambda j, t: pv_step(j, j - j_lo, t, m), acc_init)
    l = jnp.sum(l8, axis=0, keepdims=True)
    o = (acc / jnp.maximum(l, 1e-30)).T
    if merged:
        for r in range(rep):
            o_ref[0, :, r * dv:(r + 1) * dv] = o[r * tq:(r + 1) * tq].astype(o_ref.dtype)
    else:
        o_ref[0] = o.reshape(rep, tq, dv).astype(o_ref.dtype)


def _flash(q, k, v, *, name, tq, tk=512, scale=1.0, window=None, v_div=1, merged_groups=None, out_dtype=F32):
    g, rep, s, dk = q.shape
    gv, _, dv = v.shape
    assert s % tk == 0 and tk % tq == 0
    vt = v.reshape(gv, s // tk, tk, dv).transpose(0, 1, 3, 2)
    n_slots = s // tk if window is None else (window + tq - 2) // tk + 2
    in_specs = [
        pl.BlockSpec((1, rep, tq, dk), lambda i, j: (i, 0, j, 0)),
        pl.BlockSpec((1, s, dk), lambda i, j: (i, 0, 0)),
        pl.BlockSpec((1, s // tk, dv, tk), lambda i, j: (i // v_div, 0, 0, 0)),
    ]
    if merged_groups is None:
        out_spec = pl.BlockSpec((1, rep, tq, dv), lambda i, j: (i, 0, j, 0))
        out_shape = jax.ShapeDtypeStruct((g, rep, s, dv), out_dtype)
    else:
        n = merged_groups
        out_spec = pl.BlockSpec((1, tq, rep * dv), lambda i, j: (i // n, j, i % n))
        out_shape = jax.ShapeDtypeStruct((g // n, s, n * rep * dv), out_dtype)
    body = functools.partial(_flash_body, rep=rep, tq=tq, tk=tk, scale=scale, window=window,
                             merged=merged_groups is not None)
    return pl.pallas_call(
        body, name=name, grid=(g, s // tq), in_specs=in_specs, out_specs=out_spec, out_shape=out_shape,
        scratch_shapes=[pltpu.VMEM((n_slots, tk, rep * tq), F32)],
        compiler_params=_cparams(("parallel", "arbitrary")),
    )(q, k, vt)


def _topk_mask(score, row, k):
    sel = None
    for _ in range(k):
        mx = jnp.max(score, axis=0, keepdims=True)
        idx = jnp.min(jnp.where(score == mx, row, LANES), axis=0, keepdims=True)
        hit = row == idx
        pick = jnp.logical_and(hit, mx > -jnp.inf)
        sel = pick if sel is None else jnp.logical_or(sel, pick)
        score = jnp.where(hit, -jnp.inf, score)
    return sel


def _moba_gate_body(q_ref, k_ref, qa_ref, ka_ref, kmean_ref, *, tq, seq):
    qi = pl.program_id(1)

    @pl.when(qi == 0)
    def _():
        rowblk = lax.broadcasted_iota(jnp.int32, (LANES, seq), 0) - HEAD_DIM
        colblk = lax.broadcasted_iota(jnp.int32, (LANES, seq), 1) // MOBA_BLOCK
        ind = jnp.where(rowblk == colblk, 1.0, 0.0).astype(BF16)
        ksum = jnp.dot(ind, k_ref[0], preferred_element_type=F32)
        kmean_ref[...] = ksum * (1.0 / MOBA_BLOCK)

    q = q_ref[0]
    q0 = pl.multiple_of(qi * tq, tq)
    km = kmean_ref[...]
    km_hi = km.astype(BF16)
    km_lo = (km - km_hi.astype(F32)).astype(BF16)
    gate = (lax.dot_general(km_hi, q, _NT, preferred_element_type=F32)
            + lax.dot_general(km_lo, q, _NT, preferred_element_type=F32))
    row = lax.broadcasted_iota(jnp.int32, (LANES, tq), 0)
    blk_t = row - HEAD_DIM
    own_t = (q0 + lax.broadcasted_iota(jnp.int32, (LANES, tq), 1)) // MOBA_BLOCK
    gate = jnp.where(jnp.logical_and(blk_t >= 0, blk_t < own_t), gate, -jnp.inf)
    sel = _topk_mask(gate, row, MOBA_TOPK)
    sel = jnp.logical_or(sel, jnp.logical_or(blk_t == own_t, blk_t < 0))
    bias = jnp.where(sel, 0.0, NEG).T
    eye = (lax.broadcasted_iota(jnp.int32, (HEAD_DIM, LANES), 0)
           == lax.broadcasted_iota(jnp.int32, (HEAD_DIM, LANES), 1))
    place_q = jnp.where(eye, HEAD_DIM ** -0.5, 0.0).astype(BF16)
    place_k = jnp.where(eye, 1.0, 0.0).astype(BF16)
    qa_ref[0] = (jnp.dot(q, place_q, preferred_element_type=F32) + bias).astype(BF16)
    blk = lax.broadcasted_iota(jnp.int32, (tq, LANES), 1) - HEAD_DIM
    own = (q0 + lax.broadcasted_iota(jnp.int32, (tq, LANES), 0)) // MOBA_BLOCK
    onehot = jnp.where(blk == own, 1.0, 0.0)
    ka_ref[0] = (jnp.dot(k_ref[0, pl.ds(q0, tq), :], place_k, preferred_element_type=F32) + onehot).astype(BF16)


def _moba_gate(q, k, *, tq=512):
    g, s, d = q.shape
    assert s % MOBA_BLOCK == 0 and s // MOBA_BLOCK <= LANES - HEAD_DIM and s % tq == 0
    out = jax.ShapeDtypeStruct((g, s, LANES), BF16)
    return pl.pallas_call(
        functools.partial(_moba_gate_body, tq=tq, seq=s), name="moba_gate",
        grid=(g, s // tq),
        in_specs=[pl.BlockSpec((1, tq, d), lambda i, j: (i, j, 0)),
                  pl.BlockSpec((1, s, d), lambda i, j: (i, 0, 0))],
        out_specs=[pl.BlockSpec((1, tq, LANES), lambda i, j: (i, j, 0)),
                   pl.BlockSpec((1, tq, LANES), lambda i, j: (i, j, 0))],
        out_shape=[out, out],
        scratch_shapes=[pltpu.VMEM((LANES, d), F32)],
        compiler_params=_cparams(("parallel", "arbitrary")),
    )(q, k)


def _diff_combine_body(o1_ref, o2_ref, lq1_ref, lk1_ref, lq2_ref, lk2_ref, sg_ref, o_ref, *, lambda_init):
    lam = (jnp.exp(jnp.sum(lq1_ref[...] * lk1_ref[...], axis=-1, keepdims=True))
           - jnp.exp(jnp.sum(lq2_ref[...] * lk2_ref[...], axis=-1, keepdims=True)) + lambda_init)
    d = o1_ref[0, 0] - lam * o2_ref[0, 0]
    y = d * lax.rsqrt(jnp.mean(d * d, axis=-1, keepdims=True) + NORM_EPS)
    o_ref[0] = ((y * sg_ref[...]) * (1.0 - lambda_init)).astype(o_ref.dtype)


def _diff_combine(o, lq1, lk1, lq2, lk2, subln, lambda_init, *, batch, tq=512):
    g, _, s, dv = o.shape
    vec = lambda a: a.reshape(1, -1).astype(F32)
    vspec = lambda n: pl.BlockSpec((1, n), lambda i, j: (0, 0))
    return pl.pallas_call(
        functools.partial(_diff_combine_body, lambda_init=lambda_init), name="diff_combine",
        grid=(g, s // tq),
        in_specs=[pl.BlockSpec((1, 1, tq, dv), lambda i, j: (i, 0, j, 0)),
                  pl.BlockSpec((1, 1, tq, dv), lambda i, j: (i, 1, j, 0)),
                  vspec(HEAD_DIM), vspec(HEAD_DIM), vspec(HEAD_DIM), vspec(HEAD_DIM), vspec(dv)],
        out_specs=pl.BlockSpec((1, tq, dv), lambda i, j: (i // DIFF_HEADS, j, i % DIFF_HEADS)),
        out_shape=jax.ShapeDtypeStruct((batch, s, DIFF_HEADS * dv), BF16),
        compiler_params=_cparams(("parallel", "parallel")),
    )(o, o, vec(lq1), vec(lk1), vec(lq2), vec(lk2), vec(subln))


def _nsa_compress_body(x_ref, pe_ref, w1_ref, w2_ref, c_ref, s1_ref, s2_ref, o_ref, *, rope):
    half = w1_ref.shape[0] // 2
    x = x_ref[0].astype(F32)
    lo = (x + pe_ref[:, 0:half]).astype(BF16)
    hi = (x + pe_ref[:, half:2 * half]).astype(BF16)
    a = jnp.dot(lo, w1_ref[0:half, :], preferred_element_type=F32)
    b = jnp.dot(hi, w1_ref[half:2 * half, :], preferred_element_type=F32)
    n = a.shape[0]
    h1 = a + pltpu.roll(b, n - 1, 0)
    y = jnp.dot(jax.nn.gelu(h1).astype(BF16), w2_ref[...], preferred_element_type=F32)
    if rope:
        y = _apply_rope(y, c_ref[0], s1_ref[0], s2_ref[0], ROPE_DIM // 2)
    o_ref[0] = y[:, 0:HEAD_DIM].astype(o_ref.dtype)


def _nsa_compress(xr, pe, w1, w2, tables, *, rope, groups):
    g, n, w = xr.shape
    hid = w1.shape[1]
    w2p = jnp.pad(w2, ((0, 0), (0, LANES - w2.shape[1])))
    tspec = pl.BlockSpec((1, n, LANES), lambda i: (i // groups, 0, 0))
    return pl.pallas_call(
        functools.partial(_nsa_compress_body, rope=rope), name="nsa_compress",
        grid=(g,),
        in_specs=[pl.BlockSpec((1, n, w), lambda i: (i, 0, 0)),
                  pl.BlockSpec((1, 2 * w), lambda i: (0, 0)),
                  pl.BlockSpec((2 * w, hid), lambda i: (0, 0)),
                  pl.BlockSpec((hid, LANES), lambda i: (0, 0)),
                  tspec, tspec, tspec],
        out_specs=pl.BlockSpec((1, n, HEAD_DIM), lambda i: (i, 0, 0)),
        out_shape=jax.ShapeDtypeStruct((g, n, HEAD_DIM), BF16),
        compiler_params=_cparams(("parallel",)),
    )(xr, pe.reshape(1, 2 * w).astype(F32), w1, w2p, *tables)


def _nsa_cmp_body(q_ref, kc_ref, vc_ref, oc_ref, mb_ref, *, tq, rep):
    qi = pl.program_id(1)
    q0 = qi * tq
    nc = kc_ref.shape[1]
    kc = kc_ref[0]
    vct = vc_ref[0]
    tpos = q0 + lax.broadcasted_iota(jnp.int32, (nc, tq), 1)
    cend = lax.broadcasted_iota(jnp.int32, (nc, tq), 0) * NSA_CMP_STRIDE + (NSA_CMP_LEN - 1)
    ok = cend <= tpos
    psum = jnp.zeros((nc, tq), F32)
    for r in range(rep):
        s = lax.dot_general(kc, q_ref[0, r], _NT, preferred_element_type=F32)
        s = jnp.where(ok, s, NEG)
        m = jnp.max(s, axis=0, keepdims=True)
        e = jnp.where(ok, jnp.exp(s - m), 0.0)
        p = e * (1.0 / jnp.maximum(jnp.sum(e, axis=0, keepdims=True), 1e-30))
        o_t = jnp.dot(vct, p.astype(BF16), preferred_element_type=F32)
        oc_ref[0, :, r * HEAD_DIM:(r + 1) * HEAD_DIM] = o_t.T.astype(oc_ref.dtype)
        psum = psum + p
    sstart = lax.broadcasted_iota(jnp.int32, (LANES, nc), 0) * NSA_SEL_BLOCK
    cstart = lax.broadcasted_iota(jnp.int32, (LANES, nc), 1) * NSA_CMP_STRIDE
    ov = jnp.where(jnp.logical_and(cstart < sstart + NSA_SEL_BLOCK, cstart + NSA_CMP_LEN > sstart),
                   1.0, 0.0).astype(BF16)
    p_hi = psum.astype(BF16)
    r1 = psum - p_hi.astype(F32)
    p_mid = r1.astype(BF16)
    p_lo = (r1 - p_mid.astype(F32)).astype(BF16)
    imp = (jnp.dot(ov, p_hi, preferred_element_type=F32) + jnp.dot(ov, p_mid, preferred_element_type=F32)
           + jnp.dot(ov, p_lo, preferred_element_type=F32))
    blk = lax.broadcasted_iota(jnp.int32, (LANES, tq), 0)
    qb = (q0 + lax.broadcasted_iota(jnp.int32, (LANES, tq), 1)) // NSA_SEL_BLOCK
    forced = jnp.logical_or(blk == 0, jnp.logical_or(blk == qb, blk == qb - 1))
    imp = jnp.where(forced, imp + NSA_FORCE_BONUS, imp)
    imp = jnp.where(blk <= qb, imp, -jnp.inf)
    sel = _topk_mask(imp, blk, NSA_SEL_TOPK)
    mb_ref[0] = jnp.where(sel, 0.0, NEG).T.astype(mb_ref.dtype)


def _nsa_cmp(q, kc, vc, *, tq=512):
    g, rep, s, d = q.shape
    nc = kc.shape[1]
    assert s // NSA_SEL_BLOCK <= LANES and s % tq == 0
    n = NSA_GROUPS
    return pl.pallas_call(
        functools.partial(_nsa_cmp_body, tq=tq, rep=rep), name="nsa_cmp",
        grid=(g, s // tq),
        in_specs=[pl.BlockSpec((1, rep, tq, d), lambda i, j: (i, 0, j, 0)),
                  pl.BlockSpec((1, nc, d), lambda i, j: (i, 0, 0)),
                  pl.BlockSpec((1, d, nc), lambda i, j: (i, 0, 0))],
        out_specs=[pl.BlockSpec((1, tq, rep * d), lambda i, j: (i // n, j, i % n)),
                   pl.BlockSpec((1, tq, LANES), lambda i, j: (i, j, 0))],
        out_shape=[jax.ShapeDtypeStruct((g // n, s, n * rep * d), F32),
                   jax.ShapeDtypeStruct((g, s, LANES), BF16)],
        compiler_params=_cparams(("parallel", "parallel")),
    )(q, kc, vc)


def _nsa_combine_body(oc_ref, os_ref, ow_ref, g_ref, b_ref, e_ref, o_ref):
    gs = jax.nn.sigmoid(g_ref[...] + b_ref[...])
    g_hi = gs.astype(BF16)
    g_lo = (gs - g_hi.astype(F32)).astype(BF16)
    out = None
    for i, ref in enumerate((oc_ref, os_ref, ow_ref)):
        w = (jnp.dot(g_hi, e_ref[i], preferred_element_type=F32)
             + jnp.dot(g_lo, e_ref[i], preferred_element_type=F32))
        term = w * ref[...]
        out = term if out is None else out + term
    o_ref[...] = out.astype(o_ref.dtype)


def _nsa_combine(oc, osel, ow, g_raw, gate_b, *, tm=512):
    t, n = oc.shape
    nh = NSA_HEADS * 3
    gp = jnp.pad(g_raw, ((0, 0), (0, LANES - nh)))
    bp = jnp.pad(gate_b.reshape(1, nh).astype(F32), ((0, 0), (0, LANES - nh)))
    row = jnp.arange(LANES)[:, None]
    col = jnp.arange(n)[None, :] // HEAD_DIM
    expand = jnp.stack([(row == col * 3 + i) for i in range(3)]).astype(BF16)
    tok = pl.BlockSpec((tm, n), lambda i: (i, 0))
    return pl.pallas_call(
        _nsa_combine_body, name="nsa_combine", grid=(t // tm,),
        in_specs=[tok, tok, tok, pl.BlockSpec((tm, LANES), lambda i: (i, 0)),
                  pl.BlockSpec((1, LANES), lambda i: (0, 0)),
                  pl.BlockSpec((3, LANES, n), lambda i: (0, 0, 0))],
        out_specs=tok,
        out_shape=jax.ShapeDtypeStruct((t, n), BF16),
        compiler_params=_cparams(("parallel",)),
    )(oc, osel, ow, gp, bp, expand)


def _to_heads(x, b, s, n):
    return x.reshape(b, s, n, -1).transpose(0, 2, 1, 3).reshape(b * n, s, -1)


def _from_heads(o, b, n):
    _, s, d = o.shape
    return o.reshape(b, n, s, d).transpose(0, 2, 1, 3).reshape(b * s, n * d)


def _even_mixer(h, gain, layer_idx, rope16, w_in, w_out, lq1, lk1, lq2, lk2, subln, b, s):
    na = MOBA_HEADS * HEAD_DIM
    nb = DIFF_HEADS * 2 * HEAD_DIM
    w_rope = jnp.concatenate([w_in[:, 0:2 * na], w_in[:, 3 * na:3 * na + 2 * nb]], axis=1)
    w_rest = jnp.concatenate([w_in[:, 2 * na:3 * na], w_in[:, 3 * na + 2 * nb:]], axis=1)
    w = jnp.concatenate([w_rope, w_rest], axis=1).astype(BF16)
    y = _proj(h, w, name="proj_even_in", gain=gain, rope=rope16, rope_cols=w_rope.shape[1], shift=ROPE_DIM // 2, out_dtype=BF16)
    q_a, k_a, q_b, k_b, v_a, v_b = jnp.split(y, [na, 2 * na, 2 * na + nb, 2 * na + 2 * nb, 3 * na + 2 * nb], axis=1)
    qa, ka = _moba_gate(_to_heads(q_a, b, s, MOBA_HEADS), _to_heads(k_a, b, s, MOBA_HEADS))
    o_a = _flash(qa[:, None], ka, _to_heads(v_a, b, s, MOBA_HEADS), name="flash_moba", tq=512, out_dtype=BF16)
    o_a = _from_heads(o_a[:, 0], b, MOBA_HEADS)
    nm = DIFF_HEADS * 2
    o_b = _flash(_to_heads(q_b, b, s, nm)[:, None], _to_heads(k_b, b, s, nm), _to_heads(v_b, b, s, DIFF_HEADS),
                 name="flash_diff", tq=512, scale=HEAD_DIM ** -0.5, v_div=2, out_dtype=F32)
    lambda_init = 0.8 - 0.6 * math.exp(-0.3 * layer_idx)
    o_b = _diff_combine(o_b.reshape(b * DIFF_HEADS, 2, s, DIFF_V), lq1, lk1, lq2, lk2, subln, lambda_init, batch=b)
    o = jnp.concatenate([o_a, o_b.reshape(b * s, DIFF_HEADS * DIFF_V)], axis=1)
    return _proj(o, w_out.astype(BF16), name="proj_out", res=h, out_dtype=F32)


def _odd_mixer(h, gain, positions, rope16, rope32, w_in, w_out, gate_b, pe_k, pe_v, k_w1, k_w2, v_w1, v_w2,
               q_norm, w_uq, kv_norm, w_ukv, b, s):
    G, R, d = NSA_GROUPS, NSA_REP, HEAD_DIM
    sizes = [NSA_HEADS * d] + [G * d] * 6 + [NSA_HEADS * 3, MLA_Q_RANK, MLA_KV_RANK, MLA_ROPE]
    offs = [0]
    for z in sizes:
        offs.append(offs[-1] + z)
    col = lambda i: w_in[:, offs[i]:offs[i + 1]]
    w_r = jnp.concatenate([col(0), col(3), col(5)], axis=1).astype(BF16)
    yr = _proj(h, w_r, name="proj_odd_rope", gain=gain, rope=rope16, rope_cols=w_r.shape[1], shift=ROPE_DIM // 2, out_dtype=BF16)
    q_c, k_slc, k_win = jnp.split(yr, [NSA_HEADS * d, NSA_HEADS * d + G * d], axis=1)
    w_p = jnp.concatenate([col(1), col(2), col(4), col(6), col(8), col(9), col(7)], axis=1)
    pad = (-w_p.shape[1]) % LANES
    w_p = jnp.pad(w_p, ((0, 0), (0, pad))).astype(BF16)
    yp = _proj(h, w_p, name="proj_odd_plain", gain=gain, out_dtype=F32)
    k_cmp, v_cmp, v_slc, v_win, c_q, c_kv, g_raw = jnp.split(
        yp[:, :w_p.shape[1] - pad],
        [G * d, 2 * G * d, 3 * G * d, 4 * G * d, 4 * G * d + MLA_Q_RANK, 4 * G * d + MLA_Q_RANK + MLA_KV_RANK], axis=1)
    w_pe = jnp.pad(col(10), ((0, 0), (0, LANES - MLA_ROPE))).astype(BF16)
    k_pe = _proj(h, w_pe, name="proj_mla_kpe", gain=gain, rope=rope32, rope_cols=LANES, shift=MLA_ROPE // 2, out_dtype=BF16)[:, :MLA_ROPE]

    scale = d ** -0.5
    qh = (q_c * scale).reshape(b, s, G, R, d).transpose(0, 2, 3, 1, 4).reshape(b * G, R, s, d)
    nc = s // NSA_CMP_STRIDE
    cpos = jnp.concatenate([positions[:, NSA_CMP_LEN - 1::NSA_CMP_STRIDE], positions[:, -1:]], axis=1)
    ctab = [t.reshape(b, nc, LANES) for t in _rope_tables(cpos, ROPE_DIM, HEAD_DIM)]
    xk = _to_heads(k_cmp, b, s, G).reshape(b * G, nc, NSA_CMP_STRIDE * d)
    xv = _to_heads(v_cmp, b, s, G).reshape(b * G, nc, NSA_CMP_STRIDE * d)
    kc = _nsa_compress(xk, pe_k, k_w1.astype(BF16), k_w2.astype(BF16), ctab, rope=True, groups=G)
    vc = _nsa_compress(xv, pe_v, v_w1.astype(BF16), v_w2.astype(BF16), ctab, rope=False, groups=G)
    o_c, mbias = _nsa_cmp(qh, kc, vc.transpose(0, 2, 1))
    ns_onehot = jax.nn.one_hot(jnp.arange(s) // NSA_SEL_BLOCK, LANES, dtype=BF16)
    zq = jnp.zeros((b * G, R, s, LANES - d), BF16)
    q_aug = jnp.concatenate([jnp.broadcast_to(mbias[:, None], (b * G, R, s, LANES)), qh, zq], axis=-1)
    ks = _to_heads(k_slc, b, s, G)
    k_aug = jnp.concatenate([jnp.broadcast_to(ns_onehot[None], (b * G, s, LANES)), ks,
                             jnp.zeros((b * G, s, LANES - d), BF16)], axis=-1)
    o_s = _flash(q_aug, k_aug, _to_heads(v_slc.astype(BF16), b, s, G), name="flash_sel", tq=128,
                 merged_groups=G, out_dtype=F32)
    o_w = _flash(qh, _to_heads(k_win, b, s, G), _to_heads(v_win.astype(BF16), b, s, G), name="flash_win", tq=128,
                 window=NSA_WINDOW, merged_groups=G, out_dtype=F32)
    t = b * s
    o_nsa = _nsa_combine(o_c.reshape(t, -1), o_s.reshape(t, -1), o_w.reshape(t, -1), g_raw, gate_b)

    hq = MLA_NOPE + MLA_ROPE
    wq = w_uq.reshape(MLA_Q_RANK, MLA_HEADS, hq)
    wq = jnp.concatenate([wq[:, :, MLA_NOPE:].reshape(MLA_Q_RANK, -1), wq[:, :, :MLA_NOPE].reshape(MLA_Q_RANK, -1)],
                         axis=1).astype(BF16)
    q = _proj(c_q, wq, name="proj_mla_q", gain=q_norm, rope=rope32, rope_cols=MLA_HEADS * MLA_ROPE, shift=MLA_ROPE // 2,
              out_dtype=BF16, chunk=256)
    q_pe = q[:, :MLA_HEADS * MLA_ROPE].reshape(b, s, MLA_HEADS, MLA_ROPE)
    q_nope = q[:, MLA_HEADS * MLA_ROPE:].reshape(b, s, MLA_HEADS, MLA_NOPE)
    kv = _proj(c_kv, w_ukv.astype(BF16), name="proj_mla_kv", gain=kv_norm, out_dtype=BF16).reshape(b, s, MLA_HEADS, MLA_NOPE + MLA_V)
    zpad = jnp.zeros((b, s, MLA_HEADS, LANES - hq), BF16)
    qf = jnp.concatenate([q_nope, q_pe, zpad], axis=-1).transpose(0, 2, 1, 3).reshape(b * MLA_HEADS, 1, s, LANES)
    kpe = jnp.broadcast_to(k_pe.reshape(b, s, 1, MLA_ROPE), (b, s, MLA_HEADS, MLA_ROPE))
    kf = jnp.concatenate([kv[..., :MLA_NOPE], kpe, zpad], axis=-1).transpose(0, 2, 1, 3).reshape(b * MLA_HEADS, s, LANES)
    vf = kv[..., MLA_NOPE:].transpose(0, 2, 1, 3).reshape(b * MLA_HEADS, s, MLA_V)
    o_d = _flash(qf, kf, vf, name="flash_mla", tq=512, scale=hq ** -0.5, out_dtype=BF16)
    o_d = _from_heads(o_d[:, 0], b, MLA_HEADS)

    o = jnp.concatenate([o_nsa, o_d], axis=1)
    return _proj(o, w_out.astype(BF16), name="proj_out", res=h, out_dtype=F32)


def kernel(x, positions, attn_norm, ffn_norm, final_norm, ffn_w_gate, ffn_w_up, ffn_w_down, ev_w_in, ev_w_out, diff_lambda_q1, diff_lambda_k1, diff_lambda_q2, diff_lambda_k2, diff_subln, od_w_in, od_w_out, nsa_gate_b, nsa_pe_k, nsa_pe_v, nsa_k_w1, nsa_k_w2, nsa_v_w1, nsa_v_w2, mla_q_norm, mla_w_uq, mla_kv_norm, mla_w_ukv):
    b, s, d = x.shape
    depth = attn_norm.shape[0]
    rope16 = _rope_tables(positions, ROPE_DIM, HEAD_DIM)
    rope32 = _rope_tables(positions, MLA_ROPE, MLA_ROPE)
    h = x.reshape(b * s, d)
    for l in range(depth):
        i = l // 2
        if l % 2 == 0:
            h = _even_mixer(h, attn_norm[l], l, rope16, ev_w_in[i], ev_w_out[i], diff_lambda_q1[i],
                            diff_lambda_k1[i], diff_lambda_q2[i], diff_lambda_k2[i], diff_subln[i], b, s)
        else:
            h = _odd_mixer(h, attn_norm[l], positions, rope16, rope32, od_w_in[i], od_w_out[i], nsa_gate_b[i],
                           nsa_pe_k[i], nsa_pe_v[i], nsa_k_w1[i], nsa_k_w2[i], nsa_v_w1[i], nsa_v_w2[i],
                           mla_q_norm[i], mla_w_uq[i], mla_kv_norm[i], mla_w_ukv[i], b, s)
        h = _ffn(h, ffn_norm[l], ffn_w_gate[l].astype(BF16), ffn_w_up[l].astype(BF16), ffn_w_down[l].astype(BF16),
                 final_norm, final_norm=(l == depth - 1))
    return h.reshape(b, s, d)
```

```python
import functools
import math

import jax
import jax.numpy as jnp
from jax import lax
from jax.experimental import pallas as pl
from jax.experimental.pallas import tpu as pltpu

F32 = jnp.float32
BF16 = jnp.bfloat16

D_MODEL = 1024
HEAD_DIM = 64
ROPE_THETA = 500000.0
ROPE_DIM = HEAD_DIM // 4
NORM_EPS = 1e-5
D_FF = 2816

MOBA_HEADS = 8
MOBA_BLOCK = 256
MOBA_TOPK = 3
DIFF_HEADS = 4
DIFF_V = 2 * HEAD_DIM
NSA_HEADS = 8
NSA_GROUPS = 2
NSA_REP = NSA_HEADS // NSA_GROUPS
NSA_CMP_LEN = 32
NSA_CMP_STRIDE = 16
NSA_CMP_HIDDEN = 256
NSA_SEL_BLOCK = 64
NSA_SEL_TOPK = 16
NSA_WINDOW = 512
NSA_FORCE_BONUS = 1e3
MLA_HEADS = 8
MLA_Q_RANK = 256
MLA_KV_RANK = 128
MLA_NOPE = 64
MLA_ROPE = 32
MLA_V = 64

LANES = 128
MOBA_BIAS_LANES = 32
NEG = -1e30
M_INIT = -1e37
LOG2E = 1.4426950408889634
UNROLL = 4
VMEM_LIMIT = 56 * 1024 * 1024

_NT = (((1,), (1,)), ((), ()))


def _cparams(sem):
    return pltpu.CompilerParams(dimension_semantics=sem, vmem_limit_bytes=VMEM_LIMIT)


def _rope_tables(positions, dim, period):
    r = dim // 2
    inv = 1.0 / (ROPE_THETA ** (jnp.arange(0, dim, 2, dtype=F32) / dim))
    ang = positions.astype(F32)[..., None] * inv
    cos, sin = jnp.cos(ang), jnp.sin(ang)
    rest = period - 2 * r
    ones = jnp.ones(ang.shape[:-1] + (rest,), F32)
    zeros = jnp.zeros(ang.shape[:-1] + (rest,), F32)
    zr = jnp.zeros_like(sin)
    c = jnp.concatenate([cos, cos, ones], -1)
    s1 = jnp.concatenate([-sin, zr, zeros], -1)
    s2 = jnp.concatenate([zr, sin, zeros], -1)
    reps = LANES // period
    tile = lambda t: jnp.tile(t, (1,) * (t.ndim - 1) + (reps,)).reshape(-1, LANES)
    return tile(c), tile(s1), tile(s2)


def _apply_rope(y, c, s1, s2, shift):
    return y * c + pltpu.roll(y, LANES - shift, 1) * s1 + pltpu.roll(y, shift, 1) * s2


def _proj_body(*refs, has_norm, has_x2, has_rope, has_res, rope_cols, shift, chunk):
    it = iter(refs)
    x_ref = next(it)
    x2_ref = next(it) if has_x2 else None
    g_ref = next(it) if has_norm else None
    w_ref = next(it)
    if has_rope:
        c_ref, s1_ref, s2_ref = next(it), next(it), next(it)
    res_ref = next(it) if has_res else None
    o_ref = next(it)
    n = w_ref.shape[1]
    if has_norm:
        xf = x_ref[...].astype(F32)
        y = xf * lax.rsqrt(jnp.mean(xf * xf, axis=-1, keepdims=True) + NORM_EPS)
        xb = (y * g_ref[...]).astype(BF16)
    else:
        xb = x_ref[...].astype(BF16)
    k1 = xb.shape[1]
    for c0 in range(0, n, chunk):
        cw = min(chunk, n - c0)
        y = jnp.dot(xb, w_ref[0:k1, c0:c0 + cw], preferred_element_type=F32)
        if has_x2:
            y = y + jnp.dot(x2_ref[...].astype(BF16), w_ref[k1:, c0:c0 + cw], preferred_element_type=F32)
        if has_res:
            y = y + res_ref[:, c0:c0 + cw]
        if has_rope and c0 < rope_cols:
            for k0 in range(0, cw, LANES):
                ys = y[:, k0:k0 + LANES]
                if c0 + k0 < rope_cols:
                    ys = _apply_rope(ys, c_ref[...], s1_ref[...], s2_ref[...], shift)
                o_ref[:, c0 + k0:c0 + k0 + LANES] = ys.astype(o_ref.dtype)
        else:
            o_ref[:, c0:c0 + cw] = y.astype(o_ref.dtype)


def _proj(x, w, *, name, x2=None, gain=None, rope=None, rope_cols=0, shift=0, res=None, out_dtype=F32, tm=512,
          chunk=512):
    t, k = x.shape
    n = w.shape[1]
    assert t % tm == 0 and n % LANES == 0 and rope_cols % LANES == 0
    has_norm, has_rope, has_res, has_x2 = gain is not None, rope is not None, res is not None, x2 is not None
    assert not (has_norm and has_x2)
    args, specs = [x], [pl.BlockSpec((tm, k), lambda i: (i, 0))]
    if has_x2:
        args.append(x2)
        specs.append(pl.BlockSpec((tm, x2.shape[1]), lambda i: (i, 0)))
    if has_norm:
        args.append(gain.reshape(1, k).astype(F32))
        specs.append(pl.BlockSpec((1, k), lambda i: (0, 0)))
    args.append(w)
    specs.append(pl.BlockSpec((w.shape[0], n), lambda i: (0, 0)))
    if has_rope:
        for tb in rope:
            args.append(tb)
            specs.append(pl.BlockSpec((tm, LANES), lambda i: (i, 0)))
    if has_res:
        args.append(res)
        specs.append(pl.BlockSpec((tm, n), lambda i: (i, 0)))
    body = functools.partial(_proj_body, has_norm=has_norm, has_x2=has_x2, has_rope=has_rope, has_res=has_res,
                             rope_cols=rope_cols, shift=shift, chunk=chunk)
    return pl.pallas_call(
        body, name=name, grid=(t // tm,), in_specs=specs,
        out_specs=pl.BlockSpec((tm, n), lambda i: (i, 0)),
        out_shape=jax.ShapeDtypeStruct((t, n), out_dtype),
        compiler_params=_cparams(("parallel",)),
    )(*args)


def _ffn_body(x_ref, g_ref, wg_ref, wu_ref, wd_ref, fg_ref, o_ref, xn_ref, acc_ref, *, final_norm):
    j = pl.program_id(1)

    @pl.when(j == 0)
    def _():
        xf = x_ref[...]
        y = xf * lax.rsqrt(jnp.mean(xf * xf, axis=-1, keepdims=True) + NORM_EPS)
        xn_ref[...] = (y * g_ref[...]).astype(BF16)
        acc_ref[...] = jnp.zeros_like(acc_ref)

    xn = xn_ref[...]
    g = jnp.dot(xn, wg_ref[...], preferred_element_type=F32)
    u = jnp.dot(xn, wu_ref[...], preferred_element_type=F32)
    a = (jax.nn.silu(g) * u).astype(BF16)
    acc_ref[...] += jnp.dot(a, wd_ref[...], preferred_element_type=F32)

    @pl.when(j == pl.num_programs(1) - 1)
    def _():
        h = x_ref[...] + acc_ref[...]
        if final_norm:
            y = h * lax.rsqrt(jnp.mean(h * h, axis=-1, keepdims=True) + NORM_EPS)
            h = y * fg_ref[...]
        o_ref[...] = h


def _ffn(x, gain, wg, wu, wd, final_gain, *, final_norm, tm=512, tf=1408):
    t, d = x.shape
    f = wg.shape[1]
    assert t % tm == 0 and f % tf == 0
    return pl.pallas_call(
        functools.partial(_ffn_body, final_norm=final_norm), name="ffn",
        grid=(t // tm, f // tf),
        in_specs=[
            pl.BlockSpec((tm, d), lambda i, j: (i, 0)),
            pl.BlockSpec((1, d), lambda i, j: (0, 0)),
            pl.BlockSpec((d, tf), lambda i, j: (0, j)),
            pl.BlockSpec((d, tf), lambda i, j: (0, j)),
            pl.BlockSpec((tf, d), lambda i, j: (j, 0)),
            pl.BlockSpec((1, d), lambda i, j: (0, 0)),
        ],
        out_specs=pl.BlockSpec((tm, d), lambda i, j: (i, 0)),
        out_shape=jax.ShapeDtypeStruct((t, d), F32),
        scratch_shapes=[pltpu.VMEM((tm, d), BF16), pltpu.VMEM((tm, d), F32)],
        compiler_params=_cparams(("parallel", "arbitrary")),
    )(x, gain.reshape(1, d).astype(F32), wg, wu, wd, final_gain.reshape(1, d).astype(F32))


def _sweep(q, k_get, vt_get, s_ref, *, q0, tq, tk, dv, scale, window):
    cols = q.shape[0]

    def fold8(x, op):
        return op(x.reshape(tk // 8, 8, cols), axis=0)

    def score_step(j, slot, mrun, masked):
        ks = pl.multiple_of(j * tk, tk)
        kb = k_get(ks)
        s = lax.dot_general(kb, q, _NT, preferred_element_type=F32) * (scale * LOG2E)
        if masked:
            kpos = ks + lax.broadcasted_iota(jnp.int32, (tk, cols), 0)
            qpos = q0 + lax.rem(lax.broadcasted_iota(jnp.int32, (tk, cols), 1), tq)
            ok = kpos <= qpos
            if window is not None:
                ok = jnp.logical_and(ok, kpos > qpos - window)
            s = jnp.where(ok, s, NEG)
        s_ref[slot] = s
        return jnp.maximum(mrun, fold8(s, jnp.max))

    def pv_step(j, slot, carry, m):
        l8, acc = carry
        p = jnp.exp2(s_ref[slot] - m)
        l8 = l8 + fold8(p, jnp.sum)
        acc = acc + jnp.dot(vt_get(j), p.astype(BF16), preferred_element_type=F32)
        return l8, acc

    m_init = jnp.full((8, cols), M_INIT, F32)
    acc_init = (jnp.zeros((8, cols), F32), jnp.zeros((dv, cols), F32))
    if window is None:
        n_full = q0 // tk
        groups = n_full // UNROLL

        def score_group(i, mrun):
            for u in range(UNROLL):
                mrun = score_step(UNROLL * i + u, UNROLL * i + u, mrun, False)
            return mrun

        mrun = lax.fori_loop(0, groups, score_group, m_init)
        mrun = lax.fori_loop(groups * UNROLL, n_full, lambda j, t: score_step(j, j, t, False), mrun)
        mrun = score_step(n_full, n_full, mrun, True)
        m = jnp.max(mrun, axis=0, keepdims=True)

        def pv_group(i, carry):
            for u in range(UNROLL):
                carry = pv_step(UNROLL * i + u, UNROLL * i + u, carry, m)
            return carry

        carry = lax.fori_loop(0, groups, pv_group, acc_init)
        carry = lax.fori_loop(groups * UNROLL, n_full, lambda j, t: pv_step(j, j, t, m), carry)
        l8, acc = pv_step(n_full, n_full, carry, m)
    else:
        j_lo = jnp.maximum(q0 - window + 1, 0) // tk
        j_hi = (q0 + tq - 1) // tk
        mrun = lax.fori_loop(j_lo, j_hi + 1, lambda j, t: score_step(j, j - j_lo, t, True), m_init)
        m = jnp.max(mrun, axis=0, keepdims=True)
        l8, acc = lax.fori_loop(j_lo, j_hi + 1, lambda j, t: pv_step(j, j - j_lo, t, m), acc_init)
    l = jnp.sum(l8, axis=0, keepdims=True)
    return acc / jnp.maximum(l, 1e-30)


def _flash_body(q_ref, k_ref, vt_ref, o_ref, s_ref, *, rep, tq, tk, scale, window, merged):
    qi = pl.program_id(1)
    dk = q_ref.shape[-1]
    dv = vt_ref.shape[2]
    q = q_ref[0].reshape(rep * tq, dk)
    o = _sweep(q, lambda ks: k_ref[0, pl.ds(ks, tk), :], lambda j: vt_ref[0, j], s_ref,
               q0=qi * tq, tq=tq, tk=tk, dv=dv, scale=scale, window=window).T
    if merged:
        for r in range(rep):
            o_ref[0, :, r * dv:(r + 1) * dv] = o[r * tq:(r + 1) * tq].astype(o_ref.dtype)
    else:
        o_ref[0] = o.reshape(rep, tq, dv).astype(o_ref.dtype)


def _flash(q, k, v, *, name, tq, tk=512, scale=1.0, window=None, v_div=1, merged_groups=None, out_dtype=F32):
    g, rep, s, dk = q.shape
    gv, _, dv = v.shape
    assert s % tk == 0 and tk % tq == 0
    vt = v.reshape(gv, s // tk, tk, dv).transpose(0, 1, 3, 2)
    n_slots = s // tk if window is None else (window + tq - 2) // tk + 2
    in_specs = [
        pl.BlockSpec((1, rep, tq, dk), lambda i, j: (i, 0, j, 0)),
        pl.BlockSpec((1, s, dk), lambda i, j: (i, 0, 0)),
        pl.BlockSpec((1, s // tk, dv, tk), lambda i, j: (i // v_div, 0, 0, 0)),
    ]
    if merged_groups is None:
        out_spec = pl.BlockSpec((1, rep, tq, dv), lambda i, j: (i, 0, j, 0))
        out_shape = jax.ShapeDtypeStruct((g, rep, s, dv), out_dtype)
    else:
        n = merged_groups
        out_spec = pl.BlockSpec((1, tq, rep * dv), lambda i, j: (i // n, j, i % n))
        out_shape = jax.ShapeDtypeStruct((g // n, s, n * rep * dv), out_dtype)
    body = functools.partial(_flash_body, rep=rep, tq=tq, tk=tk, scale=scale, window=window,
                             merged=merged_groups is not None)
    return pl.pallas_call(
        body, name=name, grid=(g, s // tq), in_specs=in_specs, out_specs=out_spec, out_shape=out_shape,
        scratch_shapes=[pltpu.VMEM((n_slots, tk, rep * tq), F32)],
        compiler_params=_cparams(("parallel", "arbitrary")),
    )(q, k, vt)


def _head_lanes(mode, v, lane):
    d = HEAD_DIM
    if mode == "diff":
        return jnp.logical_and(lane >= d * v, lane < d * (v + 1))
    if mode == "moba":
        lo = 2 * d + v * MOBA_BIAS_LANES
        return jnp.logical_or(jnp.logical_and(lane >= d * v, lane < d * (v + 1)),
                              jnp.logical_and(lane >= lo, lane < lo + MOBA_BIAS_LANES))
    return jnp.logical_and(lane >= LANES * v, lane < LANES * (v + 1))


def _pflash_body(*refs, mode, tq, tk, scale, lambda_init):
    if mode == "diff":
        q_ref, k_ref, v_ref, lq1_ref, lk1_ref, lq2_ref, lk2_ref, sg_ref, o_ref, s_ref, vt_ref = refs
    else:
        q_ref, k_ref, v_ref, o_ref, s_ref, vt_ref = refs
    qi = pl.program_id(1)
    n_chunks = v_ref.shape[1] // tk

    @pl.when(qi == 0)
    def _():
        for c in range(n_chunks):
            vt_ref[c] = v_ref[0, c * tk:(c + 1) * tk, :].astype(F32).T.astype(BF16)

    q = q_ref[0]
    lane = lax.broadcasted_iota(jnp.int32, q.shape, 1)
    outs = []
    for v in range(2):
        qv = jnp.where(_head_lanes(mode, v, lane), q, jnp.zeros_like(q))
        outs.append(_sweep(qv, lambda ks: k_ref[0, pl.ds(ks, tk), :], lambda j: vt_ref[j], s_ref,
                           q0=qi * tq, tq=tq, tk=tk, dv=LANES, scale=scale, window=None))
    if mode == "diff":
        lam = (jnp.exp(jnp.sum(lq1_ref[...] * lk1_ref[...], axis=-1, keepdims=True))
               - jnp.exp(jnp.sum(lq2_ref[...] * lk2_ref[...], axis=-1, keepdims=True)) + lambda_init)
        d = outs[0] - lam * outs[1]
        y = d * lax.rsqrt(jnp.mean(d * d, axis=0, keepdims=True) + NORM_EPS)
        o_t = (y * sg_ref[...]) * (1.0 - lambda_init)
    else:
        row = lax.broadcasted_iota(jnp.int32, outs[0].shape, 0)
        o_t = jnp.where(row < HEAD_DIM, outs[0], outs[1])
    o_ref[0] = o_t.T.astype(o_ref.dtype)


def _pflash(q, k, v, *, name, mode, dk, q_map, k_map, v_map, batch, pairs, tq=512, tk=512, scale=1.0,
            diff_params=None, lambda_init=0.0):
    s = v.shape[1]
    assert s % tk == 0 and tq == tk
    args = [q, k, v]
    in_specs = [pl.BlockSpec((1, tq, dk), q_map), pl.BlockSpec((1, s, dk), k_map),
                pl.BlockSpec((1, s, LANES), v_map)]
    if mode == "diff":
        lq1, lk1, lq2, lk2, subln = diff_params
        for a in (lq1, lk1, lq2, lk2):
            args.append(a.reshape(1, HEAD_DIM).astype(F32))
            in_specs.append(pl.BlockSpec((1, HEAD_DIM), lambda i, j: (0, 0)))
        args.append(subln.reshape(LANES, 1).astype(F32))
        in_specs.append(pl.BlockSpec((LANES, 1), lambda i, j: (0, 0)))
    body = functools.partial(_pflash_body, mode=mode, tq=tq, tk=tk, scale=scale, lambda_init=lambda_init)
    return pl.pallas_call(
        body, name=name, grid=(batch * pairs, s // tq), in_specs=in_specs,
        out_specs=pl.BlockSpec((1, tq, LANES), lambda i, j: (i // pairs, j, i % pairs)),
        out_shape=jax.ShapeDtypeStruct((batch, s, pairs * LANES), BF16),
        scratch_shapes=[pltpu.VMEM((s // tk, tk, tq), F32), pltpu.VMEM((s // tk, LANES, tk), BF16)],
        compiler_params=_cparams(("parallel", "arbitrary")),
    )(*args)


def _topk_mask(score, row, k):
    sel = None
    for _ in range(k):
        mx = jnp.max(score, axis=0, keepdims=True)
        idx = jnp.min(jnp.where(score == mx, row, LANES), axis=0, keepdims=True)
        hit = row == idx
        pick = jnp.logical_and(hit, mx > -jnp.inf)
        sel = pick if sel is None else jnp.logical_or(sel, pick)
        score = jnp.where(hit, -jnp.inf, score)
    return sel


def _moba_gate_body(q_ref, k_ref, qa_ref, ka_ref, kmean_ref, *, tq, seq):
    qi = pl.program_id(1)
    nbl = MOBA_BIAS_LANES

    @pl.when(qi == 0)
    def _():
        rowblk = lax.rem(lax.broadcasted_iota(jnp.int32, (LANES, seq), 0), nbl)
        colblk = lax.broadcasted_iota(jnp.int32, (LANES, seq), 1) // MOBA_BLOCK
        ind = jnp.where(rowblk == colblk, 1.0, 0.0).astype(BF16)
        ksum = jnp.dot(ind, k_ref[0], preferred_element_type=F32)
        rhead = lax.broadcasted_iota(jnp.int32, (LANES, LANES), 0) // nbl
        lhead = lax.broadcasted_iota(jnp.int32, (LANES, LANES), 1) // HEAD_DIM
        kmean_ref[...] = jnp.where(rhead == lhead, ksum * (1.0 / MOBA_BLOCK), 0.0)

    q = q_ref[0]
    q0 = pl.multiple_of(qi * tq, tq)
    km = kmean_ref[...]
    km_hi = km.astype(BF16)
    km_lo = (km - km_hi.astype(F32)).astype(BF16)
    gate = (lax.dot_general(km_hi, q, _NT, preferred_element_type=F32)
            + lax.dot_general(km_lo, q, _NT, preferred_element_type=F32))
    blk_t = lax.broadcasted_iota(jnp.int32, (nbl, tq), 0)
    own_t = (q0 + lax.broadcasted_iota(jnp.int32, (nbl, tq), 1)) // MOBA_BLOCK
    parts = []
    for v in range(2):
        g_v = jnp.where(blk_t < own_t, gate[v * nbl:(v + 1) * nbl], -jnp.inf)
        sel = jnp.logical_or(_topk_mask(g_v, blk_t, MOBA_TOPK), blk_t == own_t)
        parts.append(jnp.where(sel, 0.0, NEG))
    parts.append(jnp.zeros((LANES - 2 * nbl, tq), F32))
    bias = jnp.concatenate(parts, axis=0).T
    qa_ref[0, :, 0:LANES] = q * (HEAD_DIM ** -0.5)
    qa_ref[0, :, LANES:2 * LANES] = bias.astype(BF16)
    lane = lax.broadcasted_iota(jnp.int32, (tq, LANES), 1)
    own = (q0 + lax.broadcasted_iota(jnp.int32, (tq, LANES), 0)) // MOBA_BLOCK
    onehot = jnp.where(jnp.logical_and(lane < 2 * nbl, lax.rem(lane, nbl) == own), 1.0, 0.0)
    ka_ref[0, :, 0:LANES] = k_ref[0, pl.ds(q0, tq), :]
    ka_ref[0, :, LANES:2 * LANES] = onehot.astype(BF16)


def _moba_gate(y3, *, q_blk, k_blk, pairs, tq=512):
    b, s, _ = y3.shape
    assert s % MOBA_BLOCK == 0 and s // MOBA_BLOCK <= MOBA_BIAS_LANES and s % tq == 0
    out = jax.ShapeDtypeStruct((b * pairs, s, 2 * LANES), BF16)
    return pl.pallas_call(
        functools.partial(_moba_gate_body, tq=tq, seq=s), name="moba_gate",
        grid=(b * pairs, s // tq),
        in_specs=[pl.BlockSpec((1, tq, LANES), lambda i, j: (i // pairs, j, q_blk + i % pairs)),
                  pl.BlockSpec((1, s, LANES), lambda i, j: (i // pairs, 0, k_blk + i % pairs))],
        out_specs=[pl.BlockSpec((1, tq, 2 * LANES), lambda i, j: (i, j, 0)),
                   pl.BlockSpec((1, tq, 2 * LANES), lambda i, j: (i, j, 0))],
        out_shape=[out, out],
        scratch_shapes=[pltpu.VMEM((LANES, LANES), F32)],
        compiler_params=_cparams(("parallel", "arbitrary")),
    )(y3, y3)


def _diff_combine_body(o1_ref, o2_ref, lq1_ref, lk1_ref, lq2_ref, lk2_ref, sg_ref, o_ref, *, lambda_init):
    lam = (jnp.exp(jnp.sum(lq1_ref[...] * lk1_ref[...], axis=-1, keepdims=True))
           - jnp.exp(jnp.sum(lq2_ref[...] * lk2_ref[...], axis=-1, keepdims=True)) + lambda_init)
    d = o1_ref[0, 0] - lam * o2_ref[0, 0]
    y = d * lax.rsqrt(jnp.mean(d * d, axis=-1, keepdims=True) + NORM_EPS)
    o_ref[0] = ((y * sg_ref[...]) * (1.0 - lambda_init)).astype(o_ref.dtype)


def _diff_combine(o, lq1, lk1, lq2, lk2, subln, lambda_init, *, batch, tq=512):
    g, _, s, dv = o.shape
    vec = lambda a: a.reshape(1, -1).astype(F32)
    vspec = lambda n: pl.BlockSpec((1, n), lambda i, j: (0, 0))
    return pl.pallas_call(
        functools.partial(_diff_combine_body, lambda_init=lambda_init), name="diff_combine",
        grid=(g, s // tq),
        in_specs=[pl.BlockSpec((1, 1, tq, dv), lambda i, j: (i, 0, j, 0)),
                  pl.BlockSpec((1, 1, tq, dv), lambda i, j: (i, 1, j, 0)),
                  vspec(HEAD_DIM), vspec(HEAD_DIM), vspec(HEAD_DIM), vspec(HEAD_DIM), vspec(dv)],
        out_specs=pl.BlockSpec((1, tq, dv), lambda i, j: (i // DIFF_HEADS, j, i % DIFF_HEADS)),
        out_shape=jax.ShapeDtypeStruct((batch, s, DIFF_HEADS * dv), BF16),
        compiler_params=_cparams(("parallel", "parallel")),
    )(o, o, vec(lq1), vec(lk1), vec(lq2), vec(lk2), vec(subln))


def _nsa_compress_body(x_ref, pe_ref, w1_ref, w2_ref, c_ref, s1_ref, s2_ref, o_ref, *, rope):
    half = w1_ref.shape[0] // 2
    x = x_ref[0].astype(F32)
    lo = (x + pe_ref[:, 0:half]).astype(BF16)
    hi = (x + pe_ref[:, half:2 * half]).astype(BF16)
    a = jnp.dot(lo, w1_ref[0:half, :], preferred_element_type=F32)
    b = jnp.dot(hi, w1_ref[half:2 * half, :], preferred_element_type=F32)
    n = a.shape[0]
    h1 = a + pltpu.roll(b, n - 1, 0)
    y = jnp.dot(jax.nn.gelu(h1).astype(BF16), w2_ref[...], preferred_element_type=F32)
    if rope:
        y = _apply_rope(y, c_ref[0], s1_ref[0], s2_ref[0], ROPE_DIM // 2)
    o_ref[0] = y[:, 0:HEAD_DIM].astype(o_ref.dtype)


def _nsa_compress(xr, pe, w1, w2, tables, *, rope, groups):
    g, n, w = xr.shape
    hid = w1.shape[1]
    w2p = jnp.pad(w2, ((0, 0), (0, LANES - w2.shape[1])))
    tspec = pl.BlockSpec((1, n, LANES), lambda i: (i // groups, 0, 0))
    return pl.pallas_call(
        functools.partial(_nsa_compress_body, rope=rope), name="nsa_compress",
        grid=(g,),
        in_specs=[pl.BlockSpec((1, n, w), lambda i: (i, 0, 0)),
                  pl.BlockSpec((1, 2 * w), lambda i: (0, 0)),
                  pl.BlockSpec((2 * w, hid), lambda i: (0, 0)),
                  pl.BlockSpec((hid, LANES), lambda i: (0, 0)),
                  tspec, tspec, tspec],
        out_specs=pl.BlockSpec((1, n, HEAD_DIM), lambda i: (i, 0, 0)),
        out_shape=jax.ShapeDtypeStruct((g, n, HEAD_DIM), BF16),
        compiler_params=_cparams(("parallel",)),
    )(xr, pe.reshape(1, 2 * w).astype(F32), w1, w2p, *tables)


def _nsa_cmp_body(q_ref, kc_ref, vc_ref, oc_ref, mb_ref, *, tq, rep):
    qi = pl.program_id(1)
    q0 = qi * tq
    nc = kc_ref.shape[1]
    kc = kc_ref[0]
    vct = vc_ref[0]
    tpos = q0 + lax.broadcasted_iota(jnp.int32, (nc, tq), 1)
    cend = lax.broadcasted_iota(jnp.int32, (nc, tq), 0) * NSA_CMP_STRIDE + (NSA_CMP_LEN - 1)
    ok = cend <= tpos
    psum = jnp.zeros((nc, tq), F32)
    for r in range(rep):
        s = lax.dot_general(kc, q_ref[0, r], _NT, preferred_element_type=F32)
        s = jnp.where(ok, s, NEG)
        m = jnp.max(s, axis=0, keepdims=True)
        e = jnp.where(ok, jnp.exp(s - m), 0.0)
        p = e * (1.0 / jnp.maximum(jnp.sum(e, axis=0, keepdims=True), 1e-30))
        o_t = jnp.dot(vct, p.astype(BF16), preferred_element_type=F32)
        oc_ref[0, :, r * HEAD_DIM:(r + 1) * HEAD_DIM] = o_t.T.astype(oc_ref.dtype)
        psum = psum + p
    sstart = lax.broadcasted_iota(jnp.int32, (LANES, nc), 0) * NSA_SEL_BLOCK
    cstart = lax.broadcasted_iota(jnp.int32, (LANES, nc), 1) * NSA_CMP_STRIDE
    ov = jnp.where(jnp.logical_and(cstart < sstart + NSA_SEL_BLOCK, cstart + NSA_CMP_LEN > sstart),
                   1.0, 0.0).astype(BF16)
    p_hi = psum.astype(BF16)
    r1 = psum - p_hi.astype(F32)
    p_mid = r1.astype(BF16)
    p_lo = (r1 - p_mid.astype(F32)).astype(BF16)
    imp = (jnp.dot(ov, p_hi, preferred_element_type=F32) + jnp.dot(ov, p_mid, preferred_element_type=F32)
           + jnp.dot(ov, p_lo, preferred_element_type=F32))
    blk = lax.broadcasted_iota(jnp.int32, (LANES, tq), 0)
    qb = (q0 + lax.broadcasted_iota(jnp.int32, (LANES, tq), 1)) // NSA_SEL_BLOCK
    forced = jnp.logical_or(blk == 0, jnp.logical_or(blk == qb, blk == qb - 1))
    imp = jnp.where(forced, imp + NSA_FORCE_BONUS, imp)
    imp = jnp.where(blk <= qb, imp, -jnp.inf)
    sel = _topk_mask(imp, blk, NSA_SEL_TOPK)
    mb_ref[0] = jnp.where(sel, 0.0, NEG).T.astype(mb_ref.dtype)


def _nsa_cmp(q, kc, vc, *, tq=512):
    g, rep, s, d = q.shape
    nc = kc.shape[1]
    assert s // NSA_SEL_BLOCK <= LANES and s % tq == 0
    n = NSA_GROUPS
    return pl.pallas_call(
        functools.partial(_nsa_cmp_body, tq=tq, rep=rep), name="nsa_cmp",
        grid=(g, s // tq),
        in_specs=[pl.BlockSpec((1, rep, tq, d), lambda i, j: (i, 0, j, 0)),
                  pl.BlockSpec((1, nc, d), lambda i, j: (i, 0, 0)),
                  pl.BlockSpec((1, d, nc), lambda i, j: (i, 0, 0))],
        out_specs=[pl.BlockSpec((1, tq, rep * d), lambda i, j: (i // n, j, i % n)),
                   pl.BlockSpec((1, tq, LANES), lambda i, j: (i, j, 0))],
        out_shape=[jax.ShapeDtypeStruct((g // n, s, n * rep * d), F32),
                   jax.ShapeDtypeStruct((g, s, LANES), BF16)],
        compiler_params=_cparams(("parallel", "parallel")),
    )(q, kc, vc)


def _nsa_combine_body(oc_ref, os_ref, ow_ref, g_ref, b_ref, e_ref, o_ref):
    gs = jax.nn.sigmoid(g_ref[...] + b_ref[...])
    g_hi = gs.astype(BF16)
    g_lo = (gs - g_hi.astype(F32)).astype(BF16)
    out = None
    for i, ref in enumerate((oc_ref, os_ref, ow_ref)):
        w = (jnp.dot(g_hi, e_ref[i], preferred_element_type=F32)
             + jnp.dot(g_lo, e_ref[i], preferred_element_type=F32))
        term = w * ref[...]
        out = term if out is None else out + term
    o_ref[...] = out.astype(o_ref.dtype)


def _nsa_combine(oc, osel, ow, g_raw, gate_b, *, tm=512):
    t, n = oc.shape
    nh = NSA_HEADS * 3
    gp = jnp.pad(g_raw, ((0, 0), (0, LANES - nh)))
    bp = jnp.pad(gate_b.reshape(1, nh).astype(F32), ((0, 0), (0, LANES - nh)))
    row = jnp.arange(LANES)[:, None]
    col = jnp.arange(n)[None, :] // HEAD_DIM
    expand = jnp.stack([(row == col * 3 + i) for i in range(3)]).astype(BF16)
    tok = pl.BlockSpec((tm, n), lambda i: (i, 0))
    return pl.pallas_call(
        _nsa_combine_body, name="nsa_combine", grid=(t // tm,),
        in_specs=[tok, tok, tok, pl.BlockSpec((tm, LANES), lambda i: (i, 0)),
                  pl.BlockSpec((1, LANES), lambda i: (0, 0)),
                  pl.BlockSpec((3, LANES, n), lambda i: (0, 0, 0))],
        out_specs=tok,
        out_shape=jax.ShapeDtypeStruct((t, n), BF16),
        compiler_params=_cparams(("parallel",)),
    )(oc, osel, ow, gp, bp, expand)


def _to_heads(x, b, s, n):
    return x.reshape(b, s, n, -1).transpose(0, 2, 1, 3).reshape(b * n, s, -1)


def _from_heads(o, b, n):
    _, s, d = o.shape
    return o.reshape(b, n, s, d).transpose(0, 2, 1, 3).reshape(b * s, n * d)


def _even_mixer(h, gain, layer_idx, rope16, w_in, w_out, lq1, lk1, lq2, lk2, subln, b, s):
    na = MOBA_HEADS * HEAD_DIM
    nb = DIFF_HEADS * 2 * HEAD_DIM
    w_rope = jnp.concatenate([w_in[:, 0:2 * na], w_in[:, 3 * na:3 * na + 2 * nb]], axis=1)
    w_rest = jnp.concatenate([w_in[:, 2 * na:3 * na], w_in[:, 3 * na + 2 * nb:]], axis=1)
    w = jnp.concatenate([w_rope, w_rest], axis=1).astype(BF16)
    y = _proj(h, w, name="proj_even_in", gain=gain, rope=rope16, rope_cols=w_rope.shape[1], shift=ROPE_DIM // 2, out_dtype=BF16)
    y3 = y.reshape(b, s, -1)
    blk = lambda off: off // LANES
    pairs = na // LANES
    qa, ka = _moba_gate(y3, q_blk=blk(0), k_blk=blk(na), pairs=pairs)
    o_a = _pflash(qa, ka, y3, name="flash_moba", mode="moba", dk=2 * LANES, batch=b, pairs=pairs,
                  q_map=lambda i, j: (i, j, 0), k_map=lambda i, j: (i, 0, 0),
                  v_map=lambda i, j: (i // pairs, 0, blk(2 * na + 2 * nb) + i % pairs))
    lambda_init = 0.8 - 0.6 * math.exp(-0.3 * layer_idx)
    nh = DIFF_HEADS
    o_b = _pflash(y3, y3, y3, name="flash_diff", mode="diff", dk=LANES, batch=b, pairs=nh, scale=HEAD_DIM ** -0.5,
                  q_map=lambda i, j: (i // nh, j, blk(2 * na) + i % nh),
                  k_map=lambda i, j: (i // nh, 0, blk(2 * na + nb) + i % nh),
                  v_map=lambda i, j: (i // nh, 0, blk(3 * na + 2 * nb) + i % nh),
                  diff_params=(lq1, lk1, lq2, lk2, subln), lambda_init=lambda_init)
    t = b * s
    return _proj(o_a.reshape(t, -1), w_out.astype(BF16), name="proj_out", x2=o_b.reshape(t, -1), res=h, out_dtype=F32)


def _odd_mixer(h, gain, positions, rope16, rope32, w_in, w_out, gate_b, pe_k, pe_v, k_w1, k_w2, v_w1, v_w2,
               q_norm, w_uq, kv_norm, w_ukv, b, s):
    G, R, d = NSA_GROUPS, NSA_REP, HEAD_DIM
    sizes = [NSA_HEADS * d] + [G * d] * 6 + [NSA_HEADS * 3, MLA_Q_RANK, MLA_KV_RANK, MLA_ROPE]
    offs = [0]
    for z in sizes:
        offs.append(offs[-1] + z)
    col = lambda i: w_in[:, offs[i]:offs[i + 1]]
    w_r = jnp.concatenate([col(0), col(3), col(5)], axis=1).astype(BF16)
    yr = _proj(h, w_r, name="proj_odd_rope", gain=gain, rope=rope16, rope_cols=w_r.shape[1], shift=ROPE_DIM // 2, out_dtype=BF16)
    q_c, k_slc, k_win = jnp.split(yr, [NSA_HEADS * d, NSA_HEADS * d + G * d], axis=1)
    w_p = jnp.concatenate([col(1), col(2), col(4), col(6), col(8), col(9), col(7)], axis=1)
    pad = (-w_p.shape[1]) % LANES
    w_p = jnp.pad(w_p, ((0, 0), (0, pad))).astype(BF16)
    yp = _proj(h, w_p, name="proj_odd_plain", gain=gain, out_dtype=F32)
    k_cmp, v_cmp, v_slc, v_win, c_q, c_kv, g_raw = jnp.split(
        yp[:, :w_p.shape[1] - pad],
        [G * d, 2 * G * d, 3 * G * d, 4 * G * d, 4 * G * d + MLA_Q_RANK, 4 * G * d + MLA_Q_RANK + MLA_KV_RANK], axis=1)
    w_pe = jnp.pad(col(10), ((0, 0), (0, LANES - MLA_ROPE))).astype(BF16)
    k_pe = _proj(h, w_pe, name="proj_mla_kpe", gain=gain, rope=rope32, rope_cols=LANES, shift=MLA_ROPE // 2, out_dtype=BF16)[:, :MLA_ROPE]

    scale = d ** -0.5
    qh = (q_c * scale).reshape(b, s, G, R, d).transpose(0, 2, 3, 1, 4).reshape(b * G, R, s, d)
    nc = s // NSA_CMP_STRIDE
    cpos = jnp.concatenate([positions[:, NSA_CMP_LEN - 1::NSA_CMP_STRIDE], positions[:, -1:]], axis=1)
    ctab = [t.reshape(b, nc, LANES) for t in _rope_tables(cpos, ROPE_DIM, HEAD_DIM)]
    xk = _to_heads(k_cmp, b, s, G).reshape(b * G, nc, NSA_CMP_STRIDE * d)
    xv = _to_heads(v_cmp, b, s, G).reshape(b * G, nc, NSA_CMP_STRIDE * d)
    kc = _nsa_compress(xk, pe_k, k_w1.astype(BF16), k_w2.astype(BF16), ctab, rope=True, groups=G)
    vc = _nsa_compress(xv, pe_v, v_w1.astype(BF16), v_w2.astype(BF16), ctab, rope=False, groups=G)
    o_c, mbias = _nsa_cmp(qh, kc, vc.transpose(0, 2, 1))
    ns_onehot = jax.nn.one_hot(jnp.arange(s) // NSA_SEL_BLOCK, LANES, dtype=BF16)
    zq = jnp.zeros((b * G, R, s, LANES - d), BF16)
    q_aug = jnp.concatenate([jnp.broadcast_to(mbias[:, None], (b * G, R, s, LANES)), qh, zq], axis=-1)
    ks = _to_heads(k_slc, b, s, G)
    k_aug = jnp.concatenate([jnp.broadcast_to(ns_onehot[None], (b * G, s, LANES)), ks,
                             jnp.zeros((b * G, s, LANES - d), BF16)], axis=-1)
    o_s = _flash(q_aug, k_aug, _to_heads(v_slc.astype(BF16), b, s, G), name="flash_sel", tq=128,
                 merged_groups=G, out_dtype=F32)
    o_w = _flash(qh, _to_heads(k_win, b, s, G), _to_heads(v_win.astype(BF16), b, s, G), name="flash_win", tq=128,
                 window=NSA_WINDOW, merged_groups=G, out_dtype=F32)
    t = b * s
    o_nsa = _nsa_combine(o_c.reshape(t, -1), o_s.reshape(t, -1), o_w.reshape(t, -1), g_raw, gate_b)

    hq = MLA_NOPE + MLA_ROPE
    wq = w_uq.reshape(MLA_Q_RANK, MLA_HEADS, hq)
    wq = jnp.concatenate([wq[:, :, MLA_NOPE:].reshape(MLA_Q_RANK, -1), wq[:, :, :MLA_NOPE].reshape(MLA_Q_RANK, -1)],
                         axis=1).astype(BF16)
    q = _proj(c_q, wq, name="proj_mla_q", gain=q_norm, rope=rope32, rope_cols=MLA_HEADS * MLA_ROPE, shift=MLA_ROPE // 2,
              out_dtype=BF16, chunk=256)
    q_pe = q[:, :MLA_HEADS * MLA_ROPE].reshape(b, s, MLA_HEADS, MLA_ROPE)
    q_nope = q[:, MLA_HEADS * MLA_ROPE:].reshape(b, s, MLA_HEADS, MLA_NOPE)
    kv = _proj(c_kv, w_ukv.astype(BF16), name="proj_mla_kv", gain=kv_norm, out_dtype=BF16).reshape(b, s, MLA_HEADS, MLA_NOPE + MLA_V)
    zpad = jnp.zeros((b, s, MLA_HEADS, LANES - hq), BF16)
    qf = jnp.concatenate([q_nope, q_pe, zpad], axis=-1).transpose(0, 2, 1, 3).reshape(b * MLA_HEADS, 1, s, LANES)
    kpe = jnp.broadcast_to(k_pe.reshape(b, s, 1, MLA_ROPE), (b, s, MLA_HEADS, MLA_ROPE))
    kf = jnp.concatenate([kv[..., :MLA_NOPE], kpe, zpad], axis=-1).transpose(0, 2, 1, 3).reshape(b * MLA_HEADS, s, LANES)
    vf = kv[..., MLA_NOPE:].transpose(0, 2, 1, 3).reshape(b * MLA_HEADS, s, MLA_V)
    o_d = _flash(qf, kf, vf, name="flash_mla", tq=512, scale=hq ** -0.5, out_dtype=BF16)
    o_d = _from_heads(o_d[:, 0], b, MLA_HEADS)

    o = jnp.concatenate([o_nsa, o_d], axis=1)
    return _proj(o, w_out.astype(BF16), name="proj_out", res=h, out_dtype=F32)


def kernel(x, positions, attn_norm, ffn_norm, final_norm, ffn_w_gate, ffn_w_up, ffn_w_down, ev_w_in, ev_w_out, diff_lambda_q1, diff_lambda_k1, diff_lambda_q2, diff_lambda_k2, diff_subln, od_w_in, od_w_out, nsa_gate_b, nsa_pe_k, nsa_pe_v, nsa_k_w1, nsa_k_w2, nsa_v_w1, nsa_v_w2, mla_q_norm, mla_w_uq, mla_kv_norm, mla_w_ukv):
    b, s, d = x.shape
    depth = attn_norm.shape[0]
    rope16 = _rope_tables(positions, ROPE_DIM, HEAD_DIM)
    rope32 = _rope_tables(positions, MLA_ROPE, MLA_ROPE)
    h = x.reshape(b * s, d)
    for l in range(depth):
        i = l // 2
        if l % 2 == 0:
            h = _even_mixer(h, attn_norm[l], l, rope16, ev_w_in[i], ev_w_out[i], diff_lambda_q1[i],
                            diff_lambda_k1[i], diff_lambda_q2[i], diff_lambda_k2[i], diff_subln[i], b, s)
        else:
            h = _odd_mixer(h, attn_norm[l], positions, rope16, rope32, od_w_in[i], od_w_out[i], nsa_gate_b[i],
                           nsa_pe_k[i], nsa_pe_v[i], nsa_k_w1[i], nsa_k_w2[i], nsa_v_w1[i], nsa_v_w2[i],
                           mla_q_norm[i], mla_w_uq[i], mla_kv_norm[i], mla_w_ukv[i], b, s)
        h = _ffn(h, ffn_norm[l], ffn_w_gate[l].astype(BF16), ffn_w_up[l].astype(BF16), ffn_w_down[l].astype(BF16),
                 final_norm, final_norm=(l == depth - 1))
    return h.reshape(b, s, d)
```

```python
import functools
import math

import jax
import jax.numpy as jnp
from jax import lax
from jax.experimental import pallas as pl
from jax.experimental.pallas import tpu as pltpu

F32 = jnp.float32
BF16 = jnp.bfloat16

D_MODEL = 1024
HEAD_DIM = 64
ROPE_THETA = 500000.0
ROPE_DIM = HEAD_DIM // 4
NORM_EPS = 1e-5
D_FF = 2816

MOBA_HEADS = 8
MOBA_BLOCK = 256
MOBA_TOPK = 3
DIFF_HEADS = 4
DIFF_V = 2 * HEAD_DIM
NSA_HEADS = 8
NSA_GROUPS = 2
NSA_REP = NSA_HEADS // NSA_GROUPS
NSA_CMP_LEN = 32
NSA_CMP_STRIDE = 16
NSA_CMP_HIDDEN = 256
NSA_SEL_BLOCK = 64
NSA_SEL_TOPK = 16
NSA_WINDOW = 512
NSA_FORCE_BONUS = 1e3
MLA_HEADS = 8
MLA_Q_RANK = 256
MLA_KV_RANK = 128
MLA_NOPE = 64
MLA_ROPE = 32
MLA_V = 64

LANES = 128
MOBA_BIAS_LANES = 32
NEG = -1e30
M_INIT = -1e37
LOG2E = 1.4426950408889634
UNROLL = 4
VMEM_LIMIT = 56 * 1024 * 1024

_NT = (((1,), (1,)), ((), ()))


def _cparams(sem):
    return pltpu.CompilerParams(dimension_semantics=sem, vmem_limit_bytes=VMEM_LIMIT)


def _rope_tables(positions, dim, period, offset=0):
    r = dim // 2
    inv = 1.0 / (ROPE_THETA ** (jnp.arange(0, dim, 2, dtype=F32) / dim))
    ang = positions.astype(F32)[..., None] * inv
    cos, sin = jnp.cos(ang), jnp.sin(ang)
    const = lambda n, val: jnp.full(ang.shape[:-1] + (n,), val, F32)
    rest = period - offset - 2 * r
    zr = jnp.zeros_like(sin)
    c = jnp.concatenate([const(offset, 1.0), cos, cos, const(rest, 1.0)], -1)
    s1 = jnp.concatenate([const(offset, 0.0), -sin, zr, const(rest, 0.0)], -1)
    s2 = jnp.concatenate([const(offset, 0.0), zr, sin, const(rest, 0.0)], -1)
    reps = LANES // period
    tile = lambda t: jnp.tile(t, (1,) * (t.ndim - 1) + (reps,)).reshape(-1, LANES)
    return tile(c), tile(s1), tile(s2)


def _apply_rope(y, c, s1, s2, shift):
    return y * c + pltpu.roll(y, LANES - shift, 1) * s1 + pltpu.roll(y, shift, 1) * s2


def _proj_body(*refs, has_norm, has_x2, has_rope, has_res, add_cols, rope_cols, shift, chunk):
    it = iter(refs)
    x_ref = next(it)
    x2_ref = next(it) if has_x2 else None
    g_ref = next(it) if has_norm else None
    w_ref = next(it)
    if has_rope:
        c_ref, s1_ref, s2_ref = next(it), next(it), next(it)
    res_ref = next(it) if has_res else None
    add_ref = next(it) if add_cols else None
    o_ref = next(it)
    n = w_ref.shape[1]
    if has_norm:
        xf = x_ref[...].astype(F32)
        y = xf * lax.rsqrt(jnp.mean(xf * xf, axis=-1, keepdims=True) + NORM_EPS)
        xb = (y * g_ref[...]).astype(BF16)
    else:
        xb = x_ref[...].astype(BF16)
    k1 = xb.shape[1]
    for c0 in range(0, n, chunk):
        cw = min(chunk, n - c0)
        y = jnp.dot(xb, w_ref[0:k1, c0:c0 + cw], preferred_element_type=F32)
        if has_x2:
            y = y + jnp.dot(x2_ref[...].astype(BF16), w_ref[k1:, c0:c0 + cw], preferred_element_type=F32)
        if has_res:
            y = y + res_ref[:, c0:c0 + cw]
        if (has_rope and c0 < rope_cols) or c0 < add_cols:
            for k0 in range(0, cw, LANES):
                ys = y[:, k0:k0 + LANES]
                if has_rope and c0 + k0 < rope_cols:
                    ys = _apply_rope(ys, c_ref[...], s1_ref[...], s2_ref[...], shift)
                if c0 + k0 < add_cols:
                    ys = ys + add_ref[...].astype(F32)
                o_ref[:, c0 + k0:c0 + k0 + LANES] = ys.astype(o_ref.dtype)
        else:
            o_ref[:, c0:c0 + cw] = y.astype(o_ref.dtype)


def _proj(x, w, *, name, x_blk=None, x2=None, gain=None, rope=None, rope_cols=0, shift=0, res=None, add=None,
          add_cols=0, out_dtype=F32, tm=512, chunk=512):
    t = x.shape[0]
    k, xj = (x.shape[1], 0) if x_blk is None else x_blk
    n = w.shape[1]
    assert t % tm == 0 and n % LANES == 0 and rope_cols % LANES == 0 and add_cols % LANES == 0
    has_norm, has_rope, has_res, has_x2 = gain is not None, rope is not None, res is not None, x2 is not None
    assert not (has_norm and has_x2)
    args, specs = [x], [pl.BlockSpec((tm, k), lambda i: (i, xj))]
    if has_x2:
        args.append(x2)
        specs.append(pl.BlockSpec((tm, x2.shape[1]), lambda i: (i, 0)))
    if has_norm:
        args.append(gain.reshape(1, k).astype(F32))
        specs.append(pl.BlockSpec((1, k), lambda i: (0, 0)))
    args.append(w)
    specs.append(pl.BlockSpec((w.shape[0], n), lambda i: (0, 0)))
    if has_rope:
        for tb in rope:
            args.append(tb)
            specs.append(pl.BlockSpec((tm, LANES), lambda i: (i, 0)))
    if has_res:
        args.append(res)
        specs.append(pl.BlockSpec((tm, n), lambda i: (i, 0)))
    if add_cols:
        args.append(add)
        specs.append(pl.BlockSpec((tm, LANES), lambda i: (i, 0)))
    body = functools.partial(_proj_body, has_norm=has_norm, has_x2=has_x2, has_rope=has_rope, has_res=has_res,
                             add_cols=add_cols,
                             rope_cols=rope_cols, shift=shift, chunk=chunk)
    return pl.pallas_call(
        body, name=name, grid=(t // tm,), in_specs=specs,
        out_specs=pl.BlockSpec((tm, n), lambda i: (i, 0)),
        out_shape=jax.ShapeDtypeStruct((t, n), out_dtype),
        compiler_params=_cparams(("parallel",)),
    )(*args)


def _ffn_body(x_ref, g_ref, wg_ref, wu_ref, wd_ref, fg_ref, o_ref, xn_ref, acc_ref, *, final_norm):
    j = pl.program_id(1)

    @pl.when(j == 0)
    def _():
        xf = x_ref[...]
        y = xf * lax.rsqrt(jnp.mean(xf * xf, axis=-1, keepdims=True) + NORM_EPS)
        xn_ref[...] = (y * g_ref[...]).astype(BF16)
        acc_ref[...] = jnp.zeros_like(acc_ref)

    xn = xn_ref[...]
    g = jnp.dot(xn, wg_ref[...], preferred_element_type=F32)
    u = jnp.dot(xn, wu_ref[...], preferred_element_type=F32)
    a = (jax.nn.silu(g) * u).astype(BF16)
    acc_ref[...] += jnp.dot(a, wd_ref[...], preferred_element_type=F32)

    @pl.when(j == pl.num_programs(1) - 1)
    def _():
        h = x_ref[...] + acc_ref[...]
        if final_norm:
            y = h * lax.rsqrt(jnp.mean(h * h, axis=-1, keepdims=True) + NORM_EPS)
            h = y * fg_ref[...]
        o_ref[...] = h


def _ffn(x, gain, wg, wu, wd, final_gain, *, final_norm, tm=512, tf=1408):
    t, d = x.shape
    f = wg.shape[1]
    assert t % tm == 0 and f % tf == 0
    return pl.pallas_call(
        functools.partial(_ffn_body, final_norm=final_norm), name="ffn",
        grid=(t // tm, f // tf),
        in_specs=[
            pl.BlockSpec((tm, d), lambda i, j: (i, 0)),
            pl.BlockSpec((1, d), lambda i, j: (0, 0)),
            pl.BlockSpec((d, tf), lambda i, j: (0, j)),
            pl.BlockSpec((d, tf), lambda i, j: (0, j)),
            pl.BlockSpec((tf, d), lambda i, j: (j, 0)),
            pl.BlockSpec((1, d), lambda i, j: (0, 0)),
        ],
        out_specs=pl.BlockSpec((tm, d), lambda i, j: (i, 0)),
        out_shape=jax.ShapeDtypeStruct((t, d), F32),
        scratch_shapes=[pltpu.VMEM((tm, d), BF16), pltpu.VMEM((tm, d), F32)],
        compiler_params=_cparams(("parallel", "arbitrary")),
    )(x, gain.reshape(1, d).astype(F32), wg, wu, wd, final_gain.reshape(1, d).astype(F32))


def _sweep(q, k_get, vt_get, s_ref, *, q0, tq, tk, dv, scale, window):
    cols = q.shape[0]

    def fold8(x, op):
        return op(x.reshape(tk // 8, 8, cols), axis=0)

    def score_step(j, slot, mrun, masked):
        ks = pl.multiple_of(j * tk, tk)
        kb = k_get(ks)
        s = lax.dot_general(kb, q, _NT, preferred_element_type=F32) * (scale * LOG2E)
        if masked:
            kpos = ks + lax.broadcasted_iota(jnp.int32, (tk, cols), 0)
            qpos = q0 + lax.rem(lax.broadcasted_iota(jnp.int32, (tk, cols), 1), tq)
            ok = kpos <= qpos
            if window is not None:
                ok = jnp.logical_and(ok, kpos > qpos - window)
            s = jnp.where(ok, s, NEG)
        s_ref[slot] = s
        return jnp.maximum(mrun, fold8(s, jnp.max))

    def pv_step(j, slot, carry, m):
        l8, acc = carry
        p = jnp.exp2(s_ref[slot] - m)
        l8 = l8 + fold8(p, jnp.sum)
        acc = acc + jnp.dot(vt_get(j), p.astype(BF16), preferred_element_type=F32)
        return l8, acc

    m_init = jnp.full((8, cols), M_INIT, F32)
    acc_init = (jnp.zeros((8, cols), F32), jnp.zeros((dv, cols), F32))
    if window is None:
        n_full = q0 // tk
        groups = n_full // UNROLL

        def score_group(i, mrun):
            for u in range(UNROLL):
                mrun = score_step(UNROLL * i + u, UNROLL * i + u, mrun, False)
            return mrun

        mrun = lax.fori_loop(0, groups, score_group, m_init)
        mrun = lax.fori_loop(groups * UNROLL, n_full, lambda j, t: score_step(j, j, t, False), mrun)
        mrun = score_step(n_full, n_full, mrun, True)
        m = jnp.max(mrun, axis=0, keepdims=True)

        def pv_group(i, carry):
            for u in range(UNROLL):
                carry = pv_step(UNROLL * i + u, UNROLL * i + u, carry, m)
            return carry

        carry = lax.fori_loop(0, groups, pv_group, acc_init)
        carry = lax.fori_loop(groups * UNROLL, n_full, lambda j, t: pv_step(j, j, t, m), carry)
        l8, acc = pv_step(n_full, n_full, carry, m)
    else:
        j_lo = jnp.maximum(q0 - window + 1, 0) // tk
        j_hi = (q0 + tq - 1) // tk
        mrun = lax.fori_loop(j_lo, j_hi + 1, lambda j, t: score_step(j, j - j_lo, t, True), m_init)
        m = jnp.max(mrun, axis=0, keepdims=True)
        l8, acc = lax.fori_loop(j_lo, j_hi + 1, lambda j, t: pv_step(j, j - j_lo, t, m), acc_init)
    l = jnp.sum(l8, axis=0, keepdims=True)
    return acc / jnp.maximum(l, 1e-30)


def _flash_body(q_ref, k_ref, vt_ref, o_ref, s_ref, *, rep, tq, tk, scale, window, merged):
    qi = pl.program_id(1)
    dk = q_ref.shape[-1]
    dv = vt_ref.shape[2]
    q = q_ref[0].reshape(rep * tq, dk)
    o = _sweep(q, lambda ks: k_ref[0, pl.ds(ks, tk), :], lambda j: vt_ref[0, j], s_ref,
               q0=qi * tq, tq=tq, tk=tk, dv=dv, scale=scale, window=window).T
    if merged:
        for r in range(rep):
            o_ref[0, :, r * dv:(r + 1) * dv] = o[r * tq:(r + 1) * tq].astype(o_ref.dtype)
    else:
        o_ref[0] = o.reshape(rep, tq, dv).astype(o_ref.dtype)


def _flash(q, k, v, *, name, tq, tk=512, scale=1.0, window=None, v_div=1, merged_groups=None, out_dtype=F32):
    g, rep, s, dk = q.shape
    gv, _, dv = v.shape
    assert s % tk == 0 and tk % tq == 0
    vt = v.reshape(gv, s // tk, tk, dv).transpose(0, 1, 3, 2)
    n_slots = s // tk if window is None else (window + tq - 2) // tk + 2
    in_specs = [
        pl.BlockSpec((1, rep, tq, dk), lambda i, j: (i, 0, j, 0)),
        pl.BlockSpec((1, s, dk), lambda i, j: (i, 0, 0)),
        pl.BlockSpec((1, s // tk, dv, tk), lambda i, j: (i // v_div, 0, 0, 0)),
    ]
    if merged_groups is None:
        out_spec = pl.BlockSpec((1, rep, tq, dv), lambda i, j: (i, 0, j, 0))
        out_shape = jax.ShapeDtypeStruct((g, rep, s, dv), out_dtype)
    else:
        n = merged_groups
        out_spec = pl.BlockSpec((1, tq, rep * dv), lambda i, j: (i // n, j, i % n))
        out_shape = jax.ShapeDtypeStruct((g // n, s, n * rep * dv), out_dtype)
    body = functools.partial(_flash_body, rep=rep, tq=tq, tk=tk, scale=scale, window=window,
                             merged=merged_groups is not None)
    return pl.pallas_call(
        body, name=name, grid=(g, s // tq), in_specs=in_specs, out_specs=out_spec, out_shape=out_shape,
        scratch_shapes=[pltpu.VMEM((n_slots, tk, rep * tq), F32)],
        compiler_params=_cparams(("parallel", "arbitrary")),
    )(q, k, vt)


def _head_lanes(mode, v, lane):
    d = HEAD_DIM
    if mode == "diff":
        return jnp.logical_and(lane >= d * v, lane < d * (v + 1))
    if mode == "moba":
        lo = 2 * d + v * MOBA_BIAS_LANES
        return jnp.logical_or(jnp.logical_and(lane >= d * v, lane < d * (v + 1)),
                              jnp.logical_and(lane >= lo, lane < lo + MOBA_BIAS_LANES))
    return jnp.logical_and(lane >= LANES * v, lane < LANES * (v + 1))


def _pflash_body(*refs, mode, tq, tk, scale, lambda_init):
    if mode == "diff":
        q_ref, k_ref, v_ref, lq1_ref, lk1_ref, lq2_ref, lk2_ref, sg_ref, o_ref, s_ref, vt_ref = refs
    else:
        q_ref, k_ref, v_ref, o_ref, s_ref, vt_ref = refs
    qi = pl.program_id(1)
    n_chunks = v_ref.shape[1] // tk

    @pl.when(qi == 0)
    def _():
        for c in range(n_chunks):
            vt_ref[c] = v_ref[0, c * tk:(c + 1) * tk, :].astype(F32).T.astype(BF16)

    q = q_ref[0]
    lane = lax.broadcasted_iota(jnp.int32, q.shape, 1)
    outs = []
    for v in range(2):
        qv = jnp.where(_head_lanes(mode, v, lane), q, jnp.zeros_like(q))
        outs.append(_sweep(qv, lambda ks: k_ref[0, pl.ds(ks, tk), :], lambda j: vt_ref[j], s_ref,
                           q0=qi * tq, tq=tq, tk=tk, dv=LANES, scale=scale, window=None))
    if mode == "diff":
        lam = (jnp.exp(jnp.sum(lq1_ref[...] * lk1_ref[...], axis=-1, keepdims=True))
               - jnp.exp(jnp.sum(lq2_ref[...] * lk2_ref[...], axis=-1, keepdims=True)) + lambda_init)
        d = outs[0] - lam * outs[1]
        y = d * lax.rsqrt(jnp.mean(d * d, axis=0, keepdims=True) + NORM_EPS)
        o_t = (y * sg_ref[...]) * (1.0 - lambda_init)
    else:
        row = lax.broadcasted_iota(jnp.int32, outs[0].shape, 0)
        o_t = jnp.where(row < HEAD_DIM, outs[0], outs[1])
    o_ref[0] = o_t.T.astype(o_ref.dtype)


def _pflash(q, k, v, *, name, mode, dk, q_map, k_map, v_map, batch, pairs, tq=512, tk=512, scale=1.0,
            diff_params=None, lambda_init=0.0):
    s = v.shape[1]
    assert s % tk == 0 and tq == tk
    args = [q, k, v]
    in_specs = [pl.BlockSpec((1, tq, dk), q_map), pl.BlockSpec((1, s, dk), k_map),
                pl.BlockSpec((1, s, LANES), v_map)]
    if mode == "diff":
        lq1, lk1, lq2, lk2, subln = diff_params
        for a in (lq1, lk1, lq2, lk2):
            args.append(a.reshape(1, HEAD_DIM).astype(F32))
            in_specs.append(pl.BlockSpec((1, HEAD_DIM), lambda i, j: (0, 0)))
        args.append(subln.reshape(LANES, 1).astype(F32))
        in_specs.append(pl.BlockSpec((LANES, 1), lambda i, j: (0, 0)))
    body = functools.partial(_pflash_body, mode=mode, tq=tq, tk=tk, scale=scale, lambda_init=lambda_init)
    return pl.pallas_call(
        body, name=name, grid=(batch * pairs, s // tq), in_specs=in_specs,
        out_specs=pl.BlockSpec((1, tq, LANES), lambda i, j: (i // pairs, j, i % pairs)),
        out_shape=jax.ShapeDtypeStruct((batch, s, pairs * LANES), BF16),
        scratch_shapes=[pltpu.VMEM((s // tk, tk, tq), F32), pltpu.VMEM((s // tk, LANES, tk), BF16)],
        compiler_params=_cparams(("parallel", "arbitrary")),
    )(*args)


def _gflash_body(*refs, tq, tk, rep, groups, window, select):
    if select:
        q_ref, k_ref, v_ref, mb_ref, o_ref, s_ref, vt_ref, ka_ref, qa_ref = refs
    else:
        q_ref, k_ref, v_ref, o_ref, s_ref, vt_ref, ka_ref, qa_ref = refs
    d = HEAD_DIM
    g = pl.program_id(0) % groups
    qi = pl.program_id(1)
    n_chunks = v_ref.shape[1] // tk

    @pl.when(qi == 0)
    def _():
        move = _place(LANES, LANES, g * d)
        for c in range(n_chunks):
            rows = slice(c * tk, (c + 1) * tk)
            vt_ref[c] = v_ref[0, rows, :].astype(F32).T.astype(BF16)
            kg = jnp.dot(k_ref[0, rows, :], move, preferred_element_type=F32).astype(BF16)
            if select:
                blk = (c * tk + lax.broadcasted_iota(jnp.int32, (tk, LANES), 0)) // NSA_SEL_BLOCK
                lane = lax.broadcasted_iota(jnp.int32, (tk, LANES), 1)
                ka_ref[rows, 0:LANES] = jnp.where(blk == lane, 1.0, 0.0).astype(BF16)
                ka_ref[rows, LANES:2 * LANES] = kg
            else:
                ka_ref[rows, :] = kg

    q = q_ref[0]
    for r in range(rep):
        qr = jnp.dot(q, _place(rep * d, LANES, r * d, d ** -0.5), preferred_element_type=F32).astype(BF16)
        rows = slice(r * tq, (r + 1) * tq)
        if select:
            qa_ref[rows, 0:LANES] = mb_ref[0]
            qa_ref[rows, LANES:2 * LANES] = qr
        else:
            qa_ref[rows, :] = qr
    o_t = _sweep(qa_ref[...], lambda ks: ka_ref[pl.ds(ks, tk), :], lambda j: vt_ref[j], s_ref,
                 q0=qi * tq, tq=tq, tk=tk, dv=LANES, scale=1.0, window=window)
    o_g = jnp.where(g == 0, o_t[0:d], o_t[d:2 * d])
    for r in range(rep):
        o_ref[0, :, r * d:(r + 1) * d] = o_g[:, r * tq:(r + 1) * tq].T.astype(o_ref.dtype)


def _gflash(yq, yk, yv, mbias, *, name, q_blk, k_blk, v_blk, tq=128, tk=512, window=None):
    b, s, _ = yq.shape
    rep, groups, d = NSA_REP, NSA_GROUPS, HEAD_DIM
    assert groups * d == LANES and s % tk == 0 and tk % tq == 0
    select = mbias is not None
    dk = 2 * LANES if select else LANES
    n_slots = s // tk if window is None else (window + tq - 2) // tk + 2
    args = [yq, yk, yv]
    in_specs = [pl.BlockSpec((1, tq, rep * d), lambda i, j: (i // groups, j, q_blk + i % groups)),
                pl.BlockSpec((1, s, LANES), lambda i, j: (i // groups, 0, k_blk)),
                pl.BlockSpec((1, s, LANES), lambda i, j: (i // groups, 0, v_blk))]
    if select:
        args.append(mbias)
        in_specs.append(pl.BlockSpec((1, tq, LANES), lambda i, j: (i, j, 0)))
    body = functools.partial(_gflash_body, tq=tq, tk=tk, rep=rep, groups=groups, window=window, select=select)
    return pl.pallas_call(
        body, name=name, grid=(b * groups, s // tq), in_specs=in_specs,
        out_specs=pl.BlockSpec((1, tq, rep * d), lambda i, j: (i // groups, j, i % groups)),
        out_shape=jax.ShapeDtypeStruct((b, s, groups * rep * d), F32),
        scratch_shapes=[pltpu.VMEM((n_slots, tk, rep * tq), F32), pltpu.VMEM((s // tk, LANES, tk), BF16),
                        pltpu.VMEM((s, dk), BF16), pltpu.VMEM((rep * tq, dk), BF16)],
        compiler_params=_cparams(("parallel", "arbitrary")),
    )(*args)


def _topk_mask(score, row, k):
    sel = None
    for _ in range(k):
        mx = jnp.max(score, axis=0, keepdims=True)
        idx = jnp.min(jnp.where(score == mx, row, LANES), axis=0, keepdims=True)
        hit = row == idx
        pick = jnp.logical_and(hit, mx > -jnp.inf)
        sel = pick if sel is None else jnp.logical_or(sel, pick)
        score = jnp.where(hit, -jnp.inf, score)
    return sel


def _moba_gate_body(q_ref, k_ref, qa_ref, ka_ref, kmean_ref, *, tq, seq):
    qi = pl.program_id(1)
    nbl = MOBA_BIAS_LANES

    @pl.when(qi == 0)
    def _():
        rowblk = lax.rem(lax.broadcasted_iota(jnp.int32, (LANES, seq), 0), nbl)
        colblk = lax.broadcasted_iota(jnp.int32, (LANES, seq), 1) // MOBA_BLOCK
        ind = jnp.where(rowblk == colblk, 1.0, 0.0).astype(BF16)
        ksum = jnp.dot(ind, k_ref[0], preferred_element_type=F32)
        rhead = lax.broadcasted_iota(jnp.int32, (LANES, LANES), 0) // nbl
        lhead = lax.broadcasted_iota(jnp.int32, (LANES, LANES), 1) // HEAD_DIM
        kmean_ref[...] = jnp.where(rhead == lhead, ksum * (1.0 / MOBA_BLOCK), 0.0)

    q = q_ref[0]
    q0 = pl.multiple_of(qi * tq, tq)
    km = kmean_ref[...]
    km_hi = km.astype(BF16)
    km_lo = (km - km_hi.astype(F32)).astype(BF16)
    gate = (lax.dot_general(km_hi, q, _NT, preferred_element_type=F32)
            + lax.dot_general(km_lo, q, _NT, preferred_element_type=F32))
    blk_t = lax.broadcasted_iota(jnp.int32, (nbl, tq), 0)
    own_t = (q0 + lax.broadcasted_iota(jnp.int32, (nbl, tq), 1)) // MOBA_BLOCK
    parts = []
    for v in range(2):
        g_v = jnp.where(blk_t < own_t, gate[v * nbl:(v + 1) * nbl], -jnp.inf)
        sel = jnp.logical_or(_topk_mask(g_v, blk_t, MOBA_TOPK), blk_t == own_t)
        parts.append(jnp.where(sel, 0.0, NEG))
    parts.append(jnp.zeros((LANES - 2 * nbl, tq), F32))
    bias = jnp.concatenate(parts, axis=0).T
    qa_ref[0, :, 0:LANES] = q * (HEAD_DIM ** -0.5)
    qa_ref[0, :, LANES:2 * LANES] = bias.astype(BF16)
    lane = lax.broadcasted_iota(jnp.int32, (tq, LANES), 1)
    own = (q0 + lax.broadcasted_iota(jnp.int32, (tq, LANES), 0)) // MOBA_BLOCK
    onehot = jnp.where(jnp.logical_and(lane < 2 * nbl, lax.rem(lane, nbl) == own), 1.0, 0.0)
    ka_ref[0, :, 0:LANES] = k_ref[0, pl.ds(q0, tq), :]
    ka_ref[0, :, LANES:2 * LANES] = onehot.astype(BF16)


def _moba_gate(y3, *, q_blk, k_blk, pairs, tq=512):
    b, s, _ = y3.shape
    assert s % MOBA_BLOCK == 0 and s // MOBA_BLOCK <= MOBA_BIAS_LANES and s % tq == 0
    out = jax.ShapeDtypeStruct((b * pairs, s, 2 * LANES), BF16)
    return pl.pallas_call(
        functools.partial(_moba_gate_body, tq=tq, seq=s), name="moba_gate",
        grid=(b * pairs, s // tq),
        in_specs=[pl.BlockSpec((1, tq, LANES), lambda i, j: (i // pairs, j, q_blk + i % pairs)),
                  pl.BlockSpec((1, s, LANES), lambda i, j: (i // pairs, 0, k_blk + i % pairs))],
        out_specs=[pl.BlockSpec((1, tq, 2 * LANES), lambda i, j: (i, j, 0)),
                   pl.BlockSpec((1, tq, 2 * LANES), lambda i, j: (i, j, 0))],
        out_shape=[out, out],
        scratch_shapes=[pltpu.VMEM((LANES, LANES), F32)],
        compiler_params=_cparams(("parallel", "arbitrary")),
    )(y3, y3)


def _diff_combine_body(o1_ref, o2_ref, lq1_ref, lk1_ref, lq2_ref, lk2_ref, sg_ref, o_ref, *, lambda_init):
    lam = (jnp.exp(jnp.sum(lq1_ref[...] * lk1_ref[...], axis=-1, keepdims=True))
           - jnp.exp(jnp.sum(lq2_ref[...] * lk2_ref[...], axis=-1, keepdims=True)) + lambda_init)
    d = o1_ref[0, 0] - lam * o2_ref[0, 0]
    y = d * lax.rsqrt(jnp.mean(d * d, axis=-1, keepdims=True) + NORM_EPS)
    o_ref[0] = ((y * sg_ref[...]) * (1.0 - lambda_init)).astype(o_ref.dtype)


def _diff_combine(o, lq1, lk1, lq2, lk2, subln, lambda_init, *, batch, tq=512):
    g, _, s, dv = o.shape
    vec = lambda a: a.reshape(1, -1).astype(F32)
    vspec = lambda n: pl.BlockSpec((1, n), lambda i, j: (0, 0))
    return pl.pallas_call(
        functools.partial(_diff_combine_body, lambda_init=lambda_init), name="diff_combine",
        grid=(g, s // tq),
        in_specs=[pl.BlockSpec((1, 1, tq, dv), lambda i, j: (i, 0, j, 0)),
                  pl.BlockSpec((1, 1, tq, dv), lambda i, j: (i, 1, j, 0)),
                  vspec(HEAD_DIM), vspec(HEAD_DIM), vspec(HEAD_DIM), vspec(HEAD_DIM), vspec(dv)],
        out_specs=pl.BlockSpec((1, tq, dv), lambda i, j: (i // DIFF_HEADS, j, i % DIFF_HEADS)),
        out_shape=jax.ShapeDtypeStruct((batch, s, DIFF_HEADS * dv), BF16),
        compiler_params=_cparams(("parallel", "parallel")),
    )(o, o, vec(lq1), vec(lk1), vec(lq2), vec(lk2), vec(subln))


def _nsa_compress_body(x_ref, pe_ref, w1_ref, w2_ref, c_ref, s1_ref, s2_ref, o_ref, *, rope):
    half = w1_ref.shape[0] // 2
    x = x_ref[0].astype(F32)
    lo = (x + pe_ref[:, 0:half]).astype(BF16)
    hi = (x + pe_ref[:, half:2 * half]).astype(BF16)
    a = jnp.dot(lo, w1_ref[0:half, :], preferred_element_type=F32)
    b = jnp.dot(hi, w1_ref[half:2 * half, :], preferred_element_type=F32)
    n = a.shape[0]
    h1 = a + pltpu.roll(b, n - 1, 0)
    y = jnp.dot(jax.nn.gelu(h1).astype(BF16), w2_ref[...], preferred_element_type=F32)
    if rope:
        y = _apply_rope(y, c_ref[0], s1_ref[0], s2_ref[0], ROPE_DIM // 2)
    o_ref[0] = y[:, 0:HEAD_DIM].astype(o_ref.dtype)


def _nsa_compress(xr, pe, w1, w2, tables, *, rope, groups):
    g, n, w = xr.shape
    hid = w1.shape[1]
    w2p = jnp.pad(w2, ((0, 0), (0, LANES - w2.shape[1])))
    tspec = pl.BlockSpec((1, n, LANES), lambda i: (i // groups, 0, 0))
    return pl.pallas_call(
        functools.partial(_nsa_compress_body, rope=rope), name="nsa_compress",
        grid=(g,),
        in_specs=[pl.BlockSpec((1, n, w), lambda i: (i, 0, 0)),
                  pl.BlockSpec((1, 2 * w), lambda i: (0, 0)),
                  pl.BlockSpec((2 * w, hid), lambda i: (0, 0)),
                  pl.BlockSpec((hid, LANES), lambda i: (0, 0)),
                  tspec, tspec, tspec],
        out_specs=pl.BlockSpec((1, n, HEAD_DIM), lambda i: (i, 0, 0)),
        out_shape=jax.ShapeDtypeStruct((g, n, HEAD_DIM), BF16),
        compiler_params=_cparams(("parallel",)),
    )(xr, pe.reshape(1, 2 * w).astype(F32), w1, w2p, *tables)


def _place(n_src, n_dst, shift, value=1.0):
    src = lax.broadcasted_iota(jnp.int32, (n_src, n_dst), 0)
    dst = lax.broadcasted_iota(jnp.int32, (n_src, n_dst), 1)
    return jnp.where(jnp.logical_and(src == dst + shift, dst < HEAD_DIM), value, 0.0).astype(BF16)


def _nsa_cmp_body(q_ref, kc_ref, vc_ref, oc_ref, mb_ref, kcp_ref, *, tq, rep):
    qi = pl.program_id(1)
    q0 = qi * tq
    nc = kc_ref.shape[1]
    width = q_ref.shape[-1]

    @pl.when(qi == 0)
    def _():
        for r in range(rep):
            src = lax.broadcasted_iota(jnp.int32, (HEAD_DIM, width), 0)
            dst = lax.broadcasted_iota(jnp.int32, (HEAD_DIM, width), 1)
            spread = jnp.where(dst == src + r * HEAD_DIM, HEAD_DIM ** -0.5, 0.0).astype(BF16)
            kcp_ref[r] = jnp.dot(kc_ref[0], spread, preferred_element_type=F32).astype(BF16)

    q = q_ref[0]
    vct = vc_ref[0]
    tpos = q0 + lax.broadcasted_iota(jnp.int32, (nc, tq), 1)
    cend = lax.broadcasted_iota(jnp.int32, (nc, tq), 0) * NSA_CMP_STRIDE + (NSA_CMP_LEN - 1)
    ok = cend <= tpos
    psum = jnp.zeros((nc, tq), F32)
    for r in range(rep):
        s = lax.dot_general(kcp_ref[r], q, _NT, preferred_element_type=F32)
        s = jnp.where(ok, s, NEG)
        m = jnp.max(s, axis=0, keepdims=True)
        e = jnp.where(ok, jnp.exp(s - m), 0.0)
        p = e * (1.0 / jnp.maximum(jnp.sum(e, axis=0, keepdims=True), 1e-30))
        o_t = jnp.dot(vct, p.astype(BF16), preferred_element_type=F32)
        oc_ref[0, :, r * HEAD_DIM:(r + 1) * HEAD_DIM] = o_t.T.astype(oc_ref.dtype)
        psum = psum + p
    sstart = lax.broadcasted_iota(jnp.int32, (LANES, nc), 0) * NSA_SEL_BLOCK
    cstart = lax.broadcasted_iota(jnp.int32, (LANES, nc), 1) * NSA_CMP_STRIDE
    ov = jnp.where(jnp.logical_and(cstart < sstart + NSA_SEL_BLOCK, cstart + NSA_CMP_LEN > sstart),
                   1.0, 0.0).astype(BF16)
    p_hi = psum.astype(BF16)
    r1 = psum - p_hi.astype(F32)
    p_mid = r1.astype(BF16)
    p_lo = (r1 - p_mid.astype(F32)).astype(BF16)
    imp = (jnp.dot(ov, p_hi, preferred_element_type=F32) + jnp.dot(ov, p_mid, preferred_element_type=F32)
           + jnp.dot(ov, p_lo, preferred_element_type=F32))
    blk = lax.broadcasted_iota(jnp.int32, (LANES, tq), 0)
    qb = (q0 + lax.broadcasted_iota(jnp.int32, (LANES, tq), 1)) // NSA_SEL_BLOCK
    forced = jnp.logical_or(blk == 0, jnp.logical_or(blk == qb, blk == qb - 1))
    imp = jnp.where(forced, imp + NSA_FORCE_BONUS, imp)
    imp = jnp.where(blk <= qb, imp, -jnp.inf)
    sel = _topk_mask(imp, blk, NSA_SEL_TOPK)
    mb_ref[0] = jnp.where(sel, 0.0, NEG).T.astype(mb_ref.dtype)


def _nsa_cmp(y3, kc, vc, *, tq=512):
    b, s, _ = y3.shape
    g, nc, d = kc.shape
    rep, n = NSA_REP, NSA_GROUPS
    assert s // NSA_SEL_BLOCK <= LANES and s % tq == 0
    return pl.pallas_call(
        functools.partial(_nsa_cmp_body, tq=tq, rep=rep), name="nsa_cmp",
        grid=(g, s // tq),
        in_specs=[pl.BlockSpec((1, tq, rep * d), lambda i, j: (i // n, j, i % n)),
                  pl.BlockSpec((1, nc, d), lambda i, j: (i, 0, 0)),
                  pl.BlockSpec((1, d, nc), lambda i, j: (i, 0, 0))],
        out_specs=[pl.BlockSpec((1, tq, rep * d), lambda i, j: (i // n, j, i % n)),
                   pl.BlockSpec((1, tq, LANES), lambda i, j: (i, j, 0))],
        out_shape=[jax.ShapeDtypeStruct((b, s, n * rep * d), F32),
                   jax.ShapeDtypeStruct((g, s, LANES), BF16)],
        scratch_shapes=[pltpu.VMEM((rep, nc, rep * d), BF16)],
        compiler_params=_cparams(("parallel", "arbitrary")),
    )(y3, kc, vc)


def _nsa_combine_body(oc_ref, os_ref, ow_ref, g_ref, b_ref, e_ref, o_ref):
    gs = jax.nn.sigmoid(g_ref[...] + b_ref[...])
    g_hi = gs.astype(BF16)
    g_lo = (gs - g_hi.astype(F32)).astype(BF16)
    out = None
    for i, ref in enumerate((oc_ref, os_ref, ow_ref)):
        w = (jnp.dot(g_hi, e_ref[i], preferred_element_type=F32)
             + jnp.dot(g_lo, e_ref[i], preferred_element_type=F32))
        term = w * ref[...]
        out = term if out is None else out + term
    o_ref[...] = out.astype(o_ref.dtype)


def _nsa_combine(oc, osel, ow, yg, g_blk, gate_b, *, tm=512):
    t, n = oc.shape
    nh = NSA_HEADS * 3
    bp = jnp.pad(gate_b.reshape(1, nh).astype(F32), ((0, 0), (0, LANES - nh)))
    row = jnp.arange(LANES)[:, None]
    col = jnp.arange(n)[None, :] // HEAD_DIM
    expand = jnp.stack([(row == col * 3 + i) for i in range(3)]).astype(BF16)
    tok = pl.BlockSpec((tm, n), lambda i: (i, 0))
    return pl.pallas_call(
        _nsa_combine_body, name="nsa_combine", grid=(t // tm,),
        in_specs=[tok, tok, tok, pl.BlockSpec((tm, LANES), lambda i: (i, g_blk)),
                  pl.BlockSpec((1, LANES), lambda i: (0, 0)),
                  pl.BlockSpec((3, LANES, n), lambda i: (0, 0, 0))],
        out_specs=tok,
        out_shape=jax.ShapeDtypeStruct((t, n), BF16),
        compiler_params=_cparams(("parallel",)),
    )(oc, osel, ow, yg, bp, expand)


def _to_heads(x, b, s, n):
    return x.reshape(b, s, n, -1).transpose(0, 2, 1, 3).reshape(b * n, s, -1)


def _from_heads(o, b, n):
    _, s, d = o.shape
    return o.reshape(b, n, s, d).transpose(0, 2, 1, 3).reshape(b * s, n * d)


def _even_mixer(h, gain, layer_idx, rope16, w_in, w_out, lq1, lk1, lq2, lk2, subln, b, s):
    na = MOBA_HEADS * HEAD_DIM
    nb = DIFF_HEADS * 2 * HEAD_DIM
    w_rope = jnp.concatenate([w_in[:, 0:2 * na], w_in[:, 3 * na:3 * na + 2 * nb]], axis=1)
    w_rest = jnp.concatenate([w_in[:, 2 * na:3 * na], w_in[:, 3 * na + 2 * nb:]], axis=1)
    w = jnp.concatenate([w_rope, w_rest], axis=1).astype(BF16)
    y = _proj(h, w, name="proj_even_in", gain=gain, rope=rope16, rope_cols=w_rope.shape[1], shift=ROPE_DIM // 2, out_dtype=BF16)
    y3 = y.reshape(b, s, -1)
    blk = lambda off: off // LANES
    pairs = na // LANES
    qa, ka = _moba_gate(y3, q_blk=blk(0), k_blk=blk(na), pairs=pairs)
    o_a = _pflash(qa, ka, y3, name="flash_moba", mode="moba", dk=2 * LANES, batch=b, pairs=pairs,
                  q_map=lambda i, j: (i, j, 0), k_map=lambda i, j: (i, 0, 0),
                  v_map=lambda i, j: (i // pairs, 0, blk(2 * na + 2 * nb) + i % pairs))
    lambda_init = 0.8 - 0.6 * math.exp(-0.3 * layer_idx)
    nh = DIFF_HEADS
    o_b = _pflash(y3, y3, y3, name="flash_diff", mode="diff", dk=LANES, batch=b, pairs=nh, scale=HEAD_DIM ** -0.5,
                  q_map=lambda i, j: (i // nh, j, blk(2 * na) + i % nh),
                  k_map=lambda i, j: (i // nh, 0, blk(2 * na + nb) + i % nh),
                  v_map=lambda i, j: (i // nh, 0, blk(3 * na + 2 * nb) + i % nh),
                  diff_params=(lq1, lk1, lq2, lk2, subln), lambda_init=lambda_init)
    t = b * s
    return _proj(o_a.reshape(t, -1), w_out.astype(BF16), name="proj_out", x2=o_b.reshape(t, -1), res=h, out_dtype=F32)


def _odd_mixer(h, gain, positions, rope16, rope32s, w_in, w_out, gate_b, pe_k, pe_v, k_w1, k_w2, v_w1, v_w2,
               q_norm, w_uq, kv_norm, w_ukv, b, s):
    G, R, d = NSA_GROUPS, NSA_REP, HEAD_DIM
    sizes = [NSA_HEADS * d] + [G * d] * 6 + [NSA_HEADS * 3, MLA_Q_RANK, MLA_KV_RANK, MLA_ROPE]
    offs = [0]
    for z in sizes:
        offs.append(offs[-1] + z)
    col = lambda i: w_in[:, offs[i]:offs[i + 1]]
    w_r = jnp.concatenate([col(0), col(3), col(5)], axis=1).astype(BF16)
    yr = _proj(h, w_r, name="proj_odd_rope", gain=gain, rope=rope16, rope_cols=w_r.shape[1], shift=ROPE_DIM // 2, out_dtype=BF16)
    w_p = jnp.concatenate([col(1), col(2), col(4), col(6), col(8), col(9), col(7)], axis=1)
    w_p = jnp.pad(w_p, ((0, 0), (0, (-w_p.shape[1]) % LANES))).astype(BF16)
    yp = _proj(h, w_p, name="proj_odd_plain", gain=gain, out_dtype=F32)
    k_cmp, v_cmp = yp[:, 0:G * d], yp[:, G * d:2 * G * d]
    cq_blk = 4 * G * d // MLA_Q_RANK
    ckv_blk = (4 * G * d + MLA_Q_RANK) // MLA_KV_RANK
    gate_blk = (4 * G * d + MLA_Q_RANK + MLA_KV_RANK) // LANES
    w_pe = jnp.pad(col(10), ((0, 0), (MLA_NOPE, LANES - MLA_NOPE - MLA_ROPE))).astype(BF16)
    k_pe = _proj(h, w_pe, name="proj_mla_kpe", gain=gain, rope=rope32s, rope_cols=LANES, shift=MLA_ROPE // 2,
                 out_dtype=BF16)

    yr3, yp3 = yr.reshape(b, s, -1), yp.reshape(b, s, -1)
    nc = s // NSA_CMP_STRIDE
    cpos = jnp.concatenate([positions[:, NSA_CMP_LEN - 1::NSA_CMP_STRIDE], positions[:, -1:]], axis=1)
    ctab = [t.reshape(b, nc, LANES) for t in _rope_tables(cpos, ROPE_DIM, HEAD_DIM)]
    xk = _to_heads(k_cmp, b, s, G).reshape(b * G, nc, NSA_CMP_STRIDE * d)
    xv = _to_heads(v_cmp, b, s, G).reshape(b * G, nc, NSA_CMP_STRIDE * d)
    kc = _nsa_compress(xk, pe_k, k_w1.astype(BF16), k_w2.astype(BF16), ctab, rope=True, groups=G)
    vc = _nsa_compress(xv, pe_v, v_w1.astype(BF16), v_w2.astype(BF16), ctab, rope=False, groups=G)
    o_c, mbias = _nsa_cmp(yr3, kc, vc.transpose(0, 2, 1))
    k_blk = NSA_HEADS * d // LANES
    o_s = _gflash(yr3, yr3, yp3, mbias, name="flash_sel", q_blk=0, k_blk=k_blk, v_blk=2)
    o_w = _gflash(yr3, yr3, yp3, None, name="flash_win", q_blk=0, k_blk=k_blk + 1, v_blk=3, window=NSA_WINDOW)
    t = b * s
    o_nsa = _nsa_combine(o_c.reshape(t, -1), o_s.reshape(t, -1), o_w.reshape(t, -1), yp, gate_blk, gate_b)

    hq = MLA_NOPE + MLA_ROPE
    nh = MLA_HEADS
    wq = jnp.pad(w_uq.reshape(MLA_Q_RANK, nh, hq), ((0, 0), (0, 0), (0, LANES - hq))).reshape(MLA_Q_RANK, nh * LANES)
    q = _proj(yp, wq.astype(BF16), name="proj_mla_q", x_blk=(MLA_Q_RANK, cq_blk), gain=q_norm, rope=rope32s,
              rope_cols=nh * LANES, shift=MLA_ROPE // 2, out_dtype=BF16)
    wkv = w_ukv.reshape(MLA_KV_RANK, nh, MLA_NOPE + MLA_V)
    wk = jnp.pad(wkv[:, :, :MLA_NOPE], ((0, 0), (0, 0), (0, LANES - MLA_NOPE))).reshape(MLA_KV_RANK, nh * LANES)
    wkv = jnp.concatenate([wk, wkv[:, :, MLA_NOPE:].reshape(MLA_KV_RANK, nh * MLA_V)], axis=1).astype(BF16)
    kv = _proj(yp, wkv, name="proj_mla_kv", x_blk=(MLA_KV_RANK, ckv_blk), gain=kv_norm, add=k_pe, add_cols=nh * LANES,
               out_dtype=BF16)
    q3, kv3 = q.reshape(b, s, -1), kv.reshape(b, s, -1)
    pairs = nh // 2
    o_d = _pflash(q3, kv3, kv3, name="flash_mla", mode="slots", dk=2 * LANES, batch=b, pairs=pairs, scale=hq ** -0.5,
                  q_map=lambda i, j: (i // pairs, j, i % pairs), k_map=lambda i, j: (i // pairs, 0, i % pairs),
                  v_map=lambda i, j: (i // pairs, 0, nh + i % pairs))

    return _proj(o_nsa, w_out.astype(BF16), name="proj_out", x2=o_d.reshape(t, -1), res=h, out_dtype=F32)


def kernel(x, positions, attn_norm, ffn_norm, final_norm, ffn_w_gate, ffn_w_up, ffn_w_down, ev_w_in, ev_w_out, diff_lambda_q1, diff_lambda_k1, diff_lambda_q2, diff_lambda_k2, diff_subln, od_w_in, od_w_out, nsa_gate_b, nsa_pe_k, nsa_pe_v, nsa_k_w1, nsa_k_w2, nsa_v_w1, nsa_v_w2, mla_q_norm, mla_w_uq, mla_kv_norm, mla_w_ukv):
    b, s, d = x.shape
    depth = attn_norm.shape[0]
    rope16 = _rope_tables(positions, ROPE_DIM, HEAD_DIM)
    rope32s = _rope_tables(positions, MLA_ROPE, LANES, offset=MLA_NOPE)
    h = x.reshape(b * s, d)
    for l in range(depth):
        i = l // 2
        if l % 2 == 0:
            h = _even_mixer(h, attn_norm[l], l, rope16, ev_w_in[i], ev_w_out[i], diff_lambda_q1[i],
                            diff_lambda_k1[i], diff_lambda_q2[i], diff_lambda_k2[i], diff_subln[i], b, s)
        else:
            h = _odd_mixer(h, attn_norm[l], positions, rope16, rope32s, od_w_in[i], od_w_out[i], nsa_gate_b[i],
                           nsa_pe_k[i], nsa_pe_v[i], nsa_k_w1[i], nsa_k_w2[i], nsa_v_w1[i], nsa_v_w2[i],
                           mla_q_norm[i], mla_w_uq[i], mla_kv_norm[i], mla_w_ukv[i], b, s)
        h = _ffn(h, ffn_norm[l], ffn_w_gate[l].astype(BF16), ffn_w_up[l].astype(BF16), ffn_w_down[l].astype(BF16),
                 final_norm, final_norm=(l == depth - 1))
    return h.reshape(b, s, d)
```

```python
import functools
import math

import jax
import jax.numpy as jnp
from jax import lax
from jax.experimental import pallas as pl
from jax.experimental.pallas import tpu as pltpu

F32 = jnp.float32
BF16 = jnp.bfloat16

D_MODEL = 1024
HEAD_DIM = 64
ROPE_THETA = 500000.0
ROPE_DIM = HEAD_DIM // 4
NORM_EPS = 1e-5
D_FF = 2816

MOBA_HEADS = 8
MOBA_BLOCK = 256
MOBA_TOPK = 3
DIFF_HEADS = 4
DIFF_V = 2 * HEAD_DIM
NSA_HEADS = 8
NSA_GROUPS = 2
NSA_REP = NSA_HEADS // NSA_GROUPS
NSA_CMP_LEN = 32
NSA_CMP_STRIDE = 16
NSA_CMP_HIDDEN = 256
NSA_SEL_BLOCK = 64
NSA_SEL_TOPK = 16
NSA_WINDOW = 512
NSA_FORCE_BONUS = 1e3
MLA_HEADS = 8
MLA_Q_RANK = 256
MLA_KV_RANK = 128
MLA_NOPE = 64
MLA_ROPE = 32
MLA_V = 64

LANES = 128
MOBA_BIAS_LANES = 32
NEG = -1e30
M_INIT = -1e37
LOG2E = 1.4426950408889634
UNROLL = 4
VMEM_LIMIT = 56 * 1024 * 1024

_NT = (((1,), (1,)), ((), ()))


def _cparams(sem):
    return pltpu.CompilerParams(dimension_semantics=sem, vmem_limit_bytes=VMEM_LIMIT)


def _rope_tables(positions, dim, period, offset=0):
    r = dim // 2
    inv = 1.0 / (ROPE_THETA ** (jnp.arange(0, dim, 2, dtype=F32) / dim))
    ang = positions.astype(F32)[..., None] * inv
    cos, sin = jnp.cos(ang), jnp.sin(ang)
    const = lambda n, val: jnp.full(ang.shape[:-1] + (n,), val, F32)
    rest = period - offset - 2 * r
    zr = jnp.zeros_like(sin)
    c = jnp.concatenate([const(offset, 1.0), cos, cos, const(rest, 1.0)], -1)
    s1 = jnp.concatenate([const(offset, 0.0), -sin, zr, const(rest, 0.0)], -1)
    s2 = jnp.concatenate([const(offset, 0.0), zr, sin, const(rest, 0.0)], -1)
    reps = LANES // period
    tile = lambda t: jnp.tile(t, (1,) * (t.ndim - 1) + (reps,)).reshape(-1, LANES)
    return tile(c), tile(s1), tile(s2)


def _apply_rope(y, c, s1, s2, shift):
    return y * c + pltpu.roll(y, LANES - shift, 1) * s1 + pltpu.roll(y, shift, 1) * s2


def _proj_body(*refs, has_norm, has_x2, has_rope, has_res, add_cols, rope_cols, shift, chunk):
    it = iter(refs)
    x_ref = next(it)
    x2_ref = next(it) if has_x2 else None
    g_ref = next(it) if has_norm else None
    w_ref = next(it)
    if has_rope:
        c_ref, s1_ref, s2_ref = next(it), next(it), next(it)
    res_ref = next(it) if has_res else None
    add_ref = next(it) if add_cols else None
    o_ref = next(it)
    n = w_ref.shape[1]
    if has_norm:
        xf = x_ref[...].astype(F32)
        y = xf * lax.rsqrt(jnp.mean(xf * xf, axis=-1, keepdims=True) + NORM_EPS)
        xb = (y * g_ref[...]).astype(BF16)
    else:
        xb = x_ref[...].astype(BF16)
    k1 = xb.shape[1]
    for c0 in range(0, n, chunk):
        cw = min(chunk, n - c0)
        y = jnp.dot(xb, w_ref[0:k1, c0:c0 + cw], preferred_element_type=F32)
        if has_x2:
            y = y + jnp.dot(x2_ref[...].astype(BF16), w_ref[k1:, c0:c0 + cw], preferred_element_type=F32)
        if has_res:
            y = y + res_ref[:, c0:c0 + cw]
        if (has_rope and c0 < rope_cols) or c0 < add_cols:
            for k0 in range(0, cw, LANES):
                ys = y[:, k0:k0 + LANES]
                if has_rope and c0 + k0 < rope_cols:
                    ys = _apply_rope(ys, c_ref[...], s1_ref[...], s2_ref[...], shift)
                if c0 + k0 < add_cols:
                    ys = ys + add_ref[...].astype(F32)
                o_ref[:, c0 + k0:c0 + k0 + LANES] = ys.astype(o_ref.dtype)
        else:
            o_ref[:, c0:c0 + cw] = y.astype(o_ref.dtype)


def _proj(x, w, *, name, x_blk=None, x2=None, gain=None, rope=None, rope_cols=0, shift=0, res=None, add=None,
          add_cols=0, out_dtype=F32, tm=512, chunk=512):
    t = x.shape[0]
    k, xj = (x.shape[1], 0) if x_blk is None else x_blk
    n = w.shape[1]
    assert t % tm == 0 and n % LANES == 0 and rope_cols % LANES == 0 and add_cols % LANES == 0
    has_norm, has_rope, has_res, has_x2 = gain is not None, rope is not None, res is not None, x2 is not None
    assert not (has_norm and has_x2)
    args, specs = [x], [pl.BlockSpec((tm, k), lambda i: (i, xj))]
    if has_x2:
        args.append(x2)
        specs.append(pl.BlockSpec((tm, x2.shape[1]), lambda i: (i, 0)))
    if has_norm:
        args.append(gain.reshape(1, k).astype(F32))
        specs.append(pl.BlockSpec((1, k), lambda i: (0, 0)))
    args.append(w)
    specs.append(pl.BlockSpec((w.shape[0], n), lambda i: (0, 0)))
    if has_rope:
        for tb in rope:
            args.append(tb)
            specs.append(pl.BlockSpec((tm, LANES), lambda i: (i, 0)))
    if has_res:
        args.append(res)
        specs.append(pl.BlockSpec((tm, n), lambda i: (i, 0)))
    if add_cols:
        args.append(add)
        specs.append(pl.BlockSpec((tm, LANES), lambda i: (i, 0)))
    body = functools.partial(_proj_body, has_norm=has_norm, has_x2=has_x2, has_rope=has_rope, has_res=has_res,
                             add_cols=add_cols,
                             rope_cols=rope_cols, shift=shift, chunk=chunk)
    return pl.pallas_call(
        body, name=name, grid=(t // tm,), in_specs=specs,
        out_specs=pl.BlockSpec((tm, n), lambda i: (i, 0)),
        out_shape=jax.ShapeDtypeStruct((t, n), out_dtype),
        compiler_params=_cparams(("parallel",)),
    )(*args)


def _ffn_body(x_ref, g_ref, wg_ref, wu_ref, wd_ref, fg_ref, o_ref, xn_ref, acc_ref, *, final_norm):
    j = pl.program_id(1)

    @pl.when(j == 0)
    def _():
        xf = x_ref[...]
        y = xf * lax.rsqrt(jnp.mean(xf * xf, axis=-1, keepdims=True) + NORM_EPS)
        xn_ref[...] = (y * g_ref[...]).astype(BF16)
        acc_ref[...] = jnp.zeros_like(acc_ref)

    xn = xn_ref[...]
    g = jnp.dot(xn, wg_ref[...], preferred_element_type=F32)
    u = jnp.dot(xn, wu_ref[...], preferred_element_type=F32)
    a = (jax.nn.silu(g) * u).astype(BF16)
    acc_ref[...] += jnp.dot(a, wd_ref[...], preferred_element_type=F32)

    @pl.when(j == pl.num_programs(1) - 1)
    def _():
        h = x_ref[...] + acc_ref[...]
        if final_norm:
            y = h * lax.rsqrt(jnp.mean(h * h, axis=-1, keepdims=True) + NORM_EPS)
            h = y * fg_ref[...]
        o_ref[...] = h


def _ffn(x, gain, wg, wu, wd, final_gain, *, final_norm, tm=512, tf=1408):
    t, d = x.shape
    f = wg.shape[1]
    assert t % tm == 0 and f % tf == 0
    return pl.pallas_call(
        functools.partial(_ffn_body, final_norm=final_norm), name="ffn",
        grid=(t // tm, f // tf),
        in_specs=[
            pl.BlockSpec((tm, d), lambda i, j: (i, 0)),
            pl.BlockSpec((1, d), lambda i, j: (0, 0)),
            pl.BlockSpec((d, tf), lambda i, j: (0, j)),
            pl.BlockSpec((d, tf), lambda i, j: (0, j)),
            pl.BlockSpec((tf, d), lambda i, j: (j, 0)),
            pl.BlockSpec((1, d), lambda i, j: (0, 0)),
        ],
        out_specs=pl.BlockSpec((tm, d), lambda i, j: (i, 0)),
        out_shape=jax.ShapeDtypeStruct((t, d), F32),
        scratch_shapes=[pltpu.VMEM((tm, d), BF16), pltpu.VMEM((tm, d), F32)],
        compiler_params=_cparams(("parallel", "arbitrary")),
    )(x, gain.reshape(1, d).astype(F32), wg, wu, wd, final_gain.reshape(1, d).astype(F32))


def _sweep(q, k_get, vt_get, s_ref, *, q0, tq, tk, dv, scale, window):
    cols = q.shape[0]

    def fold8(x, op):
        return op(x.reshape(x.shape[0] // 8, 8, cols), axis=0)

    def score_step(j, slot, mrun, masked):
        ks = pl.multiple_of(j * tk, tk)
        kb = k_get(ks, tk)
        s = lax.dot_general(kb, q, _NT, preferred_element_type=F32) * (scale * LOG2E)
        if masked:
            kpos = ks + lax.broadcasted_iota(jnp.int32, (tk, cols), 0)
            qpos = q0 + lax.rem(lax.broadcasted_iota(jnp.int32, (tk, cols), 1), tq)
            ok = kpos <= qpos
            if window is not None:
                ok = jnp.logical_and(ok, kpos > qpos - window)
            s = jnp.where(ok, s, NEG)
        s_ref[slot] = s
        return jnp.maximum(mrun, fold8(s, jnp.max))

    def pv_step(j, slot, carry, m):
        l8, acc = carry
        p = jnp.exp2(s_ref[slot] - m)
        l8 = l8 + fold8(p, jnp.sum)
        acc = acc + jnp.dot(vt_get(j), p.astype(BF16), preferred_element_type=F32)
        return l8, acc

    m_init = jnp.full((8, cols), M_INIT, F32)
    acc_init = (jnp.zeros((8, cols), F32), jnp.zeros((dv, cols), F32))
    if window is None:
        n_full = q0 // tk
        groups = n_full // UNROLL

        def score_group(i, mrun):
            ks = pl.multiple_of(i * (UNROLL * tk), UNROLL * tk)
            s = lax.dot_general(k_get(ks, UNROLL * tk), q, _NT, preferred_element_type=F32) * (scale * LOG2E)
            for u in range(UNROLL):
                s_ref[UNROLL * i + u] = s[u * tk:(u + 1) * tk]
            return jnp.maximum(mrun, fold8(s, jnp.max))

        mrun = lax.fori_loop(0, groups, score_group, m_init)
        mrun = lax.fori_loop(groups * UNROLL, n_full, lambda j, t: score_step(j, j, t, False), mrun)
        mrun = score_step(n_full, n_full, mrun, True)
        m = jnp.max(mrun, axis=0, keepdims=True)

        def pv_group(i, carry):
            for u in range(UNROLL):
                carry = pv_step(UNROLL * i + u, UNROLL * i + u, carry, m)
            return carry

        carry = lax.fori_loop(0, groups, pv_group, acc_init)
        carry = lax.fori_loop(groups * UNROLL, n_full, lambda j, t: pv_step(j, j, t, m), carry)
        l8, acc = pv_step(n_full, n_full, carry, m)
    else:
        j_lo = jnp.maximum(q0 - window + 1, 0) // tk
        j_hi = (q0 + tq - 1) // tk
        mrun = lax.fori_loop(j_lo, j_hi + 1, lambda j, t: score_step(j, j - j_lo, t, True), m_init)
        m = jnp.max(mrun, axis=0, keepdims=True)
        l8, acc = lax.fori_loop(j_lo, j_hi + 1, lambda j, t: pv_step(j, j - j_lo, t, m), acc_init)
    l = jnp.sum(l8, axis=0, keepdims=True)
    return acc / jnp.maximum(l, 1e-30)


def _head_lanes(mode, v, lane):
    d = HEAD_DIM
    if mode == "diff":
        return jnp.logical_and(lane >= d * v, lane < d * (v + 1))
    if mode == "moba":
        lo = 2 * d + v * MOBA_BIAS_LANES
        return jnp.logical_or(jnp.logical_and(lane >= d * v, lane < d * (v + 1)),
                              jnp.logical_and(lane >= lo, lane < lo + MOBA_BIAS_LANES))
    return jnp.logical_and(lane >= LANES * v, lane < LANES * (v + 1))


def _pflash_body(*refs, mode, tq, tk, scale, lambda_init):
    if mode == "diff":
        q_ref, k_ref, v_ref, lq1_ref, lk1_ref, lq2_ref, lk2_ref, sg_ref, o_ref, s_ref, vt_ref = refs
    else:
        q_ref, k_ref, v_ref, o_ref, s_ref, vt_ref = refs
    qi = pl.program_id(1)
    n_chunks = v_ref.shape[1] // tk

    @pl.when(qi == 0)
    def _():
        for c in range(n_chunks):
            vt_ref[c] = v_ref[0, c * tk:(c + 1) * tk, :].astype(F32).T.astype(BF16)

    q = q_ref[0]
    lane = lax.broadcasted_iota(jnp.int32, q.shape, 1)
    d = HEAD_DIM
    outs = []
    for v in range(2):
        qv = jnp.where(_head_lanes(mode, v, lane), q, jnp.zeros_like(q))
        if mode == "diff":
            vt_get, dv = (lambda j: vt_ref[j]), LANES
        else:
            vt_get, dv = (lambda j, v=v: vt_ref[j, v * d:(v + 1) * d, :]), d
        outs.append(_sweep(qv, lambda ks, n: k_ref[0, pl.ds(ks, n), :], vt_get, s_ref,
                           q0=qi * tq, tq=tq, tk=tk, dv=dv, scale=scale, window=None))
    if mode == "diff":
        lam = (jnp.exp(jnp.sum(lq1_ref[...] * lk1_ref[...], axis=-1, keepdims=True))
               - jnp.exp(jnp.sum(lq2_ref[...] * lk2_ref[...], axis=-1, keepdims=True)) + lambda_init)
        d = outs[0] - lam * outs[1]
        y = d * lax.rsqrt(jnp.mean(d * d, axis=0, keepdims=True) + NORM_EPS)
        o_t = (y * sg_ref[...]) * (1.0 - lambda_init)
    else:
        o_t = jnp.concatenate(outs, axis=0)
    o_ref[0] = o_t.T.astype(o_ref.dtype)


def _pflash(q, k, v, *, name, mode, dk, q_map, k_map, v_map, batch, pairs, tq=512, tk=512, scale=1.0,
            diff_params=None, lambda_init=0.0):
    s = v.shape[1]
    assert s % tk == 0 and tq == tk
    args = [q, k, v]
    in_specs = [pl.BlockSpec((1, tq, dk), q_map), pl.BlockSpec((1, s, dk), k_map),
                pl.BlockSpec((1, s, LANES), v_map)]
    if mode == "diff":
        lq1, lk1, lq2, lk2, subln = diff_params
        for a in (lq1, lk1, lq2, lk2):
            args.append(a.reshape(1, HEAD_DIM).astype(F32))
            in_specs.append(pl.BlockSpec((1, HEAD_DIM), lambda i, j: (0, 0)))
        args.append(subln.reshape(LANES, 1).astype(F32))
        in_specs.append(pl.BlockSpec((LANES, 1), lambda i, j: (0, 0)))
    body = functools.partial(_pflash_body, mode=mode, tq=tq, tk=tk, scale=scale, lambda_init=lambda_init)
    return pl.pallas_call(
        body, name=name, grid=(batch * pairs, s // tq), in_specs=in_specs,
        out_specs=pl.BlockSpec((1, tq, LANES), lambda i, j: (i // pairs, j, i % pairs)),
        out_shape=jax.ShapeDtypeStruct((batch, s, pairs * LANES), BF16),
        scratch_shapes=[pltpu.VMEM((s // tk, tk, tq), F32), pltpu.VMEM((s // tk, LANES, tk), BF16)],
        compiler_params=_cparams(("parallel", "arbitrary")),
    )(*args)


def _gflash_body(*refs, tq, tk, rep, groups, window, select):
    if select:
        q_ref, k_ref, v_ref, mb_ref, o_ref, s_ref, vt_ref, ka_ref, qa_ref = refs
    else:
        q_ref, k_ref, v_ref, o_ref, s_ref, vt_ref, ka_ref, qa_ref = refs
    d = HEAD_DIM
    g = pl.program_id(0) % groups
    qi = pl.program_id(1)
    n_chunks = v_ref.shape[1] // tk

    @pl.when(qi == 0)
    def _():
        move = _place(LANES, LANES, g * d)
        for c in range(n_chunks):
            rows = slice(c * tk, (c + 1) * tk)
            vt = v_ref[0, rows, :].astype(F32).T
            vt_ref[c] = jnp.where(g == 0, vt[0:d], vt[d:2 * d]).astype(BF16)
            kg = jnp.dot(k_ref[0, rows, :], move, preferred_element_type=F32).astype(BF16)
            if select:
                blk = (c * tk + lax.broadcasted_iota(jnp.int32, (tk, LANES), 0)) // NSA_SEL_BLOCK
                lane = lax.broadcasted_iota(jnp.int32, (tk, LANES), 1)
                ka_ref[rows, 0:LANES] = jnp.where(blk == lane, 1.0, 0.0).astype(BF16)
                ka_ref[rows, LANES:2 * LANES] = kg
            else:
                ka_ref[rows, :] = kg

    q = q_ref[0]
    for r in range(rep):
        qr = jnp.dot(q, _place(rep * d, LANES, r * d, d ** -0.5), preferred_element_type=F32).astype(BF16)
        rows = slice(r * tq, (r + 1) * tq)
        if select:
            qa_ref[rows, 0:LANES] = mb_ref[0]
            qa_ref[rows, LANES:2 * LANES] = qr
        else:
            qa_ref[rows, :] = qr
    o_t = _sweep(qa_ref[...], lambda ks, n: ka_ref[pl.ds(ks, n), :], lambda j: vt_ref[j], s_ref,
                 q0=qi * tq, tq=tq, tk=tk, dv=d, scale=1.0, window=window)
    for r in range(rep):
        o_ref[0, :, r * d:(r + 1) * d] = o_t[:, r * tq:(r + 1) * tq].T.astype(o_ref.dtype)


def _gflash(yq, yk, yv, mbias, *, name, q_blk, k_blk, v_blk, tq=128, tk=512, window=None):
    b, s, _ = yq.shape
    rep, groups, d = NSA_REP, NSA_GROUPS, HEAD_DIM
    assert groups * d == LANES and s % tk == 0 and tk % tq == 0
    select = mbias is not None
    dk = 2 * LANES if select else LANES
    n_slots = s // tk if window is None else (window + tq - 2) // tk + 2
    args = [yq, yk, yv]
    in_specs = [pl.BlockSpec((1, tq, rep * d), lambda i, j: (i // groups, j, q_blk + i % groups)),
                pl.BlockSpec((1, s, LANES), lambda i, j: (i // groups, 0, k_blk)),
                pl.BlockSpec((1, s, LANES), lambda i, j: (i // groups, 0, v_blk))]
    if select:
        args.append(mbias)
        in_specs.append(pl.BlockSpec((1, tq, LANES), lambda i, j: (i, j, 0)))
    body = functools.partial(_gflash_body, tq=tq, tk=tk, rep=rep, groups=groups, window=window, select=select)
    return pl.pallas_call(
        body, name=name, grid=(b * groups, s // tq), in_specs=in_specs,
        out_specs=pl.BlockSpec((1, tq, rep * d), lambda i, j: (i // groups, j, i % groups)),
        out_shape=jax.ShapeDtypeStruct((b, s, groups * rep * d), F32),
        scratch_shapes=[pltpu.VMEM((n_slots, tk, rep * tq), F32), pltpu.VMEM((s // tk, d, tk), BF16),
                        pltpu.VMEM((s, dk), BF16), pltpu.VMEM((rep * tq, dk), BF16)],
        compiler_params=_cparams(("parallel", "arbitrary")),
    )(*args)


def _topk_mask(score, row, k):
    sel = None
    for _ in range(k):
        mx = jnp.max(score, axis=0, keepdims=True)
        idx = jnp.min(jnp.where(score == mx, row, LANES), axis=0, keepdims=True)
        hit = row == idx
        pick = jnp.logical_and(hit, mx > -jnp.inf)
        sel = pick if sel is None else jnp.logical_or(sel, pick)
        score = jnp.where(hit, -jnp.inf, score)
    return sel


def _moba_gate_body(q_ref, k_ref, qa_ref, ka_ref, kmean_ref, *, tq, seq):
    qi = pl.program_id(1)
    nbl = MOBA_BIAS_LANES

    @pl.when(qi == 0)
    def _():
        rowblk = lax.rem(lax.broadcasted_iota(jnp.int32, (LANES, seq), 0), nbl)
        colblk = lax.broadcasted_iota(jnp.int32, (LANES, seq), 1) // MOBA_BLOCK
        ind = jnp.where(rowblk == colblk, 1.0, 0.0).astype(BF16)
        ksum = jnp.dot(ind, k_ref[0], preferred_element_type=F32)
        rhead = lax.broadcasted_iota(jnp.int32, (LANES, LANES), 0) // nbl
        lhead = lax.broadcasted_iota(jnp.int32, (LANES, LANES), 1) // HEAD_DIM
        kmean_ref[...] = jnp.where(rhead == lhead, ksum * (1.0 / MOBA_BLOCK), 0.0)

    q = q_ref[0]
    q0 = pl.multiple_of(qi * tq, tq)
    km = kmean_ref[...]
    km_hi = km.astype(BF16)
    km_lo = (km - km_hi.astype(F32)).astype(BF16)
    gate = (lax.dot_general(km_hi, q, _NT, preferred_element_type=F32)
            + lax.dot_general(km_lo, q, _NT, preferred_element_type=F32))
    blk_t = lax.broadcasted_iota(jnp.int32, (nbl, tq), 0)
    own_t = (q0 + lax.broadcasted_iota(jnp.int32, (nbl, tq), 1)) // MOBA_BLOCK
    parts = []
    for v in range(2):
        g_v = jnp.where(blk_t < own_t, gate[v * nbl:(v + 1) * nbl], -jnp.inf)
        sel = jnp.logical_or(_topk_mask(g_v, blk_t, MOBA_TOPK), blk_t == own_t)
        parts.append(jnp.where(sel, 0.0, NEG))
    parts.append(jnp.zeros((LANES - 2 * nbl, tq), F32))
    bias = jnp.concatenate(parts, axis=0).T
    qa_ref[0, :, 0:LANES] = q * (HEAD_DIM ** -0.5)
    qa_ref[0, :, LANES:2 * LANES] = bias.astype(BF16)
    lane = lax.broadcasted_iota(jnp.int32, (tq, LANES), 1)
    own = (q0 + lax.broadcasted_iota(jnp.int32, (tq, LANES), 0)) // MOBA_BLOCK
    onehot = jnp.where(jnp.logical_and(lane < 2 * nbl, lax.rem(lane, nbl) == own), 1.0, 0.0)
    ka_ref[0, :, 0:LANES] = k_ref[0, pl.ds(q0, tq), :]
    ka_ref[0, :, LANES:2 * LANES] = onehot.astype(BF16)


def _moba_gate(y3, *, q_blk, k_blk, pairs, tq=512):
    b, s, _ = y3.shape
    assert s % MOBA_BLOCK == 0 and s // MOBA_BLOCK <= MOBA_BIAS_LANES and s % tq == 0
    out = jax.ShapeDtypeStruct((b * pairs, s, 2 * LANES), BF16)
    return pl.pallas_call(
        functools.partial(_moba_gate_body, tq=tq, seq=s), name="moba_gate",
        grid=(b * pairs, s // tq),
        in_specs=[pl.BlockSpec((1, tq, LANES), lambda i, j: (i // pairs, j, q_blk + i % pairs)),
                  pl.BlockSpec((1, s, LANES), lambda i, j: (i // pairs, 0, k_blk + i % pairs))],
        out_specs=[pl.BlockSpec((1, tq, 2 * LANES), lambda i, j: (i, j, 0)),
                   pl.BlockSpec((1, tq, 2 * LANES), lambda i, j: (i, j, 0))],
        out_shape=[out, out],
        scratch_shapes=[pltpu.VMEM((LANES, LANES), F32)],
        compiler_params=_cparams(("parallel", "arbitrary")),
    )(y3, y3)


def _nsa_compress_body(x_ref, pe_ref, w1_ref, w2_ref, c_ref, s1_ref, s2_ref, o_ref, *, rope):
    half = w1_ref.shape[0] // 2
    x = x_ref[0].astype(F32)
    lo = (x + pe_ref[:, 0:half]).astype(BF16)
    hi = (x + pe_ref[:, half:2 * half]).astype(BF16)
    a = jnp.dot(lo, w1_ref[0:half, :], preferred_element_type=F32)
    b = jnp.dot(hi, w1_ref[half:2 * half, :], preferred_element_type=F32)
    n = a.shape[0]
    h1 = a + pltpu.roll(b, n - 1, 0)
    y = jnp.dot(jax.nn.gelu(h1).astype(BF16), w2_ref[...], preferred_element_type=F32)
    if rope:
        y = _apply_rope(y, c_ref[0], s1_ref[0], s2_ref[0], ROPE_DIM // 2)
    o_ref[0] = y[:, 0:HEAD_DIM].astype(o_ref.dtype)


def _nsa_compress(xr, pe, w1, w2, tables, *, rope, groups):
    g, n, w = xr.shape
    hid = w1.shape[1]
    w2p = jnp.pad(w2, ((0, 0), (0, LANES - w2.shape[1])))
    tspec = pl.BlockSpec((1, n, LANES), lambda i: (i // groups, 0, 0))
    return pl.pallas_call(
        functools.partial(_nsa_compress_body, rope=rope), name="nsa_compress",
        grid=(g,),
        in_specs=[pl.BlockSpec((1, n, w), lambda i: (i, 0, 0)),
                  pl.BlockSpec((1, 2 * w), lambda i: (0, 0)),
                  pl.BlockSpec((2 * w, hid), lambda i: (0, 0)),
                  pl.BlockSpec((hid, LANES), lambda i: (0, 0)),
                  tspec, tspec, tspec],
        out_specs=pl.BlockSpec((1, n, HEAD_DIM), lambda i: (i, 0, 0)),
        out_shape=jax.ShapeDtypeStruct((g, n, HEAD_DIM), BF16),
        compiler_params=_cparams(("parallel",)),
    )(xr, pe.reshape(1, 2 * w).astype(F32), w1, w2p, *tables)


def _place(n_src, n_dst, shift, value=1.0):
    src = lax.broadcasted_iota(jnp.int32, (n_src, n_dst), 0)
    dst = lax.broadcasted_iota(jnp.int32, (n_src, n_dst), 1)
    return jnp.where(jnp.logical_and(src == dst + shift, dst < HEAD_DIM), value, 0.0).astype(BF16)


def _nsa_cmp_body(q_ref, kc_ref, vc_ref, oc_ref, mb_ref, kcp_ref, *, tq, rep):
    qi = pl.program_id(1)
    q0 = qi * tq
    nc = kc_ref.shape[1]
    width = q_ref.shape[-1]

    @pl.when(qi == 0)
    def _():
        for r in range(rep):
            src = lax.broadcasted_iota(jnp.int32, (HEAD_DIM, width), 0)
            dst = lax.broadcasted_iota(jnp.int32, (HEAD_DIM, width), 1)
            spread = jnp.where(dst == src + r * HEAD_DIM, HEAD_DIM ** -0.5, 0.0).astype(BF16)
            kcp_ref[r] = jnp.dot(kc_ref[0], spread, preferred_element_type=F32).astype(BF16)

    q = q_ref[0]
    vct = vc_ref[0]
    tpos = q0 + lax.broadcasted_iota(jnp.int32, (nc, tq), 1)
    cend = lax.broadcasted_iota(jnp.int32, (nc, tq), 0) * NSA_CMP_STRIDE + (NSA_CMP_LEN - 1)
    ok = cend <= tpos
    psum = jnp.zeros((nc, tq), F32)
    for r in range(rep):
        s = lax.dot_general(kcp_ref[r], q, _NT, preferred_element_type=F32)
        s = jnp.where(ok, s, NEG)
        m = jnp.max(s, axis=0, keepdims=True)
        e = jnp.where(ok, jnp.exp(s - m), 0.0)
        p = e * (1.0 / jnp.maximum(jnp.sum(e, axis=0, keepdims=True), 1e-30))
        o_t = jnp.dot(vct, p.astype(BF16), preferred_element_type=F32)
        oc_ref[0, :, r * HEAD_DIM:(r + 1) * HEAD_DIM] = o_t.T.astype(oc_ref.dtype)
        psum = psum + p
    sstart = lax.broadcasted_iota(jnp.int32, (LANES, nc), 0) * NSA_SEL_BLOCK
    cstart = lax.broadcasted_iota(jnp.int32, (LANES, nc), 1) * NSA_CMP_STRIDE
    ov = jnp.where(jnp.logical_and(cstart < sstart + NSA_SEL_BLOCK, cstart + NSA_CMP_LEN > sstart),
                   1.0, 0.0).astype(BF16)
    p_hi = psum.astype(BF16)
    r1 = psum - p_hi.astype(F32)
    p_mid = r1.astype(BF16)
    p_lo = (r1 - p_mid.astype(F32)).astype(BF16)
    imp = (jnp.dot(ov, p_hi, preferred_element_type=F32) + jnp.dot(ov, p_mid, preferred_element_type=F32)
           + jnp.dot(ov, p_lo, preferred_element_type=F32))
    blk = lax.broadcasted_iota(jnp.int32, (LANES, tq), 0)
    qb = (q0 + lax.broadcasted_iota(jnp.int32, (LANES, tq), 1)) // NSA_SEL_BLOCK
    forced = jnp.logical_or(blk == 0, jnp.logical_or(blk == qb, blk == qb - 1))
    imp = jnp.where(forced, imp + NSA_FORCE_BONUS, imp)
    imp = jnp.where(blk <= qb, imp, -jnp.inf)
    sel = _topk_mask(imp, blk, NSA_SEL_TOPK)
    mb_ref[0] = jnp.where(sel, 0.0, NEG).T.astype(mb_ref.dtype)


def _nsa_cmp(y3, kc, vc, *, tq=512):
    b, s, _ = y3.shape
    g, nc, d = kc.shape
    rep, n = NSA_REP, NSA_GROUPS
    assert s // NSA_SEL_BLOCK <= LANES and s % tq == 0
    return pl.pallas_call(
        functools.partial(_nsa_cmp_body, tq=tq, rep=rep), name="nsa_cmp",
        grid=(g, s // tq),
        in_specs=[pl.BlockSpec((1, tq, rep * d), lambda i, j: (i // n, j, i % n)),
                  pl.BlockSpec((1, nc, d), lambda i, j: (i, 0, 0)),
                  pl.BlockSpec((1, d, nc), lambda i, j: (i, 0, 0))],
        out_specs=[pl.BlockSpec((1, tq, rep * d), lambda i, j: (i // n, j, i % n)),
                   pl.BlockSpec((1, tq, LANES), lambda i, j: (i, j, 0))],
        out_shape=[jax.ShapeDtypeStruct((b, s, n * rep * d), F32),
                   jax.ShapeDtypeStruct((g, s, LANES), BF16)],
        scratch_shapes=[pltpu.VMEM((rep, nc, rep * d), BF16)],
        compiler_params=_cparams(("parallel", "arbitrary")),
    )(y3, kc, vc)


def _nsa_combine_body(oc_ref, os_ref, ow_ref, g_ref, b_ref, e_ref, o_ref):
    gs = jax.nn.sigmoid(g_ref[...] + b_ref[...])
    g_hi = gs.astype(BF16)
    g_lo = (gs - g_hi.astype(F32)).astype(BF16)
    out = None
    for i, ref in enumerate((oc_ref, os_ref, ow_ref)):
        w = (jnp.dot(g_hi, e_ref[i], preferred_element_type=F32)
             + jnp.dot(g_lo, e_ref[i], preferred_element_type=F32))
        term = w * ref[...]
        out = term if out is None else out + term
    o_ref[...] = out.astype(o_ref.dtype)


def _nsa_combine(oc, osel, ow, yg, g_blk, gate_b, *, tm=512):
    t, n = oc.shape
    nh = NSA_HEADS * 3
    bp = jnp.pad(gate_b.reshape(1, nh).astype(F32), ((0, 0), (0, LANES - nh)))
    row = jnp.arange(LANES)[:, None]
    col = jnp.arange(n)[None, :] // HEAD_DIM
    expand = jnp.stack([(row == col * 3 + i) for i in range(3)]).astype(BF16)
    tok = pl.BlockSpec((tm, n), lambda i: (i, 0))
    return pl.pallas_call(
        _nsa_combine_body, name="nsa_combine", grid=(t // tm,),
        in_specs=[tok, tok, tok, pl.BlockSpec((tm, LANES), lambda i: (i, g_blk)),
                  pl.BlockSpec((1, LANES), lambda i: (0, 0)),
                  pl.BlockSpec((3, LANES, n), lambda i: (0, 0, 0))],
        out_specs=tok,
        out_shape=jax.ShapeDtypeStruct((t, n), BF16),
        compiler_params=_cparams(("parallel",)),
    )(oc, osel, ow, yg, bp, expand)


def _to_heads(x, b, s, n):
    return x.reshape(b, s, n, -1).transpose(0, 2, 1, 3).reshape(b * n, s, -1)


def _even_mixer(h, gain, layer_idx, rope16, w_in, w_out, lq1, lk1, lq2, lk2, subln, b, s):
    na = MOBA_HEADS * HEAD_DIM
    nb = DIFF_HEADS * 2 * HEAD_DIM
    w_rope = jnp.concatenate([w_in[:, 0:2 * na], w_in[:, 3 * na:3 * na + 2 * nb]], axis=1)
    w_rest = jnp.concatenate([w_in[:, 2 * na:3 * na], w_in[:, 3 * na + 2 * nb:]], axis=1)
    w = jnp.concatenate([w_rope, w_rest], axis=1).astype(BF16)
    y = _proj(h, w, name="proj_even_in", gain=gain, rope=rope16, rope_cols=w_rope.shape[1], shift=ROPE_DIM // 2, out_dtype=BF16)
    y3 = y.reshape(b, s, -1)
    blk = lambda off: off // LANES
    pairs = na // LANES
    qa, ka = _moba_gate(y3, q_blk=blk(0), k_blk=blk(na), pairs=pairs)
    o_a = _pflash(qa, ka, y3, name="flash_moba", mode="moba", dk=2 * LANES, batch=b, pairs=pairs,
                  q_map=lambda i, j: (i, j, 0), k_map=lambda i, j: (i, 0, 0),
                  v_map=lambda i, j: (i // pairs, 0, blk(2 * na + 2 * nb) + i % pairs))
    lambda_init = 0.8 - 0.6 * math.exp(-0.3 * layer_idx)
    nh = DIFF_HEADS
    o_b = _pflash(y3, y3, y3, name="flash_diff", mode="diff", dk=LANES, batch=b, pairs=nh, scale=HEAD_DIM ** -0.5,
                  q_map=lambda i, j: (i // nh, j, blk(2 * na) + i % nh),
                  k_map=lambda i, j: (i // nh, 0, blk(2 * na + nb) + i % nh),
                  v_map=lambda i, j: (i // nh, 0, blk(3 * na + 2 * nb) + i % nh),
                  diff_params=(lq1, lk1, lq2, lk2, subln), lambda_init=lambda_init)
    t = b * s
    return _proj(o_a.reshape(t, -1), w_out.astype(BF16), name="proj_out", x2=o_b.reshape(t, -1), res=h, out_dtype=F32)


def _odd_mixer(h, gain, positions, rope16, rope32s, w_in, w_out, gate_b, pe_k, pe_v, k_w1, k_w2, v_w1, v_w2,
               q_norm, w_uq, kv_norm, w_ukv, b, s):
    G, R, d = NSA_GROUPS, NSA_REP, HEAD_DIM
    sizes = [NSA_HEADS * d] + [G * d] * 6 + [NSA_HEADS * 3, MLA_Q_RANK, MLA_KV_RANK, MLA_ROPE]
    offs = [0]
    for z in sizes:
        offs.append(offs[-1] + z)
    col = lambda i: w_in[:, offs[i]:offs[i + 1]]
    w_r = jnp.concatenate([col(0), col(3), col(5)], axis=1).astype(BF16)
    yr = _proj(h, w_r, name="proj_odd_rope", gain=gain, rope=rope16, rope_cols=w_r.shape[1], shift=ROPE_DIM // 2, out_dtype=BF16)
    w_p = jnp.concatenate([col(1), col(2), col(4), col(6), col(8), col(9), col(7)], axis=1)
    w_p = jnp.pad(w_p, ((0, 0), (0, (-w_p.shape[1]) % LANES))).astype(BF16)
    yp = _proj(h, w_p, name="proj_odd_plain", gain=gain, out_dtype=F32)
    k_cmp, v_cmp = yp[:, 0:G * d], yp[:, G * d:2 * G * d]
    cq_blk = 4 * G * d // MLA_Q_RANK
    ckv_blk = (4 * G * d + MLA_Q_RANK) // MLA_KV_RANK
    gate_blk = (4 * G * d + MLA_Q_RANK + MLA_KV_RANK) // LANES
    w_pe = jnp.pad(col(10), ((0, 0), (MLA_NOPE, LANES - MLA_NOPE - MLA_ROPE))).astype(BF16)
    k_pe = _proj(h, w_pe, name="proj_mla_kpe", gain=gain, rope=rope32s, rope_cols=LANES, shift=MLA_ROPE // 2,
                 out_dtype=BF16)

    yr3, yp3 = yr.reshape(b, s, -1), yp.reshape(b, s, -1)
    nc = s // NSA_CMP_STRIDE
    cpos = jnp.concatenate([positions[:, NSA_CMP_LEN - 1::NSA_CMP_STRIDE], positions[:, -1:]], axis=1)
    ctab = [t.reshape(b, nc, LANES) for t in _rope_tables(cpos, ROPE_DIM, HEAD_DIM)]
    xk = _to_heads(k_cmp, b, s, G).reshape(b * G, nc, NSA_CMP_STRIDE * d)
    xv = _to_heads(v_cmp, b, s, G).reshape(b * G, nc, NSA_CMP_STRIDE * d)
    kc = _nsa_compress(xk, pe_k, k_w1.astype(BF16), k_w2.astype(BF16), ctab, rope=True, groups=G)
    vc = _nsa_compress(xv, pe_v, v_w1.astype(BF16), v_w2.astype(BF16), ctab, rope=False, groups=G)
    o_c, mbias = _nsa_cmp(yr3, kc, vc.transpose(0, 2, 1))
    k_blk = NSA_HEADS * d // LANES
    o_s = _gflash(yr3, yr3, yp3, mbias, name="flash_sel", q_blk=0, k_blk=k_blk, v_blk=2)
    o_w = _gflash(yr3, yr3, yp3, None, name="flash_win", q_blk=0, k_blk=k_blk + 1, v_blk=3, window=NSA_WINDOW)
    t = b * s
    o_nsa = _nsa_combine(o_c.reshape(t, -1), o_s.reshape(t, -1), o_w.reshape(t, -1), yp, gate_blk, gate_b)

    hq = MLA_NOPE + MLA_ROPE
    nh = MLA_HEADS
    wq = jnp.pad(w_uq.reshape(MLA_Q_RANK, nh, hq), ((0, 0), (0, 0), (0, LANES - hq))).reshape(MLA_Q_RANK, nh * LANES)
    q = _proj(yp, wq.astype(BF16), name="proj_mla_q", x_blk=(MLA_Q_RANK, cq_blk), gain=q_norm, rope=rope32s,
              rope_cols=nh * LANES, shift=MLA_ROPE // 2, out_dtype=BF16)
    wkv = w_ukv.reshape(MLA_KV_RANK, nh, MLA_NOPE + MLA_V)
    wk = jnp.pad(wkv[:, :, :MLA_NOPE], ((0, 0), (0, 0), (0, LANES - MLA_NOPE))).reshape(MLA_KV_RANK, nh * LANES)
    wkv = jnp.concatenate([wk, wkv[:, :, MLA_NOPE:].reshape(MLA_KV_RANK, nh * MLA_V)], axis=1).astype(BF16)
    kv = _proj(yp, wkv, name="proj_mla_kv", x_blk=(MLA_KV_RANK, ckv_blk), gain=kv_norm, add=k_pe, add_cols=nh * LANES,
               out_dtype=BF16)
    q3, kv3 = q.reshape(b, s, -1), kv.reshape(b, s, -1)
    pairs = nh // 2
    o_d = _pflash(q3, kv3, kv3, name="flash_mla", mode="slots", dk=2 * LANES, batch=b, pairs=pairs, scale=hq ** -0.5,
                  q_map=lambda i, j: (i // pairs, j, i % pairs), k_map=lambda i, j: (i // pairs, 0, i % pairs),
                  v_map=lambda i, j: (i // pairs, 0, nh + i % pairs))

    return _proj(o_nsa, w_out.astype(BF16), name="proj_out", x2=o_d.reshape(t, -1), res=h, out_dtype=F32)


def kernel(x, positions, attn_norm, ffn_norm, final_norm, ffn_w_gate, ffn_w_up, ffn_w_down, ev_w_in, ev_w_out, diff_lambda_q1, diff_lambda_k1, diff_lambda_q2, diff_lambda_k2, diff_subln, od_w_in, od_w_out, nsa_gate_b, nsa_pe_k, nsa_pe_v, nsa_k_w1, nsa_k_w2, nsa_v_w1, nsa_v_w2, mla_q_norm, mla_w_uq, mla_kv_norm, mla_w_ukv):
    b, s, d = x.shape
    depth = attn_norm.shape[0]
    rope16 = _rope_tables(positions, ROPE_DIM, HEAD_DIM)
    rope32s = _rope_tables(positions, MLA_ROPE, LANES, offset=MLA_NOPE)
    h = x.reshape(b * s, d)
    for l in range(depth):
        i = l // 2
        if l % 2 == 0:
            h = _even_mixer(h, attn_norm[l], l, rope16, ev_w_in[i], ev_w_out[i], diff_lambda_q1[i],
                            diff_lambda_k1[i], diff_lambda_q2[i], diff_lambda_k2[i], diff_subln[i], b, s)
        else:
            h = _odd_mixer(h, attn_norm[l], positions, rope16, rope32s, od_w_in[i], od_w_out[i], nsa_gate_b[i],
                           nsa_pe_k[i], nsa_pe_v[i], nsa_k_w1[i], nsa_k_w2[i], nsa_v_w1[i], nsa_v_w2[i],
                           mla_q_norm[i], mla_w_uq[i], mla_kv_norm[i], mla_w_ukv[i], b, s)
        h = _ffn(h, ffn_norm[l], ffn_w_gate[l].astype(BF16), ffn_w_up[l].astype(BF16), ffn_w_down[l].astype(BF16),
                 final_norm, final_norm=(l == depth - 1))
    return h.reshape(b, s, d)
```

```python
import functools
import math

import jax
import jax.numpy as jnp
from jax import lax
from jax.experimental import pallas as pl
from jax.experimental.pallas import tpu as pltpu

F32 = jnp.float32
BF16 = jnp.bfloat16

D_MODEL = 1024
HEAD_DIM = 64
ROPE_THETA = 500000.0
ROPE_DIM = HEAD_DIM // 4
NORM_EPS = 1e-5
D_FF = 2816

MOBA_HEADS = 8
MOBA_BLOCK = 256
MOBA_TOPK = 3
DIFF_HEADS = 4
DIFF_V = 2 * HEAD_DIM
NSA_HEADS = 8
NSA_GROUPS = 2
NSA_REP = NSA_HEADS // NSA_GROUPS
NSA_CMP_LEN = 32
NSA_CMP_STRIDE = 16
NSA_CMP_HIDDEN = 256
NSA_SEL_BLOCK = 64
NSA_SEL_TOPK = 16
NSA_WINDOW = 512
NSA_FORCE_BONUS = 1e3
MLA_HEADS = 8
MLA_Q_RANK = 256
MLA_KV_RANK = 128
MLA_NOPE = 64
MLA_ROPE = 32
MLA_V = 64

LANES = 128
MOBA_BIAS_LANES = 32
NEG = -1e30
M_INIT = -1e37
LOG2E = 1.4426950408889634
UNROLL = 4
VMEM_LIMIT = 56 * 1024 * 1024

_NT = (((1,), (1,)), ((), ()))


def _cparams(sem):
    return pltpu.CompilerParams(dimension_semantics=sem, vmem_limit_bytes=VMEM_LIMIT)


def _rope_tables(positions, dim, period, offset=0):
    r = dim // 2
    inv = 1.0 / (ROPE_THETA ** (jnp.arange(0, dim, 2, dtype=F32) / dim))
    ang = positions.astype(F32)[..., None] * inv
    cos, sin = jnp.cos(ang), jnp.sin(ang)
    const = lambda n, val: jnp.full(ang.shape[:-1] + (n,), val, F32)
    rest = period - offset - 2 * r
    zr = jnp.zeros_like(sin)
    c = jnp.concatenate([const(offset, 1.0), cos, cos, const(rest, 1.0)], -1)
    s1 = jnp.concatenate([const(offset, 0.0), -sin, zr, const(rest, 0.0)], -1)
    s2 = jnp.concatenate([const(offset, 0.0), zr, sin, const(rest, 0.0)], -1)
    reps = LANES // period
    tile = lambda t: jnp.tile(t, (1,) * (t.ndim - 1) + (reps,)).reshape(-1, LANES)
    return tile(c), tile(s1), tile(s2)


def _apply_rope(y, c, s1, s2, shift):
    return y * c + pltpu.roll(y, LANES - shift, 1) * s1 + pltpu.roll(y, shift, 1) * s2


def _proj_body(*refs, has_norm, has_x2, has_rope, has_res, add_cols, rope_cols, shift, chunk):
    it = iter(refs)
    x_ref = next(it)
    x2_ref = next(it) if has_x2 else None
    g_ref = next(it) if has_norm else None
    w_ref = next(it)
    if has_rope:
        c_ref, s1_ref, s2_ref = next(it), next(it), next(it)
    res_ref = next(it) if has_res else None
    add_ref = next(it) if add_cols else None
    o_ref = next(it)
    n = w_ref.shape[1]
    if has_norm:
        xf = x_ref[...].astype(F32)
        y = xf * lax.rsqrt(jnp.mean(xf * xf, axis=-1, keepdims=True) + NORM_EPS)
        xb = (y * g_ref[...]).astype(BF16)
    else:
        xb = x_ref[...].astype(BF16)
    k1 = xb.shape[1]
    for c0 in range(0, n, chunk):
        cw = min(chunk, n - c0)
        y = jnp.dot(xb, w_ref[0:k1, c0:c0 + cw], preferred_element_type=F32)
        if has_x2:
            y = y + jnp.dot(x2_ref[...].astype(BF16), w_ref[k1:, c0:c0 + cw], preferred_element_type=F32)
        if has_res:
            y = y + res_ref[:, c0:c0 + cw]
        if (has_rope and c0 < rope_cols) or c0 < add_cols:
            for k0 in range(0, cw, LANES):
                ys = y[:, k0:k0 + LANES]
                if has_rope and c0 + k0 < rope_cols:
                    ys = _apply_rope(ys, c_ref[...], s1_ref[...], s2_ref[...], shift)
                if c0 + k0 < add_cols:
                    ys = ys + add_ref[...].astype(F32)
                o_ref[:, c0 + k0:c0 + k0 + LANES] = ys.astype(o_ref.dtype)
        else:
            o_ref[:, c0:c0 + cw] = y.astype(o_ref.dtype)


def _proj(x, w, *, name, x_blk=None, x2=None, gain=None, rope=None, rope_cols=0, shift=0, res=None, add=None,
          add_cols=0, out_dtype=F32, tm=512, chunk=512):
    t = x.shape[0]
    k, xj = (x.shape[1], 0) if x_blk is None else x_blk
    n = w.shape[1]
    assert t % tm == 0 and n % LANES == 0 and rope_cols % LANES == 0 and add_cols % LANES == 0
    has_norm, has_rope, has_res, has_x2 = gain is not None, rope is not None, res is not None, x2 is not None
    assert not (has_norm and has_x2)
    args, specs = [x], [pl.BlockSpec((tm, k), lambda i: (i, xj))]
    if has_x2:
        args.append(x2)
        specs.append(pl.BlockSpec((tm, x2.shape[1]), lambda i: (i, 0)))
    if has_norm:
        args.append(gain.reshape(1, k).astype(F32))
        specs.append(pl.BlockSpec((1, k), lambda i: (0, 0)))
    args.append(w)
    specs.append(pl.BlockSpec((w.shape[0], n), lambda i: (0, 0)))
    if has_rope:
        for tb in rope:
            args.append(tb)
            specs.append(pl.BlockSpec((tm, LANES), lambda i: (i, 0)))
    if has_res:
        args.append(res)
        specs.append(pl.BlockSpec((tm, n), lambda i: (i, 0)))
    if add_cols:
        args.append(add)
        specs.append(pl.BlockSpec((tm, LANES), lambda i: (i, 0)))
    body = functools.partial(_proj_body, has_norm=has_norm, has_x2=has_x2, has_rope=has_rope, has_res=has_res,
                             add_cols=add_cols,
                             rope_cols=rope_cols, shift=shift, chunk=chunk)
    return pl.pallas_call(
        body, name=name, grid=(t // tm,), in_specs=specs,
        out_specs=pl.BlockSpec((tm, n), lambda i: (i, 0)),
        out_shape=jax.ShapeDtypeStruct((t, n), out_dtype),
        compiler_params=_cparams(("parallel",)),
    )(*args)


def _ffn_body(x_ref, g_ref, wg_ref, wu_ref, wd_ref, fg_ref, o_ref, xn_ref, acc_ref, *, final_norm):
    j = pl.program_id(1)

    @pl.when(j == 0)
    def _():
        xf = x_ref[...]
        y = xf * lax.rsqrt(jnp.mean(xf * xf, axis=-1, keepdims=True) + NORM_EPS)
        xn_ref[...] = (y * g_ref[...]).astype(BF16)
        acc_ref[...] = jnp.zeros_like(acc_ref)

    xn = xn_ref[...]
    g = jnp.dot(xn, wg_ref[...], preferred_element_type=F32)
    u = jnp.dot(xn, wu_ref[...], preferred_element_type=F32)
    a = (jax.nn.silu(g) * u).astype(BF16)
    acc_ref[...] += jnp.dot(a, wd_ref[...], preferred_element_type=F32)

    @pl.when(j == pl.num_programs(1) - 1)
    def _():
        h = x_ref[...] + acc_ref[...]
        if final_norm:
            y = h * lax.rsqrt(jnp.mean(h * h, axis=-1, keepdims=True) + NORM_EPS)
            h = y * fg_ref[...]
        o_ref[...] = h


def _ffn(x, gain, wg, wu, wd, final_gain, *, final_norm, tm=512, tf=1408):
    t, d = x.shape
    f = wg.shape[1]
    assert t % tm == 0 and f % tf == 0
    return pl.pallas_call(
        functools.partial(_ffn_body, final_norm=final_norm), name="ffn",
        grid=(t // tm, f // tf),
        in_specs=[
            pl.BlockSpec((tm, d), lambda i, j: (i, 0)),
            pl.BlockSpec((1, d), lambda i, j: (0, 0)),
            pl.BlockSpec((d, tf), lambda i, j: (0, j)),
            pl.BlockSpec((d, tf), lambda i, j: (0, j)),
            pl.BlockSpec((tf, d), lambda i, j: (j, 0)),
            pl.BlockSpec((1, d), lambda i, j: (0, 0)),
        ],
        out_specs=pl.BlockSpec((tm, d), lambda i, j: (i, 0)),
        out_shape=jax.ShapeDtypeStruct((t, d), F32),
        scratch_shapes=[pltpu.VMEM((tm, d), BF16), pltpu.VMEM((tm, d), F32)],
        compiler_params=_cparams(("parallel", "arbitrary")),
    )(x, gain.reshape(1, d).astype(F32), wg, wu, wd, final_gain.reshape(1, d).astype(F32))


def _sweep(q, k_get, vt_get, s_ref, *, q0, tq, tk, dv, scale):
    cols = q.shape[0]

    def fold8(x, op):
        return op(x.reshape(x.shape[0] // 8, 8, cols), axis=0)

    def score_block(j0, n, mrun, diag_last):
        ks = pl.multiple_of(j0 * tk, tk)
        s = lax.dot_general(k_get(ks, n * tk), q, _NT, preferred_element_type=F32) * (scale * LOG2E)
        for u in range(n):
            su = s[u * tk:(u + 1) * tk]
            if diag_last and u == n - 1:
                kpos = ks + u * tk + lax.broadcasted_iota(jnp.int32, (tk, cols), 0)
                qpos = q0 + lax.rem(lax.broadcasted_iota(jnp.int32, (tk, cols), 1), tq)
                su = jnp.where(kpos <= qpos, su, NEG)
            s_ref[j0 + u] = su
            mrun = jnp.maximum(mrun, fold8(su, jnp.max))
        return mrun

    def pv_block(j0, n, carry, m):
        l8, acc = carry
        for u in range(n):
            p = jnp.exp2(s_ref[j0 + u] - m)
            l8 = l8 + fold8(p, jnp.sum)
            acc = acc + jnp.dot(vt_get(j0 + u), p.astype(BF16), preferred_element_type=F32)
        return l8, acc

    m_init = jnp.full((8, cols), M_INIT, F32)
    acc_init = (jnp.zeros((8, cols), F32), jnp.zeros((dv, cols), F32))
    n_full = q0 // tk
    groups = n_full // UNROLL
    rem = n_full - groups * UNROLL
    tail0 = groups * UNROLL

    mrun = lax.fori_loop(0, groups, lambda i, t: score_block(UNROLL * i, UNROLL, t, False), m_init)
    mrun = lax.switch(rem, [functools.partial(score_block, tail0, r + 1, diag_last=True) for r in range(UNROLL)],
                      mrun)
    m = jnp.max(mrun, axis=0, keepdims=True)
    carry = lax.fori_loop(0, groups, lambda i, t: pv_block(UNROLL * i, UNROLL, t, m), acc_init)
    l8, acc = lax.switch(rem, [functools.partial(pv_block, tail0, r + 1, m=m) for r in range(UNROLL)], carry)
    l = jnp.sum(l8, axis=0, keepdims=True)
    return acc / jnp.maximum(l, 1e-30)


def _head_lanes(mode, v, lane):
    d = HEAD_DIM
    if mode == "diff":
        return jnp.logical_and(lane >= d * v, lane < d * (v + 1))
    if mode == "moba":
        lo = 2 * d + v * MOBA_BIAS_LANES
        return jnp.logical_or(jnp.logical_and(lane >= d * v, lane < d * (v + 1)),
                              jnp.logical_and(lane >= lo, lane < lo + MOBA_BIAS_LANES))
    return jnp.logical_and(lane >= LANES * v, lane < LANES * (v + 1))


def _pflash_body(*refs, mode, tq, tk, scale, lambda_init):
    if mode == "diff":
        q_ref, k_ref, v_ref, lq1_ref, lk1_ref, lq2_ref, lk2_ref, sg_ref, o_ref, s_ref, vt_ref = refs
    else:
        q_ref, k_ref, v_ref, o_ref, s_ref, vt_ref = refs
    qi = pl.program_id(1)
    n_chunks = v_ref.shape[1] // tk

    @pl.when(qi == 0)
    def _():
        for c in range(n_chunks):
            vt_ref[c] = v_ref[0, c * tk:(c + 1) * tk, :].astype(F32).T.astype(BF16)

    q = q_ref[0]
    lane = lax.broadcasted_iota(jnp.int32, q.shape, 1)
    d = HEAD_DIM
    outs = []
    for v in range(2):
        qv = jnp.where(_head_lanes(mode, v, lane), q, jnp.zeros_like(q))
        if mode == "diff":
            vt_get, dv = (lambda j: vt_ref[j]), LANES
        else:
            vt_get, dv = (lambda j, v=v: vt_ref[j, v * d:(v + 1) * d, :]), d
        outs.append(_sweep(qv, lambda ks, n: k_ref[0, pl.ds(ks, n), :], vt_get, s_ref,
                           q0=qi * tq, tq=tq, tk=tk, dv=dv, scale=scale))
    if mode == "diff":
        lam = (jnp.exp(jnp.sum(lq1_ref[...] * lk1_ref[...], axis=-1, keepdims=True))
               - jnp.exp(jnp.sum(lq2_ref[...] * lk2_ref[...], axis=-1, keepdims=True)) + lambda_init)
        d = outs[0] - lam * outs[1]
        y = d * lax.rsqrt(jnp.mean(d * d, axis=0, keepdims=True) + NORM_EPS)
        o_t = (y * sg_ref[...]) * (1.0 - lambda_init)
    else:
        o_t = jnp.concatenate(outs, axis=0)
    o_ref[0] = o_t.T.astype(o_ref.dtype)


def _pflash(q, k, v, *, name, mode, dk, q_map, k_map, v_map, batch, pairs, tq=512, tk=512, scale=1.0,
            diff_params=None, lambda_init=0.0):
    s = v.shape[1]
    assert s % tk == 0 and tq == tk
    args = [q, k, v]
    in_specs = [pl.BlockSpec((1, tq, dk), q_map), pl.BlockSpec((1, s, dk), k_map),
                pl.BlockSpec((1, s, LANES), v_map)]
    if mode == "diff":
        lq1, lk1, lq2, lk2, subln = diff_params
        for a in (lq1, lk1, lq2, lk2):
            args.append(a.reshape(1, HEAD_DIM).astype(F32))
            in_specs.append(pl.BlockSpec((1, HEAD_DIM), lambda i, j: (0, 0)))
        args.append(subln.reshape(LANES, 1).astype(F32))
        in_specs.append(pl.BlockSpec((LANES, 1), lambda i, j: (0, 0)))
    body = functools.partial(_pflash_body, mode=mode, tq=tq, tk=tk, scale=scale, lambda_init=lambda_init)
    return pl.pallas_call(
        body, name=name, grid=(batch * pairs, s // tq), in_specs=in_specs,
        out_specs=pl.BlockSpec((1, tq, LANES), lambda i, j: (i // pairs, j, i % pairs)),
        out_shape=jax.ShapeDtypeStruct((batch, s, pairs * LANES), BF16),
        scratch_shapes=[pltpu.VMEM((s // tk, tk, tq), F32), pltpu.VMEM((s // tk, LANES, tk), BF16)],
        compiler_params=_cparams(("parallel", "arbitrary")),
    )(*args)


def _band(q, ka_ref, vt_ref, *, q0, tq, window):
    cols = q.shape[0]
    band = window + tq
    start = pl.multiple_of(jnp.maximum(q0 - window, 0), tq)
    s = lax.dot_general(ka_ref[pl.ds(start, band), :], q, _NT, preferred_element_type=F32) * LOG2E
    kpos = start + lax.broadcasted_iota(jnp.int32, (band, cols), 0)
    qpos = q0 + lax.rem(lax.broadcasted_iota(jnp.int32, (band, cols), 1), tq)
    ok = jnp.logical_and(kpos <= qpos, kpos > qpos - window)
    s = jnp.where(ok, s, NEG)
    p = jnp.exp2(s - jnp.max(s, axis=0, keepdims=True))
    l = jnp.sum(p, axis=0, keepdims=True)
    pb = p.astype(BF16)
    acc = None
    for c in range(band // tq):
        part = jnp.dot(vt_ref[start // tq + c], pb[c * tq:(c + 1) * tq], preferred_element_type=F32)
        acc = part if acc is None else acc + part
    return acc / jnp.maximum(l, 1e-30)


def _gflash_body(*refs, tq, tk, rep, groups, window, select):
    if select:
        q_ref, k_ref, v_ref, mb_ref, o_ref, s_ref, vt_ref, ka_ref, qa_ref = refs
    else:
        q_ref, k_ref, v_ref, o_ref, vt_ref, ka_ref, qa_ref = refs
    d = HEAD_DIM
    tv = vt_ref.shape[2]
    g = pl.program_id(0) % groups
    qi = pl.program_id(1)
    n_chunks = v_ref.shape[1] // tk

    @pl.when(qi == 0)
    def _():
        move = _place(LANES, LANES, g * d)
        for c in range(n_chunks):
            rows = slice(c * tk, (c + 1) * tk)
            vt = v_ref[0, rows, :].astype(F32).T
            vt = jnp.where(g == 0, vt[0:d], vt[d:2 * d]).astype(BF16)
            for u in range(tk // tv):
                vt_ref[c * (tk // tv) + u] = vt[:, u * tv:(u + 1) * tv]
            kg = jnp.dot(k_ref[0, rows, :], move, preferred_element_type=F32).astype(BF16)
            if select:
                blk = (c * tk + lax.broadcasted_iota(jnp.int32, (tk, LANES), 0)) // NSA_SEL_BLOCK
                lane = lax.broadcasted_iota(jnp.int32, (tk, LANES), 1)
                ka_ref[rows, 0:LANES] = jnp.where(blk == lane, 1.0, 0.0).astype(BF16)
                ka_ref[rows, LANES:2 * LANES] = kg
            else:
                ka_ref[rows, :] = kg

    q = q_ref[0]
    for r in range(rep):
        qr = jnp.dot(q, _place(rep * d, LANES, r * d, d ** -0.5), preferred_element_type=F32).astype(BF16)
        rows = slice(r * tq, (r + 1) * tq)
        if select:
            qa_ref[rows, 0:LANES] = mb_ref[0]
            qa_ref[rows, LANES:2 * LANES] = qr
        else:
            qa_ref[rows, :] = qr
    if window is None:
        o_t = _sweep(qa_ref[...], lambda ks, n: ka_ref[pl.ds(ks, n), :], lambda j: vt_ref[j], s_ref,
                     q0=qi * tq, tq=tq, tk=tk, dv=d, scale=1.0)
    else:
        o_t = _band(qa_ref[...], ka_ref, vt_ref, q0=qi * tq, tq=tq, window=window)
    for r in range(rep):
        o_ref[0, :, r * d:(r + 1) * d] = o_t[:, r * tq:(r + 1) * tq].T.astype(o_ref.dtype)


def _gflash(yq, yk, yv, mbias, *, name, q_blk, k_blk, v_blk, tq=128, tk=512, window=None):
    b, s, _ = yq.shape
    rep, groups, d = NSA_REP, NSA_GROUPS, HEAD_DIM
    assert groups * d == LANES and s % tk == 0 and tk % tq == 0
    select = mbias is not None
    dk = 2 * LANES if select else LANES
    assert window is None or (not select and window % tq == 0 and s >= window + tq)
    tv = tk if window is None else tq
    scratch = [pltpu.VMEM((s // tv, d, tv), BF16), pltpu.VMEM((s, dk), BF16), pltpu.VMEM((rep * tq, dk), BF16)]
    if window is None:
        scratch.insert(0, pltpu.VMEM((s // tk, tk, rep * tq), F32))
    args = [yq, yk, yv]
    in_specs = [pl.BlockSpec((1, tq, rep * d), lambda i, j: (i // groups, j, q_blk + i % groups)),
                pl.BlockSpec((1, s, LANES), lambda i, j: (i // groups, 0, k_blk)),
                pl.BlockSpec((1, s, LANES), lambda i, j: (i // groups, 0, v_blk))]
    if select:
        args.append(mbias)
        in_specs.append(pl.BlockSpec((1, tq, LANES), lambda i, j: (i, j, 0)))
    body = functools.partial(_gflash_body, tq=tq, tk=tk, rep=rep, groups=groups, window=window, select=select)
    return pl.pallas_call(
        body, name=name, grid=(b * groups, s // tq), in_specs=in_specs,
        out_specs=pl.BlockSpec((1, tq, rep * d), lambda i, j: (i // groups, j, i % groups)),
        out_shape=jax.ShapeDtypeStruct((b, s, groups * rep * d), F32),
        scratch_shapes=scratch,
        compiler_params=_cparams(("parallel", "arbitrary")),
    )(*args)


def _topk_mask(score, row, k):
    sel = None
    for _ in range(k):
        mx = jnp.max(score, axis=0, keepdims=True)
        idx = jnp.min(jnp.where(score == mx, row, LANES), axis=0, keepdims=True)
        hit = row == idx
        pick = jnp.logical_and(hit, mx > -jnp.inf)
        sel = pick if sel is None else jnp.logical_or(sel, pick)
        score = jnp.where(hit, -jnp.inf, score)
    return sel


def _moba_gate_body(q_ref, k_ref, qa_ref, ka_ref, kmean_ref, *, tq, seq):
    qi = pl.program_id(1)
    nbl = MOBA_BIAS_LANES

    @pl.when(qi == 0)
    def _():
        rowblk = lax.rem(lax.broadcasted_iota(jnp.int32, (LANES, seq), 0), nbl)
        colblk = lax.broadcasted_iota(jnp.int32, (LANES, seq), 1) // MOBA_BLOCK
        ind = jnp.where(rowblk == colblk, 1.0, 0.0).astype(BF16)
        ksum = jnp.dot(ind, k_ref[0], preferred_element_type=F32)
        rhead = lax.broadcasted_iota(jnp.int32, (LANES, LANES), 0) // nbl
        lhead = lax.broadcasted_iota(jnp.int32, (LANES, LANES), 1) // HEAD_DIM
        kmean_ref[...] = jnp.where(rhead == lhead, ksum * (1.0 / MOBA_BLOCK), 0.0)

    q = q_ref[0]
    q0 = pl.multiple_of(qi * tq, tq)
    km = kmean_ref[...]
    km_hi = km.astype(BF16)
    km_lo = (km - km_hi.astype(F32)).astype(BF16)
    gate = (lax.dot_general(km_hi, q, _NT, preferred_element_type=F32)
            + lax.dot_general(km_lo, q, _NT, preferred_element_type=F32))
    blk_t = lax.broadcasted_iota(jnp.int32, (nbl, tq), 0)
    own_t = (q0 + lax.broadcasted_iota(jnp.int32, (nbl, tq), 1)) // MOBA_BLOCK
    parts = []
    for v in range(2):
        g_v = jnp.where(blk_t < own_t, gate[v * nbl:(v + 1) * nbl], -jnp.inf)
        sel = jnp.logical_or(_topk_mask(g_v, blk_t, MOBA_TOPK), blk_t == own_t)
        parts.append(jnp.where(sel, 0.0, NEG))
    parts.append(jnp.zeros((LANES - 2 * nbl, tq), F32))
    bias = jnp.concatenate(parts, axis=0).T
    qa_ref[0, :, 0:LANES] = q * (HEAD_DIM ** -0.5)
    qa_ref[0, :, LANES:2 * LANES] = bias.astype(BF16)
    lane = lax.broadcasted_iota(jnp.int32, (tq, LANES), 1)
    own = (q0 + lax.broadcasted_iota(jnp.int32, (tq, LANES), 0)) // MOBA_BLOCK
    onehot = jnp.where(jnp.logical_and(lane < 2 * nbl, lax.rem(lane, nbl) == own), 1.0, 0.0)
    ka_ref[0, :, 0:LANES] = k_ref[0, pl.ds(q0, tq), :]
    ka_ref[0, :, LANES:2 * LANES] = onehot.astype(BF16)


def _moba_gate(y3, *, q_blk, k_blk, pairs, tq=512):
    b, s, _ = y3.shape
    assert s % MOBA_BLOCK == 0 and s // MOBA_BLOCK <= MOBA_BIAS_LANES and s % tq == 0
    out = jax.ShapeDtypeStruct((b * pairs, s, 2 * LANES), BF16)
    return pl.pallas_call(
        functools.partial(_moba_gate_body, tq=tq, seq=s), name="moba_gate",
        grid=(b * pairs, s // tq),
        in_specs=[pl.BlockSpec((1, tq, LANES), lambda i, j: (i // pairs, j, q_blk + i % pairs)),
                  pl.BlockSpec((1, s, LANES), lambda i, j: (i // pairs, 0, k_blk + i % pairs))],
        out_specs=[pl.BlockSpec((1, tq, 2 * LANES), lambda i, j: (i, j, 0)),
                   pl.BlockSpec((1, tq, 2 * LANES), lambda i, j: (i, j, 0))],
        out_shape=[out, out],
        scratch_shapes=[pltpu.VMEM((LANES, LANES), F32)],
        compiler_params=_cparams(("parallel", "arbitrary")),
    )(y3, y3)


def _nsa_compress_body(x_ref, pe_ref, w1_ref, w2_ref, c_ref, s1_ref, s2_ref, o_ref, *, rope):
    half = w1_ref.shape[0] // 2
    x = x_ref[0].astype(F32)
    lo = (x + pe_ref[:, 0:half]).astype(BF16)
    hi = (x + pe_ref[:, half:2 * half]).astype(BF16)
    a = jnp.dot(lo, w1_ref[0:half, :], preferred_element_type=F32)
    b = jnp.dot(hi, w1_ref[half:2 * half, :], preferred_element_type=F32)
    n = a.shape[0]
    h1 = a + pltpu.roll(b, n - 1, 0)
    y = jnp.dot(jax.nn.gelu(h1).astype(BF16), w2_ref[...], preferred_element_type=F32)
    if rope:
        y = _apply_rope(y, c_ref[0], s1_ref[0], s2_ref[0], ROPE_DIM // 2)
    o_ref[0] = y[:, 0:HEAD_DIM].astype(o_ref.dtype)


def _nsa_compress(xr, pe, w1, w2, tables, *, rope, groups):
    g, n, w = xr.shape
    hid = w1.shape[1]
    w2p = jnp.pad(w2, ((0, 0), (0, LANES - w2.shape[1])))
    tspec = pl.BlockSpec((1, n, LANES), lambda i: (i // groups, 0, 0))
    return pl.pallas_call(
        functools.partial(_nsa_compress_body, rope=rope), name="nsa_compress",
        grid=(g,),
        in_specs=[pl.BlockSpec((1, n, w), lambda i: (i, 0, 0)),
                  pl.BlockSpec((1, 2 * w), lambda i: (0, 0)),
                  pl.BlockSpec((2 * w, hid), lambda i: (0, 0)),
                  pl.BlockSpec((hid, LANES), lambda i: (0, 0)),
                  tspec, tspec, tspec],
        out_specs=pl.BlockSpec((1, n, HEAD_DIM), lambda i: (i, 0, 0)),
        out_shape=jax.ShapeDtypeStruct((g, n, HEAD_DIM), BF16),
        compiler_params=_cparams(("parallel",)),
    )(xr, pe.reshape(1, 2 * w).astype(F32), w1, w2p, *tables)


def _place(n_src, n_dst, shift, value=1.0):
    src = lax.broadcasted_iota(jnp.int32, (n_src, n_dst), 0)
    dst = lax.broadcasted_iota(jnp.int32, (n_src, n_dst), 1)
    return jnp.where(jnp.logical_and(src == dst + shift, dst < HEAD_DIM), value, 0.0).astype(BF16)


def _nsa_cmp_body(q_ref, kc_ref, vc_ref, oc_ref, mb_ref, kcp_ref, *, tq, rep):
    qi = pl.program_id(1)
    q0 = qi * tq
    nc = kc_ref.shape[1]
    width = q_ref.shape[-1]

    @pl.when(qi == 0)
    def _():
        for r in range(rep):
            src = lax.broadcasted_iota(jnp.int32, (HEAD_DIM, width), 0)
            dst = lax.broadcasted_iota(jnp.int32, (HEAD_DIM, width), 1)
            spread = jnp.where(dst == src + r * HEAD_DIM, HEAD_DIM ** -0.5, 0.0).astype(BF16)
            kcp_ref[r] = jnp.dot(kc_ref[0], spread, preferred_element_type=F32).astype(BF16)

    q = q_ref[0]
    vct = vc_ref[0]
    tpos = q0 + lax.broadcasted_iota(jnp.int32, (nc, tq), 1)
    cend = lax.broadcasted_iota(jnp.int32, (nc, tq), 0) * NSA_CMP_STRIDE + (NSA_CMP_LEN - 1)
    ok = cend <= tpos
    psum = jnp.zeros((nc, tq), F32)
    for r in range(rep):
        s = lax.dot_general(kcp_ref[r], q, _NT, preferred_element_type=F32)
        s = jnp.where(ok, s, NEG)
        m = jnp.max(s, axis=0, keepdims=True)
        e = jnp.where(ok, jnp.exp(s - m), 0.0)
        p = e * (1.0 / jnp.maximum(jnp.sum(e, axis=0, keepdims=True), 1e-30))
        o_t = jnp.dot(vct, p.astype(BF16), preferred_element_type=F32)
        oc_ref[0, :, r * HEAD_DIM:(r + 1) * HEAD_DIM] = o_t.T.astype(oc_ref.dtype)
        psum = psum + p
    sstart = lax.broadcasted_iota(jnp.int32, (LANES, nc), 0) * NSA_SEL_BLOCK
    cstart = lax.broadcasted_iota(jnp.int32, (LANES, nc), 1) * NSA_CMP_STRIDE
    ov = jnp.where(jnp.logical_and(cstart < sstart + NSA_SEL_BLOCK, cstart + NSA_CMP_LEN > sstart),
                   1.0, 0.0).astype(BF16)
    p_hi = psum.astype(BF16)
    r1 = psum - p_hi.astype(F32)
    p_mid = r1.astype(BF16)
    p_lo = (r1 - p_mid.astype(F32)).astype(BF16)
    imp = (jnp.dot(ov, p_hi, preferred_element_type=F32) + jnp.dot(ov, p_mid, preferred_element_type=F32)
           + jnp.dot(ov, p_lo, preferred_element_type=F32))
    blk = lax.broadcasted_iota(jnp.int32, (LANES, tq), 0)
    qb = (q0 + lax.broadcasted_iota(jnp.int32, (LANES, tq), 1)) // NSA_SEL_BLOCK
    forced = jnp.logical_or(blk == 0, jnp.logical_or(blk == qb, blk == qb - 1))
    imp = jnp.where(forced, imp + NSA_FORCE_BONUS, imp)
    imp = jnp.where(blk <= qb, imp, -jnp.inf)
    sel = _topk_mask(imp, blk, NSA_SEL_TOPK)
    mb_ref[0] = jnp.where(sel, 0.0, NEG).T.astype(mb_ref.dtype)


def _nsa_cmp(y3, kc, vc, *, tq=512):
    b, s, _ = y3.shape
    g, nc, d = kc.shape
    rep, n = NSA_REP, NSA_GROUPS
    assert s // NSA_SEL_BLOCK <= LANES and s % tq == 0
    return pl.pallas_call(
        functools.partial(_nsa_cmp_body, tq=tq, rep=rep), name="nsa_cmp",
        grid=(g, s // tq),
        in_specs=[pl.BlockSpec((1, tq, rep * d), lambda i, j: (i // n, j, i % n)),
                  pl.BlockSpec((1, nc, d), lambda i, j: (i, 0, 0)),
                  pl.BlockSpec((1, d, nc), lambda i, j: (i, 0, 0))],
        out_specs=[pl.BlockSpec((1, tq, rep * d), lambda i, j: (i // n, j, i % n)),
                   pl.BlockSpec((1, tq, LANES), lambda i, j: (i, j, 0))],
        out_shape=[jax.ShapeDtypeStruct((b, s, n * rep * d), F32),
                   jax.ShapeDtypeStruct((g, s, LANES), BF16)],
        scratch_shapes=[pltpu.VMEM((rep, nc, rep * d), BF16)],
        compiler_params=_cparams(("parallel", "arbitrary")),
    )(y3, kc, vc)


def _nsa_combine_body(oc_ref, os_ref, ow_ref, g_ref, b_ref, e_ref, o_ref):
    gs = jax.nn.sigmoid(g_ref[...] + b_ref[...])
    g_hi = gs.astype(BF16)
    g_lo = (gs - g_hi.astype(F32)).astype(BF16)
    out = None
    for i, ref in enumerate((oc_ref, os_ref, ow_ref)):
        w = (jnp.dot(g_hi, e_ref[i], preferred_element_type=F32)
             + jnp.dot(g_lo, e_ref[i], preferred_element_type=F32))
        term = w * ref[...]
        out = term if out is None else out + term
    o_ref[...] = out.astype(o_ref.dtype)


def _nsa_combine(oc, osel, ow, yg, g_blk, gate_b, *, tm=512):
    t, n = oc.shape
    nh = NSA_HEADS * 3
    bp = jnp.pad(gate_b.reshape(1, nh).astype(F32), ((0, 0), (0, LANES - nh)))
    row = jnp.arange(LANES)[:, None]
    col = jnp.arange(n)[None, :] // HEAD_DIM
    expand = jnp.stack([(row == col * 3 + i) for i in range(3)]).astype(BF16)
    tok = pl.BlockSpec((tm, n), lambda i: (i, 0))
    return pl.pallas_call(
        _nsa_combine_body, name="nsa_combine", grid=(t // tm,),
        in_specs=[tok, tok, tok, pl.BlockSpec((tm, LANES), lambda i: (i, g_blk)),
                  pl.BlockSpec((1, LANES), lambda i: (0, 0)),
                  pl.BlockSpec((3, LANES, n), lambda i: (0, 0, 0))],
        out_specs=tok,
        out_shape=jax.ShapeDtypeStruct((t, n), BF16),
        compiler_params=_cparams(("parallel",)),
    )(oc, osel, ow, yg, bp, expand)


def _to_heads(x, b, s, n):
    return x.reshape(b, s, n, -1).transpose(0, 2, 1, 3).reshape(b * n, s, -1)


def _even_mixer(h, gain, layer_idx, rope16, w_in, w_out, lq1, lk1, lq2, lk2, subln, b, s):
    na = MOBA_HEADS * HEAD_DIM
    nb = DIFF_HEADS * 2 * HEAD_DIM
    w_rope = jnp.concatenate([w_in[:, 0:2 * na], w_in[:, 3 * na:3 * na + 2 * nb]], axis=1)
    w_rest = jnp.concatenate([w_in[:, 2 * na:3 * na], w_in[:, 3 * na + 2 * nb:]], axis=1)
    w = jnp.concatenate([w_rope, w_rest], axis=1).astype(BF16)
    y = _proj(h, w, name="proj_even_in", gain=gain, rope=rope16, rope_cols=w_rope.shape[1], shift=ROPE_DIM // 2, out_dtype=BF16)
    y3 = y.reshape(b, s, -1)
    blk = lambda off: off // LANES
    pairs = na // LANES
    qa, ka = _moba_gate(y3, q_blk=blk(0), k_blk=blk(na), pairs=pairs)
    o_a = _pflash(qa, ka, y3, name="flash_moba", mode="moba", dk=2 * LANES, batch=b, pairs=pairs,
                  q_map=lambda i, j: (i, j, 0), k_map=lambda i, j: (i, 0, 0),
                  v_map=lambda i, j: (i // pairs, 0, blk(2 * na + 2 * nb) + i % pairs))
    lambda_init = 0.8 - 0.6 * math.exp(-0.3 * layer_idx)
    nh = DIFF_HEADS
    o_b = _pflash(y3, y3, y3, name="flash_diff", mode="diff", dk=LANES, batch=b, pairs=nh, scale=HEAD_DIM ** -0.5,
                  q_map=lambda i, j: (i // nh, j, blk(2 * na) + i % nh),
                  k_map=lambda i, j: (i // nh, 0, blk(2 * na + nb) + i % nh),
                  v_map=lambda i, j: (i // nh, 0, blk(3 * na + 2 * nb) + i % nh),
                  diff_params=(lq1, lk1, lq2, lk2, subln), lambda_init=lambda_init)
    t = b * s
    return _proj(o_a.reshape(t, -1), w_out.astype(BF16), name="proj_out", x2=o_b.reshape(t, -1), res=h, out_dtype=F32)


def _odd_mixer(h, gain, positions, rope16, rope32s, w_in, w_out, gate_b, pe_k, pe_v, k_w1, k_w2, v_w1, v_w2,
               q_norm, w_uq, kv_norm, w_ukv, b, s):
    G, R, d = NSA_GROUPS, NSA_REP, HEAD_DIM
    sizes = [NSA_HEADS * d] + [G * d] * 6 + [NSA_HEADS * 3, MLA_Q_RANK, MLA_KV_RANK, MLA_ROPE]
    offs = [0]
    for z in sizes:
        offs.append(offs[-1] + z)
    col = lambda i: w_in[:, offs[i]:offs[i + 1]]
    w_r = jnp.concatenate([col(0), col(3), col(5)], axis=1).astype(BF16)
    yr = _proj(h, w_r, name="proj_odd_rope", gain=gain, rope=rope16, rope_cols=w_r.shape[1], shift=ROPE_DIM // 2, out_dtype=BF16)
    w_p = jnp.concatenate([col(1), col(2), col(4), col(6), col(8), col(9), col(7)], axis=1)
    w_p = jnp.pad(w_p, ((0, 0), (0, (-w_p.shape[1]) % LANES))).astype(BF16)
    yp = _proj(h, w_p, name="proj_odd_plain", gain=gain, out_dtype=F32)
    k_cmp, v_cmp = yp[:, 0:G * d], yp[:, G * d:2 * G * d]
    cq_blk = 4 * G * d // MLA_Q_RANK
    ckv_blk = (4 * G * d + MLA_Q_RANK) // MLA_KV_RANK
    gate_blk = (4 * G * d + MLA_Q_RANK + MLA_KV_RANK) // LANES
    w_pe = jnp.pad(col(10), ((0, 0), (MLA_NOPE, LANES - MLA_NOPE - MLA_ROPE))).astype(BF16)
    k_pe = _proj(h, w_pe, name="proj_mla_kpe", gain=gain, rope=rope32s, rope_cols=LANES, shift=MLA_ROPE // 2,
                 out_dtype=BF16)

    yr3, yp3 = yr.reshape(b, s, -1), yp.reshape(b, s, -1)
    nc = s // NSA_CMP_STRIDE
    cpos = jnp.concatenate([positions[:, NSA_CMP_LEN - 1::NSA_CMP_STRIDE], positions[:, -1:]], axis=1)
    ctab = [t.reshape(b, nc, LANES) for t in _rope_tables(cpos, ROPE_DIM, HEAD_DIM)]
    xk = _to_heads(k_cmp, b, s, G).reshape(b * G, nc, NSA_CMP_STRIDE * d)
    xv = _to_heads(v_cmp, b, s, G).reshape(b * G, nc, NSA_CMP_STRIDE * d)
    kc = _nsa_compress(xk, pe_k, k_w1.astype(BF16), k_w2.astype(BF16), ctab, rope=True, groups=G)
    vc = _nsa_compress(xv, pe_v, v_w1.astype(BF16), v_w2.astype(BF16), ctab, rope=False, groups=G)
    o_c, mbias = _nsa_cmp(yr3, kc, vc.transpose(0, 2, 1))
    k_blk = NSA_HEADS * d // LANES
    o_s = _gflash(yr3, yr3, yp3, mbias, name="flash_sel", q_blk=0, k_blk=k_blk, v_blk=2)
    o_w = _gflash(yr3, yr3, yp3, None, name="flash_win", q_blk=0, k_blk=k_blk + 1, v_blk=3, window=NSA_WINDOW)
    t = b * s
    o_nsa = _nsa_combine(o_c.reshape(t, -1), o_s.reshape(t, -1), o_w.reshape(t, -1), yp, gate_blk, gate_b)

    hq = MLA_NOPE + MLA_ROPE
    nh = MLA_HEADS
    wq = jnp.pad(w_uq.reshape(MLA_Q_RANK, nh, hq), ((0, 0), (0, 0), (0, LANES - hq))).reshape(MLA_Q_RANK, nh * LANES)
    q = _proj(yp, wq.astype(BF16), name="proj_mla_q", x_blk=(MLA_Q_RANK, cq_blk), gain=q_norm, rope=rope32s,
              rope_cols=nh * LANES, shift=MLA_ROPE // 2, out_dtype=BF16)
    wkv = w_ukv.reshape(MLA_KV_RANK, nh, MLA_NOPE + MLA_V)
    wk = jnp.pad(wkv[:, :, :MLA_NOPE], ((0, 0), (0, 0), (0, LANES - MLA_NOPE))).reshape(MLA_KV_RANK, nh * LANES)
    wkv = jnp.concatenate([wk, wkv[:, :, MLA_NOPE:].reshape(MLA_KV_RANK, nh * MLA_V)], axis=1).astype(BF16)
    kv = _proj(yp, wkv, name="proj_mla_kv", x_blk=(MLA_KV_RANK, ckv_blk), gain=kv_norm, add=k_pe, add_cols=nh * LANES,
               out_dtype=BF16)
    q3, kv3 = q.reshape(b, s, -1), kv.reshape(b, s, -1)
    pairs = nh // 2
    o_d = _pflash(q3, kv3, kv3, name="flash_mla", mode="slots", dk=2 * LANES, batch=b, pairs=pairs, scale=hq ** -0.5,
                  q_map=lambda i, j: (i // pairs, j, i % pairs), k_map=lambda i, j: (i // pairs, 0, i % pairs),
                  v_map=lambda i, j: (i // pairs, 0, nh + i % pairs))

    return _proj(o_nsa, w_out.astype(BF16), name="proj_out", x2=o_d.reshape(t, -1), res=h, out_dtype=F32)


def kernel(x, positions, attn_norm, ffn_norm, final_norm, ffn_w_gate, ffn_w_up, ffn_w_down, ev_w_in, ev_w_out, diff_lambda_q1, diff_lambda_k1, diff_lambda_q2, diff_lambda_k2, diff_subln, od_w_in, od_w_out, nsa_gate_b, nsa_pe_k, nsa_pe_v, nsa_k_w1, nsa_k_w2, nsa_v_w1, nsa_v_w2, mla_q_norm, mla_w_uq, mla_kv_norm, mla_w_ukv):
    b, s, d = x.shape
    depth = attn_norm.shape[0]
    rope16 = _rope_tables(positions, ROPE_DIM, HEAD_DIM)
    rope32s = _rope_tables(positions, MLA_ROPE, LANES, offset=MLA_NOPE)
    h = x.reshape(b * s, d)
    for l in range(depth):
        i = l // 2
        if l % 2 == 0:
            h = _even_mixer(h, attn_norm[l], l, rope16, ev_w_in[i], ev_w_out[i], diff_lambda_q1[i],
                            diff_lambda_k1[i], diff_lambda_q2[i], diff_lambda_k2[i], diff_subln[i], b, s)
        else:
            h = _odd_mixer(h, attn_norm[l], positions, rope16, rope32s, od_w_in[i], od_w_out[i], nsa_gate_b[i],
                           nsa_pe_k[i], nsa_pe_v[i], nsa_k_w1[i], nsa_k_w2[i], nsa_v_w1[i], nsa_v_w2[i],
                           mla_q_norm[i], mla_w_uq[i], mla_kv_norm[i], mla_w_ukv[i], b, s)
        h = _ffn(h, ffn_norm[l], ffn_w_gate[l].astype(BF16), ffn_w_up[l].astype(BF16), ffn_w_down[l].astype(BF16),
                 final_norm, final_norm=(l == depth - 1))
    return h.reshape(b, s, d)
```

```python
import functools
import math

import jax
import jax.numpy as jnp
from jax import lax
from jax.experimental import pallas as pl
from jax.experimental.pallas import tpu as pltpu

F32 = jnp.float32
BF16 = jnp.bfloat16

D_MODEL = 1024
HEAD_DIM = 64
ROPE_THETA = 500000.0
ROPE_DIM = HEAD_DIM // 4
NORM_EPS = 1e-5
D_FF = 2816

MOBA_HEADS = 8
MOBA_BLOCK = 256
MOBA_TOPK = 3
DIFF_HEADS = 4
DIFF_V = 2 * HEAD_DIM
NSA_HEADS = 8
NSA_GROUPS = 2
NSA_REP = NSA_HEADS // NSA_GROUPS
NSA_CMP_LEN = 32
NSA_CMP_STRIDE = 16
NSA_CMP_HIDDEN = 256
NSA_SEL_BLOCK = 64
NSA_SEL_TOPK = 16
NSA_WINDOW = 512
NSA_FORCE_BONUS = 1e3
MLA_HEADS = 8
MLA_Q_RANK = 256
MLA_KV_RANK = 128
MLA_NOPE = 64
MLA_ROPE = 32
MLA_V = 64

LANES = 128
MOBA_BIAS_LANES = 32
NEG = -1e30
M_INIT = -1e37
LOG2E = 1.4426950408889634
UNROLL = 4
VMEM_LIMIT = 56 * 1024 * 1024

_NT = (((1,), (1,)), ((), ()))


def _cparams(sem):
    return pltpu.CompilerParams(dimension_semantics=sem, vmem_limit_bytes=VMEM_LIMIT)


def _rope_tables(positions, dim, period, offset=0):
    r = dim // 2
    inv = 1.0 / (ROPE_THETA ** (jnp.arange(0, dim, 2, dtype=F32) / dim))
    ang = positions.astype(F32)[..., None] * inv
    cos, sin = jnp.cos(ang), jnp.sin(ang)
    const = lambda n, val: jnp.full(ang.shape[:-1] + (n,), val, F32)
    rest = period - offset - 2 * r
    zr = jnp.zeros_like(sin)
    c = jnp.concatenate([const(offset, 1.0), cos, cos, const(rest, 1.0)], -1)
    s1 = jnp.concatenate([const(offset, 0.0), -sin, zr, const(rest, 0.0)], -1)
    s2 = jnp.concatenate([const(offset, 0.0), zr, sin, const(rest, 0.0)], -1)
    reps = LANES // period
    tile = lambda t: jnp.tile(t, (1,) * (t.ndim - 1) + (reps,)).reshape(-1, LANES)
    return tile(c), tile(s1), tile(s2)


def _apply_rope(y, c, s1, s2, shift):
    return y * c + pltpu.roll(y, LANES - shift, 1) * s1 + pltpu.roll(y, shift, 1) * s2


def _proj_body(*refs, has_norm, has_x2, has_rope, has_res, add_cols, rope_cols, shift, chunk):
    it = iter(refs)
    x_ref = next(it)
    x2_ref = next(it) if has_x2 else None
    g_ref = next(it) if has_norm else None
    w_ref = next(it)
    if has_rope:
        c_ref, s1_ref, s2_ref = next(it), next(it), next(it)
    res_ref = next(it) if has_res else None
    add_ref = next(it) if add_cols else None
    o_ref = next(it)
    n = w_ref.shape[1]
    if has_norm:
        xf = x_ref[...].astype(F32)
        y = xf * lax.rsqrt(jnp.mean(xf * xf, axis=-1, keepdims=True) + NORM_EPS)
        xb = (y * g_ref[...]).astype(BF16)
    else:
        xb = x_ref[...].astype(BF16)
    k1 = xb.shape[1]
    for c0 in range(0, n, chunk):
        cw = min(chunk, n - c0)
        y = jnp.dot(xb, w_ref[0:k1, c0:c0 + cw], preferred_element_type=F32)
        if has_x2:
            y = y + jnp.dot(x2_ref[...].astype(BF16), w_ref[k1:, c0:c0 + cw], preferred_element_type=F32)
        if has_res:
            y = y + res_ref[:, c0:c0 + cw]
        if (has_rope and c0 < rope_cols) or c0 < add_cols:
            for k0 in range(0, cw, LANES):
                ys = y[:, k0:k0 + LANES]
                if has_rope and c0 + k0 < rope_cols:
                    ys = _apply_rope(ys, c_ref[...], s1_ref[...], s2_ref[...], shift)
                if c0 + k0 < add_cols:
                    ys = ys + add_ref[...].astype(F32)
                o_ref[:, c0 + k0:c0 + k0 + LANES] = ys.astype(o_ref.dtype)
        else:
            o_ref[:, c0:c0 + cw] = y.astype(o_ref.dtype)


def _proj(x, w, *, name, x_blk=None, x2=None, gain=None, rope=None, rope_cols=0, shift=0, res=None, add=None,
          add_cols=0, out_dtype=F32, tm=512, chunk=512):
    t = x.shape[0]
    k, xj = (x.shape[1], 0) if x_blk is None else x_blk
    n = w.shape[1]
    assert t % tm == 0 and n % LANES == 0 and rope_cols % LANES == 0 and add_cols % LANES == 0
    has_norm, has_rope, has_res, has_x2 = gain is not None, rope is not None, res is not None, x2 is not None
    assert not (has_norm and has_x2)
    args, specs = [x], [pl.BlockSpec((tm, k), lambda i: (i, xj))]
    if has_x2:
        args.append(x2)
        specs.append(pl.BlockSpec((tm, x2.shape[1]), lambda i: (i, 0)))
    if has_norm:
        args.append(gain.reshape(1, k).astype(F32))
        specs.append(pl.BlockSpec((1, k), lambda i: (0, 0)))
    args.append(w)
    specs.append(pl.BlockSpec((w.shape[0], n), lambda i: (0, 0)))
    if has_rope:
        for tb in rope:
            args.append(tb)
            specs.append(pl.BlockSpec((tm, LANES), lambda i: (i, 0)))
    if has_res:
        args.append(res)
        specs.append(pl.BlockSpec((tm, n), lambda i: (i, 0)))
    if add_cols:
        args.append(add)
        specs.append(pl.BlockSpec((tm, LANES), lambda i: (i, 0)))
    body = functools.partial(_proj_body, has_norm=has_norm, has_x2=has_x2, has_rope=has_rope, has_res=has_res,
                             add_cols=add_cols,
                             rope_cols=rope_cols, shift=shift, chunk=chunk)
    return pl.pallas_call(
        body, name=name, grid=(t // tm,), in_specs=specs,
        out_specs=pl.BlockSpec((tm, n), lambda i: (i, 0)),
        out_shape=jax.ShapeDtypeStruct((t, n), out_dtype),
        compiler_params=_cparams(("parallel",)),
    )(*args)


def _ffn_body(x_ref, g_ref, wg_ref, wu_ref, wd_ref, fg_ref, o_ref, xn_ref, acc_ref, *, final_norm):
    j = pl.program_id(1)

    @pl.when(j == 0)
    def _():
        xf = x_ref[...]
        y = xf * lax.rsqrt(jnp.mean(xf * xf, axis=-1, keepdims=True) + NORM_EPS)
        xn_ref[...] = (y * g_ref[...]).astype(BF16)
        acc_ref[...] = jnp.zeros_like(acc_ref)

    xn = xn_ref[...]
    g = jnp.dot(xn, wg_ref[...], preferred_element_type=F32)
    u = jnp.dot(xn, wu_ref[...], preferred_element_type=F32)
    a = (jax.nn.silu(g) * u).astype(BF16)
    acc_ref[...] += jnp.dot(a, wd_ref[...], preferred_element_type=F32)

    @pl.when(j == pl.num_programs(1) - 1)
    def _():
        h = x_ref[...] + acc_ref[...]
        if final_norm:
            y = h * lax.rsqrt(jnp.mean(h * h, axis=-1, keepdims=True) + NORM_EPS)
            h = y * fg_ref[...]
        o_ref[...] = h


def _ffn(x, gain, wg, wu, wd, final_gain, *, final_norm, tm=512, tf=1408):
    t, d = x.shape
    f = wg.shape[1]
    assert t % tm == 0 and f % tf == 0
    return pl.pallas_call(
        functools.partial(_ffn_body, final_norm=final_norm), name="ffn",
        grid=(t // tm, f // tf),
        in_specs=[
            pl.BlockSpec((tm, d), lambda i, j: (i, 0)),
            pl.BlockSpec((1, d), lambda i, j: (0, 0)),
            pl.BlockSpec((d, tf), lambda i, j: (0, j)),
            pl.BlockSpec((d, tf), lambda i, j: (0, j)),
            pl.BlockSpec((tf, d), lambda i, j: (j, 0)),
            pl.BlockSpec((1, d), lambda i, j: (0, 0)),
        ],
        out_specs=pl.BlockSpec((tm, d), lambda i, j: (i, 0)),
        out_shape=jax.ShapeDtypeStruct((t, d), F32),
        scratch_shapes=[pltpu.VMEM((tm, d), BF16), pltpu.VMEM((tm, d), F32)],
        compiler_params=_cparams(("parallel", "arbitrary")),
    )(x, gain.reshape(1, d).astype(F32), wg, wu, wd, final_gain.reshape(1, d).astype(F32))


def _sweep(q, k_get, vt_get, s_ref, *, q0, tq, tk, dv, scale):
    cols = q.shape[0]

    def fold8(x, op):
        return op(x.reshape(x.shape[0] // 8, 8, cols), axis=0)

    def score_block(j0, n, mrun, n_masked):
        ks = pl.multiple_of(j0 * tk, tk)
        s = lax.dot_general(k_get(ks, n * tk), q, _NT, preferred_element_type=F32) * (scale * LOG2E)
        for u in range(n):
            su = s[u * tk:(u + 1) * tk]
            if u >= n - n_masked:
                kpos = ks + u * tk + lax.broadcasted_iota(jnp.int32, (tk, cols), 0)
                qpos = q0 + lax.rem(lax.broadcasted_iota(jnp.int32, (tk, cols), 1), tq)
                su = jnp.where(kpos <= qpos, su, NEG)
            s_ref[j0 + u] = su
            mrun = jnp.maximum(mrun, fold8(su, jnp.max))
        return mrun

    def pv_block(j0, n, carry, m):
        l8, acc = carry
        for u in range(n):
            p = jnp.exp2(s_ref[j0 + u] - m)
            l8 = l8 + fold8(p, jnp.sum)
            acc = acc + jnp.dot(vt_get(j0 + u), p.astype(BF16), preferred_element_type=F32)
        return l8, acc

    m_init = jnp.full((8, cols), M_INIT, F32)
    acc_init = (jnp.zeros((8, cols), F32), jnp.zeros((dv, cols), F32))
    n_diag = max(tq // tk, 1)
    assert UNROLL % n_diag == 0
    n_full = q0 // tk
    groups = n_full // UNROLL
    rem = n_full - groups * UNROLL
    tail0 = groups * UNROLL
    tails = range(n_diag, UNROLL + n_diag, n_diag)

    mrun = lax.fori_loop(0, groups, lambda i, t: score_block(UNROLL * i, UNROLL, t, 0), m_init)
    mrun = lax.switch(rem // n_diag, [functools.partial(score_block, tail0, n, n_masked=n_diag) for n in tails],
                      mrun)
    m = jnp.max(mrun, axis=0, keepdims=True)
    carry = lax.fori_loop(0, groups, lambda i, t: pv_block(UNROLL * i, UNROLL, t, m), acc_init)
    l8, acc = lax.switch(rem // n_diag, [functools.partial(pv_block, tail0, n, m=m) for n in tails], carry)
    l = jnp.sum(l8, axis=0, keepdims=True)
    return acc / jnp.maximum(l, 1e-30)


def _head_lanes(mode, v, lane):
    d = HEAD_DIM
    if mode == "diff":
        return jnp.logical_and(lane >= d * v, lane < d * (v + 1))
    if mode == "moba":
        lo = 2 * d + v * MOBA_BIAS_LANES
        return jnp.logical_or(jnp.logical_and(lane >= d * v, lane < d * (v + 1)),
                              jnp.logical_and(lane >= lo, lane < lo + MOBA_BIAS_LANES))
    return jnp.logical_and(lane >= LANES * v, lane < LANES * (v + 1))


def _pflash_body(*refs, mode, tq, tk, scale, lambda_init):
    if mode == "diff":
        q_ref, k_ref, v_ref, lq1_ref, lk1_ref, lq2_ref, lk2_ref, sg_ref, o_ref, s_ref, vt_ref = refs
    else:
        q_ref, k_ref, v_ref, o_ref, s_ref, vt_ref = refs
    qi = pl.program_id(1)
    n_chunks = v_ref.shape[1] // tk

    @pl.when(qi == 0)
    def _():
        for c in range(n_chunks):
            vt_ref[c] = v_ref[0, c * tk:(c + 1) * tk, :].astype(F32).T.astype(BF16)

    q = q_ref[0]
    lane = lax.broadcasted_iota(jnp.int32, q.shape, 1)
    d = HEAD_DIM
    outs = []
    for v in range(2):
        qv = jnp.where(_head_lanes(mode, v, lane), q, jnp.zeros_like(q))
        if mode == "diff":
            vt_get, dv = (lambda j: vt_ref[j]), LANES
        else:
            vt_get, dv = (lambda j, v=v: vt_ref[j, v * d:(v + 1) * d, :]), d
        outs.append(_sweep(qv, lambda ks, n: k_ref[0, pl.ds(ks, n), :], vt_get, s_ref,
                           q0=qi * tq, tq=tq, tk=tk, dv=dv, scale=scale))
    if mode == "diff":
        lam = (jnp.exp(jnp.sum(lq1_ref[...] * lk1_ref[...], axis=-1, keepdims=True))
               - jnp.exp(jnp.sum(lq2_ref[...] * lk2_ref[...], axis=-1, keepdims=True)) + lambda_init)
        d = outs[0] - lam * outs[1]
        y = d * lax.rsqrt(jnp.mean(d * d, axis=0, keepdims=True) + NORM_EPS)
        o_t = (y * sg_ref[...]) * (1.0 - lambda_init)
    else:
        o_t = jnp.concatenate(outs, axis=0)
    o_ref[0] = o_t.T.astype(o_ref.dtype)


def _pflash(q, k, v, *, name, mode, dk, q_map, k_map, v_map, batch, pairs, tq=1024, tk=512, scale=1.0,
            diff_params=None, lambda_init=0.0):
    s = v.shape[1]
    assert s % tk == 0 and tq % tk == 0 and s % tq == 0
    args = [q, k, v]
    in_specs = [pl.BlockSpec((1, tq, dk), q_map), pl.BlockSpec((1, s, dk), k_map),
                pl.BlockSpec((1, s, LANES), v_map)]
    if mode == "diff":
        lq1, lk1, lq2, lk2, subln = diff_params
        for a in (lq1, lk1, lq2, lk2):
            args.append(a.reshape(1, HEAD_DIM).astype(F32))
            in_specs.append(pl.BlockSpec((1, HEAD_DIM), lambda i, j: (0, 0)))
        args.append(subln.reshape(LANES, 1).astype(F32))
        in_specs.append(pl.BlockSpec((LANES, 1), lambda i, j: (0, 0)))
    body = functools.partial(_pflash_body, mode=mode, tq=tq, tk=tk, scale=scale, lambda_init=lambda_init)
    return pl.pallas_call(
        body, name=name, grid=(batch * pairs, s // tq), in_specs=in_specs,
        out_specs=pl.BlockSpec((1, tq, LANES), lambda i, j: (i // pairs, j, i % pairs)),
        out_shape=jax.ShapeDtypeStruct((batch, s, pairs * LANES), BF16),
        scratch_shapes=[pltpu.VMEM((s // tk, tk, tq), F32), pltpu.VMEM((s // tk, LANES, tk), BF16)],
        compiler_params=_cparams(("parallel", "arbitrary")),
    )(*args)


def _band(q, ka_ref, vt_ref, *, q0, tq, window):
    cols = q.shape[0]
    band = window + tq
    start = pl.multiple_of(jnp.maximum(q0 - window, 0), tq)
    s = lax.dot_general(ka_ref[pl.ds(start, band), :], q, _NT, preferred_element_type=F32) * LOG2E
    kpos = start + lax.broadcasted_iota(jnp.int32, (band, cols), 0)
    qpos = q0 + lax.rem(lax.broadcasted_iota(jnp.int32, (band, cols), 1), tq)
    ok = jnp.logical_and(kpos <= qpos, kpos > qpos - window)
    s = jnp.where(ok, s, NEG)
    p = jnp.exp2(s - jnp.max(s, axis=0, keepdims=True))
    l = jnp.sum(p, axis=0, keepdims=True)
    pb = p.astype(BF16)
    acc = None
    for c in range(band // tq):
        part = jnp.dot(vt_ref[start // tq + c], pb[c * tq:(c + 1) * tq], preferred_element_type=F32)
        acc = part if acc is None else acc + part
    return acc / jnp.maximum(l, 1e-30)


def _gflash_body(*refs, tq, tk, rep, groups, window, select):
    if select:
        q_ref, k_ref, v_ref, mb_ref, o_ref, s_ref, vt_ref, ka_ref, qa_ref = refs
    else:
        q_ref, k_ref, v_ref, o_ref, vt_ref, ka_ref, qa_ref = refs
    d = HEAD_DIM
    tv = vt_ref.shape[2]
    g = pl.program_id(0) % groups
    qi = pl.program_id(1)
    n_chunks = v_ref.shape[1] // tk

    @pl.when(qi == 0)
    def _():
        move = _place(LANES, LANES, g * d)
        for c in range(n_chunks):
            rows = slice(c * tk, (c + 1) * tk)
            vt = v_ref[0, rows, :].astype(F32).T
            vt = jnp.where(g == 0, vt[0:d], vt[d:2 * d]).astype(BF16)
            for u in range(tk // tv):
                vt_ref[c * (tk // tv) + u] = vt[:, u * tv:(u + 1) * tv]
            kg = jnp.dot(k_ref[0, rows, :], move, preferred_element_type=F32).astype(BF16)
            if select:
                blk = (c * tk + lax.broadcasted_iota(jnp.int32, (tk, LANES), 0)) // NSA_SEL_BLOCK
                lane = lax.broadcasted_iota(jnp.int32, (tk, LANES), 1)
                ka_ref[rows, 0:LANES] = jnp.where(blk == lane, 1.0, 0.0).astype(BF16)
                ka_ref[rows, LANES:2 * LANES] = kg
            else:
                ka_ref[rows, :] = kg

    q = q_ref[0]
    for r in range(rep):
        qr = jnp.dot(q, _place(rep * d, LANES, r * d, d ** -0.5), preferred_element_type=F32).astype(BF16)
        rows = slice(r * tq, (r + 1) * tq)
        if select:
            qa_ref[rows, 0:LANES] = mb_ref[0]
            qa_ref[rows, LANES:2 * LANES] = qr
        else:
            qa_ref[rows, :] = qr
    if window is None:
        o_t = _sweep(qa_ref[...], lambda ks, n: ka_ref[pl.ds(ks, n), :], lambda j: vt_ref[j], s_ref,
                     q0=qi * tq, tq=tq, tk=tk, dv=d, scale=1.0)
    else:
        o_t = _band(qa_ref[...], ka_ref, vt_ref, q0=qi * tq, tq=tq, window=window)
    for r in range(rep):
        o_ref[0, :, r * d:(r + 1) * d] = o_t[:, r * tq:(r + 1) * tq].T.astype(o_ref.dtype)


def _gflash(yq, yk, yv, mbias, *, name, q_blk, k_blk, v_blk, tq=128, tk=512, window=None):
    b, s, _ = yq.shape
    rep, groups, d = NSA_REP, NSA_GROUPS, HEAD_DIM
    assert groups * d == LANES and s % tk == 0 and tk % tq == 0
    select = mbias is not None
    dk = 2 * LANES if select else LANES
    assert window is None or (not select and window % tq == 0 and s >= window + tq)
    tv = tk if window is None else tq
    scratch = [pltpu.VMEM((s // tv, d, tv), BF16), pltpu.VMEM((s, dk), BF16), pltpu.VMEM((rep * tq, dk), BF16)]
    if window is None:
        scratch.insert(0, pltpu.VMEM((s // tk, tk, rep * tq), F32))
    args = [yq, yk, yv]
    in_specs = [pl.BlockSpec((1, tq, rep * d), lambda i, j: (i // groups, j, q_blk + i % groups)),
                pl.BlockSpec((1, s, LANES), lambda i, j: (i // groups, 0, k_blk)),
                pl.BlockSpec((1, s, LANES), lambda i, j: (i // groups, 0, v_blk))]
    if select:
        args.append(mbias)
        in_specs.append(pl.BlockSpec((1, tq, LANES), lambda i, j: (i, j, 0)))
    body = functools.partial(_gflash_body, tq=tq, tk=tk, rep=rep, groups=groups, window=window, select=select)
    return pl.pallas_call(
        body, name=name, grid=(b * groups, s // tq), in_specs=in_specs,
        out_specs=pl.BlockSpec((1, tq, rep * d), lambda i, j: (i // groups, j, i % groups)),
        out_shape=jax.ShapeDtypeStruct((b, s, groups * rep * d), F32),
        scratch_shapes=scratch,
        compiler_params=_cparams(("parallel", "arbitrary")),
    )(*args)


def _topk_mask(score, row, k):
    sel = None
    for _ in range(k):
        mx = jnp.max(score, axis=0, keepdims=True)
        idx = jnp.min(jnp.where(score == mx, row, LANES), axis=0, keepdims=True)
        hit = row == idx
        pick = jnp.logical_and(hit, mx > -jnp.inf)
        sel = pick if sel is None else jnp.logical_or(sel, pick)
        score = jnp.where(hit, -jnp.inf, score)
    return sel


def _moba_gate_body(q_ref, k_ref, qa_ref, ka_ref, kmean_ref, *, tq, seq):
    qi = pl.program_id(1)
    nbl = MOBA_BIAS_LANES

    @pl.when(qi == 0)
    def _():
        rowblk = lax.rem(lax.broadcasted_iota(jnp.int32, (LANES, seq), 0), nbl)
        colblk = lax.broadcasted_iota(jnp.int32, (LANES, seq), 1) // MOBA_BLOCK
        ind = jnp.where(rowblk == colblk, 1.0, 0.0).astype(BF16)
        ksum = jnp.dot(ind, k_ref[0], preferred_element_type=F32)
        rhead = lax.broadcasted_iota(jnp.int32, (LANES, LANES), 0) // nbl
        lhead = lax.broadcasted_iota(jnp.int32, (LANES, LANES), 1) // HEAD_DIM
        kmean_ref[...] = jnp.where(rhead == lhead, ksum * (1.0 / MOBA_BLOCK), 0.0)

    q = q_ref[0]
    q0 = pl.multiple_of(qi * tq, tq)
    km = kmean_ref[...]
    km_hi = km.astype(BF16)
    km_lo = (km - km_hi.astype(F32)).astype(BF16)
    gate = (lax.dot_general(km_hi, q, _NT, preferred_element_type=F32)
            + lax.dot_general(km_lo, q, _NT, preferred_element_type=F32))
    blk_t = lax.broadcasted_iota(jnp.int32, (nbl, tq), 0)
    own_t = (q0 + lax.broadcasted_iota(jnp.int32, (nbl, tq), 1)) // MOBA_BLOCK
    parts = []
    for v in range(2):
        g_v = jnp.where(blk_t < own_t, gate[v * nbl:(v + 1) * nbl], -jnp.inf)
        sel = jnp.logical_or(_topk_mask(g_v, blk_t, MOBA_TOPK), blk_t == own_t)
        parts.append(jnp.where(sel, 0.0, NEG))
    parts.append(jnp.zeros((LANES - 2 * nbl, tq), F32))
    bias = jnp.concatenate(parts, axis=0).T
    qa_ref[0, :, 0:LANES] = q * (HEAD_DIM ** -0.5)
    qa_ref[0, :, LANES:2 * LANES] = bias.astype(BF16)
    lane = lax.broadcasted_iota(jnp.int32, (tq, LANES), 1)
    own = (q0 + lax.broadcasted_iota(jnp.int32, (tq, LANES), 0)) // MOBA_BLOCK
    onehot = jnp.where(jnp.logical_and(lane < 2 * nbl, lax.rem(lane, nbl) == own), 1.0, 0.0)
    ka_ref[0, :, 0:LANES] = k_ref[0, pl.ds(q0, tq), :]
    ka_ref[0, :, LANES:2 * LANES] = onehot.astype(BF16)


def _moba_gate(y3, *, q_blk, k_blk, pairs, tq=512):
    b, s, _ = y3.shape
    assert s % MOBA_BLOCK == 0 and s // MOBA_BLOCK <= MOBA_BIAS_LANES and s % tq == 0
    out = jax.ShapeDtypeStruct((b * pairs, s, 2 * LANES), BF16)
    return pl.pallas_call(
        functools.partial(_moba_gate_body, tq=tq, seq=s), name="moba_gate",
        grid=(b * pairs, s // tq),
        in_specs=[pl.BlockSpec((1, tq, LANES), lambda i, j: (i // pairs, j, q_blk + i % pairs)),
                  pl.BlockSpec((1, s, LANES), lambda i, j: (i // pairs, 0, k_blk + i % pairs))],
        out_specs=[pl.BlockSpec((1, tq, 2 * LANES), lambda i, j: (i, j, 0)),
                   pl.BlockSpec((1, tq, 2 * LANES), lambda i, j: (i, j, 0))],
        out_shape=[out, out],
        scratch_shapes=[pltpu.VMEM((LANES, LANES), F32)],
        compiler_params=_cparams(("parallel", "arbitrary")),
    )(y3, y3)


def _nsa_compress_body(x_ref, pe_ref, w1_ref, w2_ref, c_ref, s1_ref, s2_ref, o_ref, *, rope):
    half = w1_ref.shape[0] // 2
    x = x_ref[0].astype(F32)
    lo = (x + pe_ref[:, 0:half]).astype(BF16)
    hi = (x + pe_ref[:, half:2 * half]).astype(BF16)
    a = jnp.dot(lo, w1_ref[0:half, :], preferred_element_type=F32)
    b = jnp.dot(hi, w1_ref[half:2 * half, :], preferred_element_type=F32)
    n = a.shape[0]
    h1 = a + pltpu.roll(b, n - 1, 0)
    y = jnp.dot(jax.nn.gelu(h1).astype(BF16), w2_ref[...], preferred_element_type=F32)
    if rope:
        y = _apply_rope(y, c_ref[0], s1_ref[0], s2_ref[0], ROPE_DIM // 2)
    o_ref[0] = y[:, 0:HEAD_DIM].astype(o_ref.dtype)


def _nsa_compress(xr, pe, w1, w2, tables, *, rope, groups):
    g, n, w = xr.shape
    hid = w1.shape[1]
    w2p = jnp.pad(w2, ((0, 0), (0, LANES - w2.shape[1])))
    tspec = pl.BlockSpec((1, n, LANES), lambda i: (i // groups, 0, 0))
    return pl.pallas_call(
        functools.partial(_nsa_compress_body, rope=rope), name="nsa_compress",
        grid=(g,),
        in_specs=[pl.BlockSpec((1, n, w), lambda i: (i, 0, 0)),
                  pl.BlockSpec((1, 2 * w), lambda i: (0, 0)),
                  pl.BlockSpec((2 * w, hid), lambda i: (0, 0)),
                  pl.BlockSpec((hid, LANES), lambda i: (0, 0)),
                  tspec, tspec, tspec],
        out_specs=pl.BlockSpec((1, n, HEAD_DIM), lambda i: (i, 0, 0)),
        out_shape=jax.ShapeDtypeStruct((g, n, HEAD_DIM), BF16),
        compiler_params=_cparams(("parallel",)),
    )(xr, pe.reshape(1, 2 * w).astype(F32), w1, w2p, *tables)


def _place(n_src, n_dst, shift, value=1.0):
    src = lax.broadcasted_iota(jnp.int32, (n_src, n_dst), 0)
    dst = lax.broadcasted_iota(jnp.int32, (n_src, n_dst), 1)
    return jnp.where(jnp.logical_and(src == dst + shift, dst < HEAD_DIM), value, 0.0).astype(BF16)


def _nsa_cmp_body(q_ref, kc_ref, vc_ref, oc_ref, mb_ref, kcp_ref, *, tq, rep):
    qi = pl.program_id(1)
    q0 = qi * tq
    nc = kc_ref.shape[1]
    width = q_ref.shape[-1]

    @pl.when(qi == 0)
    def _():
        for r in range(rep):
            src = lax.broadcasted_iota(jnp.int32, (HEAD_DIM, width), 0)
            dst = lax.broadcasted_iota(jnp.int32, (HEAD_DIM, width), 1)
            spread = jnp.where(dst == src + r * HEAD_DIM, HEAD_DIM ** -0.5, 0.0).astype(BF16)
            kcp_ref[r] = jnp.dot(kc_ref[0], spread, preferred_element_type=F32).astype(BF16)

    q = q_ref[0]
    vct = vc_ref[0]
    tpos = q0 + lax.broadcasted_iota(jnp.int32, (nc, tq), 1)
    cend = lax.broadcasted_iota(jnp.int32, (nc, tq), 0) * NSA_CMP_STRIDE + (NSA_CMP_LEN - 1)
    ok = cend <= tpos
    psum = jnp.zeros((nc, tq), F32)
    for r in range(rep):
        s = lax.dot_general(kcp_ref[r], q, _NT, preferred_element_type=F32)
        s = jnp.where(ok, s, NEG)
        m = jnp.max(s, axis=0, keepdims=True)
        e = jnp.where(ok, jnp.exp(s - m), 0.0)
        p = e * (1.0 / jnp.maximum(jnp.sum(e, axis=0, keepdims=True), 1e-30))
        o_t = jnp.dot(vct, p.astype(BF16), preferred_element_type=F32)
        oc_ref[0, :, r * HEAD_DIM:(r + 1) * HEAD_DIM] = o_t.T.astype(oc_ref.dtype)
        psum = psum + p
    sstart = lax.broadcasted_iota(jnp.int32, (LANES, nc), 0) * NSA_SEL_BLOCK
    cstart = lax.broadcasted_iota(jnp.int32, (LANES, nc), 1) * NSA_CMP_STRIDE
    ov = jnp.where(jnp.logical_and(cstart < sstart + NSA_SEL_BLOCK, cstart + NSA_CMP_LEN > sstart),
                   1.0, 0.0).astype(BF16)
    p_hi = psum.astype(BF16)
    r1 = psum - p_hi.astype(F32)
    p_mid = r1.astype(BF16)
    p_lo = (r1 - p_mid.astype(F32)).astype(BF16)
    imp = (jnp.dot(ov, p_hi, preferred_element_type=F32) + jnp.dot(ov, p_mid, preferred_element_type=F32)
           + jnp.dot(ov, p_lo, preferred_element_type=F32))
    blk = lax.broadcasted_iota(jnp.int32, (LANES, tq), 0)
    qb = (q0 + lax.broadcasted_iota(jnp.int32, (LANES, tq), 1)) // NSA_SEL_BLOCK
    forced = jnp.logical_or(blk == 0, jnp.logical_or(blk == qb, blk == qb - 1))
    imp = jnp.where(forced, imp + NSA_FORCE_BONUS, imp)
    imp = jnp.where(blk <= qb, imp, -jnp.inf)
    sel = _topk_mask(imp, blk, NSA_SEL_TOPK)
    mb_ref[0] = jnp.where(sel, 0.0, NEG).T.astype(mb_ref.dtype)


def _nsa_cmp(y3, kc, vc, *, tq=512):
    b, s, _ = y3.shape
    g, nc, d = kc.shape
    rep, n = NSA_REP, NSA_GROUPS
    assert s // NSA_SEL_BLOCK <= LANES and s % tq == 0
    return pl.pallas_call(
        functools.partial(_nsa_cmp_body, tq=tq, rep=rep), name="nsa_cmp",
        grid=(g, s // tq),
        in_specs=[pl.BlockSpec((1, tq, rep * d), lambda i, j: (i // n, j, i % n)),
                  pl.BlockSpec((1, nc, d), lambda i, j: (i, 0, 0)),
                  pl.BlockSpec((1, d, nc), lambda i, j: (i, 0, 0))],
        out_specs=[pl.BlockSpec((1, tq, rep * d), lambda i, j: (i // n, j, i % n)),
                   pl.BlockSpec((1, tq, LANES), lambda i, j: (i, j, 0))],
        out_shape=[jax.ShapeDtypeStruct((b, s, n * rep * d), F32),
                   jax.ShapeDtypeStruct((g, s, LANES), BF16)],
        scratch_shapes=[pltpu.VMEM((rep, nc, rep * d), BF16)],
        compiler_params=_cparams(("parallel", "arbitrary")),
    )(y3, kc, vc)


def _nsa_combine_body(oc_ref, os_ref, ow_ref, g_ref, b_ref, e_ref, o_ref):
    gs = jax.nn.sigmoid(g_ref[...] + b_ref[...])
    g_hi = gs.astype(BF16)
    g_lo = (gs - g_hi.astype(F32)).astype(BF16)
    out = None
    for i, ref in enumerate((oc_ref, os_ref, ow_ref)):
        w = (jnp.dot(g_hi, e_ref[i], preferred_element_type=F32)
             + jnp.dot(g_lo, e_ref[i], preferred_element_type=F32))
        term = w * ref[...]
        out = term if out is None else out + term
    o_ref[...] = out.astype(o_ref.dtype)


def _nsa_combine(oc, osel, ow, yg, g_blk, gate_b, *, tm=512):
    t, n = oc.shape
    nh = NSA_HEADS * 3
    bp = jnp.pad(gate_b.reshape(1, nh).astype(F32), ((0, 0), (0, LANES - nh)))
    row = jnp.arange(LANES)[:, None]
    col = jnp.arange(n)[None, :] // HEAD_DIM
    expand = jnp.stack([(row == col * 3 + i) for i in range(3)]).astype(BF16)
    tok = pl.BlockSpec((tm, n), lambda i: (i, 0))
    return pl.pallas_call(
        _nsa_combine_body, name="nsa_combine", grid=(t // tm,),
        in_specs=[tok, tok, tok, pl.BlockSpec((tm, LANES), lambda i: (i, g_blk)),
                  pl.BlockSpec((1, LANES), lambda i: (0, 0)),
                  pl.BlockSpec((3, LANES, n), lambda i: (0, 0, 0))],
        out_specs=tok,
        out_shape=jax.ShapeDtypeStruct((t, n), BF16),
        compiler_params=_cparams(("parallel",)),
    )(oc, osel, ow, yg, bp, expand)


def _to_heads(x, b, s, n):
    return x.reshape(b, s, n, -1).transpose(0, 2, 1, 3).reshape(b * n, s, -1)


def _even_mixer(h, gain, layer_idx, rope16, w_in, w_out, lq1, lk1, lq2, lk2, subln, b, s):
    na = MOBA_HEADS * HEAD_DIM
    nb = DIFF_HEADS * 2 * HEAD_DIM
    w_rope = jnp.concatenate([w_in[:, 0:2 * na], w_in[:, 3 * na:3 * na + 2 * nb]], axis=1)
    w_rest = jnp.concatenate([w_in[:, 2 * na:3 * na], w_in[:, 3 * na + 2 * nb:]], axis=1)
    w = jnp.concatenate([w_rope, w_rest], axis=1).astype(BF16)
    y = _proj(h, w, name="proj_even_in", gain=gain, rope=rope16, rope_cols=w_rope.shape[1], shift=ROPE_DIM // 2, out_dtype=BF16)
    y3 = y.reshape(b, s, -1)
    blk = lambda off: off // LANES
    pairs = na // LANES
    qa, ka = _moba_gate(y3, q_blk=blk(0), k_blk=blk(na), pairs=pairs)
    o_a = _pflash(qa, ka, y3, name="flash_moba", mode="moba", dk=2 * LANES, batch=b, pairs=pairs,
                  q_map=lambda i, j: (i, j, 0), k_map=lambda i, j: (i, 0, 0),
                  v_map=lambda i, j: (i // pairs, 0, blk(2 * na + 2 * nb) + i % pairs))
    lambda_init = 0.8 - 0.6 * math.exp(-0.3 * layer_idx)
    nh = DIFF_HEADS
    o_b = _pflash(y3, y3, y3, name="flash_diff", mode="diff", dk=LANES, batch=b, pairs=nh, scale=HEAD_DIM ** -0.5,
                  q_map=lambda i, j: (i // nh, j, blk(2 * na) + i % nh),
                  k_map=lambda i, j: (i // nh, 0, blk(2 * na + nb) + i % nh),
                  v_map=lambda i, j: (i // nh, 0, blk(3 * na + 2 * nb) + i % nh),
                  diff_params=(lq1, lk1, lq2, lk2, subln), lambda_init=lambda_init)
    t = b * s
    return _proj(o_a.reshape(t, -1), w_out.astype(BF16), name="proj_out", x2=o_b.reshape(t, -1), res=h, out_dtype=F32)


def _odd_mixer(h, gain, positions, rope16, rope32s, w_in, w_out, gate_b, pe_k, pe_v, k_w1, k_w2, v_w1, v_w2,
               q_norm, w_uq, kv_norm, w_ukv, b, s):
    G, R, d = NSA_GROUPS, NSA_REP, HEAD_DIM
    sizes = [NSA_HEADS * d] + [G * d] * 6 + [NSA_HEADS * 3, MLA_Q_RANK, MLA_KV_RANK, MLA_ROPE]
    offs = [0]
    for z in sizes:
        offs.append(offs[-1] + z)
    col = lambda i: w_in[:, offs[i]:offs[i + 1]]
    w_r = jnp.concatenate([col(0), col(3), col(5)], axis=1).astype(BF16)
    yr = _proj(h, w_r, name="proj_odd_rope", gain=gain, rope=rope16, rope_cols=w_r.shape[1], shift=ROPE_DIM // 2, out_dtype=BF16)
    w_p = jnp.concatenate([col(1), col(2), col(4), col(6), col(8), col(9), col(7)], axis=1)
    w_p = jnp.pad(w_p, ((0, 0), (0, (-w_p.shape[1]) % LANES))).astype(BF16)
    yp = _proj(h, w_p, name="proj_odd_plain", gain=gain, out_dtype=F32)
    k_cmp, v_cmp = yp[:, 0:G * d], yp[:, G * d:2 * G * d]
    cq_blk = 4 * G * d // MLA_Q_RANK
    ckv_blk = (4 * G * d + MLA_Q_RANK) // MLA_KV_RANK
    gate_blk = (4 * G * d + MLA_Q_RANK + MLA_KV_RANK) // LANES
    w_pe = jnp.pad(col(10), ((0, 0), (MLA_NOPE, LANES - MLA_NOPE - MLA_ROPE))).astype(BF16)
    k_pe = _proj(h, w_pe, name="proj_mla_kpe", gain=gain, rope=rope32s, rope_cols=LANES, shift=MLA_ROPE // 2,
                 out_dtype=BF16)

    yr3, yp3 = yr.reshape(b, s, -1), yp.reshape(b, s, -1)
    nc = s // NSA_CMP_STRIDE
    cpos = jnp.concatenate([positions[:, NSA_CMP_LEN - 1::NSA_CMP_STRIDE], positions[:, -1:]], axis=1)
    ctab = [t.reshape(b, nc, LANES) for t in _rope_tables(cpos, ROPE_DIM, HEAD_DIM)]
    xk = _to_heads(k_cmp, b, s, G).reshape(b * G, nc, NSA_CMP_STRIDE * d)
    xv = _to_heads(v_cmp, b, s, G).reshape(b * G, nc, NSA_CMP_STRIDE * d)
    kc = _nsa_compress(xk, pe_k, k_w1.astype(BF16), k_w2.astype(BF16), ctab, rope=True, groups=G)
    vc = _nsa_compress(xv, pe_v, v_w1.astype(BF16), v_w2.astype(BF16), ctab, rope=False, groups=G)
    o_c, mbias = _nsa_cmp(yr3, kc, vc.transpose(0, 2, 1))
    k_blk = NSA_HEADS * d // LANES
    vsw = yp[:, 2 * G * d:4 * G * d].astype(BF16).reshape(b, s, -1)
    o_s = _gflash(yr3, yr3, vsw, mbias, name="flash_sel", q_blk=0, k_blk=k_blk, v_blk=0, tq=256)
    o_w = _gflash(yr3, yr3, vsw, None, name="flash_win", q_blk=0, k_blk=k_blk + 1, v_blk=1, window=NSA_WINDOW)
    t = b * s
    o_nsa = _nsa_combine(o_c.reshape(t, -1), o_s.reshape(t, -1), o_w.reshape(t, -1), yp, gate_blk, gate_b)

    hq = MLA_NOPE + MLA_ROPE
    nh = MLA_HEADS
    wq = jnp.pad(w_uq.reshape(MLA_Q_RANK, nh, hq), ((0, 0), (0, 0), (0, LANES - hq))).reshape(MLA_Q_RANK, nh * LANES)
    q = _proj(yp, wq.astype(BF16), name="proj_mla_q", x_blk=(MLA_Q_RANK, cq_blk), gain=q_norm, rope=rope32s,
              rope_cols=nh * LANES, shift=MLA_ROPE // 2, out_dtype=BF16)
    wkv = w_ukv.reshape(MLA_KV_RANK, nh, MLA_NOPE + MLA_V)
    wk = jnp.pad(wkv[:, :, :MLA_NOPE], ((0, 0), (0, 0), (0, LANES - MLA_NOPE))).reshape(MLA_KV_RANK, nh * LANES)
    wkv = jnp.concatenate([wk, wkv[:, :, MLA_NOPE:].reshape(MLA_KV_RANK, nh * MLA_V)], axis=1).astype(BF16)
    kv = _proj(yp, wkv, name="proj_mla_kv", x_blk=(MLA_KV_RANK, ckv_blk), gain=kv_norm, add=k_pe, add_cols=nh * LANES,
               out_dtype=BF16)
    q3, kv3 = q.reshape(b, s, -1), kv.reshape(b, s, -1)
    pairs = nh // 2
    o_d = _pflash(q3, kv3, kv3, name="flash_mla", mode="slots", dk=2 * LANES, batch=b, pairs=pairs, scale=hq ** -0.5,
                  q_map=lambda i, j: (i // pairs, j, i % pairs), k_map=lambda i, j: (i // pairs, 0, i % pairs),
                  v_map=lambda i, j: (i // pairs, 0, nh + i % pairs))

    return _proj(o_nsa, w_out.astype(BF16), name="proj_out", x2=o_d.reshape(t, -1), res=h, out_dtype=F32)


def kernel(x, positions, attn_norm, ffn_norm, final_norm, ffn_w_gate, ffn_w_up, ffn_w_down, ev_w_in, ev_w_out, diff_lambda_q1, diff_lambda_k1, diff_lambda_q2, diff_lambda_k2, diff_subln, od_w_in, od_w_out, nsa_gate_b, nsa_pe_k, nsa_pe_v, nsa_k_w1, nsa_k_w2, nsa_v_w1, nsa_v_w2, mla_q_norm, mla_w_uq, mla_kv_norm, mla_w_ukv):
    b, s, d = x.shape
    depth = attn_norm.shape[0]
    rope16 = _rope_tables(positions, ROPE_DIM, HEAD_DIM)
    rope32s = _rope_tables(positions, MLA_ROPE, LANES, offset=MLA_NOPE)
    h = x.reshape(b * s, d)
    for l in range(depth):
        i = l // 2
        if l % 2 == 0:
            h = _even_mixer(h, attn_norm[l], l, rope16, ev_w_in[i], ev_w_out[i], diff_lambda_q1[i],
                            diff_lambda_k1[i], diff_lambda_q2[i], diff_lambda_k2[i], diff_subln[i], b, s)
        else:
            h = _odd_mixer(h, attn_norm[l], positions, rope16, rope32s, od_w_in[i], od_w_out[i], nsa_gate_b[i],
                           nsa_pe_k[i], nsa_pe_v[i], nsa_k_w1[i], nsa_k_w2[i], nsa_v_w1[i], nsa_v_w2[i],
                           mla_q_norm[i], mla_w_uq[i], mla_kv_norm[i], mla_w_ukv[i], b, s)
        h = _ffn(h, ffn_norm[l], ffn_w_gate[l].astype(BF16), ffn_w_up[l].astype(BF16), ffn_w_down[l].astype(BF16),
                 final_norm, final_norm=(l == depth - 1))
    return h.reshape(b, s, d)
```

```python
import functools
import math

import jax
import jax.numpy as jnp
from jax import lax
from jax.experimental import pallas as pl
from jax.experimental.pallas import tpu as pltpu

F32 = jnp.float32
BF16 = jnp.bfloat16

D_MODEL = 1024
HEAD_DIM = 64
ROPE_THETA = 500000.0
ROPE_DIM = HEAD_DIM // 4
NORM_EPS = 1e-5
D_FF = 2816

MOBA_HEADS = 8
MOBA_BLOCK = 256
MOBA_TOPK = 3
DIFF_HEADS = 4
DIFF_V = 2 * HEAD_DIM
NSA_HEADS = 8
NSA_GROUPS = 2
NSA_REP = NSA_HEADS // NSA_GROUPS
NSA_CMP_LEN = 32
NSA_CMP_STRIDE = 16
NSA_CMP_HIDDEN = 256
NSA_SEL_BLOCK = 64
NSA_SEL_TOPK = 16
NSA_WINDOW = 512
NSA_FORCE_BONUS = 1e3
MLA_HEADS = 8
MLA_Q_RANK = 256
MLA_KV_RANK = 128
MLA_NOPE = 64
MLA_ROPE = 32
MLA_V = 64

LANES = 128
MOBA_BIAS_LANES = 32
NEG = -1e30
M_INIT = -1e37
LOG2E = 1.4426950408889634
UNROLL = 4
VMEM_LIMIT = 56 * 1024 * 1024

_NT = (((1,), (1,)), ((), ()))


def _cparams(sem):
    return pltpu.CompilerParams(dimension_semantics=sem, vmem_limit_bytes=VMEM_LIMIT)


def _rope_tables(positions, dim, period, offset=0):
    r = dim // 2
    inv = 1.0 / (ROPE_THETA ** (jnp.arange(0, dim, 2, dtype=F32) / dim))
    ang = positions.astype(F32)[..., None] * inv
    cos, sin = jnp.cos(ang), jnp.sin(ang)
    const = lambda n, val: jnp.full(ang.shape[:-1] + (n,), val, F32)
    rest = period - offset - 2 * r
    zr = jnp.zeros_like(sin)
    c = jnp.concatenate([const(offset, 1.0), cos, cos, const(rest, 1.0)], -1)
    s1 = jnp.concatenate([const(offset, 0.0), -sin, zr, const(rest, 0.0)], -1)
    s2 = jnp.concatenate([const(offset, 0.0), zr, sin, const(rest, 0.0)], -1)
    reps = LANES // period
    tile = lambda t: jnp.tile(t, (1,) * (t.ndim - 1) + (reps,)).reshape(-1, LANES)
    return tile(c), tile(s1), tile(s2)


def _apply_rope(y, c, s1, s2, shift):
    return y * c + pltpu.roll(y, LANES - shift, 1) * s1 + pltpu.roll(y, shift, 1) * s2


def _proj_body(*refs, has_norm, has_x2, has_rope, has_res, add_cols, rope_add, rope_cols, shift, chunk):
    it = iter(refs)
    x_ref = next(it)
    x2_ref = next(it) if has_x2 else None
    g_ref = next(it) if has_norm else None
    w_ref = next(it)
    if has_rope:
        c_ref, s1_ref, s2_ref = next(it), next(it), next(it)
    res_ref = next(it) if has_res else None
    add_ref = next(it) if add_cols else None
    o_ref = next(it)
    n = w_ref.shape[1]
    if has_norm:
        xf = x_ref[...].astype(F32)
        y = xf * lax.rsqrt(jnp.mean(xf * xf, axis=-1, keepdims=True) + NORM_EPS)
        xb = (y * g_ref[...]).astype(BF16)
    else:
        xb = x_ref[...].astype(BF16)
    k1 = xb.shape[1]
    if add_cols:
        add = add_ref[...].astype(F32)
        if rope_add:
            add = _apply_rope(add, c_ref[...], s1_ref[...], s2_ref[...], shift)
    for c0 in range(0, n, chunk):
        cw = min(chunk, n - c0)
        y = jnp.dot(xb, w_ref[0:k1, c0:c0 + cw], preferred_element_type=F32)
        if has_x2:
            y = y + jnp.dot(x2_ref[...].astype(BF16), w_ref[k1:, c0:c0 + cw], preferred_element_type=F32)
        if has_res:
            y = y + res_ref[:, c0:c0 + cw]
        if (has_rope and c0 < rope_cols) or c0 < add_cols:
            for k0 in range(0, cw, LANES):
                ys = y[:, k0:k0 + LANES]
                if has_rope and c0 + k0 < rope_cols:
                    ys = _apply_rope(ys, c_ref[...], s1_ref[...], s2_ref[...], shift)
                if c0 + k0 < add_cols:
                    ys = ys + add
                o_ref[:, c0 + k0:c0 + k0 + LANES] = ys.astype(o_ref.dtype)
        else:
            o_ref[:, c0:c0 + cw] = y.astype(o_ref.dtype)


def _proj(x, w, *, name, x_blk=None, x2=None, gain=None, rope=None, rope_cols=0, shift=0, res=None, add=None,
          add_blk=0, add_cols=0, rope_add=False, out_dtype=F32, tm=512, chunk=512):
    t = x.shape[0]
    k, xj = (x.shape[1], 0) if x_blk is None else x_blk
    n = w.shape[1]
    assert t % tm == 0 and n % LANES == 0 and rope_cols % LANES == 0 and add_cols % LANES == 0
    has_norm, has_rope, has_res, has_x2 = gain is not None, rope is not None, res is not None, x2 is not None
    assert not (has_norm and has_x2)
    args, specs = [x], [pl.BlockSpec((tm, k), lambda i: (i, xj))]
    if has_x2:
        args.append(x2)
        specs.append(pl.BlockSpec((tm, x2.shape[1]), lambda i: (i, 0)))
    if has_norm:
        args.append(gain.reshape(1, k).astype(F32))
        specs.append(pl.BlockSpec((1, k), lambda i: (0, 0)))
    args.append(w)
    specs.append(pl.BlockSpec((w.shape[0], n), lambda i: (0, 0)))
    if has_rope:
        for tb in rope:
            args.append(tb)
            specs.append(pl.BlockSpec((tm, LANES), lambda i: (i, 0)))
    if has_res:
        args.append(res)
        specs.append(pl.BlockSpec((tm, n), lambda i: (i, 0)))
    if add_cols:
        args.append(add)
        specs.append(pl.BlockSpec((tm, LANES), lambda i: (i, add_blk)))
    body = functools.partial(_proj_body, has_norm=has_norm, has_x2=has_x2, has_rope=has_rope, has_res=has_res,
                             add_cols=add_cols, rope_add=rope_add,
                             rope_cols=rope_cols, shift=shift, chunk=chunk)
    return pl.pallas_call(
        body, name=name, grid=(t // tm,), in_specs=specs,
        out_specs=pl.BlockSpec((tm, n), lambda i: (i, 0)),
        out_shape=jax.ShapeDtypeStruct((t, n), out_dtype),
        compiler_params=_cparams(("parallel",)),
    )(*args)


def _ffn_body(x_ref, oa_ref, ob_ref, wo_ref, g_ref, wg_ref, wu_ref, wd_ref, fg_ref, o_ref, h_ref, xn_ref, acc_ref, *,
              final_norm):
    j = pl.program_id(1)

    @pl.when(j == 0)
    def _():
        ka = oa_ref.shape[1]
        h = (x_ref[...] + jnp.dot(oa_ref[...], wo_ref[0:ka, :], preferred_element_type=F32)
             + jnp.dot(ob_ref[...], wo_ref[ka:, :], preferred_element_type=F32))
        h_ref[...] = h
        y = h * lax.rsqrt(jnp.mean(h * h, axis=-1, keepdims=True) + NORM_EPS)
        xn_ref[...] = (y * g_ref[...]).astype(BF16)
        acc_ref[...] = jnp.zeros_like(acc_ref)

    xn = xn_ref[...]
    g = jnp.dot(xn, wg_ref[...], preferred_element_type=F32)
    u = jnp.dot(xn, wu_ref[...], preferred_element_type=F32)
    a = (jax.nn.silu(g) * u).astype(BF16)
    acc_ref[...] += jnp.dot(a, wd_ref[...], preferred_element_type=F32)

    @pl.when(j == pl.num_programs(1) - 1)
    def _():
        h = h_ref[...] + acc_ref[...]
        if final_norm:
            y = h * lax.rsqrt(jnp.mean(h * h, axis=-1, keepdims=True) + NORM_EPS)
            h = y * fg_ref[...]
        o_ref[...] = h


def _ffn(x, oa, ob, w_out, gain, wg, wu, wd, final_gain, *, final_norm, tm=512, tf=1408):
    t, d = x.shape
    f = wg.shape[1]
    assert t % tm == 0 and f % tf == 0 and oa.shape[1] + ob.shape[1] == w_out.shape[0]
    return pl.pallas_call(
        functools.partial(_ffn_body, final_norm=final_norm), name="ffn",
        grid=(t // tm, f // tf),
        in_specs=[
            pl.BlockSpec((tm, d), lambda i, j: (i, 0)),
            pl.BlockSpec((tm, oa.shape[1]), lambda i, j: (i, 0)),
            pl.BlockSpec((tm, ob.shape[1]), lambda i, j: (i, 0)),
            pl.BlockSpec(w_out.shape, lambda i, j: (0, 0)),
            pl.BlockSpec((1, d), lambda i, j: (0, 0)),
            pl.BlockSpec((d, tf), lambda i, j: (0, j)),
            pl.BlockSpec((d, tf), lambda i, j: (0, j)),
            pl.BlockSpec((tf, d), lambda i, j: (j, 0)),
            pl.BlockSpec((1, d), lambda i, j: (0, 0)),
        ],
        out_specs=pl.BlockSpec((tm, d), lambda i, j: (i, 0)),
        out_shape=jax.ShapeDtypeStruct((t, d), F32),
        scratch_shapes=[pltpu.VMEM((tm, d), F32), pltpu.VMEM((tm, d), BF16), pltpu.VMEM((tm, d), F32)],
        compiler_params=_cparams(("parallel", "arbitrary")),
    )(x, oa, ob, w_out, gain.reshape(1, d).astype(F32), wg, wu, wd, final_gain.reshape(1, d).astype(F32))


def _sweep(q, k_get, vt_get, s_ref, *, q0, tq, tk, dv, scale):
    cols = q.shape[0]

    def fold8(x, op):
        return op(x.reshape(x.shape[0] // 8, 8, cols), axis=0)

    def score_block(j0, n, mrun, n_masked):
        ks = pl.multiple_of(j0 * tk, tk)
        s = lax.dot_general(k_get(ks, n * tk), q, _NT, preferred_element_type=F32) * (scale * LOG2E)
        for u in range(n):
            su = s[u * tk:(u + 1) * tk]
            if u >= n - n_masked:
                kpos = ks + u * tk + lax.broadcasted_iota(jnp.int32, (tk, cols), 0)
                qpos = q0 + lax.rem(lax.broadcasted_iota(jnp.int32, (tk, cols), 1), tq)
                su = jnp.where(kpos <= qpos, su, NEG)
            s_ref[j0 + u] = su
            mrun = jnp.maximum(mrun, fold8(su, jnp.max))
        return mrun

    def pv_block(j0, n, carry, m):
        l8, acc = carry
        for u in range(n):
            p = jnp.exp2(s_ref[j0 + u] - m)
            l8 = l8 + fold8(p, jnp.sum)
            acc = acc + jnp.dot(vt_get(j0 + u), p.astype(BF16), preferred_element_type=F32)
        return l8, acc

    m_init = jnp.full((8, cols), M_INIT, F32)
    acc_init = (jnp.zeros((8, cols), F32), jnp.zeros((dv, cols), F32))
    n_diag = max(tq // tk, 1)
    assert UNROLL % n_diag == 0
    n_full = q0 // tk
    groups = n_full // UNROLL
    rem = n_full - groups * UNROLL
    tail0 = groups * UNROLL
    tails = range(n_diag, UNROLL + n_diag, n_diag)

    mrun = lax.fori_loop(0, groups, lambda i, t: score_block(UNROLL * i, UNROLL, t, 0), m_init)
    mrun = lax.switch(rem // n_diag, [functools.partial(score_block, tail0, n, n_masked=n_diag) for n in tails],
                      mrun)
    m = jnp.max(mrun, axis=0, keepdims=True)
    carry = lax.fori_loop(0, groups, lambda i, t: pv_block(UNROLL * i, UNROLL, t, m), acc_init)
    l8, acc = lax.switch(rem // n_diag, [functools.partial(pv_block, tail0, n, m=m) for n in tails], carry)
    l = jnp.sum(l8, axis=0, keepdims=True)
    return acc / jnp.maximum(l, 1e-30)


def _head_lanes(mode, v, lane):
    d = HEAD_DIM
    if mode == "diff":
        return jnp.logical_and(lane >= d * v, lane < d * (v + 1))
    if mode == "moba":
        lo = 2 * d + v * MOBA_BIAS_LANES
        return jnp.logical_or(jnp.logical_and(lane >= d * v, lane < d * (v + 1)),
                              jnp.logical_and(lane >= lo, lane < lo + MOBA_BIAS_LANES))
    return jnp.logical_and(lane >= LANES * v, lane < LANES * (v + 1))


def _pflash_body(*refs, mode, tq, tk, scale, lambda_init):
    if mode == "diff":
        q_ref, k_ref, v_ref, lq1_ref, lk1_ref, lq2_ref, lk2_ref, sg_ref, o_ref, s_ref, vt_ref = refs
    else:
        q_ref, k_ref, v_ref, o_ref, s_ref, vt_ref = refs
    qi = pl.program_id(1)
    n_chunks = v_ref.shape[1] // tk

    @pl.when(qi == 0)
    def _():
        for c in range(n_chunks):
            vt_ref[c] = v_ref[0, c * tk:(c + 1) * tk, :].astype(F32).T.astype(BF16)

    q = q_ref[0]
    lane = lax.broadcasted_iota(jnp.int32, q.shape, 1)
    d = HEAD_DIM
    outs = []
    for v in range(2):
        qv = jnp.where(_head_lanes(mode, v, lane), q, jnp.zeros_like(q))
        if mode == "diff":
            vt_get, dv = (lambda j: vt_ref[j]), LANES
        else:
            vt_get, dv = (lambda j, v=v: vt_ref[j, v * d:(v + 1) * d, :]), d
        outs.append(_sweep(qv, lambda ks, n: k_ref[0, pl.ds(ks, n), :], vt_get, s_ref,
                           q0=qi * tq, tq=tq, tk=tk, dv=dv, scale=scale))
    if mode == "diff":
        lam = (jnp.exp(jnp.sum(lq1_ref[...] * lk1_ref[...], axis=-1, keepdims=True))
               - jnp.exp(jnp.sum(lq2_ref[...] * lk2_ref[...], axis=-1, keepdims=True)) + lambda_init)
        d = outs[0] - lam * outs[1]
        y = d * lax.rsqrt(jnp.mean(d * d, axis=0, keepdims=True) + NORM_EPS)
        o_t = (y * sg_ref[...]) * (1.0 - lambda_init)
    else:
        o_t = jnp.concatenate(outs, axis=0)
    o_ref[0] = o_t.T.astype(o_ref.dtype)


def _pflash(q, k, v, *, name, mode, dk, q_map, k_map, v_map, batch, pairs, tq=1024, tk=512, scale=1.0,
            diff_params=None, lambda_init=0.0):
    s = v.shape[1]
    assert s % tk == 0 and tq % tk == 0 and s % tq == 0
    args = [q, k, v]
    in_specs = [pl.BlockSpec((1, tq, dk), q_map), pl.BlockSpec((1, s, dk), k_map),
                pl.BlockSpec((1, s, LANES), v_map)]
    if mode == "diff":
        lq1, lk1, lq2, lk2, subln = diff_params
        for a in (lq1, lk1, lq2, lk2):
            args.append(a.reshape(1, HEAD_DIM).astype(F32))
            in_specs.append(pl.BlockSpec((1, HEAD_DIM), lambda i, j: (0, 0)))
        args.append(subln.reshape(LANES, 1).astype(F32))
        in_specs.append(pl.BlockSpec((LANES, 1), lambda i, j: (0, 0)))
    body = functools.partial(_pflash_body, mode=mode, tq=tq, tk=tk, scale=scale, lambda_init=lambda_init)
    return pl.pallas_call(
        body, name=name, grid=(batch * pairs, s // tq), in_specs=in_specs,
        out_specs=pl.BlockSpec((1, tq, LANES), lambda i, j: (i // pairs, j, i % pairs)),
        out_shape=jax.ShapeDtypeStruct((batch, s, pairs * LANES), BF16),
        scratch_shapes=[pltpu.VMEM((s // tk, tk, tq), F32), pltpu.VMEM((s // tk, LANES, tk), BF16)],
        compiler_params=_cparams(("parallel", "arbitrary")),
    )(*args)


def _band(q, ka_ref, vt_ref, *, q0, tq, window):
    cols = q.shape[0]
    band = window + tq
    start = pl.multiple_of(jnp.maximum(q0 - window, 0), tq)
    s = lax.dot_general(ka_ref[pl.ds(start, band), :], q, _NT, preferred_element_type=F32) * LOG2E
    kpos = start + lax.broadcasted_iota(jnp.int32, (band, cols), 0)
    qpos = q0 + lax.rem(lax.broadcasted_iota(jnp.int32, (band, cols), 1), tq)
    ok = jnp.logical_and(kpos <= qpos, kpos > qpos - window)
    s = jnp.where(ok, s, NEG)
    p = jnp.exp2(s - jnp.max(s, axis=0, keepdims=True))
    l = jnp.sum(p, axis=0, keepdims=True)
    pb = p.astype(BF16)
    acc = None
    for c in range(band // tq):
        part = jnp.dot(vt_ref[start // tq + c], pb[c * tq:(c + 1) * tq], preferred_element_type=F32)
        acc = part if acc is None else acc + part
    return acc / jnp.maximum(l, 1e-30)


def _gflash_body(*refs, tq, tk, rep, groups, window, select):
    if select:
        q_ref, k_ref, v_ref, mb_ref, o_ref, s_ref, vt_ref, ka_ref, qa_ref = refs
    else:
        q_ref, k_ref, v_ref, o_ref, vt_ref, ka_ref, qa_ref = refs
    d = HEAD_DIM
    tv = vt_ref.shape[2]
    g = pl.program_id(0) % groups
    qi = pl.program_id(1)
    n_chunks = v_ref.shape[1] // tk

    @pl.when(qi == 0)
    def _():
        move = _place(LANES, LANES, g * d)
        for c in range(n_chunks):
            rows = slice(c * tk, (c + 1) * tk)
            vt = v_ref[0, rows, :].astype(F32).T
            vt = jnp.where(g == 0, vt[0:d], vt[d:2 * d]).astype(BF16)
            for u in range(tk // tv):
                vt_ref[c * (tk // tv) + u] = vt[:, u * tv:(u + 1) * tv]
            kg = jnp.dot(k_ref[0, rows, :], move, preferred_element_type=F32).astype(BF16)
            if select:
                blk = (c * tk + lax.broadcasted_iota(jnp.int32, (tk, LANES), 0)) // NSA_SEL_BLOCK
                lane = lax.broadcasted_iota(jnp.int32, (tk, LANES), 1)
                ka_ref[rows, 0:LANES] = jnp.where(blk == lane, 1.0, 0.0).astype(BF16)
                ka_ref[rows, LANES:2 * LANES] = kg
            else:
                ka_ref[rows, :] = kg

    q = q_ref[0]
    for r in range(rep):
        qr = jnp.dot(q, _place(rep * d, LANES, r * d, d ** -0.5), preferred_element_type=F32).astype(BF16)
        rows = slice(r * tq, (r + 1) * tq)
        if select:
            qa_ref[rows, 0:LANES] = mb_ref[0]
            qa_ref[rows, LANES:2 * LANES] = qr
        else:
            qa_ref[rows, :] = qr
    if window is None:
        o_t = _sweep(qa_ref[...], lambda ks, n: ka_ref[pl.ds(ks, n), :], lambda j: vt_ref[j], s_ref,
                     q0=qi * tq, tq=tq, tk=tk, dv=d, scale=1.0)
    else:
        o_t = _band(qa_ref[...], ka_ref, vt_ref, q0=qi * tq, tq=tq, window=window)
    for r in range(rep):
        o_ref[0, :, r * d:(r + 1) * d] = o_t[:, r * tq:(r + 1) * tq].T.astype(o_ref.dtype)


def _gflash(yq, yk, yv, mbias, *, name, q_blk, k_blk, v_blk, tq=128, tk=512, window=None):
    b, s, _ = yq.shape
    rep, groups, d = NSA_REP, NSA_GROUPS, HEAD_DIM
    assert groups * d == LANES and s % tk == 0 and tk % tq == 0
    select = mbias is not None
    dk = 2 * LANES if select else LANES
    assert window is None or (not select and window % tq == 0 and s >= window + tq)
    tv = tk if window is None else tq
    scratch = [pltpu.VMEM((s // tv, d, tv), BF16), pltpu.VMEM((s, dk), BF16), pltpu.VMEM((rep * tq, dk), BF16)]
    if window is None:
        scratch.insert(0, pltpu.VMEM((s // tk, tk, rep * tq), F32))
    args = [yq, yk, yv]
    in_specs = [pl.BlockSpec((1, tq, rep * d), lambda i, j: (i // groups, j, q_blk + i % groups)),
                pl.BlockSpec((1, s, LANES), lambda i, j: (i // groups, 0, k_blk)),
                pl.BlockSpec((1, s, LANES), lambda i, j: (i // groups, 0, v_blk))]
    if select:
        args.append(mbias)
        in_specs.append(pl.BlockSpec((1, tq, LANES), lambda i, j: (i, j, 0)))
    body = functools.partial(_gflash_body, tq=tq, tk=tk, rep=rep, groups=groups, window=window, select=select)
    return pl.pallas_call(
        body, name=name, grid=(b * groups, s // tq), in_specs=in_specs,
        out_specs=pl.BlockSpec((1, tq, rep * d), lambda i, j: (i // groups, j, i % groups)),
        out_shape=jax.ShapeDtypeStruct((b, s, groups * rep * d), F32),
        scratch_shapes=scratch,
        compiler_params=_cparams(("parallel", "arbitrary")),
    )(*args)


def _topk_mask(score, row, k):
    sel = None
    for _ in range(k):
        mx = jnp.max(score, axis=0, keepdims=True)
        idx = jnp.min(jnp.where(score == mx, row, LANES), axis=0, keepdims=True)
        hit = row == idx
        pick = jnp.logical_and(hit, mx > -jnp.inf)
        sel = pick if sel is None else jnp.logical_or(sel, pick)
        score = jnp.where(hit, -jnp.inf, score)
    return sel


def _moba_gate_body(q_ref, k_ref, qa_ref, ka_ref, kmean_ref, *, tq, seq):
    qi = pl.program_id(1)
    nbl = MOBA_BIAS_LANES

    @pl.when(qi == 0)
    def _():
        rowblk = lax.rem(lax.broadcasted_iota(jnp.int32, (LANES, seq), 0), nbl)
        colblk = lax.broadcasted_iota(jnp.int32, (LANES, seq), 1) // MOBA_BLOCK
        ind = jnp.where(rowblk == colblk, 1.0, 0.0).astype(BF16)
        ksum = jnp.dot(ind, k_ref[0], preferred_element_type=F32)
        rhead = lax.broadcasted_iota(jnp.int32, (LANES, LANES), 0) // nbl
        lhead = lax.broadcasted_iota(jnp.int32, (LANES, LANES), 1) // HEAD_DIM
        kmean_ref[...] = jnp.where(rhead == lhead, ksum * (1.0 / MOBA_BLOCK), 0.0)

    q = q_ref[0]
    q0 = pl.multiple_of(qi * tq, tq)
    km = kmean_ref[...]
    km_hi = km.astype(BF16)
    km_lo = (km - km_hi.astype(F32)).astype(BF16)
    gate = (lax.dot_general(km_hi, q, _NT, preferred_element_type=F32)
            + lax.dot_general(km_lo, q, _NT, preferred_element_type=F32))
    blk_t = lax.broadcasted_iota(jnp.int32, (nbl, tq), 0)
    own_t = (q0 + lax.broadcasted_iota(jnp.int32, (nbl, tq), 1)) // MOBA_BLOCK
    parts = []
    for v in range(2):
        g_v = jnp.where(blk_t < own_t, gate[v * nbl:(v + 1) * nbl], -jnp.inf)
        sel = jnp.logical_or(_topk_mask(g_v, blk_t, MOBA_TOPK), blk_t == own_t)
        parts.append(jnp.where(sel, 0.0, NEG))
    parts.append(jnp.zeros((LANES - 2 * nbl, tq), F32))
    bias = jnp.concatenate(parts, axis=0).T
    qa_ref[0, :, 0:LANES] = q * (HEAD_DIM ** -0.5)
    qa_ref[0, :, LANES:2 * LANES] = bias.astype(BF16)
    lane = lax.broadcasted_iota(jnp.int32, (tq, LANES), 1)
    own = (q0 + lax.broadcasted_iota(jnp.int32, (tq, LANES), 0)) // MOBA_BLOCK
    onehot = jnp.where(jnp.logical_and(lane < 2 * nbl, lax.rem(lane, nbl) == own), 1.0, 0.0)
    ka_ref[0, :, 0:LANES] = k_ref[0, pl.ds(q0, tq), :]
    ka_ref[0, :, LANES:2 * LANES] = onehot.astype(BF16)


def _moba_gate(y3, *, q_blk, k_blk, pairs, tq=512):
    b, s, _ = y3.shape
    assert s % MOBA_BLOCK == 0 and s // MOBA_BLOCK <= MOBA_BIAS_LANES and s % tq == 0
    out = jax.ShapeDtypeStruct((b * pairs, s, 2 * LANES), BF16)
    return pl.pallas_call(
        functools.partial(_moba_gate_body, tq=tq, seq=s), name="moba_gate",
        grid=(b * pairs, s // tq),
        in_specs=[pl.BlockSpec((1, tq, LANES), lambda i, j: (i // pairs, j, q_blk + i % pairs)),
                  pl.BlockSpec((1, s, LANES), lambda i, j: (i // pairs, 0, k_blk + i % pairs))],
        out_specs=[pl.BlockSpec((1, tq, 2 * LANES), lambda i, j: (i, j, 0)),
                   pl.BlockSpec((1, tq, 2 * LANES), lambda i, j: (i, j, 0))],
        out_shape=[out, out],
        scratch_shapes=[pltpu.VMEM((LANES, LANES), F32)],
        compiler_params=_cparams(("parallel", "arbitrary")),
    )(y3, y3)


def _nsa_compress_body(x_ref, pe_ref, w1_ref, w2_ref, c_ref, s1_ref, s2_ref, o_ref, *, rope):
    half = w1_ref.shape[0] // 2
    x = x_ref[0].astype(F32)
    lo = (x + pe_ref[:, 0:half]).astype(BF16)
    hi = (x + pe_ref[:, half:2 * half]).astype(BF16)
    a = jnp.dot(lo, w1_ref[0:half, :], preferred_element_type=F32)
    b = jnp.dot(hi, w1_ref[half:2 * half, :], preferred_element_type=F32)
    n = a.shape[0]
    h1 = a + pltpu.roll(b, n - 1, 0)
    y = jnp.dot(jax.nn.gelu(h1).astype(BF16), w2_ref[...], preferred_element_type=F32)
    if rope:
        y = _apply_rope(y, c_ref[0], s1_ref[0], s2_ref[0], ROPE_DIM // 2)
    o_ref[0] = y[:, 0:HEAD_DIM].astype(o_ref.dtype)


def _nsa_compress(xr, pe, w1, w2, tables, *, rope, groups):
    g, n, w = xr.shape
    hid = w1.shape[1]
    w2p = jnp.pad(w2, ((0, 0), (0, LANES - w2.shape[1])))
    tspec = pl.BlockSpec((1, n, LANES), lambda i: (i // groups, 0, 0))
    return pl.pallas_call(
        functools.partial(_nsa_compress_body, rope=rope), name="nsa_compress",
        grid=(g,),
        in_specs=[pl.BlockSpec((1, n, w), lambda i: (i, 0, 0)),
                  pl.BlockSpec((1, 2 * w), lambda i: (0, 0)),
                  pl.BlockSpec((2 * w, hid), lambda i: (0, 0)),
                  pl.BlockSpec((hid, LANES), lambda i: (0, 0)),
                  tspec, tspec, tspec],
        out_specs=pl.BlockSpec((1, n, HEAD_DIM), lambda i: (i, 0, 0)),
        out_shape=jax.ShapeDtypeStruct((g, n, HEAD_DIM), BF16),
        compiler_params=_cparams(("parallel",)),
    )(xr, pe.reshape(1, 2 * w).astype(F32), w1, w2p, *tables)


def _place(n_src, n_dst, shift, value=1.0):
    src = lax.broadcasted_iota(jnp.int32, (n_src, n_dst), 0)
    dst = lax.broadcasted_iota(jnp.int32, (n_src, n_dst), 1)
    return jnp.where(jnp.logical_and(src == dst + shift, dst < HEAD_DIM), value, 0.0).astype(BF16)


def _nsa_cmp_body(q_ref, kc_ref, vc_ref, oc_ref, mb_ref, kcp_ref, *, tq, rep):
    qi = pl.program_id(1)
    q0 = qi * tq
    nc = kc_ref.shape[1]
    width = q_ref.shape[-1]

    @pl.when(qi == 0)
    def _():
        for r in range(rep):
            src = lax.broadcasted_iota(jnp.int32, (HEAD_DIM, width), 0)
            dst = lax.broadcasted_iota(jnp.int32, (HEAD_DIM, width), 1)
            spread = jnp.where(dst == src + r * HEAD_DIM, HEAD_DIM ** -0.5, 0.0).astype(BF16)
            kcp_ref[r] = jnp.dot(kc_ref[0], spread, preferred_element_type=F32).astype(BF16)

    q = q_ref[0]

    def branches(nr):
        def run():
            vct = vc_ref[0, :, 0:nr]
            tpos = q0 + lax.broadcasted_iota(jnp.int32, (nr, tq), 1)
            cend = lax.broadcasted_iota(jnp.int32, (nr, tq), 0) * NSA_CMP_STRIDE + (NSA_CMP_LEN - 1)
            ok = cend <= tpos
            psum = jnp.zeros((nr, tq), F32)
            for r in range(rep):
                s = lax.dot_general(kcp_ref[r, 0:nr, :], q, _NT, preferred_element_type=F32)
                s = jnp.where(ok, s, NEG)
                m = jnp.max(s, axis=0, keepdims=True)
                e = jnp.where(ok, jnp.exp(s - m), 0.0)
                p = e * (1.0 / jnp.maximum(jnp.sum(e, axis=0, keepdims=True), 1e-30))
                o_t = jnp.dot(vct, p.astype(BF16), preferred_element_type=F32)
                oc_ref[0, :, r * HEAD_DIM:(r + 1) * HEAD_DIM] = o_t.T.astype(oc_ref.dtype)
                psum = psum + p
            sstart = lax.broadcasted_iota(jnp.int32, (LANES, nr), 0) * NSA_SEL_BLOCK
            cstart = lax.broadcasted_iota(jnp.int32, (LANES, nr), 1) * NSA_CMP_STRIDE
            ov = jnp.where(jnp.logical_and(cstart < sstart + NSA_SEL_BLOCK, cstart + NSA_CMP_LEN > sstart),
                           1.0, 0.0).astype(BF16)
            p_hi = psum.astype(BF16)
            r1 = psum - p_hi.astype(F32)
            p_mid = r1.astype(BF16)
            p_lo = (r1 - p_mid.astype(F32)).astype(BF16)
            return (jnp.dot(ov, p_hi, preferred_element_type=F32) + jnp.dot(ov, p_mid, preferred_element_type=F32)
                    + jnp.dot(ov, p_lo, preferred_element_type=F32))
        return run

    sizes = list(range(LANES, nc + 1, LANES))
    needed = (q0 + tq) // NSA_CMP_STRIDE
    imp = lax.switch(jnp.minimum((needed + LANES - 1) // LANES, len(sizes)) - 1, [branches(nr) for nr in sizes])
    blk = lax.broadcasted_iota(jnp.int32, (LANES, tq), 0)
    qb = (q0 + lax.broadcasted_iota(jnp.int32, (LANES, tq), 1)) // NSA_SEL_BLOCK
    forced = jnp.logical_or(blk == 0, jnp.logical_or(blk == qb, blk == qb - 1))
    imp = jnp.where(forced, imp + NSA_FORCE_BONUS, imp)
    imp = jnp.where(blk <= qb, imp, -jnp.inf)
    sel = _topk_mask(imp, blk, NSA_SEL_TOPK)
    mb_ref[0] = jnp.where(sel, 0.0, NEG).T.astype(mb_ref.dtype)


def _nsa_cmp(y3, kc, vc, *, tq=512):
    b, s, _ = y3.shape
    g, nc, d = kc.shape
    rep, n = NSA_REP, NSA_GROUPS
    assert s // NSA_SEL_BLOCK <= LANES and s % tq == 0
    return pl.pallas_call(
        functools.partial(_nsa_cmp_body, tq=tq, rep=rep), name="nsa_cmp",
        grid=(g, s // tq),
        in_specs=[pl.BlockSpec((1, tq, rep * d), lambda i, j: (i // n, j, i % n)),
                  pl.BlockSpec((1, nc, d), lambda i, j: (i, 0, 0)),
                  pl.BlockSpec((1, d, nc), lambda i, j: (i, 0, 0))],
        out_specs=[pl.BlockSpec((1, tq, rep * d), lambda i, j: (i // n, j, i % n)),
                   pl.BlockSpec((1, tq, LANES), lambda i, j: (i, j, 0))],
        out_shape=[jax.ShapeDtypeStruct((b, s, n * rep * d), F32),
                   jax.ShapeDtypeStruct((g, s, LANES), BF16)],
        scratch_shapes=[pltpu.VMEM((rep, nc, rep * d), BF16)],
        compiler_params=_cparams(("parallel", "arbitrary")),
    )(y3, kc, vc)


def _nsa_combine_body(oc_ref, os_ref, ow_ref, g_ref, b_ref, e_ref, o_ref):
    gs = jax.nn.sigmoid(g_ref[...] + b_ref[...])
    g_hi = gs.astype(BF16)
    g_lo = (gs - g_hi.astype(F32)).astype(BF16)
    out = None
    for i, ref in enumerate((oc_ref, os_ref, ow_ref)):
        w = (jnp.dot(g_hi, e_ref[i], preferred_element_type=F32)
             + jnp.dot(g_lo, e_ref[i], preferred_element_type=F32))
        term = w * ref[...]
        out = term if out is None else out + term
    o_ref[...] = out.astype(o_ref.dtype)


def _nsa_combine(oc, osel, ow, yg, g_blk, gate_b, *, tm=512):
    t, n = oc.shape
    nh = NSA_HEADS * 3
    bp = jnp.pad(gate_b.reshape(1, nh).astype(F32), ((0, 0), (0, LANES - nh)))
    row = jnp.arange(LANES)[:, None]
    col = jnp.arange(n)[None, :] // HEAD_DIM
    expand = jnp.stack([(row == col * 3 + i) for i in range(3)]).astype(BF16)
    tok = pl.BlockSpec((tm, n), lambda i: (i, 0))
    return pl.pallas_call(
        _nsa_combine_body, name="nsa_combine", grid=(t // tm,),
        in_specs=[tok, tok, tok, pl.BlockSpec((tm, LANES), lambda i: (i, g_blk)),
                  pl.BlockSpec((1, LANES), lambda i: (0, 0)),
                  pl.BlockSpec((3, LANES, n), lambda i: (0, 0, 0))],
        out_specs=tok,
        out_shape=jax.ShapeDtypeStruct((t, n), BF16),
        compiler_params=_cparams(("parallel",)),
    )(oc, osel, ow, yg, bp, expand)


def _to_heads(x, b, s, n):
    return x.reshape(b, s, n, -1).transpose(0, 2, 1, 3).reshape(b * n, s, -1)


def _even_mixer(h, gain, layer_idx, rope16, w_in, lq1, lk1, lq2, lk2, subln, b, s):
    na = MOBA_HEADS * HEAD_DIM
    nb = DIFF_HEADS * 2 * HEAD_DIM
    w_rope = jnp.concatenate([w_in[:, 0:2 * na], w_in[:, 3 * na:3 * na + 2 * nb]], axis=1)
    w_rest = jnp.concatenate([w_in[:, 2 * na:3 * na], w_in[:, 3 * na + 2 * nb:]], axis=1)
    w = jnp.concatenate([w_rope, w_rest], axis=1).astype(BF16)
    y = _proj(h, w, name="proj_even_in", gain=gain, rope=rope16, rope_cols=w_rope.shape[1], shift=ROPE_DIM // 2, out_dtype=BF16)
    y3 = y.reshape(b, s, -1)
    blk = lambda off: off // LANES
    pairs = na // LANES
    qa, ka = _moba_gate(y3, q_blk=blk(0), k_blk=blk(na), pairs=pairs)
    o_a = _pflash(qa, ka, y3, name="flash_moba", mode="moba", dk=2 * LANES, batch=b, pairs=pairs,
                  q_map=lambda i, j: (i, j, 0), k_map=lambda i, j: (i, 0, 0),
                  v_map=lambda i, j: (i // pairs, 0, blk(2 * na + 2 * nb) + i % pairs))
    lambda_init = 0.8 - 0.6 * math.exp(-0.3 * layer_idx)
    nh = DIFF_HEADS
    o_b = _pflash(y3, y3, y3, name="flash_diff", mode="diff", dk=LANES, batch=b, pairs=nh, scale=HEAD_DIM ** -0.5,
                  q_map=lambda i, j: (i // nh, j, blk(2 * na) + i % nh),
                  k_map=lambda i, j: (i // nh, 0, blk(2 * na + nb) + i % nh),
                  v_map=lambda i, j: (i // nh, 0, blk(3 * na + 2 * nb) + i % nh),
                  diff_params=(lq1, lk1, lq2, lk2, subln), lambda_init=lambda_init)
    t = b * s
    return o_a.reshape(t, -1), o_b.reshape(t, -1)


def _odd_mixer(h, gain, positions, rope16, rope32s, w_in, gate_b, pe_k, pe_v, k_w1, k_w2, v_w1, v_w2,
               q_norm, w_uq, kv_norm, w_ukv, b, s):
    G, R, d = NSA_GROUPS, NSA_REP, HEAD_DIM
    sizes = [NSA_HEADS * d] + [G * d] * 6 + [NSA_HEADS * 3, MLA_Q_RANK, MLA_KV_RANK, MLA_ROPE]
    offs = [0]
    for z in sizes:
        offs.append(offs[-1] + z)
    col = lambda i: w_in[:, offs[i]:offs[i + 1]]
    w_r = jnp.concatenate([col(0), col(3), col(5)], axis=1).astype(BF16)
    yr = _proj(h, w_r, name="proj_odd_rope", gain=gain, rope=rope16, rope_cols=w_r.shape[1], shift=ROPE_DIM // 2, out_dtype=BF16)
    w_p = jnp.concatenate([col(1), col(2), col(4), col(6), col(8), col(9), col(7)], axis=1)
    w_p = jnp.pad(w_p, ((0, 0), (0, (-w_p.shape[1]) % LANES)))
    w_pe = jnp.pad(col(10), ((0, 0), (MLA_NOPE, LANES - MLA_NOPE - MLA_ROPE)))
    yp = _proj(h, jnp.concatenate([w_p, w_pe], axis=1).astype(BF16), name="proj_odd_plain", gain=gain, out_dtype=F32)
    k_cmp, v_cmp = yp[:, 0:G * d], yp[:, G * d:2 * G * d]
    cq_blk = 4 * G * d // MLA_Q_RANK
    ckv_blk = (4 * G * d + MLA_Q_RANK) // MLA_KV_RANK
    gate_blk = (4 * G * d + MLA_Q_RANK + MLA_KV_RANK) // LANES
    kpe_blk = w_p.shape[1] // LANES

    yr3, yp3 = yr.reshape(b, s, -1), yp.reshape(b, s, -1)
    nc = s // NSA_CMP_STRIDE
    cpos = jnp.concatenate([positions[:, NSA_CMP_LEN - 1::NSA_CMP_STRIDE], positions[:, -1:]], axis=1)
    ctab = [t.reshape(b, nc, LANES) for t in _rope_tables(cpos, ROPE_DIM, HEAD_DIM)]
    xk = _to_heads(k_cmp, b, s, G).reshape(b * G, nc, NSA_CMP_STRIDE * d)
    xv = _to_heads(v_cmp, b, s, G).reshape(b * G, nc, NSA_CMP_STRIDE * d)
    kc = _nsa_compress(xk, pe_k, k_w1.astype(BF16), k_w2.astype(BF16), ctab, rope=True, groups=G)
    vc = _nsa_compress(xv, pe_v, v_w1.astype(BF16), v_w2.astype(BF16), ctab, rope=False, groups=G)
    o_c, mbias = _nsa_cmp(yr3, kc, vc.transpose(0, 2, 1))
    k_blk = NSA_HEADS * d // LANES
    vsw = yp[:, 2 * G * d:4 * G * d].astype(BF16).reshape(b, s, -1)
    o_s = _gflash(yr3, yr3, vsw, mbias, name="flash_sel", q_blk=0, k_blk=k_blk, v_blk=0, tq=256)
    o_w = _gflash(yr3, yr3, vsw, None, name="flash_win", q_blk=0, k_blk=k_blk + 1, v_blk=1, window=NSA_WINDOW)
    t = b * s
    o_nsa = _nsa_combine(o_c.reshape(t, -1), o_s.reshape(t, -1), o_w.reshape(t, -1), yp, gate_blk, gate_b)

    hq = MLA_NOPE + MLA_ROPE
    nh = MLA_HEADS
    wq = jnp.pad(w_uq.reshape(MLA_Q_RANK, nh, hq), ((0, 0), (0, 0), (0, LANES - hq))).reshape(MLA_Q_RANK, nh * LANES)
    q = _proj(yp, wq.astype(BF16), name="proj_mla_q", x_blk=(MLA_Q_RANK, cq_blk), gain=q_norm, rope=rope32s,
              rope_cols=nh * LANES, shift=MLA_ROPE // 2, out_dtype=BF16)
    wkv = w_ukv.reshape(MLA_KV_RANK, nh, MLA_NOPE + MLA_V)
    wk = jnp.pad(wkv[:, :, :MLA_NOPE], ((0, 0), (0, 0), (0, LANES - MLA_NOPE))).reshape(MLA_KV_RANK, nh * LANES)
    wkv = jnp.concatenate([wk, wkv[:, :, MLA_NOPE:].reshape(MLA_KV_RANK, nh * MLA_V)], axis=1).astype(BF16)
    kv = _proj(yp, wkv, name="proj_mla_kv", x_blk=(MLA_KV_RANK, ckv_blk), gain=kv_norm, rope=rope32s,
               shift=MLA_ROPE // 2, add=yp, add_blk=kpe_blk, add_cols=nh * LANES, rope_add=True,
               out_dtype=BF16)
    q3, kv3 = q.reshape(b, s, -1), kv.reshape(b, s, -1)
    pairs = nh // 2
    o_d = _pflash(q3, kv3, kv3, name="flash_mla", mode="slots", dk=2 * LANES, batch=b, pairs=pairs, scale=hq ** -0.5,
                  q_map=lambda i, j: (i // pairs, j, i % pairs), k_map=lambda i, j: (i // pairs, 0, i % pairs),
                  v_map=lambda i, j: (i // pairs, 0, nh + i % pairs))

    return o_nsa, o_d.reshape(t, -1)


def kernel(x, positions, attn_norm, ffn_norm, final_norm, ffn_w_gate, ffn_w_up, ffn_w_down, ev_w_in, ev_w_out, diff_lambda_q1, diff_lambda_k1, diff_lambda_q2, diff_lambda_k2, diff_subln, od_w_in, od_w_out, nsa_gate_b, nsa_pe_k, nsa_pe_v, nsa_k_w1, nsa_k_w2, nsa_v_w1, nsa_v_w2, mla_q_norm, mla_w_uq, mla_kv_norm, mla_w_ukv):
    b, s, d = x.shape
    depth = attn_norm.shape[0]
    rope16 = _rope_tables(positions, ROPE_DIM, HEAD_DIM)
    rope32s = _rope_tables(positions, MLA_ROPE, LANES, offset=MLA_NOPE)
    h = x.reshape(b * s, d)
    for l in range(depth):
        i = l // 2
        if l % 2 == 0:
            oa, ob = _even_mixer(h, attn_norm[l], l, rope16, ev_w_in[i], diff_lambda_q1[i],
                                 diff_lambda_k1[i], diff_lambda_q2[i], diff_lambda_k2[i], diff_subln[i], b, s)
            w_out = ev_w_out[i]
        else:
            oa, ob = _odd_mixer(h, attn_norm[l], positions, rope16, rope32s, od_w_in[i], nsa_gate_b[i],
                                nsa_pe_k[i], nsa_pe_v[i], nsa_k_w1[i], nsa_k_w2[i], nsa_v_w1[i], nsa_v_w2[i],
                                mla_q_norm[i], mla_w_uq[i], mla_kv_norm[i], mla_w_ukv[i], b, s)
            w_out = od_w_out[i]
        h = _ffn(h, oa, ob, w_out.astype(BF16), ffn_norm[l], ffn_w_gate[l].astype(BF16), ffn_w_up[l].astype(BF16),
                 ffn_w_down[l].astype(BF16), final_norm, final_norm=(l == depth - 1))
    return h.reshape(b, s, d)
```

```python
import functools
import math

import jax
import jax.numpy as jnp
from jax import lax
from jax.experimental import pallas as pl
from jax.experimental.pallas import tpu as pltpu

F32 = jnp.float32
BF16 = jnp.bfloat16

D_MODEL = 1024
HEAD_DIM = 64
ROPE_THETA = 500000.0
ROPE_DIM = HEAD_DIM // 4
NORM_EPS = 1e-5
D_FF = 2816

MOBA_HEADS = 8
MOBA_BLOCK = 256
MOBA_TOPK = 3
DIFF_HEADS = 4
DIFF_V = 2 * HEAD_DIM
NSA_HEADS = 8
NSA_GROUPS = 2
NSA_REP = NSA_HEADS // NSA_GROUPS
NSA_CMP_LEN = 32
NSA_CMP_STRIDE = 16
NSA_CMP_HIDDEN = 256
NSA_SEL_BLOCK = 64
NSA_SEL_TOPK = 16
NSA_WINDOW = 512
NSA_FORCE_BONUS = 1e3
MLA_HEADS = 8
MLA_Q_RANK = 256
MLA_KV_RANK = 128
MLA_NOPE = 64
MLA_ROPE = 32
MLA_V = 64

LANES = 128
MOBA_BIAS_LANES = 32
NEG = -1e30
M_INIT = -1e37
LOG2E = 1.4426950408889634
UNROLL = 4
VMEM_LIMIT = 56 * 1024 * 1024

_NT = (((1,), (1,)), ((), ()))


def _cparams(sem):
    return pltpu.CompilerParams(dimension_semantics=sem, vmem_limit_bytes=VMEM_LIMIT)


def _rope_tables(positions, dim, period, offset=0):
    r = dim // 2
    inv = 1.0 / (ROPE_THETA ** (jnp.arange(0, dim, 2, dtype=F32) / dim))
    ang = positions.astype(F32)[..., None] * inv
    cos, sin = jnp.cos(ang), jnp.sin(ang)
    const = lambda n, val: jnp.full(ang.shape[:-1] + (n,), val, F32)
    rest = period - offset - 2 * r
    zr = jnp.zeros_like(sin)
    c = jnp.concatenate([const(offset, 1.0), cos, cos, const(rest, 1.0)], -1)
    s1 = jnp.concatenate([const(offset, 0.0), -sin, zr, const(rest, 0.0)], -1)
    s2 = jnp.concatenate([const(offset, 0.0), zr, sin, const(rest, 0.0)], -1)
    reps = LANES // period
    tile = lambda t: jnp.tile(t, (1,) * (t.ndim - 1) + (reps,)).reshape(-1, LANES)
    return tile(c), tile(s1), tile(s2)


def _apply_rope(y, c, s1, s2, shift):
    return y * c + pltpu.roll(y, LANES - shift, 1) * s1 + pltpu.roll(y, shift, 1) * s2


def _proj_body(*refs, has_norm, has_x2, has_rope, has_res, add_cols, rope_add, rope_cols, shift, chunk):
    it = iter(refs)
    x_ref = next(it)
    x2_ref = next(it) if has_x2 else None
    g_ref = next(it) if has_norm else None
    w_ref = next(it)
    if has_rope:
        c_ref, s1_ref, s2_ref = next(it), next(it), next(it)
    res_ref = next(it) if has_res else None
    add_ref = next(it) if add_cols else None
    o_ref = next(it)
    n = w_ref.shape[1]
    if has_norm:
        xf = x_ref[...].astype(F32)
        y = xf * lax.rsqrt(jnp.mean(xf * xf, axis=-1, keepdims=True) + NORM_EPS)
        xb = (y * g_ref[...]).astype(BF16)
    else:
        xb = x_ref[...].astype(BF16)
    k1 = xb.shape[1]
    if add_cols:
        add = add_ref[...].astype(F32)
        if rope_add:
            add = _apply_rope(add, c_ref[...], s1_ref[...], s2_ref[...], shift)
    for c0 in range(0, n, chunk):
        cw = min(chunk, n - c0)
        y = jnp.dot(xb, w_ref[0:k1, c0:c0 + cw], preferred_element_type=F32)
        if has_x2:
            y = y + jnp.dot(x2_ref[...].astype(BF16), w_ref[k1:, c0:c0 + cw], preferred_element_type=F32)
        if has_res:
            y = y + res_ref[:, c0:c0 + cw]
        if (has_rope and c0 < rope_cols) or c0 < add_cols:
            for k0 in range(0, cw, LANES):
                ys = y[:, k0:k0 + LANES]
                if has_rope and c0 + k0 < rope_cols:
                    ys = _apply_rope(ys, c_ref[...], s1_ref[...], s2_ref[...], shift)
                if c0 + k0 < add_cols:
                    ys = ys + add
                o_ref[:, c0 + k0:c0 + k0 + LANES] = ys.astype(o_ref.dtype)
        else:
            o_ref[:, c0:c0 + cw] = y.astype(o_ref.dtype)


def _proj(x, w, *, name, x_blk=None, x2=None, gain=None, rope=None, rope_cols=0, shift=0, res=None, add=None,
          add_blk=0, add_cols=0, rope_add=False, out_dtype=F32, tm=512, chunk=512):
    t = x.shape[0]
    k, xj = (x.shape[1], 0) if x_blk is None else x_blk
    n = w.shape[1]
    assert t % tm == 0 and n % LANES == 0 and rope_cols % LANES == 0 and add_cols % LANES == 0
    has_norm, has_rope, has_res, has_x2 = gain is not None, rope is not None, res is not None, x2 is not None
    assert not (has_norm and has_x2)
    args, specs = [x], [pl.BlockSpec((tm, k), lambda i: (i, xj))]
    if has_x2:
        args.append(x2)
        specs.append(pl.BlockSpec((tm, x2.shape[1]), lambda i: (i, 0)))
    if has_norm:
        args.append(gain.reshape(1, k).astype(F32))
        specs.append(pl.BlockSpec((1, k), lambda i: (0, 0)))
    args.append(w)
    specs.append(pl.BlockSpec((w.shape[0], n), lambda i: (0, 0)))
    if has_rope:
        for tb in rope:
            args.append(tb)
            specs.append(pl.BlockSpec((tm, LANES), lambda i: (i, 0)))
    if has_res:
        args.append(res)
        specs.append(pl.BlockSpec((tm, n), lambda i: (i, 0)))
    if add_cols:
        args.append(add)
        specs.append(pl.BlockSpec((tm, LANES), lambda i: (i, add_blk)))
    body = functools.partial(_proj_body, has_norm=has_norm, has_x2=has_x2, has_rope=has_rope, has_res=has_res,
                             add_cols=add_cols, rope_add=rope_add,
                             rope_cols=rope_cols, shift=shift, chunk=chunk)
    return pl.pallas_call(
        body, name=name, grid=(t // tm,), in_specs=specs,
        out_specs=pl.BlockSpec((tm, n), lambda i: (i, 0)),
        out_shape=jax.ShapeDtypeStruct((t, n), out_dtype),
        compiler_params=_cparams(("parallel",)),
    )(*args)


def _ffn_body(x_ref, oa_ref, ob_ref, wo_ref, g_ref, wg_ref, wu_ref, wd_ref, fg_ref, o_ref, h_ref, xn_ref, acc_ref, *,
              final_norm):
    j = pl.program_id(1)

    @pl.when(j == 0)
    def _():
        ka = oa_ref.shape[1]
        h = (x_ref[...] + jnp.dot(oa_ref[...], wo_ref[0:ka, :], preferred_element_type=F32)
             + jnp.dot(ob_ref[...], wo_ref[ka:, :], preferred_element_type=F32))
        h_ref[...] = h
        y = h * lax.rsqrt(jnp.mean(h * h, axis=-1, keepdims=True) + NORM_EPS)
        xn_ref[...] = (y * g_ref[...]).astype(BF16)
        acc_ref[...] = jnp.zeros_like(acc_ref)

    xn = xn_ref[...]
    g = jnp.dot(xn, wg_ref[...], preferred_element_type=F32)
    u = jnp.dot(xn, wu_ref[...], preferred_element_type=F32)
    a = (jax.nn.silu(g) * u).astype(BF16)
    acc_ref[...] += jnp.dot(a, wd_ref[...], preferred_element_type=F32)

    @pl.when(j == pl.num_programs(1) - 1)
    def _():
        h = h_ref[...] + acc_ref[...]
        if final_norm:
            y = h * lax.rsqrt(jnp.mean(h * h, axis=-1, keepdims=True) + NORM_EPS)
            h = y * fg_ref[...]
        o_ref[...] = h


def _ffn(x, oa, ob, w_out, gain, wg, wu, wd, final_gain, *, final_norm, tm=512, tf=1408):
    t, d = x.shape
    f = wg.shape[1]
    assert t % tm == 0 and f % tf == 0 and oa.shape[1] + ob.shape[1] == w_out.shape[0]
    return pl.pallas_call(
        functools.partial(_ffn_body, final_norm=final_norm), name="ffn",
        grid=(t // tm, f // tf),
        in_specs=[
            pl.BlockSpec((tm, d), lambda i, j: (i, 0)),
            pl.BlockSpec((tm, oa.shape[1]), lambda i, j: (i, 0)),
            pl.BlockSpec((tm, ob.shape[1]), lambda i, j: (i, 0)),
            pl.BlockSpec(w_out.shape, lambda i, j: (0, 0)),
            pl.BlockSpec((1, d), lambda i, j: (0, 0)),
            pl.BlockSpec((d, tf), lambda i, j: (0, j)),
            pl.BlockSpec((d, tf), lambda i, j: (0, j)),
            pl.BlockSpec((tf, d), lambda i, j: (j, 0)),
            pl.BlockSpec((1, d), lambda i, j: (0, 0)),
        ],
        out_specs=pl.BlockSpec((tm, d), lambda i, j: (i, 0)),
        out_shape=jax.ShapeDtypeStruct((t, d), F32),
        scratch_shapes=[pltpu.VMEM((tm, d), F32), pltpu.VMEM((tm, d), BF16), pltpu.VMEM((tm, d), F32)],
        compiler_params=_cparams(("parallel", "arbitrary")),
    )(x, oa, ob, w_out, gain.reshape(1, d).astype(F32), wg, wu, wd, final_gain.reshape(1, d).astype(F32))


def _sweep(q, k_get, vt_get, s_ref, *, q0, tq, tk, dv, scale):
    cols = q.shape[0]

    def fold8(x, op):
        return op(x.reshape(x.shape[0] // 8, 8, x.shape[1]), axis=0)

    def first_col(u, n, n_masked):
        d = u - (n - n_masked)
        return d * tk if (d > 0 and cols == tq) else 0

    def upd(full, c0, f):
        return f(full) if c0 == 0 else jnp.concatenate([full[:, :c0], f(full[:, c0:])], axis=1)

    def score_block(j0, n, mrun, n_masked):
        ks = pl.multiple_of(j0 * tk, tk)
        n_wide = sum(1 for u in range(n) if first_col(u, n, n_masked) == 0)
        s = lax.dot_general(k_get(ks, n_wide * tk), q, _NT, preferred_element_type=F32) * (scale * LOG2E)
        for u in range(n):
            c0 = first_col(u, n, n_masked)
            if c0 == 0:
                su = s[u * tk:(u + 1) * tk]
            else:
                ku = k_get(pl.multiple_of(ks + u * tk, tk), tk)
                su = lax.dot_general(ku, q[c0:], _NT, preferred_element_type=F32) * (scale * LOG2E)
            if u >= n - n_masked:
                kpos = ks + u * tk + lax.broadcasted_iota(jnp.int32, su.shape, 0)
                qpos = q0 + c0 + lax.rem(lax.broadcasted_iota(jnp.int32, su.shape, 1), tq)
                su = jnp.where(kpos <= qpos, su, NEG)
            s_ref[j0 + u, :, c0:cols] = su
            mrun = upd(mrun, c0, lambda t: jnp.maximum(t, fold8(su, jnp.max)))
        return mrun

    def pv_block(j0, n, carry, m, n_masked=0):
        l8, acc = carry
        for u in range(n):
            c0 = first_col(u, n, n_masked)
            p = jnp.exp2(s_ref[j0 + u, :, c0:cols] - m[:, c0:])
            l8 = upd(l8, c0, lambda t: t + fold8(p, jnp.sum))
            pv = jnp.dot(vt_get(j0 + u), p.astype(BF16), preferred_element_type=F32)
            acc = upd(acc, c0, lambda t: t + pv)
        return l8, acc

    m_init = jnp.full((8, cols), M_INIT, F32)
    acc_init = (jnp.zeros((8, cols), F32), jnp.zeros((dv, cols), F32))
    n_diag = max(tq // tk, 1)
    assert UNROLL % n_diag == 0
    n_full = q0 // tk
    groups = n_full // UNROLL
    rem = n_full - groups * UNROLL
    tail0 = groups * UNROLL
    tails = range(n_diag, UNROLL + n_diag, n_diag)

    mrun = lax.fori_loop(0, groups, lambda i, t: score_block(UNROLL * i, UNROLL, t, 0), m_init)
    mrun = lax.switch(rem // n_diag, [functools.partial(score_block, tail0, n, n_masked=n_diag) for n in tails],
                      mrun)
    m = jnp.max(mrun, axis=0, keepdims=True)
    carry = lax.fori_loop(0, groups, lambda i, t: pv_block(UNROLL * i, UNROLL, t, m), acc_init)
    l8, acc = lax.switch(rem // n_diag, [functools.partial(pv_block, tail0, n, m=m, n_masked=n_diag) for n in tails],
                         carry)
    l = jnp.sum(l8, axis=0, keepdims=True)
    return acc / jnp.maximum(l, 1e-30)


def _head_lanes(mode, v, lane):
    d = HEAD_DIM
    if mode == "diff":
        return jnp.logical_and(lane >= d * v, lane < d * (v + 1))
    if mode == "moba":
        lo = 2 * d + v * MOBA_BIAS_LANES
        return jnp.logical_or(jnp.logical_and(lane >= d * v, lane < d * (v + 1)),
                              jnp.logical_and(lane >= lo, lane < lo + MOBA_BIAS_LANES))
    return jnp.logical_and(lane >= LANES * v, lane < LANES * (v + 1))


def _pflash_body(*refs, mode, tq, tk, scale, lambda_init):
    if mode == "diff":
        q_ref, k_ref, v_ref, lq1_ref, lk1_ref, lq2_ref, lk2_ref, sg_ref, o_ref, s_ref, vt_ref = refs
    else:
        q_ref, k_ref, v_ref, o_ref, s_ref, vt_ref = refs
    qi = pl.program_id(1)
    n_chunks = v_ref.shape[1] // tk

    @pl.when(qi == 0)
    def _():
        for c in range(n_chunks):
            vt_ref[c] = v_ref[0, c * tk:(c + 1) * tk, :].astype(F32).T.astype(BF16)

    q = q_ref[0]
    lane = lax.broadcasted_iota(jnp.int32, q.shape, 1)
    d = HEAD_DIM
    outs = []
    for v in range(2):
        qv = jnp.where(_head_lanes(mode, v, lane), q, jnp.zeros_like(q))
        if mode == "diff":
            vt_get, dv = (lambda j: vt_ref[j]), LANES
        else:
            vt_get, dv = (lambda j, v=v: vt_ref[j, v * d:(v + 1) * d, :]), d
        outs.append(_sweep(qv, lambda ks, n: k_ref[0, pl.ds(ks, n), :], vt_get, s_ref,
                           q0=qi * tq, tq=tq, tk=tk, dv=dv, scale=scale))
    if mode == "diff":
        lam = (jnp.exp(jnp.sum(lq1_ref[...] * lk1_ref[...], axis=-1, keepdims=True))
               - jnp.exp(jnp.sum(lq2_ref[...] * lk2_ref[...], axis=-1, keepdims=True)) + lambda_init)
        d = outs[0] - lam * outs[1]
        y = d * lax.rsqrt(jnp.mean(d * d, axis=0, keepdims=True) + NORM_EPS)
        o_t = (y * sg_ref[...]) * (1.0 - lambda_init)
    else:
        o_t = jnp.concatenate(outs, axis=0)
    o_ref[0] = o_t.T.astype(o_ref.dtype)


def _pflash(q, k, v, *, name, mode, dk, q_map, k_map, v_map, batch, pairs, tq=1024, tk=512, scale=1.0,
            diff_params=None, lambda_init=0.0):
    s = v.shape[1]
    assert s % tk == 0 and tq % tk == 0 and s % tq == 0
    args = [q, k, v]
    in_specs = [pl.BlockSpec((1, tq, dk), q_map), pl.BlockSpec((1, s, dk), k_map),
                pl.BlockSpec((1, s, LANES), v_map)]
    if mode == "diff":
        lq1, lk1, lq2, lk2, subln = diff_params
        for a in (lq1, lk1, lq2, lk2):
            args.append(a.reshape(1, HEAD_DIM).astype(F32))
            in_specs.append(pl.BlockSpec((1, HEAD_DIM), lambda i, j: (0, 0)))
        args.append(subln.reshape(LANES, 1).astype(F32))
        in_specs.append(pl.BlockSpec((LANES, 1), lambda i, j: (0, 0)))
    body = functools.partial(_pflash_body, mode=mode, tq=tq, tk=tk, scale=scale, lambda_init=lambda_init)
    return pl.pallas_call(
        body, name=name, grid=(batch * pairs, s // tq), in_specs=in_specs,
        out_specs=pl.BlockSpec((1, tq, LANES), lambda i, j: (i // pairs, j, i % pairs)),
        out_shape=jax.ShapeDtypeStruct((batch, s, pairs * LANES), BF16),
        scratch_shapes=[pltpu.VMEM((s // tk, tk, tq), F32), pltpu.VMEM((s // tk, LANES, tk), BF16)],
        compiler_params=_cparams(("parallel", "arbitrary")),
    )(*args)


def _band(q, ka_ref, vt_ref, *, q0, tq, window):
    cols = q.shape[0]
    band = window + tq
    start = pl.multiple_of(jnp.maximum(q0 - window, 0), tq)
    s = lax.dot_general(ka_ref[pl.ds(start, band), :], q, _NT, preferred_element_type=F32) * LOG2E
    kpos = start + lax.broadcasted_iota(jnp.int32, (band, cols), 0)
    qpos = q0 + lax.rem(lax.broadcasted_iota(jnp.int32, (band, cols), 1), tq)
    ok = jnp.logical_and(kpos <= qpos, kpos > qpos - window)
    s = jnp.where(ok, s, NEG)
    p = jnp.exp2(s - jnp.max(s, axis=0, keepdims=True))
    l = jnp.sum(p, axis=0, keepdims=True)
    pb = p.astype(BF16)
    acc = None
    for c in range(band // tq):
        part = jnp.dot(vt_ref[start // tq + c], pb[c * tq:(c + 1) * tq], preferred_element_type=F32)
        acc = part if acc is None else acc + part
    return acc / jnp.maximum(l, 1e-30)


def _gflash_body(*refs, tq, tk, rep, groups, window, select):
    if select:
        q_ref, k_ref, v_ref, mb_ref, o_ref, s_ref, vt_ref, ka_ref, qa_ref = refs
    else:
        q_ref, k_ref, v_ref, o_ref, vt_ref, ka_ref, qa_ref = refs
    d = HEAD_DIM
    tv = vt_ref.shape[2]
    g = pl.program_id(0) % groups
    qi = pl.program_id(1)
    n_chunks = v_ref.shape[1] // tk

    @pl.when(qi == 0)
    def _():
        move = _place(LANES, LANES, g * d)
        for c in range(n_chunks):
            rows = slice(c * tk, (c + 1) * tk)
            vt = v_ref[0, rows, :].astype(F32).T
            vt = jnp.where(g == 0, vt[0:d], vt[d:2 * d]).astype(BF16)
            for u in range(tk // tv):
                vt_ref[c * (tk // tv) + u] = vt[:, u * tv:(u + 1) * tv]
            kg = jnp.dot(k_ref[0, rows, :], move, preferred_element_type=F32).astype(BF16)
            if select:
                blk = (c * tk + lax.broadcasted_iota(jnp.int32, (tk, LANES), 0)) // NSA_SEL_BLOCK
                lane = lax.broadcasted_iota(jnp.int32, (tk, LANES), 1)
                ka_ref[rows, 0:LANES] = jnp.where(blk == lane, 1.0, 0.0).astype(BF16)
                ka_ref[rows, LANES:2 * LANES] = kg
            else:
                ka_ref[rows, :] = kg

    q = q_ref[0]
    for r in range(rep):
        qr = jnp.dot(q, _place(rep * d, LANES, r * d, d ** -0.5), preferred_element_type=F32).astype(BF16)
        rows = slice(r * tq, (r + 1) * tq)
        if select:
            qa_ref[rows, 0:LANES] = mb_ref[0]
            qa_ref[rows, LANES:2 * LANES] = qr
        else:
            qa_ref[rows, :] = qr
    if window is None:
        o_t = _sweep(qa_ref[...], lambda ks, n: ka_ref[pl.ds(ks, n), :], lambda j: vt_ref[j], s_ref,
                     q0=qi * tq, tq=tq, tk=tk, dv=d, scale=1.0)
    else:
        o_t = _band(qa_ref[...], ka_ref, vt_ref, q0=qi * tq, tq=tq, window=window)
    for r in range(rep):
        o_ref[0, :, r * d:(r + 1) * d] = o_t[:, r * tq:(r + 1) * tq].T.astype(o_ref.dtype)


def _gflash(yq, yk, yv, mbias, *, name, q_blk, k_blk, v_blk, tq=128, tk=512, window=None):
    b, s, _ = yq.shape
    rep, groups, d = NSA_REP, NSA_GROUPS, HEAD_DIM
    assert groups * d == LANES and s % tk == 0 and tk % tq == 0
    select = mbias is not None
    dk = 2 * LANES if select else LANES
    assert window is None or (not select and window % tq == 0 and s >= window + tq)
    tv = tk if window is None else tq
    scratch = [pltpu.VMEM((s // tv, d, tv), BF16), pltpu.VMEM((s, dk), BF16), pltpu.VMEM((rep * tq, dk), BF16)]
    if window is None:
        scratch.insert(0, pltpu.VMEM((s // tk, tk, rep * tq), F32))
    args = [yq, yk, yv]
    in_specs = [pl.BlockSpec((1, tq, rep * d), lambda i, j: (i // groups, j, q_blk + i % groups)),
                pl.BlockSpec((1, s, LANES), lambda i, j: (i // groups, 0, k_blk)),
                pl.BlockSpec((1, s, LANES), lambda i, j: (i // groups, 0, v_blk))]
    if select:
        args.append(mbias)
        in_specs.append(pl.BlockSpec((1, tq, LANES), lambda i, j: (i, j, 0)))
    body = functools.partial(_gflash_body, tq=tq, tk=tk, rep=rep, groups=groups, window=window, select=select)
    return pl.pallas_call(
        body, name=name, grid=(b * groups, s // tq), in_specs=in_specs,
        out_specs=pl.BlockSpec((1, tq, rep * d), lambda i, j: (i // groups, j, i % groups)),
        out_shape=jax.ShapeDtypeStruct((b, s, groups * rep * d), F32),
        scratch_shapes=scratch,
        compiler_params=_cparams(("parallel", "arbitrary")),
    )(*args)


def _topk_mask(score, row, k):
    sel = None
    for _ in range(k):
        mx = jnp.max(score, axis=0, keepdims=True)
        idx = jnp.min(jnp.where(score == mx, row, LANES), axis=0, keepdims=True)
        hit = row == idx
        pick = jnp.logical_and(hit, mx > -jnp.inf)
        sel = pick if sel is None else jnp.logical_or(sel, pick)
        score = jnp.where(hit, -jnp.inf, score)
    return sel


def _moba_gate_body(q_ref, k_ref, qa_ref, ka_ref, kmean_ref, *, tq, seq):
    qi = pl.program_id(1)
    nbl = MOBA_BIAS_LANES

    @pl.when(qi == 0)
    def _():
        rowblk = lax.rem(lax.broadcasted_iota(jnp.int32, (LANES, seq), 0), nbl)
        colblk = lax.broadcasted_iota(jnp.int32, (LANES, seq), 1) // MOBA_BLOCK
        ind = jnp.where(rowblk == colblk, 1.0, 0.0).astype(BF16)
        ksum = jnp.dot(ind, k_ref[0], preferred_element_type=F32)
        rhead = lax.broadcasted_iota(jnp.int32, (LANES, LANES), 0) // nbl
        lhead = lax.broadcasted_iota(jnp.int32, (LANES, LANES), 1) // HEAD_DIM
        kmean_ref[...] = jnp.where(rhead == lhead, ksum * (1.0 / MOBA_BLOCK), 0.0)

    q = q_ref[0]
    q0 = pl.multiple_of(qi * tq, tq)
    km = kmean_ref[...]
    km_hi = km.astype(BF16)
    km_lo = (km - km_hi.astype(F32)).astype(BF16)
    gate = (lax.dot_general(km_hi, q, _NT, preferred_element_type=F32)
            + lax.dot_general(km_lo, q, _NT, preferred_element_type=F32))
    blk_t = lax.broadcasted_iota(jnp.int32, (nbl, tq), 0)
    own_t = (q0 + lax.broadcasted_iota(jnp.int32, (nbl, tq), 1)) // MOBA_BLOCK
    parts = []
    for v in range(2):
        g_v = jnp.where(blk_t < own_t, gate[v * nbl:(v + 1) * nbl], -jnp.inf)
        sel = jnp.logical_or(_topk_mask(g_v, blk_t, MOBA_TOPK), blk_t == own_t)
        parts.append(jnp.where(sel, 0.0, NEG))
    parts.append(jnp.zeros((LANES - 2 * nbl, tq), F32))
    bias = jnp.concatenate(parts, axis=0).T
    qa_ref[0, :, 0:LANES] = q * (HEAD_DIM ** -0.5)
    qa_ref[0, :, LANES:2 * LANES] = bias.astype(BF16)
    lane = lax.broadcasted_iota(jnp.int32, (tq, LANES), 1)
    own = (q0 + lax.broadcasted_iota(jnp.int32, (tq, LANES), 0)) // MOBA_BLOCK
    onehot = jnp.where(jnp.logical_and(lane < 2 * nbl, lax.rem(lane, nbl) == own), 1.0, 0.0)
    ka_ref[0, :, 0:LANES] = k_ref[0, pl.ds(q0, tq), :]
    ka_ref[0, :, LANES:2 * LANES] = onehot.astype(BF16)


def _moba_gate(y3, *, q_blk, k_blk, pairs, tq=512):
    b, s, _ = y3.shape
    assert s % MOBA_BLOCK == 0 and s // MOBA_BLOCK <= MOBA_BIAS_LANES and s % tq == 0
    out = jax.ShapeDtypeStruct((b * pairs, s, 2 * LANES), BF16)
    return pl.pallas_call(
        functools.partial(_moba_gate_body, tq=tq, seq=s), name="moba_gate",
        grid=(b * pairs, s // tq),
        in_specs=[pl.BlockSpec((1, tq, LANES), lambda i, j: (i // pairs, j, q_blk + i % pairs)),
                  pl.BlockSpec((1, s, LANES), lambda i, j: (i // pairs, 0, k_blk + i % pairs))],
        out_specs=[pl.BlockSpec((1, tq, 2 * LANES), lambda i, j: (i, j, 0)),
                   pl.BlockSpec((1, tq, 2 * LANES), lambda i, j: (i, j, 0))],
        out_shape=[out, out],
        scratch_shapes=[pltpu.VMEM((LANES, LANES), F32)],
        compiler_params=_cparams(("parallel", "arbitrary")),
    )(y3, y3)


def _nsa_compress_body(x_ref, pe_ref, w1_ref, w2_ref, c_ref, s1_ref, s2_ref, o_ref, *, rope):
    half = w1_ref.shape[0] // 2
    x = x_ref[0].astype(F32)
    lo = (x + pe_ref[:, 0:half]).astype(BF16)
    hi = (x + pe_ref[:, half:2 * half]).astype(BF16)
    a = jnp.dot(lo, w1_ref[0:half, :], preferred_element_type=F32)
    b = jnp.dot(hi, w1_ref[half:2 * half, :], preferred_element_type=F32)
    n = a.shape[0]
    h1 = a + pltpu.roll(b, n - 1, 0)
    y = jnp.dot(jax.nn.gelu(h1).astype(BF16), w2_ref[...], preferred_element_type=F32)
    if rope:
        y = _apply_rope(y, c_ref[0], s1_ref[0], s2_ref[0], ROPE_DIM // 2)
    o_ref[0] = y[:, 0:HEAD_DIM].astype(o_ref.dtype)


def _nsa_compress(xr, pe, w1, w2, tables, *, rope, groups):
    g, n, w = xr.shape
    hid = w1.shape[1]
    w2p = jnp.pad(w2, ((0, 0), (0, LANES - w2.shape[1])))
    tspec = pl.BlockSpec((1, n, LANES), lambda i: (i // groups, 0, 0))
    return pl.pallas_call(
        functools.partial(_nsa_compress_body, rope=rope), name="nsa_compress",
        grid=(g,),
        in_specs=[pl.BlockSpec((1, n, w), lambda i: (i, 0, 0)),
                  pl.BlockSpec((1, 2 * w), lambda i: (0, 0)),
                  pl.BlockSpec((2 * w, hid), lambda i: (0, 0)),
                  pl.BlockSpec((hid, LANES), lambda i: (0, 0)),
                  tspec, tspec, tspec],
        out_specs=pl.BlockSpec((1, n, HEAD_DIM), lambda i: (i, 0, 0)),
        out_shape=jax.ShapeDtypeStruct((g, n, HEAD_DIM), BF16),
        compiler_params=_cparams(("parallel",)),
    )(xr, pe.reshape(1, 2 * w).astype(F32), w1, w2p, *tables)


def _place(n_src, n_dst, shift, value=1.0):
    src = lax.broadcasted_iota(jnp.int32, (n_src, n_dst), 0)
    dst = lax.broadcasted_iota(jnp.int32, (n_src, n_dst), 1)
    return jnp.where(jnp.logical_and(src == dst + shift, dst < HEAD_DIM), value, 0.0).astype(BF16)


def _nsa_cmp_body(q_ref, kc_ref, vc_ref, oc_ref, mb_ref, kcp_ref, *, tq, rep):
    qi = pl.program_id(1)
    q0 = qi * tq
    nc = kc_ref.shape[1]
    width = q_ref.shape[-1]

    @pl.when(qi == 0)
    def _():
        for r in range(rep):
            src = lax.broadcasted_iota(jnp.int32, (HEAD_DIM, width), 0)
            dst = lax.broadcasted_iota(jnp.int32, (HEAD_DIM, width), 1)
            spread = jnp.where(dst == src + r * HEAD_DIM, HEAD_DIM ** -0.5, 0.0).astype(BF16)
            kcp_ref[r] = jnp.dot(kc_ref[0], spread, preferred_element_type=F32).astype(BF16)

    q = q_ref[0]

    def branches(nr):
        def run():
            vct = vc_ref[0, :, 0:nr]
            tpos = q0 + lax.broadcasted_iota(jnp.int32, (nr, tq), 1)
            cend = lax.broadcasted_iota(jnp.int32, (nr, tq), 0) * NSA_CMP_STRIDE + (NSA_CMP_LEN - 1)
            ok = cend <= tpos
            psum = jnp.zeros((nr, tq), F32)
            for r in range(rep):
                s = lax.dot_general(kcp_ref[r, 0:nr, :], q, _NT, preferred_element_type=F32)
                s = jnp.where(ok, s, NEG)
                m = jnp.max(s, axis=0, keepdims=True)
                e = jnp.where(ok, jnp.exp(s - m), 0.0)
                p = e * (1.0 / jnp.maximum(jnp.sum(e, axis=0, keepdims=True), 1e-30))
                o_t = jnp.dot(vct, p.astype(BF16), preferred_element_type=F32)
                oc_ref[0, :, r * HEAD_DIM:(r + 1) * HEAD_DIM] = o_t.T.astype(oc_ref.dtype)
                psum = psum + p
            sstart = lax.broadcasted_iota(jnp.int32, (LANES, nr), 0) * NSA_SEL_BLOCK
            cstart = lax.broadcasted_iota(jnp.int32, (LANES, nr), 1) * NSA_CMP_STRIDE
            ov = jnp.where(jnp.logical_and(cstart < sstart + NSA_SEL_BLOCK, cstart + NSA_CMP_LEN > sstart),
                           1.0, 0.0).astype(BF16)
            p_hi = psum.astype(BF16)
            r1 = psum - p_hi.astype(F32)
            p_mid = r1.astype(BF16)
            p_lo = (r1 - p_mid.astype(F32)).astype(BF16)
            return (jnp.dot(ov, p_hi, preferred_element_type=F32) + jnp.dot(ov, p_mid, preferred_element_type=F32)
                    + jnp.dot(ov, p_lo, preferred_element_type=F32))
        return run

    sizes = list(range(LANES, nc + 1, LANES))
    needed = (q0 + tq) // NSA_CMP_STRIDE
    imp = lax.switch(jnp.minimum((needed + LANES - 1) // LANES, len(sizes)) - 1, [branches(nr) for nr in sizes])
    blk = lax.broadcasted_iota(jnp.int32, (LANES, tq), 0)
    qb = (q0 + lax.broadcasted_iota(jnp.int32, (LANES, tq), 1)) // NSA_SEL_BLOCK
    forced = jnp.logical_or(blk == 0, jnp.logical_or(blk == qb, blk == qb - 1))
    imp = jnp.where(forced, imp + NSA_FORCE_BONUS, imp)
    imp = jnp.where(blk <= qb, imp, -jnp.inf)
    sel = _topk_mask(imp, blk, NSA_SEL_TOPK)
    mb_ref[0] = jnp.where(sel, 0.0, NEG).T.astype(mb_ref.dtype)


def _nsa_cmp(y3, kc, vc, *, tq=512):
    b, s, _ = y3.shape
    g, nc, d = kc.shape
    rep, n = NSA_REP, NSA_GROUPS
    assert s // NSA_SEL_BLOCK <= LANES and s % tq == 0
    return pl.pallas_call(
        functools.partial(_nsa_cmp_body, tq=tq, rep=rep), name="nsa_cmp",
        grid=(g, s // tq),
        in_specs=[pl.BlockSpec((1, tq, rep * d), lambda i, j: (i // n, j, i % n)),
                  pl.BlockSpec((1, nc, d), lambda i, j: (i, 0, 0)),
                  pl.BlockSpec((1, d, nc), lambda i, j: (i, 0, 0))],
        out_specs=[pl.BlockSpec((1, tq, rep * d), lambda i, j: (i // n, j, i % n)),
                   pl.BlockSpec((1, tq, LANES), lambda i, j: (i, j, 0))],
        out_shape=[jax.ShapeDtypeStruct((b, s, n * rep * d), F32),
                   jax.ShapeDtypeStruct((g, s, LANES), BF16)],
        scratch_shapes=[pltpu.VMEM((rep, nc, rep * d), BF16)],
        compiler_params=_cparams(("parallel", "arbitrary")),
    )(y3, kc, vc)


def _nsa_combine_body(oc_ref, os_ref, ow_ref, g_ref, b_ref, e_ref, o_ref):
    gs = jax.nn.sigmoid(g_ref[...] + b_ref[...])
    g_hi = gs.astype(BF16)
    g_lo = (gs - g_hi.astype(F32)).astype(BF16)
    out = None
    for i, ref in enumerate((oc_ref, os_ref, ow_ref)):
        w = (jnp.dot(g_hi, e_ref[i], preferred_element_type=F32)
             + jnp.dot(g_lo, e_ref[i], preferred_element_type=F32))
        term = w * ref[...]
        out = term if out is None else out + term
    o_ref[...] = out.astype(o_ref.dtype)


def _nsa_combine(oc, osel, ow, yg, g_blk, gate_b, *, tm=512):
    t, n = oc.shape
    nh = NSA_HEADS * 3
    bp = jnp.pad(gate_b.reshape(1, nh).astype(F32), ((0, 0), (0, LANES - nh)))
    row = jnp.arange(LANES)[:, None]
    col = jnp.arange(n)[None, :] // HEAD_DIM
    expand = jnp.stack([(row == col * 3 + i) for i in range(3)]).astype(BF16)
    tok = pl.BlockSpec((tm, n), lambda i: (i, 0))
    return pl.pallas_call(
        _nsa_combine_body, name="nsa_combine", grid=(t // tm,),
        in_specs=[tok, tok, tok, pl.BlockSpec((tm, LANES), lambda i: (i, g_blk)),
                  pl.BlockSpec((1, LANES), lambda i: (0, 0)),
                  pl.BlockSpec((3, LANES, n), lambda i: (0, 0, 0))],
        out_specs=tok,
        out_shape=jax.ShapeDtypeStruct((t, n), BF16),
        compiler_params=_cparams(("parallel",)),
    )(oc, osel, ow, yg, bp, expand)


def _to_heads(x, b, s, n):
    return x.reshape(b, s, n, -1).transpose(0, 2, 1, 3).reshape(b * n, s, -1)


def _even_mixer(h, gain, layer_idx, rope16, w_in, lq1, lk1, lq2, lk2, subln, b, s):
    na = MOBA_HEADS * HEAD_DIM
    nb = DIFF_HEADS * 2 * HEAD_DIM
    w_rope = jnp.concatenate([w_in[:, 0:2 * na], w_in[:, 3 * na:3 * na + 2 * nb]], axis=1)
    w_rest = jnp.concatenate([w_in[:, 2 * na:3 * na], w_in[:, 3 * na + 2 * nb:]], axis=1)
    w = jnp.concatenate([w_rope, w_rest], axis=1).astype(BF16)
    y = _proj(h, w, name="proj_even_in", gain=gain, rope=rope16, rope_cols=w_rope.shape[1], shift=ROPE_DIM // 2, out_dtype=BF16)
    y3 = y.reshape(b, s, -1)
    blk = lambda off: off // LANES
    pairs = na // LANES
    qa, ka = _moba_gate(y3, q_blk=blk(0), k_blk=blk(na), pairs=pairs)
    o_a = _pflash(qa, ka, y3, name="flash_moba", mode="moba", dk=2 * LANES, batch=b, pairs=pairs,
                  q_map=lambda i, j: (i, j, 0), k_map=lambda i, j: (i, 0, 0),
                  v_map=lambda i, j: (i // pairs, 0, blk(2 * na + 2 * nb) + i % pairs))
    lambda_init = 0.8 - 0.6 * math.exp(-0.3 * layer_idx)
    nh = DIFF_HEADS
    o_b = _pflash(y3, y3, y3, name="flash_diff", mode="diff", dk=LANES, batch=b, pairs=nh, scale=HEAD_DIM ** -0.5,
                  q_map=lambda i, j: (i // nh, j, blk(2 * na) + i % nh),
                  k_map=lambda i, j: (i // nh, 0, blk(2 * na + nb) + i % nh),
                  v_map=lambda i, j: (i // nh, 0, blk(3 * na + 2 * nb) + i % nh),
                  diff_params=(lq1, lk1, lq2, lk2, subln), lambda_init=lambda_init)
    t = b * s
    return o_a.reshape(t, -1), o_b.reshape(t, -1)


def _odd_mixer(h, gain, positions, rope16, rope32s, w_in, gate_b, pe_k, pe_v, k_w1, k_w2, v_w1, v_w2,
               q_norm, w_uq, kv_norm, w_ukv, b, s):
    G, R, d = NSA_GROUPS, NSA_REP, HEAD_DIM
    sizes = [NSA_HEADS * d] + [G * d] * 6 + [NSA_HEADS * 3, MLA_Q_RANK, MLA_KV_RANK, MLA_ROPE]
    offs = [0]
    for z in sizes:
        offs.append(offs[-1] + z)
    col = lambda i: w_in[:, offs[i]:offs[i + 1]]
    w_r = jnp.concatenate([col(0), col(3), col(5)], axis=1).astype(BF16)
    yr = _proj(h, w_r, name="proj_odd_rope", gain=gain, rope=rope16, rope_cols=w_r.shape[1], shift=ROPE_DIM // 2, out_dtype=BF16)
    w_p = jnp.concatenate([col(1), col(2), col(4), col(6), col(8), col(9), col(7)], axis=1)
    w_p = jnp.pad(w_p, ((0, 0), (0, (-w_p.shape[1]) % LANES)))
    w_pe = jnp.pad(col(10), ((0, 0), (MLA_NOPE, LANES - MLA_NOPE - MLA_ROPE)))
    yp = _proj(h, jnp.concatenate([w_p, w_pe], axis=1).astype(BF16), name="proj_odd_plain", gain=gain, out_dtype=F32)
    k_cmp, v_cmp = yp[:, 0:G * d], yp[:, G * d:2 * G * d]
    cq_blk = 4 * G * d // MLA_Q_RANK
    ckv_blk = (4 * G * d + MLA_Q_RANK) // MLA_KV_RANK
    gate_blk = (4 * G * d + MLA_Q_RANK + MLA_KV_RANK) // LANES
    kpe_blk = w_p.shape[1] // LANES

    yr3, yp3 = yr.reshape(b, s, -1), yp.reshape(b, s, -1)
    nc = s // NSA_CMP_STRIDE
    cpos = jnp.concatenate([positions[:, NSA_CMP_LEN - 1::NSA_CMP_STRIDE], positions[:, -1:]], axis=1)
    ctab = [t.reshape(b, nc, LANES) for t in _rope_tables(cpos, ROPE_DIM, HEAD_DIM)]
    xk = _to_heads(k_cmp, b, s, G).reshape(b * G, nc, NSA_CMP_STRIDE * d)
    xv = _to_heads(v_cmp, b, s, G).reshape(b * G, nc, NSA_CMP_STRIDE * d)
    kc = _nsa_compress(xk, pe_k, k_w1.astype(BF16), k_w2.astype(BF16), ctab, rope=True, groups=G)
    vc = _nsa_compress(xv, pe_v, v_w1.astype(BF16), v_w2.astype(BF16), ctab, rope=False, groups=G)
    o_c, mbias = _nsa_cmp(yr3, kc, vc.transpose(0, 2, 1))
    k_blk = NSA_HEADS * d // LANES
    vsw = yp[:, 2 * G * d:4 * G * d].astype(BF16).reshape(b, s, -1)
    o_s = _gflash(yr3, yr3, vsw, mbias, name="flash_sel", q_blk=0, k_blk=k_blk, v_blk=0, tq=256)
    o_w = _gflash(yr3, yr3, vsw, None, name="flash_win", q_blk=0, k_blk=k_blk + 1, v_blk=1, tq=256, window=NSA_WINDOW)
    t = b * s
    o_nsa = _nsa_combine(o_c.reshape(t, -1), o_s.reshape(t, -1), o_w.reshape(t, -1), yp, gate_blk, gate_b)

    hq = MLA_NOPE + MLA_ROPE
    nh = MLA_HEADS
    wq = jnp.pad(w_uq.reshape(MLA_Q_RANK, nh, hq), ((0, 0), (0, 0), (0, LANES - hq))).reshape(MLA_Q_RANK, nh * LANES)
    q = _proj(yp, wq.astype(BF16), name="proj_mla_q", x_blk=(MLA_Q_RANK, cq_blk), gain=q_norm, rope=rope32s,
              rope_cols=nh * LANES, shift=MLA_ROPE // 2, out_dtype=BF16)
    wkv = w_ukv.reshape(MLA_KV_RANK, nh, MLA_NOPE + MLA_V)
    wk = jnp.pad(wkv[:, :, :MLA_NOPE], ((0, 0), (0, 0), (0, LANES - MLA_NOPE))).reshape(MLA_KV_RANK, nh * LANES)
    wkv = jnp.concatenate([wk, wkv[:, :, MLA_NOPE:].reshape(MLA_KV_RANK, nh * MLA_V)], axis=1).astype(BF16)
    kv = _proj(yp, wkv, name="proj_mla_kv", x_blk=(MLA_KV_RANK, ckv_blk), gain=kv_norm, rope=rope32s,
               shift=MLA_ROPE // 2, add=yp, add_blk=kpe_blk, add_cols=nh * LANES, rope_add=True,
               out_dtype=BF16)
    q3, kv3 = q.reshape(b, s, -1), kv.reshape(b, s, -1)
    pairs = nh // 2
    o_d = _pflash(q3, kv3, kv3, name="flash_mla", mode="slots", dk=2 * LANES, batch=b, pairs=pairs, scale=hq ** -0.5,
                  q_map=lambda i, j: (i // pairs, j, i % pairs), k_map=lambda i, j: (i // pairs, 0, i % pairs),
                  v_map=lambda i, j: (i // pairs, 0, nh + i % pairs))

    return o_nsa, o_d.reshape(t, -1)


def kernel(x, positions, attn_norm, ffn_norm, final_norm, ffn_w_gate, ffn_w_up, ffn_w_down, ev_w_in, ev_w_out, diff_lambda_q1, diff_lambda_k1, diff_lambda_q2, diff_lambda_k2, diff_subln, od_w_in, od_w_out, nsa_gate_b, nsa_pe_k, nsa_pe_v, nsa_k_w1, nsa_k_w2, nsa_v_w1, nsa_v_w2, mla_q_norm, mla_w_uq, mla_kv_norm, mla_w_ukv):
    b, s, d = x.shape
    depth = attn_norm.shape[0]
    rope16 = _rope_tables(positions, ROPE_DIM, HEAD_DIM)
    rope32s = _rope_tables(positions, MLA_ROPE, LANES, offset=MLA_NOPE)
    h = x.reshape(b * s, d)
    for l in range(depth):
        i = l // 2
        if l % 2 == 0:
            oa, ob = _even_mixer(h, attn_norm[l], l, rope16, ev_w_in[i], diff_lambda_q1[i],
                                 diff_lambda_k1[i], diff_lambda_q2[i], diff_lambda_k2[i], diff_subln[i], b, s)
            w_out = ev_w_out[i]
        else:
            oa, ob = _odd_mixer(h, attn_norm[l], positions, rope16, rope32s, od_w_in[i], nsa_gate_b[i],
                                nsa_pe_k[i], nsa_pe_v[i], nsa_k_w1[i], nsa_k_w2[i], nsa_v_w1[i], nsa_v_w2[i],
                                mla_q_norm[i], mla_w_uq[i], mla_kv_norm[i], mla_w_ukv[i], b, s)
            w_out = od_w_out[i]
        h = _ffn(h, oa, ob, w_out.astype(BF16), ffn_norm[l], ffn_w_gate[l].astype(BF16), ffn_w_up[l].astype(BF16),
                 ffn_w_down[l].astype(BF16), final_norm, final_norm=(l == depth - 1))
    return h.reshape(b, s, d)
```

```python
import functools
import math

import jax
import jax.numpy as jnp
from jax import lax
from jax.experimental import pallas as pl
from jax.experimental.pallas import tpu as pltpu

F32 = jnp.float32
BF16 = jnp.bfloat16

D_MODEL = 1024
HEAD_DIM = 64
ROPE_THETA = 500000.0
ROPE_DIM = HEAD_DIM // 4
NORM_EPS = 1e-5
D_FF = 2816

MOBA_HEADS = 8
MOBA_BLOCK = 256
MOBA_TOPK = 3
DIFF_HEADS = 4
DIFF_V = 2 * HEAD_DIM
NSA_HEADS = 8
NSA_GROUPS = 2
NSA_REP = NSA_HEADS // NSA_GROUPS
NSA_CMP_LEN = 32
NSA_CMP_STRIDE = 16
NSA_CMP_HIDDEN = 256
NSA_SEL_BLOCK = 64
NSA_SEL_TOPK = 16
NSA_WINDOW = 512
NSA_FORCE_BONUS = 1e3
MLA_HEADS = 8
MLA_Q_RANK = 256
MLA_KV_RANK = 128
MLA_NOPE = 64
MLA_ROPE = 32
MLA_V = 64

LANES = 128
MOBA_BIAS_LANES = 32
NEG = -1e30
M_INIT = -1e37
LOG2E = 1.4426950408889634
UNROLL = 8
VMEM_LIMIT = 56 * 1024 * 1024

_NT = (((1,), (1,)), ((), ()))


def _cparams(sem):
    return pltpu.CompilerParams(dimension_semantics=sem, vmem_limit_bytes=VMEM_LIMIT)


def _rope_tables(positions, dim, period, offset=0):
    r = dim // 2
    inv = 1.0 / (ROPE_THETA ** (jnp.arange(0, dim, 2, dtype=F32) / dim))
    ang = positions.astype(F32)[..., None] * inv
    cos, sin = jnp.cos(ang), jnp.sin(ang)
    const = lambda n, val: jnp.full(ang.shape[:-1] + (n,), val, F32)
    rest = period - offset - 2 * r
    zr = jnp.zeros_like(sin)
    c = jnp.concatenate([const(offset, 1.0), cos, cos, const(rest, 1.0)], -1)
    s1 = jnp.concatenate([const(offset, 0.0), -sin, zr, const(rest, 0.0)], -1)
    s2 = jnp.concatenate([const(offset, 0.0), zr, sin, const(rest, 0.0)], -1)
    reps = LANES // period
    tile = lambda t: jnp.tile(t, (1,) * (t.ndim - 1) + (reps,)).reshape(-1, LANES)
    return tile(c), tile(s1), tile(s2)


def _apply_rope(y, c, s1, s2, shift):
    return y * c + pltpu.roll(y, LANES - shift, 1) * s1 + pltpu.roll(y, shift, 1) * s2


def _proj_body(*refs, has_norm, has_x2, has_rope, has_res, add_cols, rope_add, rope_cols, shift, chunk):
    it = iter(refs)
    x_ref = next(it)
    x2_ref = next(it) if has_x2 else None
    g_ref = next(it) if has_norm else None
    w_ref = next(it)
    if has_rope:
        c_ref, s1_ref, s2_ref = next(it), next(it), next(it)
    res_ref = next(it) if has_res else None
    add_ref = next(it) if add_cols else None
    o_ref = next(it)
    n = w_ref.shape[1]
    if has_norm:
        xf = x_ref[...].astype(F32)
        y = xf * lax.rsqrt(jnp.mean(xf * xf, axis=-1, keepdims=True) + NORM_EPS)
        xb = (y * g_ref[...]).astype(BF16)
    else:
        xb = x_ref[...].astype(BF16)
    k1 = xb.shape[1]
    if add_cols:
        add = add_ref[...].astype(F32)
        if rope_add:
            add = _apply_rope(add, c_ref[...], s1_ref[...], s2_ref[...], shift)
    for c0 in range(0, n, chunk):
        cw = min(chunk, n - c0)
        y = jnp.dot(xb, w_ref[0:k1, c0:c0 + cw], preferred_element_type=F32)
        if has_x2:
            y = y + jnp.dot(x2_ref[...].astype(BF16), w_ref[k1:, c0:c0 + cw], preferred_element_type=F32)
        if has_res:
            y = y + res_ref[:, c0:c0 + cw]
        if (has_rope and c0 < rope_cols) or c0 < add_cols:
            for k0 in range(0, cw, LANES):
                ys = y[:, k0:k0 + LANES]
                if has_rope and c0 + k0 < rope_cols:
                    ys = _apply_rope(ys, c_ref[...], s1_ref[...], s2_ref[...], shift)
                if c0 + k0 < add_cols:
                    ys = ys + add
                o_ref[:, c0 + k0:c0 + k0 + LANES] = ys.astype(o_ref.dtype)
        else:
            o_ref[:, c0:c0 + cw] = y.astype(o_ref.dtype)


def _proj(x, w, *, name, x_blk=None, x2=None, gain=None, rope=None, rope_cols=0, shift=0, res=None, add=None,
          add_blk=0, add_cols=0, rope_add=False, out_dtype=F32, tm=512, chunk=512):
    t = x.shape[0]
    k, xj = (x.shape[1], 0) if x_blk is None else x_blk
    n = w.shape[1]
    assert t % tm == 0 and n % LANES == 0 and rope_cols % LANES == 0 and add_cols % LANES == 0
    has_norm, has_rope, has_res, has_x2 = gain is not None, rope is not None, res is not None, x2 is not None
    assert not (has_norm and has_x2)
    args, specs = [x], [pl.BlockSpec((tm, k), lambda i: (i, xj))]
    if has_x2:
        args.append(x2)
        specs.append(pl.BlockSpec((tm, x2.shape[1]), lambda i: (i, 0)))
    if has_norm:
        args.append(gain.reshape(1, k).astype(F32))
        specs.append(pl.BlockSpec((1, k), lambda i: (0, 0)))
    args.append(w)
    specs.append(pl.BlockSpec((w.shape[0], n), lambda i: (0, 0)))
    if has_rope:
        for tb in rope:
            args.append(tb)
            specs.append(pl.BlockSpec((tm, LANES), lambda i: (i, 0)))
    if has_res:
        args.append(res)
        specs.append(pl.BlockSpec((tm, n), lambda i: (i, 0)))
    if add_cols:
        args.append(add)
        specs.append(pl.BlockSpec((tm, LANES), lambda i: (i, add_blk)))
    body = functools.partial(_proj_body, has_norm=has_norm, has_x2=has_x2, has_rope=has_rope, has_res=has_res,
                             add_cols=add_cols, rope_add=rope_add,
                             rope_cols=rope_cols, shift=shift, chunk=chunk)
    return pl.pallas_call(
        body, name=name, grid=(t // tm,), in_specs=specs,
        out_specs=pl.BlockSpec((tm, n), lambda i: (i, 0)),
        out_shape=jax.ShapeDtypeStruct((t, n), out_dtype),
        compiler_params=_cparams(("parallel",)),
    )(*args)


def _ffn_body(x_ref, oa_ref, ob_ref, wo_ref, g_ref, wg_ref, wu_ref, wd_ref, fg_ref, o_ref, h_ref, xn_ref, acc_ref, *,
              final_norm):
    j = pl.program_id(1)

    @pl.when(j == 0)
    def _():
        ka = oa_ref.shape[1]
        h = (x_ref[...] + jnp.dot(oa_ref[...], wo_ref[0:ka, :], preferred_element_type=F32)
             + jnp.dot(ob_ref[...], wo_ref[ka:, :], preferred_element_type=F32))
        h_ref[...] = h
        y = h * lax.rsqrt(jnp.mean(h * h, axis=-1, keepdims=True) + NORM_EPS)
        xn_ref[...] = (y * g_ref[...]).astype(BF16)
        acc_ref[...] = jnp.zeros_like(acc_ref)

    xn = xn_ref[...]
    g = jnp.dot(xn, wg_ref[...], preferred_element_type=F32)
    u = jnp.dot(xn, wu_ref[...], preferred_element_type=F32)
    a = (jax.nn.silu(g) * u).astype(BF16)
    acc_ref[...] += jnp.dot(a, wd_ref[...], preferred_element_type=F32)

    @pl.when(j == pl.num_programs(1) - 1)
    def _():
        h = h_ref[...] + acc_ref[...]
        if final_norm:
            y = h * lax.rsqrt(jnp.mean(h * h, axis=-1, keepdims=True) + NORM_EPS)
            h = y * fg_ref[...]
        o_ref[...] = h


def _ffn(x, oa, ob, w_out, gain, wg, wu, wd, final_gain, *, final_norm, tm=512, tf=1408):
    t, d = x.shape
    f = wg.shape[1]
    assert t % tm == 0 and f % tf == 0 and oa.shape[1] + ob.shape[1] == w_out.shape[0]
    return pl.pallas_call(
        functools.partial(_ffn_body, final_norm=final_norm), name="ffn",
        grid=(t // tm, f // tf),
        in_specs=[
            pl.BlockSpec((tm, d), lambda i, j: (i, 0)),
            pl.BlockSpec((tm, oa.shape[1]), lambda i, j: (i, 0)),
            pl.BlockSpec((tm, ob.shape[1]), lambda i, j: (i, 0)),
            pl.BlockSpec(w_out.shape, lambda i, j: (0, 0)),
            pl.BlockSpec((1, d), lambda i, j: (0, 0)),
            pl.BlockSpec((d, tf), lambda i, j: (0, j)),
            pl.BlockSpec((d, tf), lambda i, j: (0, j)),
            pl.BlockSpec((tf, d), lambda i, j: (j, 0)),
            pl.BlockSpec((1, d), lambda i, j: (0, 0)),
        ],
        out_specs=pl.BlockSpec((tm, d), lambda i, j: (i, 0)),
        out_shape=jax.ShapeDtypeStruct((t, d), F32),
        scratch_shapes=[pltpu.VMEM((tm, d), F32), pltpu.VMEM((tm, d), BF16), pltpu.VMEM((tm, d), F32)],
        compiler_params=_cparams(("parallel", "arbitrary")),
    )(x, oa, ob, w_out, gain.reshape(1, d).astype(F32), wg, wu, wd, final_gain.reshape(1, d).astype(F32))


def _sweep(q, k_get, vt_get, s_ref, *, q0, tq, tk, dv, scale):
    cols = q.shape[0]

    def fold8(x, op):
        return op(x.reshape(x.shape[0] // 8, 8, x.shape[1]), axis=0)

    def first_col(u, n, n_masked):
        d = u - (n - n_masked)
        return d * tk if (d > 0 and cols == tq) else 0

    def upd(full, c0, f):
        return f(full) if c0 == 0 else jnp.concatenate([full[:, :c0], f(full[:, c0:])], axis=1)

    def score_block(j0, n, mrun, n_masked):
        ks = pl.multiple_of(j0 * tk, tk)
        n_wide = sum(1 for u in range(n) if first_col(u, n, n_masked) == 0)
        s = lax.dot_general(k_get(ks, n_wide * tk), q, _NT, preferred_element_type=F32) * (scale * LOG2E)
        for u in range(n):
            c0 = first_col(u, n, n_masked)
            if c0 == 0:
                su = s[u * tk:(u + 1) * tk]
            else:
                ku = k_get(pl.multiple_of(ks + u * tk, tk), tk)
                su = lax.dot_general(ku, q[c0:], _NT, preferred_element_type=F32) * (scale * LOG2E)
            if u >= n - n_masked:
                kpos = ks + u * tk + lax.broadcasted_iota(jnp.int32, su.shape, 0)
                qpos = q0 + c0 + lax.rem(lax.broadcasted_iota(jnp.int32, su.shape, 1), tq)
                su = jnp.where(kpos <= qpos, su, NEG)
            s_ref[j0 + u, :, c0:cols] = su
            mrun = upd(mrun, c0, lambda t: jnp.maximum(t, fold8(su, jnp.max)))
        return mrun

    def pv_block(j0, n, carry, m, n_masked=0):
        l8, acc = carry
        for u in range(n):
            c0 = first_col(u, n, n_masked)
            p = jnp.exp2(s_ref[j0 + u, :, c0:cols] - m[:, c0:])
            l8 = upd(l8, c0, lambda t: t + fold8(p, jnp.sum))
            pv = jnp.dot(vt_get(j0 + u), p.astype(BF16), preferred_element_type=F32)
            acc = upd(acc, c0, lambda t: t + pv)
        return l8, acc

    m_init = jnp.full((8, cols), M_INIT, F32)
    acc_init = (jnp.zeros((8, cols), F32), jnp.zeros((dv, cols), F32))
    n_diag = max(tq // tk, 1)
    assert UNROLL % n_diag == 0
    n_full = q0 // tk
    groups = n_full // UNROLL
    rem = n_full - groups * UNROLL
    tail0 = groups * UNROLL
    tails = range(n_diag, UNROLL + n_diag, n_diag)

    mrun = lax.fori_loop(0, groups, lambda i, t: score_block(UNROLL * i, UNROLL, t, 0), m_init)
    mrun = lax.switch(rem // n_diag, [functools.partial(score_block, tail0, n, n_masked=n_diag) for n in tails],
                      mrun)
    m = jnp.max(mrun, axis=0, keepdims=True)
    carry = lax.fori_loop(0, groups, lambda i, t: pv_block(UNROLL * i, UNROLL, t, m), acc_init)
    l8, acc = lax.switch(rem // n_diag, [functools.partial(pv_block, tail0, n, m=m, n_masked=n_diag) for n in tails],
                         carry)
    l = jnp.sum(l8, axis=0, keepdims=True)
    return acc / jnp.maximum(l, 1e-30)


def _head_lanes(mode, v, lane):
    d = HEAD_DIM
    if mode == "diff":
        return jnp.logical_and(lane >= d * v, lane < d * (v + 1))
    if mode == "moba":
        lo = 2 * d + v * MOBA_BIAS_LANES
        return jnp.logical_or(jnp.logical_and(lane >= d * v, lane < d * (v + 1)),
                              jnp.logical_and(lane >= lo, lane < lo + MOBA_BIAS_LANES))
    return jnp.logical_and(lane >= LANES * v, lane < LANES * (v + 1))


def _pflash_body(*refs, mode, tq, tk, scale, lambda_init):
    if mode == "diff":
        q_ref, k_ref, v_ref, lq1_ref, lk1_ref, lq2_ref, lk2_ref, sg_ref, o_ref, s_ref, vt_ref = refs
    else:
        q_ref, k_ref, v_ref, o_ref, s_ref, vt_ref = refs
    qi = pl.program_id(1)
    n_chunks = v_ref.shape[1] // tk

    @pl.when(qi == 0)
    def _():
        for c in range(n_chunks):
            vt_ref[c] = v_ref[0, c * tk:(c + 1) * tk, :].astype(F32).T.astype(BF16)

    q = q_ref[0]
    lane = lax.broadcasted_iota(jnp.int32, q.shape, 1)
    d = HEAD_DIM
    outs = []
    for v in range(2):
        qv = jnp.where(_head_lanes(mode, v, lane), q, jnp.zeros_like(q))
        if mode == "diff":
            vt_get, dv = (lambda j: vt_ref[j]), LANES
        else:
            vt_get, dv = (lambda j, v=v: vt_ref[j, v * d:(v + 1) * d, :]), d
        outs.append(_sweep(qv, lambda ks, n: k_ref[0, pl.ds(ks, n), :], vt_get, s_ref,
                           q0=qi * tq, tq=tq, tk=tk, dv=dv, scale=scale))
    if mode == "diff":
        lam = (jnp.exp(jnp.sum(lq1_ref[...] * lk1_ref[...], axis=-1, keepdims=True))
               - jnp.exp(jnp.sum(lq2_ref[...] * lk2_ref[...], axis=-1, keepdims=True)) + lambda_init)
        d = outs[0] - lam * outs[1]
        y = d * lax.rsqrt(jnp.mean(d * d, axis=0, keepdims=True) + NORM_EPS)
        o_t = (y * sg_ref[...]) * (1.0 - lambda_init)
    else:
        o_t = jnp.concatenate(outs, axis=0)
    o_ref[0] = o_t.T.astype(o_ref.dtype)


def _pflash(q, k, v, *, name, mode, dk, q_map, k_map, v_map, batch, pairs, tq=1024, tk=512, scale=1.0,
            diff_params=None, lambda_init=0.0):
    s = v.shape[1]
    assert s % tk == 0 and tq % tk == 0 and s % tq == 0
    args = [q, k, v]
    in_specs = [pl.BlockSpec((1, tq, dk), q_map), pl.BlockSpec((1, s, dk), k_map),
                pl.BlockSpec((1, s, LANES), v_map)]
    if mode == "diff":
        lq1, lk1, lq2, lk2, subln = diff_params
        for a in (lq1, lk1, lq2, lk2):
            args.append(a.reshape(1, HEAD_DIM).astype(F32))
            in_specs.append(pl.BlockSpec((1, HEAD_DIM), lambda i, j: (0, 0)))
        args.append(subln.reshape(LANES, 1).astype(F32))
        in_specs.append(pl.BlockSpec((LANES, 1), lambda i, j: (0, 0)))
    body = functools.partial(_pflash_body, mode=mode, tq=tq, tk=tk, scale=scale, lambda_init=lambda_init)
    return pl.pallas_call(
        body, name=name, grid=(batch * pairs, s // tq), in_specs=in_specs,
        out_specs=pl.BlockSpec((1, tq, LANES), lambda i, j: (i // pairs, j, i % pairs)),
        out_shape=jax.ShapeDtypeStruct((batch, s, pairs * LANES), BF16),
        scratch_shapes=[pltpu.VMEM((s // tk, tk, tq), F32), pltpu.VMEM((s // tk, LANES, tk), BF16)],
        compiler_params=_cparams(("parallel", "arbitrary")),
    )(*args)


def _band(q, ka_ref, vt_ref, *, q0, tq, window):
    cols = q.shape[0]
    band = window + tq
    start = pl.multiple_of(jnp.maximum(q0 - window, 0), tq)
    s = lax.dot_general(ka_ref[pl.ds(start, band), :], q, _NT, preferred_element_type=F32) * LOG2E
    kpos = start + lax.broadcasted_iota(jnp.int32, (band, cols), 0)
    qpos = q0 + lax.rem(lax.broadcasted_iota(jnp.int32, (band, cols), 1), tq)
    ok = jnp.logical_and(kpos <= qpos, kpos > qpos - window)
    s = jnp.where(ok, s, NEG)
    p = jnp.exp2(s - jnp.max(s, axis=0, keepdims=True))
    l = jnp.sum(p, axis=0, keepdims=True)
    pb = p.astype(BF16)
    acc = None
    for c in range(band // tq):
        part = jnp.dot(vt_ref[start // tq + c], pb[c * tq:(c + 1) * tq], preferred_element_type=F32)
        acc = part if acc is None else acc + part
    return acc / jnp.maximum(l, 1e-30)


def _gflash_body(*refs, tq, tk, rep, groups, window, select):
    if select:
        q_ref, k_ref, v_ref, mb_ref, o_ref, s_ref, vt_ref, ka_ref, qa_ref = refs
    else:
        q_ref, k_ref, v_ref, o_ref, vt_ref, ka_ref, qa_ref = refs
    d = HEAD_DIM
    tv = vt_ref.shape[2]
    g = pl.program_id(0) % groups
    qi = pl.program_id(1)
    n_chunks = v_ref.shape[1] // tk

    @pl.when(qi == 0)
    def _():
        move = _place(LANES, LANES, g * d)
        for c in range(n_chunks):
            rows = slice(c * tk, (c + 1) * tk)
            vt = v_ref[0, rows, :].astype(F32).T
            vt = jnp.where(g == 0, vt[0:d], vt[d:2 * d]).astype(BF16)
            for u in range(tk // tv):
                vt_ref[c * (tk // tv) + u] = vt[:, u * tv:(u + 1) * tv]
            kg = jnp.dot(k_ref[0, rows, :], move, preferred_element_type=F32).astype(BF16)
            if select:
                blk = (c * tk + lax.broadcasted_iota(jnp.int32, (tk, LANES), 0)) // NSA_SEL_BLOCK
                lane = lax.broadcasted_iota(jnp.int32, (tk, LANES), 1)
                ka_ref[rows, 0:LANES] = jnp.where(blk == lane, 1.0, 0.0).astype(BF16)
                ka_ref[rows, LANES:2 * LANES] = kg
            else:
                ka_ref[rows, :] = kg

    q = q_ref[0]
    for r in range(rep):
        qr = jnp.dot(q, _place(rep * d, LANES, r * d, d ** -0.5), preferred_element_type=F32).astype(BF16)
        rows = slice(r * tq, (r + 1) * tq)
        if select:
            qa_ref[rows, 0:LANES] = mb_ref[0]
            qa_ref[rows, LANES:2 * LANES] = qr
        else:
            qa_ref[rows, :] = qr
    if window is None:
        o_t = _sweep(qa_ref[...], lambda ks, n: ka_ref[pl.ds(ks, n), :], lambda j: vt_ref[j], s_ref,
                     q0=qi * tq, tq=tq, tk=tk, dv=d, scale=1.0)
    else:
        o_t = _band(qa_ref[...], ka_ref, vt_ref, q0=qi * tq, tq=tq, window=window)
    for r in range(rep):
        o_ref[0, :, r * d:(r + 1) * d] = o_t[:, r * tq:(r + 1) * tq].T.astype(o_ref.dtype)


def _gflash(yq, yk, yv, mbias, *, name, q_blk, k_blk, v_blk, tq=128, tk=512, window=None):
    b, s, _ = yq.shape
    rep, groups, d = NSA_REP, NSA_GROUPS, HEAD_DIM
    assert groups * d == LANES and s % tk == 0 and tk % tq == 0
    select = mbias is not None
    dk = 2 * LANES if select else LANES
    assert window is None or (not select and window % tq == 0 and s >= window + tq)
    tv = tk if window is None else tq
    scratch = [pltpu.VMEM((s // tv, d, tv), BF16), pltpu.VMEM((s, dk), BF16), pltpu.VMEM((rep * tq, dk), BF16)]
    if window is None:
        scratch.insert(0, pltpu.VMEM((s // tk, tk, rep * tq), F32))
    args = [yq, yk, yv]
    in_specs = [pl.BlockSpec((1, tq, rep * d), lambda i, j: (i // groups, j, q_blk + i % groups)),
                pl.BlockSpec((1, s, LANES), lambda i, j: (i // groups, 0, k_blk)),
                pl.BlockSpec((1, s, LANES), lambda i, j: (i // groups, 0, v_blk))]
    if select:
        args.append(mbias)
        in_specs.append(pl.BlockSpec((1, tq, LANES), lambda i, j: (i, j, 0)))
    body = functools.partial(_gflash_body, tq=tq, tk=tk, rep=rep, groups=groups, window=window, select=select)
    return pl.pallas_call(
        body, name=name, grid=(b * groups, s // tq), in_specs=in_specs,
        out_specs=pl.BlockSpec((1, tq, rep * d), lambda i, j: (i // groups, j, i % groups)),
        out_shape=jax.ShapeDtypeStruct((b, s, groups * rep * d), F32),
        scratch_shapes=scratch,
        compiler_params=_cparams(("parallel", "arbitrary")),
    )(*args)


def _topk_mask(score, row, k):
    sel = None
    for _ in range(k):
        mx = jnp.max(score, axis=0, keepdims=True)
        idx = jnp.min(jnp.where(score == mx, row, LANES), axis=0, keepdims=True)
        hit = row == idx
        pick = jnp.logical_and(hit, mx > -jnp.inf)
        sel = pick if sel is None else jnp.logical_or(sel, pick)
        score = jnp.where(hit, -jnp.inf, score)
    return sel


def _moba_gate_body(q_ref, k_ref, qa_ref, ka_ref, kmean_ref, *, tq, seq):
    qi = pl.program_id(1)
    nbl = MOBA_BIAS_LANES

    @pl.when(qi == 0)
    def _():
        rowblk = lax.rem(lax.broadcasted_iota(jnp.int32, (LANES, seq), 0), nbl)
        colblk = lax.broadcasted_iota(jnp.int32, (LANES, seq), 1) // MOBA_BLOCK
        ind = jnp.where(rowblk == colblk, 1.0, 0.0).astype(BF16)
        ksum = jnp.dot(ind, k_ref[0], preferred_element_type=F32)
        rhead = lax.broadcasted_iota(jnp.int32, (LANES, LANES), 0) // nbl
        lhead = lax.broadcasted_iota(jnp.int32, (LANES, LANES), 1) // HEAD_DIM
        kmean_ref[...] = jnp.where(rhead == lhead, ksum * (1.0 / MOBA_BLOCK), 0.0)

    q = q_ref[0]
    q0 = pl.multiple_of(qi * tq, tq)
    km = kmean_ref[...]
    km_hi = km.astype(BF16)
    km_lo = (km - km_hi.astype(F32)).astype(BF16)
    gate = (lax.dot_general(km_hi, q, _NT, preferred_element_type=F32)
            + lax.dot_general(km_lo, q, _NT, preferred_element_type=F32))
    blk_t = lax.broadcasted_iota(jnp.int32, (nbl, tq), 0)
    own_t = (q0 + lax.broadcasted_iota(jnp.int32, (nbl, tq), 1)) // MOBA_BLOCK
    parts = []
    for v in range(2):
        g_v = jnp.where(blk_t < own_t, gate[v * nbl:(v + 1) * nbl], -jnp.inf)
        sel = jnp.logical_or(_topk_mask(g_v, blk_t, MOBA_TOPK), blk_t == own_t)
        parts.append(jnp.where(sel, 0.0, NEG))
    parts.append(jnp.zeros((LANES - 2 * nbl, tq), F32))
    bias = jnp.concatenate(parts, axis=0).T
    qa_ref[0, :, 0:LANES] = q * (HEAD_DIM ** -0.5)
    qa_ref[0, :, LANES:2 * LANES] = bias.astype(BF16)
    lane = lax.broadcasted_iota(jnp.int32, (tq, LANES), 1)
    own = (q0 + lax.broadcasted_iota(jnp.int32, (tq, LANES), 0)) // MOBA_BLOCK
    onehot = jnp.where(jnp.logical_and(lane < 2 * nbl, lax.rem(lane, nbl) == own), 1.0, 0.0)
    ka_ref[0, :, 0:LANES] = k_ref[0, pl.ds(q0, tq), :]
    ka_ref[0, :, LANES:2 * LANES] = onehot.astype(BF16)


def _moba_gate(y3, *, q_blk, k_blk, pairs, tq=512):
    b, s, _ = y3.shape
    assert s % MOBA_BLOCK == 0 and s // MOBA_BLOCK <= MOBA_BIAS_LANES and s % tq == 0
    out = jax.ShapeDtypeStruct((b * pairs, s, 2 * LANES), BF16)
    return pl.pallas_call(
        functools.partial(_moba_gate_body, tq=tq, seq=s), name="moba_gate",
        grid=(b * pairs, s // tq),
        in_specs=[pl.BlockSpec((1, tq, LANES), lambda i, j: (i // pairs, j, q_blk + i % pairs)),
                  pl.BlockSpec((1, s, LANES), lambda i, j: (i // pairs, 0, k_blk + i % pairs))],
        out_specs=[pl.BlockSpec((1, tq, 2 * LANES), lambda i, j: (i, j, 0)),
                   pl.BlockSpec((1, tq, 2 * LANES), lambda i, j: (i, j, 0))],
        out_shape=[out, out],
        scratch_shapes=[pltpu.VMEM((LANES, LANES), F32)],
        compiler_params=_cparams(("parallel", "arbitrary")),
    )(y3, y3)


def _nsa_compress_body(x_ref, pe_ref, w1_ref, w2_ref, c_ref, s1_ref, s2_ref, o_ref, *, rope):
    half = w1_ref.shape[0] // 2
    x = x_ref[0].astype(F32)
    lo = (x + pe_ref[:, 0:half]).astype(BF16)
    hi = (x + pe_ref[:, half:2 * half]).astype(BF16)
    a = jnp.dot(lo, w1_ref[0:half, :], preferred_element_type=F32)
    b = jnp.dot(hi, w1_ref[half:2 * half, :], preferred_element_type=F32)
    n = a.shape[0]
    h1 = a + pltpu.roll(b, n - 1, 0)
    y = jnp.dot(jax.nn.gelu(h1).astype(BF16), w2_ref[...], preferred_element_type=F32)
    if rope:
        y = _apply_rope(y, c_ref[0], s1_ref[0], s2_ref[0], ROPE_DIM // 2)
    o_ref[0] = y[:, 0:HEAD_DIM].astype(o_ref.dtype)


def _nsa_compress(xr, pe, w1, w2, tables, *, rope, groups):
    g, n, w = xr.shape
    hid = w1.shape[1]
    w2p = jnp.pad(w2, ((0, 0), (0, LANES - w2.shape[1])))
    tspec = pl.BlockSpec((1, n, LANES), lambda i: (i // groups, 0, 0))
    return pl.pallas_call(
        functools.partial(_nsa_compress_body, rope=rope), name="nsa_compress",
        grid=(g,),
        in_specs=[pl.BlockSpec((1, n, w), lambda i: (i, 0, 0)),
                  pl.BlockSpec((1, 2 * w), lambda i: (0, 0)),
                  pl.BlockSpec((2 * w, hid), lambda i: (0, 0)),
                  pl.BlockSpec((hid, LANES), lambda i: (0, 0)),
                  tspec, tspec, tspec],
        out_specs=pl.BlockSpec((1, n, HEAD_DIM), lambda i: (i, 0, 0)),
        out_shape=jax.ShapeDtypeStruct((g, n, HEAD_DIM), BF16),
        compiler_params=_cparams(("parallel",)),
    )(xr, pe.reshape(1, 2 * w).astype(F32), w1, w2p, *tables)


def _place(n_src, n_dst, shift, value=1.0):
    src = lax.broadcasted_iota(jnp.int32, (n_src, n_dst), 0)
    dst = lax.broadcasted_iota(jnp.int32, (n_src, n_dst), 1)
    return jnp.where(jnp.logical_and(src == dst + shift, dst < HEAD_DIM), value, 0.0).astype(BF16)


def _nsa_cmp_body(q_ref, kc_ref, vc_ref, oc_ref, mb_ref, kcp_ref, *, tq, rep):
    qi = pl.program_id(1)
    q0 = qi * tq
    nc = kc_ref.shape[1]
    width = q_ref.shape[-1]

    @pl.when(qi == 0)
    def _():
        for r in range(rep):
            src = lax.broadcasted_iota(jnp.int32, (HEAD_DIM, width), 0)
            dst = lax.broadcasted_iota(jnp.int32, (HEAD_DIM, width), 1)
            spread = jnp.where(dst == src + r * HEAD_DIM, HEAD_DIM ** -0.5, 0.0).astype(BF16)
            kcp_ref[r] = jnp.dot(kc_ref[0], spread, preferred_element_type=F32).astype(BF16)

    q = q_ref[0]

    def branches(nr):
        def run():
            vct = vc_ref[0, :, 0:nr]
            tpos = q0 + lax.broadcasted_iota(jnp.int32, (nr, tq), 1)
            cend = lax.broadcasted_iota(jnp.int32, (nr, tq), 0) * NSA_CMP_STRIDE + (NSA_CMP_LEN - 1)
            ok = cend <= tpos
            psum = jnp.zeros((nr, tq), F32)
            for r in range(rep):
                s = lax.dot_general(kcp_ref[r, 0:nr, :], q, _NT, preferred_element_type=F32)
                s = jnp.where(ok, s, NEG)
                m = jnp.max(s, axis=0, keepdims=True)
                e = jnp.where(ok, jnp.exp(s - m), 0.0)
                p = e * (1.0 / jnp.maximum(jnp.sum(e, axis=0, keepdims=True), 1e-30))
                o_t = jnp.dot(vct, p.astype(BF16), preferred_element_type=F32)
                oc_ref[0, :, r * HEAD_DIM:(r + 1) * HEAD_DIM] = o_t.T.astype(oc_ref.dtype)
                psum = psum + p
            sstart = lax.broadcasted_iota(jnp.int32, (LANES, nr), 0) * NSA_SEL_BLOCK
            cstart = lax.broadcasted_iota(jnp.int32, (LANES, nr), 1) * NSA_CMP_STRIDE
            ov = jnp.where(jnp.logical_and(cstart < sstart + NSA_SEL_BLOCK, cstart + NSA_CMP_LEN > sstart),
                           1.0, 0.0).astype(BF16)
            p_hi = psum.astype(BF16)
            r1 = psum - p_hi.astype(F32)
            p_mid = r1.astype(BF16)
            p_lo = (r1 - p_mid.astype(F32)).astype(BF16)
            return (jnp.dot(ov, p_hi, preferred_element_type=F32) + jnp.dot(ov, p_mid, preferred_element_type=F32)
                    + jnp.dot(ov, p_lo, preferred_element_type=F32))
        return run

    sizes = list(range(LANES, nc + 1, LANES))
    needed = (q0 + tq) // NSA_CMP_STRIDE
    imp = lax.switch(jnp.minimum((needed + LANES - 1) // LANES, len(sizes)) - 1, [branches(nr) for nr in sizes])
    blk = lax.broadcasted_iota(jnp.int32, (LANES, tq), 0)
    qb = (q0 + lax.broadcasted_iota(jnp.int32, (LANES, tq), 1)) // NSA_SEL_BLOCK
    forced = jnp.logical_or(blk == 0, jnp.logical_or(blk == qb, blk == qb - 1))
    imp = jnp.where(forced, imp + NSA_FORCE_BONUS, imp)
    imp = jnp.where(blk <= qb, imp, -jnp.inf)
    sel = _topk_mask(imp, blk, NSA_SEL_TOPK)
    mb_ref[0] = jnp.where(sel, 0.0, NEG).T.astype(mb_ref.dtype)


def _nsa_cmp(y3, kc, vc, *, tq=512):
    b, s, _ = y3.shape
    g, nc, d = kc.shape
    rep, n = NSA_REP, NSA_GROUPS
    assert s // NSA_SEL_BLOCK <= LANES and s % tq == 0
    return pl.pallas_call(
        functools.partial(_nsa_cmp_body, tq=tq, rep=rep), name="nsa_cmp",
        grid=(g, s // tq),
        in_specs=[pl.BlockSpec((1, tq, rep * d), lambda i, j: (i // n, j, i % n)),
                  pl.BlockSpec((1, nc, d), lambda i, j: (i, 0, 0)),
                  pl.BlockSpec((1, d, nc), lambda i, j: (i, 0, 0))],
        out_specs=[pl.BlockSpec((1, tq, rep * d), lambda i, j: (i // n, j, i % n)),
                   pl.BlockSpec((1, tq, LANES), lambda i, j: (i, j, 0))],
        out_shape=[jax.ShapeDtypeStruct((b, s, n * rep * d), F32),
                   jax.ShapeDtypeStruct((g, s, LANES), BF16)],
        scratch_shapes=[pltpu.VMEM((rep, nc, rep * d), BF16)],
        compiler_params=_cparams(("parallel", "arbitrary")),
    )(y3, kc, vc)


def _nsa_combine_body(oc_ref, os_ref, ow_ref, g_ref, b_ref, e_ref, o_ref):
    gs = jax.nn.sigmoid(g_ref[...] + b_ref[...])
    g_hi = gs.astype(BF16)
    g_lo = (gs - g_hi.astype(F32)).astype(BF16)
    out = None
    for i, ref in enumerate((oc_ref, os_ref, ow_ref)):
        w = (jnp.dot(g_hi, e_ref[i], preferred_element_type=F32)
             + jnp.dot(g_lo, e_ref[i], preferred_element_type=F32))
        term = w * ref[...]
        out = term if out is None else out + term
    o_ref[...] = out.astype(o_ref.dtype)


def _nsa_combine(oc, osel, ow, yg, g_blk, gate_b, *, tm=512):
    t, n = oc.shape
    nh = NSA_HEADS * 3
    bp = jnp.pad(gate_b.reshape(1, nh).astype(F32), ((0, 0), (0, LANES - nh)))
    row = jnp.arange(LANES)[:, None]
    col = jnp.arange(n)[None, :] // HEAD_DIM
    expand = jnp.stack([(row == col * 3 + i) for i in range(3)]).astype(BF16)
    tok = pl.BlockSpec((tm, n), lambda i: (i, 0))
    return pl.pallas_call(
        _nsa_combine_body, name="nsa_combine", grid=(t // tm,),
        in_specs=[tok, tok, tok, pl.BlockSpec((tm, LANES), lambda i: (i, g_blk)),
                  pl.BlockSpec((1, LANES), lambda i: (0, 0)),
                  pl.BlockSpec((3, LANES, n), lambda i: (0, 0, 0))],
        out_specs=tok,
        out_shape=jax.ShapeDtypeStruct((t, n), BF16),
        compiler_params=_cparams(("parallel",)),
    )(oc, osel, ow, yg, bp, expand)


def _to_heads(x, b, s, n):
    return x.reshape(b, s, n, -1).transpose(0, 2, 1, 3).reshape(b * n, s, -1)


def _even_mixer(h, gain, layer_idx, rope16, w_in, lq1, lk1, lq2, lk2, subln, b, s):
    na = MOBA_HEADS * HEAD_DIM
    nb = DIFF_HEADS * 2 * HEAD_DIM
    w_rope = jnp.concatenate([w_in[:, 0:2 * na], w_in[:, 3 * na:3 * na + 2 * nb]], axis=1)
    w_rest = jnp.concatenate([w_in[:, 2 * na:3 * na], w_in[:, 3 * na + 2 * nb:]], axis=1)
    w = jnp.concatenate([w_rope, w_rest], axis=1).astype(BF16)
    y = _proj(h, w, name="proj_even_in", gain=gain, rope=rope16, rope_cols=w_rope.shape[1], shift=ROPE_DIM // 2, out_dtype=BF16)
    y3 = y.reshape(b, s, -1)
    blk = lambda off: off // LANES
    pairs = na // LANES
    qa, ka = _moba_gate(y3, q_blk=blk(0), k_blk=blk(na), pairs=pairs)
    o_a = _pflash(qa, ka, y3, name="flash_moba", mode="moba", dk=2 * LANES, batch=b, pairs=pairs,
                  q_map=lambda i, j: (i, j, 0), k_map=lambda i, j: (i, 0, 0),
                  v_map=lambda i, j: (i // pairs, 0, blk(2 * na + 2 * nb) + i % pairs))
    lambda_init = 0.8 - 0.6 * math.exp(-0.3 * layer_idx)
    nh = DIFF_HEADS
    o_b = _pflash(y3, y3, y3, name="flash_diff", mode="diff", dk=LANES, batch=b, pairs=nh, scale=HEAD_DIM ** -0.5,
                  q_map=lambda i, j: (i // nh, j, blk(2 * na) + i % nh),
                  k_map=lambda i, j: (i // nh, 0, blk(2 * na + nb) + i % nh),
                  v_map=lambda i, j: (i // nh, 0, blk(3 * na + 2 * nb) + i % nh),
                  diff_params=(lq1, lk1, lq2, lk2, subln), lambda_init=lambda_init)
    t = b * s
    return o_a.reshape(t, -1), o_b.reshape(t, -1)


def _odd_mixer(h, gain, positions, rope16, rope32s, w_in, gate_b, pe_k, pe_v, k_w1, k_w2, v_w1, v_w2,
               q_norm, w_uq, kv_norm, w_ukv, b, s):
    G, R, d = NSA_GROUPS, NSA_REP, HEAD_DIM
    sizes = [NSA_HEADS * d] + [G * d] * 6 + [NSA_HEADS * 3, MLA_Q_RANK, MLA_KV_RANK, MLA_ROPE]
    offs = [0]
    for z in sizes:
        offs.append(offs[-1] + z)
    col = lambda i: w_in[:, offs[i]:offs[i + 1]]
    w_r = jnp.concatenate([col(0), col(3), col(5)], axis=1).astype(BF16)
    yr = _proj(h, w_r, name="proj_odd_rope", gain=gain, rope=rope16, rope_cols=w_r.shape[1], shift=ROPE_DIM // 2, out_dtype=BF16)
    w_p = jnp.concatenate([col(1), col(2), col(4), col(6), col(8), col(9), col(7)], axis=1)
    w_p = jnp.pad(w_p, ((0, 0), (0, (-w_p.shape[1]) % LANES)))
    w_pe = jnp.pad(col(10), ((0, 0), (MLA_NOPE, LANES - MLA_NOPE - MLA_ROPE)))
    yp = _proj(h, jnp.concatenate([w_p, w_pe], axis=1).astype(BF16), name="proj_odd_plain", gain=gain, out_dtype=F32)
    k_cmp, v_cmp = yp[:, 0:G * d], yp[:, G * d:2 * G * d]
    cq_blk = 4 * G * d // MLA_Q_RANK
    ckv_blk = (4 * G * d + MLA_Q_RANK) // MLA_KV_RANK
    gate_blk = (4 * G * d + MLA_Q_RANK + MLA_KV_RANK) // LANES
    kpe_blk = w_p.shape[1] // LANES

    yr3, yp3 = yr.reshape(b, s, -1), yp.reshape(b, s, -1)
    nc = s // NSA_CMP_STRIDE
    cpos = jnp.concatenate([positions[:, NSA_CMP_LEN - 1::NSA_CMP_STRIDE], positions[:, -1:]], axis=1)
    ctab = [t.reshape(b, nc, LANES) for t in _rope_tables(cpos, ROPE_DIM, HEAD_DIM)]
    xk = _to_heads(k_cmp, b, s, G).reshape(b * G, nc, NSA_CMP_STRIDE * d)
    xv = _to_heads(v_cmp, b, s, G).reshape(b * G, nc, NSA_CMP_STRIDE * d)
    kc = _nsa_compress(xk, pe_k, k_w1.astype(BF16), k_w2.astype(BF16), ctab, rope=True, groups=G)
    vc = _nsa_compress(xv, pe_v, v_w1.astype(BF16), v_w2.astype(BF16), ctab, rope=False, groups=G)
    o_c, mbias = _nsa_cmp(yr3, kc, vc.transpose(0, 2, 1))
    k_blk = NSA_HEADS * d // LANES
    vsw = yp[:, 2 * G * d:4 * G * d].astype(BF16).reshape(b, s, -1)
    o_s = _gflash(yr3, yr3, vsw, mbias, name="flash_sel", q_blk=0, k_blk=k_blk, v_blk=0, tq=256)
    o_w = _gflash(yr3, yr3, vsw, None, name="flash_win", q_blk=0, k_blk=k_blk + 1, v_blk=1, tq=256, window=NSA_WINDOW)
    t = b * s
    o_nsa = _nsa_combine(o_c.reshape(t, -1), o_s.reshape(t, -1), o_w.reshape(t, -1), yp, gate_blk, gate_b)

    hq = MLA_NOPE + MLA_ROPE
    nh = MLA_HEADS
    wq = jnp.pad(w_uq.reshape(MLA_Q_RANK, nh, hq), ((0, 0), (0, 0), (0, LANES - hq))).reshape(MLA_Q_RANK, nh * LANES)
    q = _proj(yp, wq.astype(BF16), name="proj_mla_q", x_blk=(MLA_Q_RANK, cq_blk), gain=q_norm, rope=rope32s,
              rope_cols=nh * LANES, shift=MLA_ROPE // 2, out_dtype=BF16)
    wkv = w_ukv.reshape(MLA_KV_RANK, nh, MLA_NOPE + MLA_V)
    wk = jnp.pad(wkv[:, :, :MLA_NOPE], ((0, 0), (0, 0), (0, LANES - MLA_NOPE))).reshape(MLA_KV_RANK, nh * LANES)
    wkv = jnp.concatenate([wk, wkv[:, :, MLA_NOPE:].reshape(MLA_KV_RANK, nh * MLA_V)], axis=1).astype(BF16)
    kv = _proj(yp, wkv, name="proj_mla_kv", x_blk=(MLA_KV_RANK, ckv_blk), gain=kv_norm, rope=rope32s,
               shift=MLA_ROPE // 2, add=yp, add_blk=kpe_blk, add_cols=nh * LANES, rope_add=True,
               out_dtype=BF16)
    q3, kv3 = q.reshape(b, s, -1), kv.reshape(b, s, -1)
    pairs = nh // 2
    o_d = _pflash(q3, kv3, kv3, name="flash_mla", mode="slots", dk=2 * LANES, batch=b, pairs=pairs, scale=hq ** -0.5,
                  q_map=lambda i, j: (i // pairs, j, i % pairs), k_map=lambda i, j: (i // pairs, 0, i % pairs),
                  v_map=lambda i, j: (i // pairs, 0, nh + i % pairs))

    return o_nsa, o_d.reshape(t, -1)


def kernel(x, positions, attn_norm, ffn_norm, final_norm, ffn_w_gate, ffn_w_up, ffn_w_down, ev_w_in, ev_w_out, diff_lambda_q1, diff_lambda_k1, diff_lambda_q2, diff_lambda_k2, diff_subln, od_w_in, od_w_out, nsa_gate_b, nsa_pe_k, nsa_pe_v, nsa_k_w1, nsa_k_w2, nsa_v_w1, nsa_v_w2, mla_q_norm, mla_w_uq, mla_kv_norm, mla_w_ukv):
    b, s, d = x.shape
    depth = attn_norm.shape[0]
    rope16 = _rope_tables(positions, ROPE_DIM, HEAD_DIM)
    rope32s = _rope_tables(positions, MLA_ROPE, LANES, offset=MLA_NOPE)
    h = x.reshape(b * s, d)
    for l in range(depth):
        i = l // 2
        if l % 2 == 0:
            oa, ob = _even_mixer(h, attn_norm[l], l, rope16, ev_w_in[i], diff_lambda_q1[i],
                                 diff_lambda_k1[i], diff_lambda_q2[i], diff_lambda_k2[i], diff_subln[i], b, s)
            w_out = ev_w_out[i]
        else:
            oa, ob = _odd_mixer(h, attn_norm[l], positions, rope16, rope32s, od_w_in[i], nsa_gate_b[i],
                                nsa_pe_k[i], nsa_pe_v[i], nsa_k_w1[i], nsa_k_w2[i], nsa_v_w1[i], nsa_v_w2[i],
                                mla_q_norm[i], mla_w_uq[i], mla_kv_norm[i], mla_w_ukv[i], b, s)
            w_out = od_w_out[i]
        h = _ffn(h, oa, ob, w_out.astype(BF16), ffn_norm[l], ffn_w_gate[l].astype(BF16), ffn_w_up[l].astype(BF16),
                 ffn_w_down[l].astype(BF16), final_norm, final_norm=(l == depth - 1))
    return h.reshape(b, s, d)
```

```python
import functools
import math

import jax
import jax.numpy as jnp
from jax import lax
from jax.experimental import pallas as pl
from jax.experimental.pallas import tpu as pltpu

F32 = jnp.float32
BF16 = jnp.bfloat16

D_MODEL = 1024
HEAD_DIM = 64
ROPE_THETA = 500000.0
ROPE_DIM = HEAD_DIM // 4
NORM_EPS = 1e-5
D_FF = 2816

MOBA_HEADS = 8
MOBA_BLOCK = 256
MOBA_TOPK = 3
DIFF_HEADS = 4
DIFF_V = 2 * HEAD_DIM
NSA_HEADS = 8
NSA_GROUPS = 2
NSA_REP = NSA_HEADS // NSA_GROUPS
NSA_CMP_LEN = 32
NSA_CMP_STRIDE = 16
NSA_CMP_HIDDEN = 256
NSA_SEL_BLOCK = 64
NSA_SEL_TOPK = 16
NSA_WINDOW = 512
NSA_FORCE_BONUS = 1e3
MLA_HEADS = 8
MLA_Q_RANK = 256
MLA_KV_RANK = 128
MLA_NOPE = 64
MLA_ROPE = 32
MLA_V = 64

LANES = 128
MOBA_BIAS_LANES = 32
NEG = -1e30
M_INIT = -1e37
LOG2E = 1.4426950408889634
UNROLL = 8
VMEM_LIMIT = 56 * 1024 * 1024

ROW_TILE = 512
FFN_TF = 1408
FLASH_COLS = 1024
FLASH_TK = 512
SELECT_TQ = 1024

_NT = (((1,), (1,)), ((), ()))


def _cparams(sem):
    return pltpu.CompilerParams(dimension_semantics=sem, vmem_limit_bytes=VMEM_LIMIT)


def _rope_tables(positions, dim, period, offset=0):
    r = dim // 2
    inv = 1.0 / (ROPE_THETA ** (jnp.arange(0, dim, 2, dtype=F32) / dim))
    ang = positions.astype(F32)[..., None] * inv
    cos, sin = jnp.cos(ang), jnp.sin(ang)
    const = lambda n, val: jnp.full(ang.shape[:-1] + (n,), val, F32)
    rest = period - offset - 2 * r
    zr = jnp.zeros_like(sin)
    c = jnp.concatenate([const(offset, 1.0), cos, cos, const(rest, 1.0)], -1)
    s1 = jnp.concatenate([const(offset, 0.0), -sin, zr, const(rest, 0.0)], -1)
    s2 = jnp.concatenate([const(offset, 0.0), zr, sin, const(rest, 0.0)], -1)
    reps = LANES // period
    tile = lambda t: jnp.tile(t, (1,) * (t.ndim - 1) + (reps,)).reshape(-1, LANES)
    return tile(c), tile(s1), tile(s2)


def _apply_rope(y, c, s1, s2, shift):
    return y * c + pltpu.roll(y, LANES - shift, 1) * s1 + pltpu.roll(y, shift, 1) * s2


def _proj_body(*refs, has_rope, add_cols, rope_add, rope_cols, shift, chunk, split):
    it = iter(refs)
    x_ref, g_ref, w_ref = next(it), next(it), next(it)
    if has_rope:
        c_ref, s1_ref, s2_ref = next(it), next(it), next(it)
    add_ref = next(it) if add_cols else None
    outs = list(it)
    n = w_ref.shape[1]
    xf = x_ref[...].astype(F32)
    y = xf * lax.rsqrt(jnp.mean(xf * xf, axis=-1, keepdims=True) + NORM_EPS)
    xb = (y * g_ref[...]).astype(BF16)
    if add_cols:
        add = add_ref[...].astype(F32)
        if rope_add:
            add = _apply_rope(add, c_ref[...], s1_ref[...], s2_ref[...], shift)
    spans = [(0, n, outs[0])] if len(outs) == 1 else [(0, split, outs[0]), (split, n, outs[1])]
    for lo, hi, o_ref in spans:
        for c0 in range(lo, hi, chunk):
            cw = min(chunk, hi - c0)
            y = jnp.dot(xb, w_ref[:, c0:c0 + cw], preferred_element_type=F32)
            if (has_rope and c0 < rope_cols) or c0 < add_cols:
                for k0 in range(0, cw, LANES):
                    ys = y[:, k0:k0 + LANES]
                    if has_rope and c0 + k0 < rope_cols:
                        ys = _apply_rope(ys, c_ref[...], s1_ref[...], s2_ref[...], shift)
                    if c0 + k0 < add_cols:
                        ys = ys + add
                    o_ref[:, c0 - lo + k0:c0 - lo + k0 + LANES] = ys.astype(o_ref.dtype)
            else:
                o_ref[:, c0 - lo:c0 - lo + cw] = y.astype(o_ref.dtype)


def _proj(x, w, gain, *, name, x_blk=None, rope=None, rope_cols=0, shift=0, add=None, add_blk=0, add_cols=0,
          rope_add=False, out_dtype=F32, split=None, out2_dtype=None, tm=ROW_TILE, chunk=512):
    t = x.shape[0]
    k, xj = (x.shape[1], 0) if x_blk is None else x_blk
    n = w.shape[1]
    assert t % tm == 0 and n % LANES == 0 and rope_cols % LANES == 0 and add_cols % LANES == 0
    assert split is None or split % LANES == 0
    has_rope = rope is not None
    args = [x, gain.reshape(1, k).astype(F32), w]
    specs = [pl.BlockSpec((tm, k), lambda i: (i, xj)), pl.BlockSpec((1, k), lambda i: (0, 0)),
             pl.BlockSpec((k, n), lambda i: (0, 0))]
    if has_rope:
        for tb in rope:
            args.append(tb)
            specs.append(pl.BlockSpec((tm, LANES), lambda i: (i, 0)))
    if add_cols:
        args.append(add)
        specs.append(pl.BlockSpec((tm, LANES), lambda i: (i, add_blk)))
    widths = [(n, out_dtype)] if split is None else [(split, out_dtype), (n - split, out2_dtype)]
    body = functools.partial(_proj_body, has_rope=has_rope, add_cols=add_cols, rope_add=rope_add,
                             rope_cols=rope_cols, shift=shift, chunk=chunk, split=split)
    out = pl.pallas_call(
        body, name=name, grid=(t // tm,), in_specs=specs,
        out_specs=[pl.BlockSpec((tm, wd), lambda i: (i, 0)) for wd, _ in widths],
        out_shape=[jax.ShapeDtypeStruct((t, wd), dt) for wd, dt in widths],
        compiler_params=_cparams(("parallel",)),
    )(*args)
    return out[0] if split is None else out


def _ffn_body(x_ref, oa_ref, ob_ref, wo_ref, g_ref, wg_ref, wu_ref, wd_ref, fg_ref, o_ref, h_ref, xn_ref, acc_ref, *,
              final_norm):
    j = pl.program_id(1)

    @pl.when(j == 0)
    def _():
        ka = oa_ref.shape[1]
        h = (x_ref[...] + jnp.dot(oa_ref[...], wo_ref[0:ka, :], preferred_element_type=F32)
             + jnp.dot(ob_ref[...], wo_ref[ka:, :], preferred_element_type=F32))
        h_ref[...] = h
        y = h * lax.rsqrt(jnp.mean(h * h, axis=-1, keepdims=True) + NORM_EPS)
        xn_ref[...] = (y * g_ref[...]).astype(BF16)
        acc_ref[...] = jnp.zeros_like(acc_ref)

    xn = xn_ref[...]
    g = jnp.dot(xn, wg_ref[...], preferred_element_type=F32)
    u = jnp.dot(xn, wu_ref[...], preferred_element_type=F32)
    a = (jax.nn.silu(g) * u).astype(BF16)
    acc_ref[...] += jnp.dot(a, wd_ref[...], preferred_element_type=F32)

    @pl.when(j == pl.num_programs(1) - 1)
    def _():
        h = h_ref[...] + acc_ref[...]
        if final_norm:
            y = h * lax.rsqrt(jnp.mean(h * h, axis=-1, keepdims=True) + NORM_EPS)
            h = y * fg_ref[...]
        o_ref[...] = h


def _ffn(x, oa, ob, w_out, gain, wg, wu, wd, final_gain, *, final_norm, tm=ROW_TILE, tf=FFN_TF):
    t, d = x.shape
    f = wg.shape[1]
    assert t % tm == 0 and f % tf == 0 and oa.shape[1] + ob.shape[1] == w_out.shape[0]
    return pl.pallas_call(
        functools.partial(_ffn_body, final_norm=final_norm), name="ffn",
        grid=(t // tm, f // tf),
        in_specs=[
            pl.BlockSpec((tm, d), lambda i, j: (i, 0)),
            pl.BlockSpec((tm, oa.shape[1]), lambda i, j: (i, 0)),
            pl.BlockSpec((tm, ob.shape[1]), lambda i, j: (i, 0)),
            pl.BlockSpec(w_out.shape, lambda i, j: (0, 0)),
            pl.BlockSpec((1, d), lambda i, j: (0, 0)),
            pl.BlockSpec((d, tf), lambda i, j: (0, j)),
            pl.BlockSpec((d, tf), lambda i, j: (0, j)),
            pl.BlockSpec((tf, d), lambda i, j: (j, 0)),
            pl.BlockSpec((1, d), lambda i, j: (0, 0)),
        ],
        out_specs=pl.BlockSpec((tm, d), lambda i, j: (i, 0)),
        out_shape=jax.ShapeDtypeStruct((t, d), F32),
        scratch_shapes=[pltpu.VMEM((tm, d), F32), pltpu.VMEM((tm, d), BF16), pltpu.VMEM((tm, d), F32)],
        compiler_params=_cparams(("parallel", "arbitrary")),
    )(x, oa, ob, w_out, gain.reshape(1, d).astype(F32), wg, wu, wd, final_gain.reshape(1, d).astype(F32))


def _sweep(q, k_get, vt_get, s_ref, *, q0, tq, tk, dv, scale):
    cols = q.shape[0]

    def fold8(x, op):
        return op(x.reshape(x.shape[0] // 8, 8, x.shape[1]), axis=0)

    def first_col(u, n, n_masked):
        d = u - (n - n_masked)
        return d * tk if (d > 0 and cols == tq) else 0

    def upd(full, c0, f):
        return f(full) if c0 == 0 else jnp.concatenate([full[:, :c0], f(full[:, c0:])], axis=1)

    def score_block(j0, n, mrun, n_masked):
        ks = pl.multiple_of(j0 * tk, tk)
        n_wide = sum(1 for u in range(n) if first_col(u, n, n_masked) == 0)
        s = lax.dot_general(k_get(ks, n_wide * tk), q, _NT, preferred_element_type=F32) * (scale * LOG2E)
        for u in range(n):
            c0 = first_col(u, n, n_masked)
            if c0 == 0:
                su = s[u * tk:(u + 1) * tk]
            else:
                ku = k_get(pl.multiple_of(ks + u * tk, tk), tk)
                su = lax.dot_general(ku, q[c0:], _NT, preferred_element_type=F32) * (scale * LOG2E)
            if u >= n - n_masked:
                kpos = ks + u * tk + lax.broadcasted_iota(jnp.int32, su.shape, 0)
                qpos = q0 + c0 + lax.rem(lax.broadcasted_iota(jnp.int32, su.shape, 1), tq)
                su = jnp.where(kpos <= qpos, su, NEG)
            s_ref[j0 + u, :, c0:cols] = su
            mrun = upd(mrun, c0, lambda t: jnp.maximum(t, fold8(su, jnp.max)))
        return mrun

    def pv_block(j0, n, carry, m, n_masked=0):
        l8, acc = carry
        for u in range(n):
            c0 = first_col(u, n, n_masked)
            p = jnp.exp2(s_ref[j0 + u, :, c0:cols] - m[:, c0:])
            l8 = upd(l8, c0, lambda t: t + fold8(p, jnp.sum))
            pv = jnp.dot(vt_get(j0 + u), p.astype(BF16), preferred_element_type=F32)
            acc = upd(acc, c0, lambda t: t + pv)
        return l8, acc

    m_init = jnp.full((8, cols), M_INIT, F32)
    acc_init = (jnp.zeros((8, cols), F32), jnp.zeros((dv, cols), F32))
    n_diag = max(tq // tk, 1)
    assert UNROLL % n_diag == 0
    n_full = q0 // tk
    groups = n_full // UNROLL
    rem = n_full - groups * UNROLL
    tail0 = groups * UNROLL
    tails = range(n_diag, UNROLL + n_diag, n_diag)

    mrun = lax.fori_loop(0, groups, lambda i, t: score_block(UNROLL * i, UNROLL, t, 0), m_init)
    mrun = lax.switch(rem // n_diag, [functools.partial(score_block, tail0, n, n_masked=n_diag) for n in tails],
                      mrun)
    m = jnp.max(mrun, axis=0, keepdims=True)
    carry = lax.fori_loop(0, groups, lambda i, t: pv_block(UNROLL * i, UNROLL, t, m), acc_init)
    l8, acc = lax.switch(rem // n_diag, [functools.partial(pv_block, tail0, n, m=m, n_masked=n_diag) for n in tails],
                         carry)
    l = jnp.sum(l8, axis=0, keepdims=True)
    return acc / jnp.maximum(l, 1e-30)


def _head_lanes(mode, v, lane):
    d = HEAD_DIM
    if mode == "diff":
        return jnp.logical_and(lane >= d * v, lane < d * (v + 1))
    if mode == "moba":
        lo = 2 * d + v * MOBA_BIAS_LANES
        return jnp.logical_or(jnp.logical_and(lane >= d * v, lane < d * (v + 1)),
                              jnp.logical_and(lane >= lo, lane < lo + MOBA_BIAS_LANES))
    return jnp.logical_and(lane >= LANES * v, lane < LANES * (v + 1))


def _pflash_body(*refs, mode, tq, tk, scale, lambda_init):
    if mode == "diff":
        q_ref, k_ref, v_ref, lq1_ref, lk1_ref, lq2_ref, lk2_ref, sg_ref, o_ref, s_ref, vt_ref = refs
    else:
        q_ref, k_ref, v_ref, o_ref, s_ref, vt_ref = refs
    qi = pl.program_id(1)
    n_chunks = v_ref.shape[1] // tk

    @pl.when(qi == 0)
    def _():
        for c in range(n_chunks):
            vt_ref[c] = v_ref[0, c * tk:(c + 1) * tk, :].astype(F32).T.astype(BF16)

    q = q_ref[0]
    lane = lax.broadcasted_iota(jnp.int32, q.shape, 1)
    d = HEAD_DIM
    outs = []
    for v in range(2):
        qv = jnp.where(_head_lanes(mode, v, lane), q, jnp.zeros_like(q))
        if mode == "diff":
            vt_get, dv = (lambda j: vt_ref[j]), LANES
        else:
            vt_get, dv = (lambda j, v=v: vt_ref[j, v * d:(v + 1) * d, :]), d
        outs.append(_sweep(qv, lambda ks, n: k_ref[0, pl.ds(ks, n), :], vt_get, s_ref,
                           q0=qi * tq, tq=tq, tk=tk, dv=dv, scale=scale))
    if mode == "diff":
        lam = (jnp.exp(jnp.sum(lq1_ref[...] * lk1_ref[...], axis=-1, keepdims=True))
               - jnp.exp(jnp.sum(lq2_ref[...] * lk2_ref[...], axis=-1, keepdims=True)) + lambda_init)
        d = outs[0] - lam * outs[1]
        y = d * lax.rsqrt(jnp.mean(d * d, axis=0, keepdims=True) + NORM_EPS)
        o_t = (y * sg_ref[...]) * (1.0 - lambda_init)
    else:
        o_t = jnp.concatenate(outs, axis=0)
    o_ref[0] = o_t.T.astype(o_ref.dtype)


def _pflash(q, k, v, *, name, mode, dk, q_map, k_map, v_map, batch, pairs, tq=FLASH_COLS, tk=FLASH_TK, scale=1.0,
            diff_params=None, lambda_init=0.0):
    s = v.shape[1]
    assert s % tk == 0 and tq % tk == 0 and s % tq == 0
    args = [q, k, v]
    in_specs = [pl.BlockSpec((1, tq, dk), q_map), pl.BlockSpec((1, s, dk), k_map),
                pl.BlockSpec((1, s, LANES), v_map)]
    if mode == "diff":
        lq1, lk1, lq2, lk2, subln = diff_params
        for a in (lq1, lk1, lq2, lk2):
            args.append(a.reshape(1, HEAD_DIM).astype(F32))
            in_specs.append(pl.BlockSpec((1, HEAD_DIM), lambda i, j: (0, 0)))
        args.append(subln.reshape(LANES, 1).astype(F32))
        in_specs.append(pl.BlockSpec((LANES, 1), lambda i, j: (0, 0)))
    body = functools.partial(_pflash_body, mode=mode, tq=tq, tk=tk, scale=scale, lambda_init=lambda_init)
    return pl.pallas_call(
        body, name=name, grid=(batch * pairs, s // tq), in_specs=in_specs,
        out_specs=pl.BlockSpec((1, tq, LANES), lambda i, j: (i // pairs, j, i % pairs)),
        out_shape=jax.ShapeDtypeStruct((batch, s, pairs * LANES), BF16),
        scratch_shapes=[pltpu.VMEM((s // tk, tk, tq), F32), pltpu.VMEM((s // tk, LANES, tk), BF16)],
        compiler_params=_cparams(("parallel", "arbitrary")),
    )(*args)


def _band_masks(tq, window):
    kl = jnp.arange(window + tq)[None, :, None]
    ql = jnp.arange(tq)[None, None, :]
    q0 = (jnp.arange(window // tq + 1) * tq)[:, None, None]
    k = jnp.maximum(q0 - window, 0) + kl
    t = q0 + ql
    return jnp.where(jnp.logical_and(k <= t, k > t - window), 0.0, NEG).astype(F32)


def _band(q, ka_ref, vt_ref, mask_ref, *, q0, tq, window):
    cols = q.shape[0]
    band = window + tq
    start = pl.multiple_of(jnp.maximum(q0 - window, 0), tq)
    s = lax.dot_general(ka_ref[pl.ds(start, band), :], q, _NT, preferred_element_type=F32) * LOG2E

    s = jnp.concatenate([s[:, c:c + tq] + mask_ref[0] for c in range(0, cols, tq)], axis=1)
    p = jnp.exp2(s - jnp.max(s, axis=0, keepdims=True))
    l = jnp.sum(p, axis=0, keepdims=True)
    pb = p.astype(BF16)
    acc = None
    for c in range(band // tq):
        part = jnp.dot(vt_ref[start // tq + c], pb[c * tq:(c + 1) * tq], preferred_element_type=F32)
        acc = part if acc is None else acc + part
    return acc / jnp.maximum(l, 1e-30)


def _gflash_body(*refs, tq, tk, rep, groups, window, select):
    if select:
        q_ref, k_ref, v_ref, mb_ref, o_ref, s_ref, vt_ref, ka_ref, qa_ref = refs
    else:
        q_ref, k_ref, v_ref, mask_ref, o_ref, vt_ref, ka_ref, qa_ref = refs
    d = HEAD_DIM
    tv = vt_ref.shape[2]
    g = pl.program_id(0) % groups
    qi = pl.program_id(1)
    n_chunks = v_ref.shape[1] // tk

    @pl.when(qi == 0)
    def _():
        move = _place(LANES, LANES, g * d)
        for c in range(n_chunks):
            rows = slice(c * tk, (c + 1) * tk)
            vt = v_ref[0, rows, :].astype(F32).T
            vt = jnp.where(g == 0, vt[0:d], vt[d:2 * d]).astype(BF16)
            for u in range(tk // tv):
                vt_ref[c * (tk // tv) + u] = vt[:, u * tv:(u + 1) * tv]
            kg = jnp.dot(k_ref[0, rows, :], move, preferred_element_type=F32).astype(BF16)
            if select:
                blk = (c * tk + lax.broadcasted_iota(jnp.int32, (tk, LANES), 0)) // NSA_SEL_BLOCK
                lane = lax.broadcasted_iota(jnp.int32, (tk, LANES), 1)
                ka_ref[rows, 0:LANES] = jnp.where(blk == lane, 1.0, 0.0).astype(BF16)
                ka_ref[rows, LANES:2 * LANES] = kg
            else:
                ka_ref[rows, :] = kg

    q = q_ref[0]
    for r in range(rep):
        qr = jnp.dot(q, _place(rep * d, LANES, r * d, d ** -0.5), preferred_element_type=F32).astype(BF16)
        rows = slice(r * tq, (r + 1) * tq)
        if select:
            qa_ref[rows, 0:LANES] = mb_ref[0]
            qa_ref[rows, LANES:2 * LANES] = qr
        else:
            qa_ref[rows, :] = qr
    if window is None:
        o_t = _sweep(qa_ref[...], lambda ks, n: ka_ref[pl.ds(ks, n), :], lambda j: vt_ref[j], s_ref,
                     q0=qi * tq, tq=tq, tk=tk, dv=d, scale=1.0)
    else:
        half = (rep // 2) * tq
        o_t = jnp.concatenate([_band(qa_ref[0:half, :], ka_ref, vt_ref, mask_ref, q0=qi * tq, tq=tq, window=window),
                               _band(qa_ref[half:, :], ka_ref, vt_ref, mask_ref, q0=qi * tq, tq=tq, window=window)],
                              axis=1)
    for r in range(rep):
        o_ref[0, :, r * d:(r + 1) * d] = o_t[:, r * tq:(r + 1) * tq].T.astype(o_ref.dtype)


def _gflash(yq, yk, yv, mbias, *, name, q_blk, k_blk, v_blk, tq=FLASH_COLS // NSA_REP, tk=FLASH_TK, window=None):
    b, s, _ = yq.shape
    rep, groups, d = NSA_REP, NSA_GROUPS, HEAD_DIM
    assert groups * d == LANES and s % tk == 0 and tk % tq == 0
    select = mbias is not None
    dk = 2 * LANES if select else LANES
    assert select == (window is None)
    assert window is None or (window % tq == 0 and s >= window + tq)
    tv = tk if window is None else tq
    scratch = [pltpu.VMEM((s // tv, d, tv), BF16), pltpu.VMEM((s, dk), BF16), pltpu.VMEM((rep * tq, dk), BF16)]
    if window is None:
        scratch.insert(0, pltpu.VMEM((s // tk, tk, rep * tq), F32))
    args = [yq, yk, yv]
    in_specs = [pl.BlockSpec((1, tq, rep * d), lambda i, j: (i // groups, j, q_blk + i % groups)),
                pl.BlockSpec((1, s, LANES), lambda i, j: (i // groups, 0, k_blk)),
                pl.BlockSpec((1, s, LANES), lambda i, j: (i // groups, 0, v_blk))]
    if select:
        args.append(mbias)
        in_specs.append(pl.BlockSpec((1, tq, LANES), lambda i, j: (i, j, 0)))
    else:
        args.append(_band_masks(tq, window))
        in_specs.append(pl.BlockSpec((1, window + tq, tq), lambda i, j: (jnp.minimum(j, window // tq), 0, 0)))
    body = functools.partial(_gflash_body, tq=tq, tk=tk, rep=rep, groups=groups, window=window, select=select)
    return pl.pallas_call(
        body, name=name, grid=(b * groups, s // tq), in_specs=in_specs,
        out_specs=pl.BlockSpec((1, tq, rep * d), lambda i, j: (i // groups, j, i % groups)),
        out_shape=jax.ShapeDtypeStruct((b, s, groups * rep * d), F32),
        scratch_shapes=scratch,
        compiler_params=_cparams(("parallel", "arbitrary")),
    )(*args)


def _topk_mask(score, row, k):
    sel = None
    for _ in range(k):
        mx = jnp.max(score, axis=0, keepdims=True)
        idx = jnp.min(jnp.where(score == mx, row, LANES), axis=0, keepdims=True)
        hit = row == idx
        pick = jnp.logical_and(hit, mx > -jnp.inf)
        sel = pick if sel is None else jnp.logical_or(sel, pick)
        score = jnp.where(hit, -jnp.inf, score)
    return sel


def _moba_gate_body(q_ref, k_ref, qa_ref, ka_ref, kmean_ref, *, tq, seq):
    qi = pl.program_id(1)
    nbl = MOBA_BIAS_LANES

    @pl.when(qi == 0)
    def _():
        rowblk = lax.rem(lax.broadcasted_iota(jnp.int32, (LANES, seq), 0), nbl)
        colblk = lax.broadcasted_iota(jnp.int32, (LANES, seq), 1) // MOBA_BLOCK
        ind = jnp.where(rowblk == colblk, 1.0, 0.0).astype(BF16)
        ksum = jnp.dot(ind, k_ref[0], preferred_element_type=F32)
        rhead = lax.broadcasted_iota(jnp.int32, (LANES, LANES), 0) // nbl
        lhead = lax.broadcasted_iota(jnp.int32, (LANES, LANES), 1) // HEAD_DIM
        kmean_ref[...] = jnp.where(rhead == lhead, ksum * (1.0 / MOBA_BLOCK), 0.0)

    q = q_ref[0]
    q0 = pl.multiple_of(qi * tq, tq)
    km = kmean_ref[...]
    km_hi = km.astype(BF16)
    km_lo = (km - km_hi.astype(F32)).astype(BF16)
    gate = (lax.dot_general(km_hi, q, _NT, preferred_element_type=F32)
            + lax.dot_general(km_lo, q, _NT, preferred_element_type=F32))
    blk_t = lax.broadcasted_iota(jnp.int32, (nbl, tq), 0)
    own_t = (q0 + lax.broadcasted_iota(jnp.int32, (nbl, tq), 1)) // MOBA_BLOCK
    parts = []
    for v in range(2):
        g_v = jnp.where(blk_t < own_t, gate[v * nbl:(v + 1) * nbl], -jnp.inf)
        sel = jnp.logical_or(_topk_mask(g_v, blk_t, MOBA_TOPK), blk_t == own_t)
        parts.append(jnp.where(sel, 0.0, NEG))
    parts.append(jnp.zeros((LANES - 2 * nbl, tq), F32))
    bias = jnp.concatenate(parts, axis=0).T
    qa_ref[0, :, 0:LANES] = q * (HEAD_DIM ** -0.5)
    qa_ref[0, :, LANES:2 * LANES] = bias.astype(BF16)
    lane = lax.broadcasted_iota(jnp.int32, (tq, LANES), 1)
    own = (q0 + lax.broadcasted_iota(jnp.int32, (tq, LANES), 0)) // MOBA_BLOCK
    onehot = jnp.where(jnp.logical_and(lane < 2 * nbl, lax.rem(lane, nbl) == own), 1.0, 0.0)
    ka_ref[0, :, 0:LANES] = k_ref[0, pl.ds(q0, tq), :]
    ka_ref[0, :, LANES:2 * LANES] = onehot.astype(BF16)


def _moba_gate(y3, *, q_blk, k_blk, pairs, tq=SELECT_TQ):
    b, s, _ = y3.shape
    assert s % MOBA_BLOCK == 0 and s // MOBA_BLOCK <= MOBA_BIAS_LANES and s % tq == 0
    out = jax.ShapeDtypeStruct((b * pairs, s, 2 * LANES), BF16)
    return pl.pallas_call(
        functools.partial(_moba_gate_body, tq=tq, seq=s), name="moba_gate",
        grid=(b * pairs, s // tq),
        in_specs=[pl.BlockSpec((1, tq, LANES), lambda i, j: (i // pairs, j, q_blk + i % pairs)),
                  pl.BlockSpec((1, s, LANES), lambda i, j: (i // pairs, 0, k_blk + i % pairs))],
        out_specs=[pl.BlockSpec((1, tq, 2 * LANES), lambda i, j: (i, j, 0)),
                   pl.BlockSpec((1, tq, 2 * LANES), lambda i, j: (i, j, 0))],
        out_shape=[out, out],
        scratch_shapes=[pltpu.VMEM((LANES, LANES), F32)],
        compiler_params=_cparams(("parallel", "arbitrary")),
    )(y3, y3)


def _nsa_compress_body(x_ref, pe_ref, w1_ref, w2_ref, c_ref, s1_ref, s2_ref, o_ref, *, rope):
    half = w1_ref.shape[0] // 2
    x = x_ref[0].astype(F32)
    lo = (x + pe_ref[:, 0:half]).astype(BF16)
    hi = (x + pe_ref[:, half:2 * half]).astype(BF16)
    a = jnp.dot(lo, w1_ref[0:half, :], preferred_element_type=F32)
    b = jnp.dot(hi, w1_ref[half:2 * half, :], preferred_element_type=F32)
    n = a.shape[0]
    h1 = a + pltpu.roll(b, n - 1, 0)
    y = jnp.dot(jax.nn.gelu(h1).astype(BF16), w2_ref[...], preferred_element_type=F32)
    if rope:
        y = _apply_rope(y, c_ref[0], s1_ref[0], s2_ref[0], ROPE_DIM // 2)
    o_ref[0] = y[:, 0:HEAD_DIM].astype(o_ref.dtype)


def _nsa_compress(xr, pe, w1, w2, tables, *, rope, groups):
    g, n, w = xr.shape
    hid = w1.shape[1]
    w2p = jnp.pad(w2, ((0, 0), (0, LANES - w2.shape[1])))
    tspec = pl.BlockSpec((1, n, LANES), lambda i: (i // groups, 0, 0))
    return pl.pallas_call(
        functools.partial(_nsa_compress_body, rope=rope), name="nsa_compress",
        grid=(g,),
        in_specs=[pl.BlockSpec((1, n, w), lambda i: (i, 0, 0)),
                  pl.BlockSpec((1, 2 * w), lambda i: (0, 0)),
                  pl.BlockSpec((2 * w, hid), lambda i: (0, 0)),
                  pl.BlockSpec((hid, LANES), lambda i: (0, 0)),
                  tspec, tspec, tspec],
        out_specs=pl.BlockSpec((1, n, HEAD_DIM), lambda i: (i, 0, 0)),
        out_shape=jax.ShapeDtypeStruct((g, n, HEAD_DIM), BF16),
        compiler_params=_cparams(("parallel",)),
    )(xr, pe.reshape(1, 2 * w).astype(F32), w1, w2p, *tables)


def _place(n_src, n_dst, shift, value=1.0):
    src = lax.broadcasted_iota(jnp.int32, (n_src, n_dst), 0)
    dst = lax.broadcasted_iota(jnp.int32, (n_src, n_dst), 1)
    return jnp.where(jnp.logical_and(src == dst + shift, dst < HEAD_DIM), value, 0.0).astype(BF16)


def _nsa_cmp_body(q_ref, kc_ref, vc_ref, oc_ref, mb_ref, kcp_ref, *, tq, rep):
    qi = pl.program_id(1)
    q0 = qi * tq
    nc = kc_ref.shape[1]
    width = q_ref.shape[-1]

    @pl.when(qi == 0)
    def _():
        for r in range(rep):
            src = lax.broadcasted_iota(jnp.int32, (HEAD_DIM, width), 0)
            dst = lax.broadcasted_iota(jnp.int32, (HEAD_DIM, width), 1)
            spread = jnp.where(dst == src + r * HEAD_DIM, HEAD_DIM ** -0.5, 0.0).astype(BF16)
            kcp_ref[r] = jnp.dot(kc_ref[0], spread, preferred_element_type=F32).astype(BF16)

    q = q_ref[0]

    def branches(nr):
        def run():
            vct = vc_ref[0, :, 0:nr]
            tpos = q0 + lax.broadcasted_iota(jnp.int32, (nr, tq), 1)
            cend = lax.broadcasted_iota(jnp.int32, (nr, tq), 0) * NSA_CMP_STRIDE + (NSA_CMP_LEN - 1)
            ok = cend <= tpos
            psum = jnp.zeros((nr, tq), F32)
            for r in range(rep):
                s = lax.dot_general(kcp_ref[r, 0:nr, :], q, _NT, preferred_element_type=F32)
                s = jnp.where(ok, s, NEG)
                m = jnp.max(s, axis=0, keepdims=True)
                e = jnp.where(ok, jnp.exp(s - m), 0.0)
                p = e * (1.0 / jnp.maximum(jnp.sum(e, axis=0, keepdims=True), 1e-30))
                o_t = jnp.dot(vct, p.astype(BF16), preferred_element_type=F32)
                oc_ref[0, :, r * HEAD_DIM:(r + 1) * HEAD_DIM] = o_t.T.astype(oc_ref.dtype)
                psum = psum + p
            nb = nr * NSA_CMP_STRIDE // NSA_SEL_BLOCK
            sstart = lax.broadcasted_iota(jnp.int32, (nb, nr), 0) * NSA_SEL_BLOCK
            cstart = lax.broadcasted_iota(jnp.int32, (nb, nr), 1) * NSA_CMP_STRIDE
            ov = jnp.where(jnp.logical_and(cstart < sstart + NSA_SEL_BLOCK, cstart + NSA_CMP_LEN > sstart),
                           1.0, 0.0).astype(BF16)
            p_hi = psum.astype(BF16)
            r1 = psum - p_hi.astype(F32)
            p_mid = r1.astype(BF16)
            p_lo = (r1 - p_mid.astype(F32)).astype(BF16)
            imp = (jnp.dot(ov, p_hi, preferred_element_type=F32) + jnp.dot(ov, p_mid, preferred_element_type=F32)
                   + jnp.dot(ov, p_lo, preferred_element_type=F32))
            blk = lax.broadcasted_iota(jnp.int32, (nb, tq), 0)
            qb = (q0 + lax.broadcasted_iota(jnp.int32, (nb, tq), 1)) // NSA_SEL_BLOCK
            forced = jnp.logical_or(blk == 0, jnp.logical_or(blk == qb, blk == qb - 1))
            imp = jnp.where(forced, imp + NSA_FORCE_BONUS, imp)
            imp = jnp.where(blk <= qb, imp, -jnp.inf)
            bias = jnp.where(_topk_mask(imp, blk, NSA_SEL_TOPK), 0.0, NEG)
            if nb < LANES:
                bias = jnp.concatenate([bias, jnp.full((LANES - nb, tq), NEG, F32)], axis=0)
            return bias
        return run

    sizes = list(range(LANES, nc + 1, LANES))
    needed = (q0 + tq) // NSA_CMP_STRIDE
    bias_t = lax.switch(jnp.minimum((needed + LANES - 1) // LANES, len(sizes)) - 1, [branches(nr) for nr in sizes])
    mb_ref[0] = bias_t.T.astype(mb_ref.dtype)


def _nsa_cmp(y3, kc, vc, *, tq=SELECT_TQ):
    b, s, _ = y3.shape
    g, nc, d = kc.shape
    rep, n = NSA_REP, NSA_GROUPS
    assert s // NSA_SEL_BLOCK <= LANES and s % tq == 0
    return pl.pallas_call(
        functools.partial(_nsa_cmp_body, tq=tq, rep=rep), name="nsa_cmp",
        grid=(g, s // tq),
        in_specs=[pl.BlockSpec((1, tq, rep * d), lambda i, j: (i // n, j, i % n)),
                  pl.BlockSpec((1, nc, d), lambda i, j: (i, 0, 0)),
                  pl.BlockSpec((1, d, nc), lambda i, j: (i, 0, 0))],
        out_specs=[pl.BlockSpec((1, tq, rep * d), lambda i, j: (i // n, j, i % n)),
                   pl.BlockSpec((1, tq, LANES), lambda i, j: (i, j, 0))],
        out_shape=[jax.ShapeDtypeStruct((b, s, n * rep * d), F32),
                   jax.ShapeDtypeStruct((g, s, LANES), BF16)],
        scratch_shapes=[pltpu.VMEM((rep, nc, rep * d), BF16)],
        compiler_params=_cparams(("parallel", "arbitrary")),
    )(y3, kc, vc)


def _nsa_combine_body(oc_ref, os_ref, ow_ref, g_ref, b_ref, e_ref, o_ref):
    gs = jax.nn.sigmoid(g_ref[...] + b_ref[...])
    g_hi = gs.astype(BF16)
    g_lo = (gs - g_hi.astype(F32)).astype(BF16)
    out = None
    for i, ref in enumerate((oc_ref, os_ref, ow_ref)):
        w = (jnp.dot(g_hi, e_ref[i], preferred_element_type=F32)
             + jnp.dot(g_lo, e_ref[i], preferred_element_type=F32))
        term = w * ref[...]
        out = term if out is None else out + term
    o_ref[...] = out.astype(o_ref.dtype)


def _nsa_combine(oc, osel, ow, yg, g_blk, gate_b, *, tm=ROW_TILE):
    t, n = oc.shape
    nh = NSA_HEADS * 3
    bp = jnp.pad(gate_b.reshape(1, nh).astype(F32), ((0, 0), (0, LANES - nh)))
    row = jnp.arange(LANES)[:, None]
    col = jnp.arange(n)[None, :] // HEAD_DIM
    expand = jnp.stack([(row == col * 3 + i) for i in range(3)]).astype(BF16)
    tok = pl.BlockSpec((tm, n), lambda i: (i, 0))
    return pl.pallas_call(
        _nsa_combine_body, name="nsa_combine", grid=(t // tm,),
        in_specs=[tok, tok, tok, pl.BlockSpec((tm, LANES), lambda i: (i, g_blk)),
                  pl.BlockSpec((1, LANES), lambda i: (0, 0)),
                  pl.BlockSpec((3, LANES, n), lambda i: (0, 0, 0))],
        out_specs=tok,
        out_shape=jax.ShapeDtypeStruct((t, n), BF16),
        compiler_params=_cparams(("parallel",)),
    )(oc, osel, ow, yg, bp, expand)


def _to_heads(x, b, s, n):
    return x.reshape(b, s, n, -1).transpose(0, 2, 1, 3).reshape(b * n, s, -1)


def _even_mixer(h, gain, layer_idx, rope16, w_in, lq1, lk1, lq2, lk2, subln, b, s):
    na = MOBA_HEADS * HEAD_DIM
    nb = DIFF_HEADS * 2 * HEAD_DIM
    w_rope = jnp.concatenate([w_in[:, 0:2 * na], w_in[:, 3 * na:3 * na + 2 * nb]], axis=1)
    w_rest = jnp.concatenate([w_in[:, 2 * na:3 * na], w_in[:, 3 * na + 2 * nb:]], axis=1)
    w = jnp.concatenate([w_rope, w_rest], axis=1).astype(BF16)
    y = _proj(h, w, gain, name="proj_even_in", rope=rope16, rope_cols=w_rope.shape[1], shift=ROPE_DIM // 2, out_dtype=BF16)
    y3 = y.reshape(b, s, -1)
    blk = lambda off: off // LANES
    pairs = na // LANES
    qa, ka = _moba_gate(y3, q_blk=blk(0), k_blk=blk(na), pairs=pairs)
    o_a = _pflash(qa, ka, y3, name="flash_moba", mode="moba", dk=2 * LANES, batch=b, pairs=pairs,
                  q_map=lambda i, j: (i, j, 0), k_map=lambda i, j: (i, 0, 0),
                  v_map=lambda i, j: (i // pairs, 0, blk(2 * na + 2 * nb) + i % pairs))
    lambda_init = 0.8 - 0.6 * math.exp(-0.3 * layer_idx)
    nh = DIFF_HEADS
    o_b = _pflash(y3, y3, y3, name="flash_diff", mode="diff", dk=LANES, batch=b, pairs=nh, scale=HEAD_DIM ** -0.5,
                  q_map=lambda i, j: (i // nh, j, blk(2 * na) + i % nh),
                  k_map=lambda i, j: (i // nh, 0, blk(2 * na + nb) + i % nh),
                  v_map=lambda i, j: (i // nh, 0, blk(3 * na + 2 * nb) + i % nh),
                  diff_params=(lq1, lk1, lq2, lk2, subln), lambda_init=lambda_init)
    t = b * s
    return o_a.reshape(t, -1), o_b.reshape(t, -1)


def _odd_mixer(h, gain, positions, rope16, rope32s, w_in, gate_b, pe_k, pe_v, k_w1, k_w2, v_w1, v_w2,
               q_norm, w_uq, kv_norm, w_ukv, b, s):
    G, R, d = NSA_GROUPS, NSA_REP, HEAD_DIM
    sizes = [NSA_HEADS * d] + [G * d] * 6 + [NSA_HEADS * 3, MLA_Q_RANK, MLA_KV_RANK, MLA_ROPE]
    offs = [0]
    for z in sizes:
        offs.append(offs[-1] + z)
    col = lambda i: w_in[:, offs[i]:offs[i + 1]]
    w_r = jnp.concatenate([col(0), col(3), col(5)], axis=1)
    w_p = jnp.concatenate([col(1), col(2), col(4), col(6), col(8), col(9), col(7)], axis=1)
    w_p = jnp.pad(w_p, ((0, 0), (0, (-w_p.shape[1]) % LANES)))
    w_pe = jnp.pad(col(10), ((0, 0), (MLA_NOPE, LANES - MLA_NOPE - MLA_ROPE)))
    yr, yp = _proj(h, jnp.concatenate([w_r, w_p, w_pe], axis=1).astype(BF16), gain, name="proj_odd_in", rope=rope16,
                   rope_cols=w_r.shape[1], shift=ROPE_DIM // 2, out_dtype=BF16, split=w_r.shape[1], out2_dtype=F32)
    k_cmp, v_cmp = yp[:, 0:G * d], yp[:, G * d:2 * G * d]
    cq_blk = 4 * G * d // MLA_Q_RANK
    ckv_blk = (4 * G * d + MLA_Q_RANK) // MLA_KV_RANK
    gate_blk = (4 * G * d + MLA_Q_RANK + MLA_KV_RANK) // LANES
    kpe_blk = w_p.shape[1] // LANES

    yr3, yp3 = yr.reshape(b, s, -1), yp.reshape(b, s, -1)
    nc = s // NSA_CMP_STRIDE
    cpos = jnp.concatenate([positions[:, NSA_CMP_LEN - 1::NSA_CMP_STRIDE], positions[:, -1:]], axis=1)
    ctab = [t.reshape(b, nc, LANES) for t in _rope_tables(cpos, ROPE_DIM, HEAD_DIM)]
    xk = _to_heads(k_cmp, b, s, G).reshape(b * G, nc, NSA_CMP_STRIDE * d)
    xv = _to_heads(v_cmp, b, s, G).reshape(b * G, nc, NSA_CMP_STRIDE * d)
    kc = _nsa_compress(xk, pe_k, k_w1.astype(BF16), k_w2.astype(BF16), ctab, rope=True, groups=G)
    vc = _nsa_compress(xv, pe_v, v_w1.astype(BF16), v_w2.astype(BF16), ctab, rope=False, groups=G)
    o_c, mbias = _nsa_cmp(yr3, kc, vc.transpose(0, 2, 1))
    k_blk = NSA_HEADS * d // LANES
    vsw = yp[:, 2 * G * d:4 * G * d].astype(BF16).reshape(b, s, -1)
    o_s = _gflash(yr3, yr3, vsw, mbias, name="flash_sel", q_blk=0, k_blk=k_blk, v_blk=0)
    o_w = _gflash(yr3, yr3, vsw, None, name="flash_win", q_blk=0, k_blk=k_blk + 1, v_blk=1, window=NSA_WINDOW)
    t = b * s
    o_nsa = _nsa_combine(o_c.reshape(t, -1), o_s.reshape(t, -1), o_w.reshape(t, -1), yp, gate_blk, gate_b)

    hq = MLA_NOPE + MLA_ROPE
    nh = MLA_HEADS
    wq = jnp.pad(w_uq.reshape(MLA_Q_RANK, nh, hq), ((0, 0), (0, 0), (0, LANES - hq))).reshape(MLA_Q_RANK, nh * LANES)
    q = _proj(yp, wq.astype(BF16), q_norm, name="proj_mla_q", x_blk=(MLA_Q_RANK, cq_blk), rope=rope32s,
              rope_cols=nh * LANES, shift=MLA_ROPE // 2, out_dtype=BF16)
    wkv = w_ukv.reshape(MLA_KV_RANK, nh, MLA_NOPE + MLA_V)
    wk = jnp.pad(wkv[:, :, :MLA_NOPE], ((0, 0), (0, 0), (0, LANES - MLA_NOPE))).reshape(MLA_KV_RANK, nh * LANES)
    wkv = jnp.concatenate([wk, wkv[:, :, MLA_NOPE:].reshape(MLA_KV_RANK, nh * MLA_V)], axis=1).astype(BF16)
    kv = _proj(yp, wkv, kv_norm, name="proj_mla_kv", x_blk=(MLA_KV_RANK, ckv_blk), rope=rope32s,
               shift=MLA_ROPE // 2, add=yp, add_blk=kpe_blk, add_cols=nh * LANES, rope_add=True,
               out_dtype=BF16)
    q3, kv3 = q.reshape(b, s, -1), kv.reshape(b, s, -1)
    pairs = nh // 2
    o_d = _pflash(q3, kv3, kv3, name="flash_mla", mode="slots", dk=2 * LANES, batch=b, pairs=pairs, scale=hq ** -0.5,
                  q_map=lambda i, j: (i // pairs, j, i % pairs), k_map=lambda i, j: (i // pairs, 0, i % pairs),
                  v_map=lambda i, j: (i // pairs, 0, nh + i % pairs))

    return o_nsa, o_d.reshape(t, -1)


def kernel(x, positions, attn_norm, ffn_norm, final_norm, ffn_w_gate, ffn_w_up, ffn_w_down, ev_w_in, ev_w_out, diff_lambda_q1, diff_lambda_k1, diff_lambda_q2, diff_lambda_k2, diff_subln, od_w_in, od_w_out, nsa_gate_b, nsa_pe_k, nsa_pe_v, nsa_k_w1, nsa_k_w2, nsa_v_w1, nsa_v_w2, mla_q_norm, mla_w_uq, mla_kv_norm, mla_w_ukv):
    b, s, d = x.shape
    depth = attn_norm.shape[0]
    rope16 = _rope_tables(positions, ROPE_DIM, HEAD_DIM)
    rope32s = _rope_tables(positions, MLA_ROPE, LANES, offset=MLA_NOPE)
    h = x.reshape(b * s, d)
    for l in range(depth):
        i = l // 2
        if l % 2 == 0:
            oa, ob = _even_mixer(h, attn_norm[l], l, rope16, ev_w_in[i], diff_lambda_q1[i],
                                 diff_lambda_k1[i], diff_lambda_q2[i], diff_lambda_k2[i], diff_subln[i], b, s)
            w_out = ev_w_out[i]
        else:
            oa, ob = _odd_mixer(h, attn_norm[l], positions, rope16, rope32s, od_w_in[i], nsa_gate_b[i],
                                nsa_pe_k[i], nsa_pe_v[i], nsa_k_w1[i], nsa_k_w2[i], nsa_v_w1[i], nsa_v_w2[i],
                                mla_q_norm[i], mla_w_uq[i], mla_kv_norm[i], mla_w_ukv[i], b, s)
            w_out = od_w_out[i]
        h = _ffn(h, oa, ob, w_out.astype(BF16), ffn_norm[l], ffn_w_gate[l].astype(BF16), ffn_w_up[l].astype(BF16),
                 ffn_w_down[l].astype(BF16), final_norm, final_norm=(l == depth - 1))
    return h.reshape(b, s, d)
```

```python
import functools
import math

import jax
import jax.numpy as jnp
from jax import lax
from jax.experimental import pallas as pl
from jax.experimental.pallas import tpu as pltpu

F32 = jnp.float32
BF16 = jnp.bfloat16

D_MODEL = 1024
HEAD_DIM = 64
ROPE_THETA = 500000.0
ROPE_DIM = HEAD_DIM // 4
NORM_EPS = 1e-5
D_FF = 2816

MOBA_HEADS = 8
MOBA_BLOCK = 256
MOBA_TOPK = 3
DIFF_HEADS = 4
DIFF_V = 2 * HEAD_DIM
NSA_HEADS = 8
NSA_GROUPS = 2
NSA_REP = NSA_HEADS // NSA_GROUPS
NSA_CMP_LEN = 32
NSA_CMP_STRIDE = 16
NSA_CMP_HIDDEN = 256
NSA_SEL_BLOCK = 64
NSA_SEL_TOPK = 16
NSA_WINDOW = 512
NSA_FORCE_BONUS = 1e3
MLA_HEADS = 8
MLA_Q_RANK = 256
MLA_KV_RANK = 128
MLA_NOPE = 64
MLA_ROPE = 32
MLA_V = 64

LANES = 128
MOBA_BIAS_LANES = 32
NEG = -1e30
M_INIT = -1e37
LOG2E = 1.4426950408889634
UNROLL = 8
VMEM_LIMIT = 56 * 1024 * 1024

ROW_TILE = 512
FLASH_COLS = 1024
FLASH_TK = 512
SELECT_TQ = 1024

_NT = (((1,), (1,)), ((), ()))


def _cparams(sem):
    return pltpu.CompilerParams(dimension_semantics=sem, vmem_limit_bytes=VMEM_LIMIT)


def _rope_tables(positions, dim, period, offset=0):
    r = dim // 2
    inv = 1.0 / (ROPE_THETA ** (jnp.arange(0, dim, 2, dtype=F32) / dim))
    ang = positions.astype(F32)[..., None] * inv
    cos, sin = jnp.cos(ang), jnp.sin(ang)
    const = lambda n, val: jnp.full(ang.shape[:-1] + (n,), val, F32)
    rest = period - offset - 2 * r
    zr = jnp.zeros_like(sin)
    c = jnp.concatenate([const(offset, 1.0), cos, cos, const(rest, 1.0)], -1)
    s1 = jnp.concatenate([const(offset, 0.0), -sin, zr, const(rest, 0.0)], -1)
    s2 = jnp.concatenate([const(offset, 0.0), zr, sin, const(rest, 0.0)], -1)
    reps = LANES // period
    tile = lambda t: jnp.tile(t, (1,) * (t.ndim - 1) + (reps,)).reshape(-1, LANES)
    return tile(c), tile(s1), tile(s2)


def _apply_rope(y, c, s1, s2, shift):
    return y * c + pltpu.roll(y, LANES - shift, 1) * s1 + pltpu.roll(y, shift, 1) * s2


def _proj_body(*refs, has_rope, add_cols, rope_add, rope_cols, shift, chunk, split):
    it = iter(refs)
    x_ref, g_ref, w_ref = next(it), next(it), next(it)
    if has_rope:
        c_ref, s1_ref, s2_ref = next(it), next(it), next(it)
    add_ref = next(it) if add_cols else None
    outs = list(it)
    n = w_ref.shape[1]
    xf = x_ref[...].astype(F32)
    y = xf * lax.rsqrt(jnp.mean(xf * xf, axis=-1, keepdims=True) + NORM_EPS)
    xb = (y * g_ref[...]).astype(BF16)
    if add_cols:
        add = add_ref[...].astype(F32)
        if rope_add:
            add = _apply_rope(add, c_ref[...], s1_ref[...], s2_ref[...], shift)
    spans = [(0, n, outs[0])] if len(outs) == 1 else [(0, split, outs[0]), (split, n, outs[1])]
    for lo, hi, o_ref in spans:
        for c0 in range(lo, hi, chunk):
            cw = min(chunk, hi - c0)
            y = jnp.dot(xb, w_ref[:, c0:c0 + cw], preferred_element_type=F32)
            if (has_rope and c0 < rope_cols) or c0 < add_cols:
                for k0 in range(0, cw, LANES):
                    ys = y[:, k0:k0 + LANES]
                    if has_rope and c0 + k0 < rope_cols:
                        ys = _apply_rope(ys, c_ref[...], s1_ref[...], s2_ref[...], shift)
                    if c0 + k0 < add_cols:
                        ys = ys + add
                    o_ref[:, c0 - lo + k0:c0 - lo + k0 + LANES] = ys.astype(o_ref.dtype)
            else:
                o_ref[:, c0 - lo:c0 - lo + cw] = y.astype(o_ref.dtype)


def _proj(x, w, gain, *, name, x_blk=None, rope=None, rope_cols=0, shift=0, add=None, add_blk=0, add_cols=0,
          rope_add=False, out_dtype=F32, split=None, out2_dtype=None, tm=ROW_TILE, chunk=512):
    t = x.shape[0]
    k, xj = (x.shape[1], 0) if x_blk is None else x_blk
    n = w.shape[1]
    assert t % tm == 0 and n % LANES == 0 and rope_cols % LANES == 0 and add_cols % LANES == 0
    assert split is None or split % LANES == 0
    has_rope = rope is not None
    args = [x, gain.reshape(1, k).astype(F32), w]
    specs = [pl.BlockSpec((tm, k), lambda i: (i, xj)), pl.BlockSpec((1, k), lambda i: (0, 0)),
             pl.BlockSpec((k, n), lambda i: (0, 0))]
    if has_rope:
        for tb in rope:
            args.append(tb)
            specs.append(pl.BlockSpec((tm, LANES), lambda i: (i, 0)))
    if add_cols:
        args.append(add)
        specs.append(pl.BlockSpec((tm, LANES), lambda i: (i, add_blk)))
    widths = [(n, out_dtype)] if split is None else [(split, out_dtype), (n - split, out2_dtype)]
    body = functools.partial(_proj_body, has_rope=has_rope, add_cols=add_cols, rope_add=rope_add,
                             rope_cols=rope_cols, shift=shift, chunk=chunk, split=split)
    out = pl.pallas_call(
        body, name=name, grid=(t // tm,), in_specs=specs,
        out_specs=[pl.BlockSpec((tm, wd), lambda i: (i, 0)) for wd, _ in widths],
        out_shape=[jax.ShapeDtypeStruct((t, wd), dt) for wd, dt in widths],
        compiler_params=_cparams(("parallel",)),
    )(*args)
    return out[0] if split is None else out


def _ffn_body(x_ref, oa_ref, ob_ref, wo_ref, g_ref, wg_ref, wu_ref, wd_ref, fg_ref, o_ref, *, final_norm):
    ka = oa_ref.shape[1]
    h = (x_ref[...] + jnp.dot(oa_ref[...], wo_ref[0:ka, :], preferred_element_type=F32)
         + jnp.dot(ob_ref[...], wo_ref[ka:, :], preferred_element_type=F32))
    y = h * lax.rsqrt(jnp.mean(h * h, axis=-1, keepdims=True) + NORM_EPS)
    xn = (y * g_ref[...]).astype(BF16)
    g = jnp.dot(xn, wg_ref[...], preferred_element_type=F32)
    u = jnp.dot(xn, wu_ref[...], preferred_element_type=F32)
    a = (jax.nn.silu(g) * u).astype(BF16)
    h = h + jnp.dot(a, wd_ref[...], preferred_element_type=F32)
    if final_norm:
        y = h * lax.rsqrt(jnp.mean(h * h, axis=-1, keepdims=True) + NORM_EPS)
        h = y * fg_ref[...]
    o_ref[...] = h


def _ffn(x, oa, ob, w_out, gain, wg, wu, wd, final_gain, *, final_norm, tm=ROW_TILE):
    t, d = x.shape
    f = wg.shape[1]
    assert t % tm == 0 and oa.shape[1] + ob.shape[1] == w_out.shape[0]
    once = pl.Buffered(1)
    return pl.pallas_call(
        functools.partial(_ffn_body, final_norm=final_norm), name="ffn",
        grid=(t // tm,),
        in_specs=[
            pl.BlockSpec((tm, d), lambda i: (i, 0)),
            pl.BlockSpec((tm, oa.shape[1]), lambda i: (i, 0)),
            pl.BlockSpec((tm, ob.shape[1]), lambda i: (i, 0)),
            pl.BlockSpec(w_out.shape, lambda i: (0, 0), pipeline_mode=once),
            pl.BlockSpec((1, d), lambda i: (0, 0)),
            pl.BlockSpec((d, f), lambda i: (0, 0), pipeline_mode=once),
            pl.BlockSpec((d, f), lambda i: (0, 0), pipeline_mode=once),
            pl.BlockSpec((f, d), lambda i: (0, 0), pipeline_mode=once),
            pl.BlockSpec((1, d), lambda i: (0, 0)),
        ],
        out_specs=pl.BlockSpec((tm, d), lambda i: (i, 0)),
        out_shape=jax.ShapeDtypeStruct((t, d), F32),
        compiler_params=_cparams(("parallel",)),
    )(x, oa, ob, w_out, gain.reshape(1, d).astype(F32), wg, wu, wd, final_gain.reshape(1, d).astype(F32))


def _sweep(q, k_get, vt_get, s_ref, *, q0, tq, tk, dv, scale):
    cols = q.shape[0]

    def fold8(x, op):
        return op(x.reshape(x.shape[0] // 8, 8, x.shape[1]), axis=0)

    def first_col(u, n, n_masked):
        d = u - (n - n_masked)
        return d * tk if (d > 0 and cols == tq) else 0

    def upd(full, c0, f):
        return f(full) if c0 == 0 else jnp.concatenate([full[:, :c0], f(full[:, c0:])], axis=1)

    def score_block(j0, n, mrun, n_masked):
        ks = pl.multiple_of(j0 * tk, tk)
        n_wide = sum(1 for u in range(n) if first_col(u, n, n_masked) == 0)
        s = lax.dot_general(k_get(ks, n_wide * tk), q, _NT, preferred_element_type=F32) * (scale * LOG2E)
        for u in range(n):
            c0 = first_col(u, n, n_masked)
            if c0 == 0:
                su = s[u * tk:(u + 1) * tk]
            else:
                ku = k_get(pl.multiple_of(ks + u * tk, tk), tk)
                su = lax.dot_general(ku, q[c0:], _NT, preferred_element_type=F32) * (scale * LOG2E)
            if u >= n - n_masked:
                kpos = ks + u * tk + lax.broadcasted_iota(jnp.int32, su.shape, 0)
                qpos = q0 + c0 + lax.rem(lax.broadcasted_iota(jnp.int32, su.shape, 1), tq)
                su = jnp.where(kpos <= qpos, su, NEG)
            s_ref[j0 + u, :, c0:cols] = su
            mrun = upd(mrun, c0, lambda t: jnp.maximum(t, fold8(su, jnp.max)))
        return mrun

    def pv_block(j0, n, carry, m, n_masked=0):
        l8, acc = carry
        for u in range(n):
            c0 = first_col(u, n, n_masked)
            p = jnp.exp2(s_ref[j0 + u, :, c0:cols] - m[:, c0:])
            l8 = upd(l8, c0, lambda t: t + fold8(p, jnp.sum))
            pv = jnp.dot(vt_get(j0 + u), p.astype(BF16), preferred_element_type=F32)
            acc = upd(acc, c0, lambda t: t + pv)
        return l8, acc

    m_init = jnp.full((8, cols), M_INIT, F32)
    acc_init = (jnp.zeros((8, cols), F32), jnp.zeros((dv, cols), F32))
    n_diag = max(tq // tk, 1)
    assert UNROLL % n_diag == 0
    n_full = q0 // tk
    groups = n_full // UNROLL
    rem = n_full - groups * UNROLL
    tail0 = groups * UNROLL
    tails = range(n_diag, UNROLL + n_diag, n_diag)

    mrun = lax.fori_loop(0, groups, lambda i, t: score_block(UNROLL * i, UNROLL, t, 0), m_init)
    mrun = lax.switch(rem // n_diag, [functools.partial(score_block, tail0, n, n_masked=n_diag) for n in tails],
                      mrun)
    m = jnp.max(mrun, axis=0, keepdims=True)
    carry = lax.fori_loop(0, groups, lambda i, t: pv_block(UNROLL * i, UNROLL, t, m), acc_init)
    l8, acc = lax.switch(rem // n_diag, [functools.partial(pv_block, tail0, n, m=m, n_masked=n_diag) for n in tails],
                         carry)
    l = jnp.sum(l8, axis=0, keepdims=True)
    return acc / jnp.maximum(l, 1e-30)


def _head_lanes(mode, v, lane):
    d = HEAD_DIM
    if mode == "diff":
        return jnp.logical_and(lane >= d * v, lane < d * (v + 1))
    if mode == "moba":
        lo = 2 * d + v * MOBA_BIAS_LANES
        return jnp.logical_or(jnp.logical_and(lane >= d * v, lane < d * (v + 1)),
                              jnp.logical_and(lane >= lo, lane < lo + MOBA_BIAS_LANES))
    return jnp.logical_and(lane >= LANES * v, lane < LANES * (v + 1))


def _pflash_body(*refs, mode, tq, tk, scale, lambda_init):
    if mode == "diff":
        q_ref, k_ref, v_ref, lq1_ref, lk1_ref, lq2_ref, lk2_ref, sg_ref, o_ref, s_ref, vt_ref = refs
    else:
        q_ref, k_ref, v_ref, o_ref, s_ref, vt_ref = refs
    qi = pl.program_id(1)
    n_chunks = v_ref.shape[1] // tk

    @pl.when(qi == 0)
    def _():
        for c in range(n_chunks):
            vt_ref[c] = v_ref[0, c * tk:(c + 1) * tk, :].astype(F32).T.astype(BF16)

    q = q_ref[0]
    lane = lax.broadcasted_iota(jnp.int32, q.shape, 1)
    d = HEAD_DIM
    outs = []
    for v in range(2):
        qv = jnp.where(_head_lanes(mode, v, lane), q, jnp.zeros_like(q))
        if mode == "diff":
            vt_get, dv = (lambda j: vt_ref[j]), LANES
        else:
            vt_get, dv = (lambda j, v=v: vt_ref[j, v * d:(v + 1) * d, :]), d
        outs.append(_sweep(qv, lambda ks, n: k_ref[0, pl.ds(ks, n), :], vt_get, s_ref,
                           q0=qi * tq, tq=tq, tk=tk, dv=dv, scale=scale))
    if mode == "diff":
        lam = (jnp.exp(jnp.sum(lq1_ref[...] * lk1_ref[...], axis=-1, keepdims=True))
               - jnp.exp(jnp.sum(lq2_ref[...] * lk2_ref[...], axis=-1, keepdims=True)) + lambda_init)
        d = outs[0] - lam * outs[1]
        y = d * lax.rsqrt(jnp.mean(d * d, axis=0, keepdims=True) + NORM_EPS)
        o_t = (y * sg_ref[...]) * (1.0 - lambda_init)
    else:
        o_t = jnp.concatenate(outs, axis=0)
    o_ref[0] = o_t.T.astype(o_ref.dtype)


def _pflash(q, k, v, *, name, mode, dk, q_map, k_map, v_map, batch, pairs, tq=FLASH_COLS, tk=FLASH_TK, scale=1.0,
            diff_params=None, lambda_init=0.0):
    s = v.shape[1]
    assert s % tk == 0 and tq % tk == 0 and s % tq == 0
    args = [q, k, v]
    in_specs = [pl.BlockSpec((1, tq, dk), q_map), pl.BlockSpec((1, s, dk), k_map),
                pl.BlockSpec((1, s, LANES), v_map)]
    if mode == "diff":
        lq1, lk1, lq2, lk2, subln = diff_params
        for a in (lq1, lk1, lq2, lk2):
            args.append(a.reshape(1, HEAD_DIM).astype(F32))
            in_specs.append(pl.BlockSpec((1, HEAD_DIM), lambda i, j: (0, 0)))
        args.append(subln.reshape(LANES, 1).astype(F32))
        in_specs.append(pl.BlockSpec((LANES, 1), lambda i, j: (0, 0)))
    body = functools.partial(_pflash_body, mode=mode, tq=tq, tk=tk, scale=scale, lambda_init=lambda_init)
    return pl.pallas_call(
        body, name=name, grid=(batch * pairs, s // tq), in_specs=in_specs,
        out_specs=pl.BlockSpec((1, tq, LANES), lambda i, j: (i // pairs, j, i % pairs)),
        out_shape=jax.ShapeDtypeStruct((batch, s, pairs * LANES), BF16),
        scratch_shapes=[pltpu.VMEM((s // tk, tk, tq), F32), pltpu.VMEM((s // tk, LANES, tk), BF16)],
        compiler_params=_cparams(("parallel", "arbitrary")),
    )(*args)


def _band_masks(tq, window):
    kl = jnp.arange(window + tq)[None, :, None]
    ql = jnp.arange(tq)[None, None, :]
    q0 = (jnp.arange(window // tq + 1) * tq)[:, None, None]
    k = jnp.maximum(q0 - window, 0) + kl
    t = q0 + ql
    return jnp.where(jnp.logical_and(k <= t, k > t - window), 0.0, NEG).astype(F32)


def _band(q, ka_ref, vt_ref, mask_ref, *, q0, tq, window):
    cols = q.shape[0]
    band = window + tq
    start = pl.multiple_of(jnp.maximum(q0 - window, 0), tq)
    s = lax.dot_general(ka_ref[pl.ds(start, band), :], q, _NT, preferred_element_type=F32) * LOG2E

    s = jnp.concatenate([s[:, c:c + tq] + mask_ref[0] for c in range(0, cols, tq)], axis=1)
    p = jnp.exp2(s - jnp.max(s, axis=0, keepdims=True))
    l = jnp.sum(p, axis=0, keepdims=True)
    pb = p.astype(BF16)
    acc = None
    for c in range(band // tq):
        part = jnp.dot(vt_ref[start // tq + c], pb[c * tq:(c + 1) * tq], preferred_element_type=F32)
        acc = part if acc is None else acc + part
    return acc / jnp.maximum(l, 1e-30)


def _gflash_body(*refs, tq, tk, rep, groups, window, select):
    if select:
        q_ref, k_ref, v_ref, mb_ref, o_ref, s_ref, vt_ref, ka_ref, qa_ref = refs
    else:
        q_ref, k_ref, v_ref, mask_ref, o_ref, vt_ref, ka_ref, qa_ref = refs
    d = HEAD_DIM
    tv = vt_ref.shape[2]
    g = pl.program_id(0) % groups
    qi = pl.program_id(1)
    n_chunks = v_ref.shape[1] // tk

    @pl.when(qi == 0)
    def _():
        move = _place(LANES, LANES, g * d)
        for c in range(n_chunks):
            rows = slice(c * tk, (c + 1) * tk)
            vt = v_ref[0, rows, :].astype(F32).T
            vt = jnp.where(g == 0, vt[0:d], vt[d:2 * d]).astype(BF16)
            for u in range(tk // tv):
                vt_ref[c * (tk // tv) + u] = vt[:, u * tv:(u + 1) * tv]
            kg = jnp.dot(k_ref[0, rows, :], move, preferred_element_type=F32).astype(BF16)
            if select:
                blk = (c * tk + lax.broadcasted_iota(jnp.int32, (tk, LANES), 0)) // NSA_SEL_BLOCK
                lane = lax.broadcasted_iota(jnp.int32, (tk, LANES), 1)
                ka_ref[rows, 0:LANES] = jnp.where(blk == lane, 1.0, 0.0).astype(BF16)
                ka_ref[rows, LANES:2 * LANES] = kg
            else:
                ka_ref[rows, :] = kg

    q = q_ref[0]
    for r in range(rep):
        qr = jnp.dot(q, _place(rep * d, LANES, r * d, d ** -0.5), preferred_element_type=F32).astype(BF16)
        rows = slice(r * tq, (r + 1) * tq)
        if select:
            qa_ref[rows, 0:LANES] = mb_ref[0]
            qa_ref[rows, LANES:2 * LANES] = qr
        else:
            qa_ref[rows, :] = qr
    if window is None:
        o_t = _sweep(qa_ref[...], lambda ks, n: ka_ref[pl.ds(ks, n), :], lambda j: vt_ref[j], s_ref,
                     q0=qi * tq, tq=tq, tk=tk, dv=d, scale=1.0)
    else:
        half = (rep // 2) * tq
        o_t = jnp.concatenate([_band(qa_ref[0:half, :], ka_ref, vt_ref, mask_ref, q0=qi * tq, tq=tq, window=window),
                               _band(qa_ref[half:, :], ka_ref, vt_ref, mask_ref, q0=qi * tq, tq=tq, window=window)],
                              axis=1)
    for r in range(rep):
        o_ref[0, :, r * d:(r + 1) * d] = o_t[:, r * tq:(r + 1) * tq].T.astype(o_ref.dtype)


def _gflash(yq, yk, yv, mbias, *, name, q_blk, k_blk, v_blk, tq=FLASH_COLS // NSA_REP, tk=FLASH_TK, window=None):
    b, s, _ = yq.shape
    rep, groups, d = NSA_REP, NSA_GROUPS, HEAD_DIM
    assert groups * d == LANES and s % tk == 0 and tk % tq == 0
    select = mbias is not None
    dk = 2 * LANES if select else LANES
    assert select == (window is None)
    assert window is None or (window % tq == 0 and s >= window + tq)
    tv = tk if window is None else tq
    scratch = [pltpu.VMEM((s // tv, d, tv), BF16), pltpu.VMEM((s, dk), BF16), pltpu.VMEM((rep * tq, dk), BF16)]
    if window is None:
        scratch.insert(0, pltpu.VMEM((s // tk, tk, rep * tq), F32))
    args = [yq, yk, yv]
    in_specs = [pl.BlockSpec((1, tq, rep * d), lambda i, j: (i // groups, j, q_blk + i % groups)),
                pl.BlockSpec((1, s, LANES), lambda i, j: (i // groups, 0, k_blk)),
                pl.BlockSpec((1, s, LANES), lambda i, j: (i // groups, 0, v_blk))]
    if select:
        args.append(mbias)
        in_specs.append(pl.BlockSpec((1, tq, LANES), lambda i, j: (i, j, 0)))
    else:
        args.append(_band_masks(tq, window))
        in_specs.append(pl.BlockSpec((1, window + tq, tq), lambda i, j: (jnp.minimum(j, window // tq), 0, 0)))
    body = functools.partial(_gflash_body, tq=tq, tk=tk, rep=rep, groups=groups, window=window, select=select)
    return pl.pallas_call(
        body, name=name, grid=(b * groups, s // tq), in_specs=in_specs,
        out_specs=pl.BlockSpec((1, tq, rep * d), lambda i, j: (i // groups, j, i % groups)),
        out_shape=jax.ShapeDtypeStruct((b, s, groups * rep * d), F32),
        scratch_shapes=scratch,
        compiler_params=_cparams(("parallel", "arbitrary")),
    )(*args)


def _topk_mask(score, row, k):
    sel = None
    for _ in range(k):
        mx = jnp.max(score, axis=0, keepdims=True)
        idx = jnp.min(jnp.where(score == mx, row, LANES), axis=0, keepdims=True)
        hit = row == idx
        pick = jnp.logical_and(hit, mx > -jnp.inf)
        sel = pick if sel is None else jnp.logical_or(sel, pick)
        score = jnp.where(hit, -jnp.inf, score)
    return sel


def _moba_gate_body(q_ref, k_ref, qa_ref, ka_ref, kmean_ref, *, tq, seq):
    qi = pl.program_id(1)
    nbl = MOBA_BIAS_LANES

    @pl.when(qi == 0)
    def _():
        rowblk = lax.rem(lax.broadcasted_iota(jnp.int32, (LANES, seq), 0), nbl)
        colblk = lax.broadcasted_iota(jnp.int32, (LANES, seq), 1) // MOBA_BLOCK
        ind = jnp.where(rowblk == colblk, 1.0, 0.0).astype(BF16)
        ksum = jnp.dot(ind, k_ref[0], preferred_element_type=F32)
        rhead = lax.broadcasted_iota(jnp.int32, (LANES, LANES), 0) // nbl
        lhead = lax.broadcasted_iota(jnp.int32, (LANES, LANES), 1) // HEAD_DIM
        kmean_ref[...] = jnp.where(rhead == lhead, ksum * (1.0 / MOBA_BLOCK), 0.0)

    q = q_ref[0]
    q0 = pl.multiple_of(qi * tq, tq)
    km = kmean_ref[...]
    km_hi = km.astype(BF16)
    km_lo = (km - km_hi.astype(F32)).astype(BF16)
    gate = (lax.dot_general(km_hi, q, _NT, preferred_element_type=F32)
            + lax.dot_general(km_lo, q, _NT, preferred_element_type=F32))
    blk_t = lax.broadcasted_iota(jnp.int32, (nbl, tq), 0)
    own_t = (q0 + lax.broadcasted_iota(jnp.int32, (nbl, tq), 1)) // MOBA_BLOCK
    parts = []
    for v in range(2):
        g_v = jnp.where(blk_t < own_t, gate[v * nbl:(v + 1) * nbl], -jnp.inf)
        sel = jnp.logical_or(_topk_mask(g_v, blk_t, MOBA_TOPK), blk_t == own_t)
        parts.append(jnp.where(sel, 0.0, NEG))
    parts.append(jnp.zeros((LANES - 2 * nbl, tq), F32))
    bias = jnp.concatenate(parts, axis=0).T
    qa_ref[0, :, 0:LANES] = q * (HEAD_DIM ** -0.5)
    qa_ref[0, :, LANES:2 * LANES] = bias.astype(BF16)
    lane = lax.broadcasted_iota(jnp.int32, (tq, LANES), 1)
    own = (q0 + lax.broadcasted_iota(jnp.int32, (tq, LANES), 0)) // MOBA_BLOCK
    onehot = jnp.where(jnp.logical_and(lane < 2 * nbl, lax.rem(lane, nbl) == own), 1.0, 0.0)
    ka_ref[0, :, 0:LANES] = k_ref[0, pl.ds(q0, tq), :]
    ka_ref[0, :, LANES:2 * LANES] = onehot.astype(BF16)


def _moba_gate(y3, *, q_blk, k_blk, pairs, tq=SELECT_TQ):
    b, s, _ = y3.shape
    assert s % MOBA_BLOCK == 0 and s // MOBA_BLOCK <= MOBA_BIAS_LANES and s % tq == 0
    out = jax.ShapeDtypeStruct((b * pairs, s, 2 * LANES), BF16)
    return pl.pallas_call(
        functools.partial(_moba_gate_body, tq=tq, seq=s), name="moba_gate",
        grid=(b * pairs, s // tq),
        in_specs=[pl.BlockSpec((1, tq, LANES), lambda i, j: (i // pairs, j, q_blk + i % pairs)),
                  pl.BlockSpec((1, s, LANES), lambda i, j: (i // pairs, 0, k_blk + i % pairs))],
        out_specs=[pl.BlockSpec((1, tq, 2 * LANES), lambda i, j: (i, j, 0)),
                   pl.BlockSpec((1, tq, 2 * LANES), lambda i, j: (i, j, 0))],
        out_shape=[out, out],
        scratch_shapes=[pltpu.VMEM((LANES, LANES), F32)],
        compiler_params=_cparams(("parallel", "arbitrary")),
    )(y3, y3)


def _nsa_compress_body(x_ref, pe_ref, w1_ref, w2_ref, c_ref, s1_ref, s2_ref, o_ref, *, rope):
    half = w1_ref.shape[0] // 2
    x = x_ref[0].astype(F32)
    lo = (x + pe_ref[:, 0:half]).astype(BF16)
    hi = (x + pe_ref[:, half:2 * half]).astype(BF16)
    a = jnp.dot(lo, w1_ref[0:half, :], preferred_element_type=F32)
    b = jnp.dot(hi, w1_ref[half:2 * half, :], preferred_element_type=F32)
    n = a.shape[0]
    h1 = a + pltpu.roll(b, n - 1, 0)
    y = jnp.dot(jax.nn.gelu(h1).astype(BF16), w2_ref[...], preferred_element_type=F32)
    if rope:
        y = _apply_rope(y, c_ref[0], s1_ref[0], s2_ref[0], ROPE_DIM // 2)
    o_ref[0] = y[:, 0:HEAD_DIM].astype(o_ref.dtype)


def _nsa_compress(xr, pe, w1, w2, tables, *, rope, groups):
    g, n, w = xr.shape
    hid = w1.shape[1]
    w2p = jnp.pad(w2, ((0, 0), (0, LANES - w2.shape[1])))
    tspec = pl.BlockSpec((1, n, LANES), lambda i: (i // groups, 0, 0))
    return pl.pallas_call(
        functools.partial(_nsa_compress_body, rope=rope), name="nsa_compress",
        grid=(g,),
        in_specs=[pl.BlockSpec((1, n, w), lambda i: (i, 0, 0)),
                  pl.BlockSpec((1, 2 * w), lambda i: (0, 0)),
                  pl.BlockSpec((2 * w, hid), lambda i: (0, 0)),
                  pl.BlockSpec((hid, LANES), lambda i: (0, 0)),
                  tspec, tspec, tspec],
        out_specs=pl.BlockSpec((1, n, HEAD_DIM), lambda i: (i, 0, 0)),
        out_shape=jax.ShapeDtypeStruct((g, n, HEAD_DIM), BF16),
        compiler_params=_cparams(("parallel",)),
    )(xr, pe.reshape(1, 2 * w).astype(F32), w1, w2p, *tables)


def _place(n_src, n_dst, shift, value=1.0):
    src = lax.broadcasted_iota(jnp.int32, (n_src, n_dst), 0)
    dst = lax.broadcasted_iota(jnp.int32, (n_src, n_dst), 1)
    return jnp.where(jnp.logical_and(src == dst + shift, dst < HEAD_DIM), value, 0.0).astype(BF16)


def _nsa_cmp_body(q_ref, kc_ref, vc_ref, oc_ref, mb_ref, kcp_ref, *, tq, rep):
    qi = pl.program_id(1)
    q0 = qi * tq
    nc = kc_ref.shape[1]
    width = q_ref.shape[-1]

    @pl.when(qi == 0)
    def _():
        for r in range(rep):
            src = lax.broadcasted_iota(jnp.int32, (HEAD_DIM, width), 0)
            dst = lax.broadcasted_iota(jnp.int32, (HEAD_DIM, width), 1)
            spread = jnp.where(dst == src + r * HEAD_DIM, HEAD_DIM ** -0.5, 0.0).astype(BF16)
            kcp_ref[r] = jnp.dot(kc_ref[0], spread, preferred_element_type=F32).astype(BF16)

    q = q_ref[0]

    def branches(nr):
        def run():
            vct = vc_ref[0, :, 0:nr]
            tpos = q0 + lax.broadcasted_iota(jnp.int32, (nr, tq), 1)
            cend = lax.broadcasted_iota(jnp.int32, (nr, tq), 0) * NSA_CMP_STRIDE + (NSA_CMP_LEN - 1)
            ok = cend <= tpos
            psum = jnp.zeros((nr, tq), F32)
            for r in range(rep):
                s = lax.dot_general(kcp_ref[r, 0:nr, :], q, _NT, preferred_element_type=F32)
                s = jnp.where(ok, s, NEG)
                m = jnp.max(s, axis=0, keepdims=True)
                e = jnp.where(ok, jnp.exp(s - m), 0.0)
                p = e * (1.0 / jnp.maximum(jnp.sum(e, axis=0, keepdims=True), 1e-30))
                o_t = jnp.dot(vct, p.astype(BF16), preferred_element_type=F32)
                oc_ref[0, :, r * HEAD_DIM:(r + 1) * HEAD_DIM] = o_t.T.astype(oc_ref.dtype)
                psum = psum + p
            nb = nr * NSA_CMP_STRIDE // NSA_SEL_BLOCK
            sstart = lax.broadcasted_iota(jnp.int32, (nb, nr), 0) * NSA_SEL_BLOCK
            cstart = lax.broadcasted_iota(jnp.int32, (nb, nr), 1) * NSA_CMP_STRIDE
            ov = jnp.where(jnp.logical_and(cstart < sstart + NSA_SEL_BLOCK, cstart + NSA_CMP_LEN > sstart),
                           1.0, 0.0).astype(BF16)
            p_hi = psum.astype(BF16)
            r1 = psum - p_hi.astype(F32)
            p_mid = r1.astype(BF16)
            p_lo = (r1 - p_mid.astype(F32)).astype(BF16)
            imp = (jnp.dot(ov, p_hi, preferred_element_type=F32) + jnp.dot(ov, p_mid, preferred_element_type=F32)
                   + jnp.dot(ov, p_lo, preferred_element_type=F32))
            blk = lax.broadcasted_iota(jnp.int32, (nb, tq), 0)
            qb = (q0 + lax.broadcasted_iota(jnp.int32, (nb, tq), 1)) // NSA_SEL_BLOCK
            forced = jnp.logical_or(blk == 0, jnp.logical_or(blk == qb, blk == qb - 1))
            imp = jnp.where(forced, imp + NSA_FORCE_BONUS, imp)
            imp = jnp.where(blk <= qb, imp, -jnp.inf)
            bias = jnp.where(_topk_mask(imp, blk, NSA_SEL_TOPK), 0.0, NEG)
            if nb < LANES:
                bias = jnp.concatenate([bias, jnp.full((LANES - nb, tq), NEG, F32)], axis=0)
            return bias
        return run

    sizes = list(range(LANES, nc + 1, LANES))
    needed = (q0 + tq) // NSA_CMP_STRIDE
    bias_t = lax.switch(jnp.minimum((needed + LANES - 1) // LANES, len(sizes)) - 1, [branches(nr) for nr in sizes])
    mb_ref[0] = bias_t.T.astype(mb_ref.dtype)


def _nsa_cmp(y3, kc, vc, *, tq=SELECT_TQ):
    b, s, _ = y3.shape
    g, nc, d = kc.shape
    rep, n = NSA_REP, NSA_GROUPS
    assert s // NSA_SEL_BLOCK <= LANES and s % tq == 0
    return pl.pallas_call(
        functools.partial(_nsa_cmp_body, tq=tq, rep=rep), name="nsa_cmp",
        grid=(g, s // tq),
        in_specs=[pl.BlockSpec((1, tq, rep * d), lambda i, j: (i // n, j, i % n)),
                  pl.BlockSpec((1, nc, d), lambda i, j: (i, 0, 0)),
                  pl.BlockSpec((1, d, nc), lambda i, j: (i, 0, 0))],
        out_specs=[pl.BlockSpec((1, tq, rep * d), lambda i, j: (i // n, j, i % n)),
                   pl.BlockSpec((1, tq, LANES), lambda i, j: (i, j, 0))],
        out_shape=[jax.ShapeDtypeStruct((b, s, n * rep * d), F32),
                   jax.ShapeDtypeStruct((g, s, LANES), BF16)],
        scratch_shapes=[pltpu.VMEM((rep, nc, rep * d), BF16)],
        compiler_params=_cparams(("parallel", "arbitrary")),
    )(y3, kc, vc)


def _nsa_combine_body(oc_ref, os_ref, ow_ref, g_ref, b_ref, e_ref, o_ref):
    gs = jax.nn.sigmoid(g_ref[...] + b_ref[...])
    g_hi = gs.astype(BF16)
    g_lo = (gs - g_hi.astype(F32)).astype(BF16)
    out = None
    for i, ref in enumerate((oc_ref, os_ref, ow_ref)):
        w = (jnp.dot(g_hi, e_ref[i], preferred_element_type=F32)
             + jnp.dot(g_lo, e_ref[i], preferred_element_type=F32))
        term = w * ref[...]
        out = term if out is None else out + term
    o_ref[...] = out.astype(o_ref.dtype)


def _nsa_combine(oc, osel, ow, yg, g_blk, gate_b, *, tm=ROW_TILE):
    t, n = oc.shape
    nh = NSA_HEADS * 3
    bp = jnp.pad(gate_b.reshape(1, nh).astype(F32), ((0, 0), (0, LANES - nh)))
    row = jnp.arange(LANES)[:, None]
    col = jnp.arange(n)[None, :] // HEAD_DIM
    expand = jnp.stack([(row == col * 3 + i) for i in range(3)]).astype(BF16)
    tok = pl.BlockSpec((tm, n), lambda i: (i, 0))
    return pl.pallas_call(
        _nsa_combine_body, name="nsa_combine", grid=(t // tm,),
        in_specs=[tok, tok, tok, pl.BlockSpec((tm, LANES), lambda i: (i, g_blk)),
                  pl.BlockSpec((1, LANES), lambda i: (0, 0)),
                  pl.BlockSpec((3, LANES, n), lambda i: (0, 0, 0))],
        out_specs=tok,
        out_shape=jax.ShapeDtypeStruct((t, n), BF16),
        compiler_params=_cparams(("parallel",)),
    )(oc, osel, ow, yg, bp, expand)


def _to_heads(x, b, s, n):
    return x.reshape(b, s, n, -1).transpose(0, 2, 1, 3).reshape(b * n, s, -1)


def _even_mixer(h, gain, layer_idx, rope16, w_in, lq1, lk1, lq2, lk2, subln, b, s):
    na = MOBA_HEADS * HEAD_DIM
    nb = DIFF_HEADS * 2 * HEAD_DIM
    w_rope = jnp.concatenate([w_in[:, 0:2 * na], w_in[:, 3 * na:3 * na + 2 * nb]], axis=1)
    w_rest = jnp.concatenate([w_in[:, 2 * na:3 * na], w_in[:, 3 * na + 2 * nb:]], axis=1)
    w = jnp.concatenate([w_rope, w_rest], axis=1).astype(BF16)
    y = _proj(h, w, gain, name="proj_even_in", rope=rope16, rope_cols=w_rope.shape[1], shift=ROPE_DIM // 2, out_dtype=BF16)
    y3 = y.reshape(b, s, -1)
    blk = lambda off: off // LANES
    pairs = na // LANES
    qa, ka = _moba_gate(y3, q_blk=blk(0), k_blk=blk(na), pairs=pairs)
    o_a = _pflash(qa, ka, y3, name="flash_moba", mode="moba", dk=2 * LANES, batch=b, pairs=pairs,
                  q_map=lambda i, j: (i, j, 0), k_map=lambda i, j: (i, 0, 0),
                  v_map=lambda i, j: (i // pairs, 0, blk(2 * na + 2 * nb) + i % pairs))
    lambda_init = 0.8 - 0.6 * math.exp(-0.3 * layer_idx)
    nh = DIFF_HEADS
    o_b = _pflash(y3, y3, y3, name="flash_diff", mode="diff", dk=LANES, batch=b, pairs=nh, scale=HEAD_DIM ** -0.5,
                  q_map=lambda i, j: (i // nh, j, blk(2 * na) + i % nh),
                  k_map=lambda i, j: (i // nh, 0, blk(2 * na + nb) + i % nh),
                  v_map=lambda i, j: (i // nh, 0, blk(3 * na + 2 * nb) + i % nh),
                  diff_params=(lq1, lk1, lq2, lk2, subln), lambda_init=lambda_init)
    t = b * s
    return o_a.reshape(t, -1), o_b.reshape(t, -1)


def _odd_mixer(h, gain, positions, rope16, rope32s, w_in, gate_b, pe_k, pe_v, k_w1, k_w2, v_w1, v_w2,
               q_norm, w_uq, kv_norm, w_ukv, b, s):
    G, R, d = NSA_GROUPS, NSA_REP, HEAD_DIM
    sizes = [NSA_HEADS * d] + [G * d] * 6 + [NSA_HEADS * 3, MLA_Q_RANK, MLA_KV_RANK, MLA_ROPE]
    offs = [0]
    for z in sizes:
        offs.append(offs[-1] + z)
    col = lambda i: w_in[:, offs[i]:offs[i + 1]]
    w_r = jnp.concatenate([col(0), col(3), col(5)], axis=1)
    w_p = jnp.concatenate([col(1), col(2), col(4), col(6), col(8), col(9), col(7)], axis=1)
    w_p = jnp.pad(w_p, ((0, 0), (0, (-w_p.shape[1]) % LANES)))
    w_pe = jnp.pad(col(10), ((0, 0), (MLA_NOPE, LANES - MLA_NOPE - MLA_ROPE)))
    yr, yp = _proj(h, jnp.concatenate([w_r, w_p, w_pe], axis=1).astype(BF16), gain, name="proj_odd_in", rope=rope16,
                   rope_cols=w_r.shape[1], shift=ROPE_DIM // 2, out_dtype=BF16, split=w_r.shape[1], out2_dtype=F32)
    k_cmp, v_cmp = yp[:, 0:G * d], yp[:, G * d:2 * G * d]
    cq_blk = 4 * G * d // MLA_Q_RANK
    ckv_blk = (4 * G * d + MLA_Q_RANK) // MLA_KV_RANK
    gate_blk = (4 * G * d + MLA_Q_RANK + MLA_KV_RANK) // LANES
    kpe_blk = w_p.shape[1] // LANES

    yr3, yp3 = yr.reshape(b, s, -1), yp.reshape(b, s, -1)
    nc = s // NSA_CMP_STRIDE
    cpos = jnp.concatenate([positions[:, NSA_CMP_LEN - 1::NSA_CMP_STRIDE], positions[:, -1:]], axis=1)
    ctab = [t.reshape(b, nc, LANES) for t in _rope_tables(cpos, ROPE_DIM, HEAD_DIM)]
    xk = _to_heads(k_cmp, b, s, G).reshape(b * G, nc, NSA_CMP_STRIDE * d)
    xv = _to_heads(v_cmp, b, s, G).reshape(b * G, nc, NSA_CMP_STRIDE * d)
    kc = _nsa_compress(xk, pe_k, k_w1.astype(BF16), k_w2.astype(BF16), ctab, rope=True, groups=G)
    vc = _nsa_compress(xv, pe_v, v_w1.astype(BF16), v_w2.astype(BF16), ctab, rope=False, groups=G)
    o_c, mbias = _nsa_cmp(yr3, kc, vc.transpose(0, 2, 1))
    k_blk = NSA_HEADS * d // LANES
    vsw = yp[:, 2 * G * d:4 * G * d].astype(BF16).reshape(b, s, -1)
    o_s = _gflash(yr3, yr3, vsw, mbias, name="flash_sel", q_blk=0, k_blk=k_blk, v_blk=0)
    o_w = _gflash(yr3, yr3, vsw, None, name="flash_win", q_blk=0, k_blk=k_blk + 1, v_blk=1, window=NSA_WINDOW)
    t = b * s
    o_nsa = _nsa_combine(o_c.reshape(t, -1), o_s.reshape(t, -1), o_w.reshape(t, -1), yp, gate_blk, gate_b)

    hq = MLA_NOPE + MLA_ROPE
    nh = MLA_HEADS
    wq = jnp.pad(w_uq.reshape(MLA_Q_RANK, nh, hq), ((0, 0), (0, 0), (0, LANES - hq))).reshape(MLA_Q_RANK, nh * LANES)
    q = _proj(yp, wq.astype(BF16), q_norm, name="proj_mla_q", x_blk=(MLA_Q_RANK, cq_blk), rope=rope32s,
              rope_cols=nh * LANES, shift=MLA_ROPE // 2, out_dtype=BF16)
    wkv = w_ukv.reshape(MLA_KV_RANK, nh, MLA_NOPE + MLA_V)
    wk = jnp.pad(wkv[:, :, :MLA_NOPE], ((0, 0), (0, 0), (0, LANES - MLA_NOPE))).reshape(MLA_KV_RANK, nh * LANES)
    wkv = jnp.concatenate([wk, wkv[:, :, MLA_NOPE:].reshape(MLA_KV_RANK, nh * MLA_V)], axis=1).astype(BF16)
    kv = _proj(yp, wkv, kv_norm, name="proj_mla_kv", x_blk=(MLA_KV_RANK, ckv_blk), rope=rope32s,
               shift=MLA_ROPE // 2, add=yp, add_blk=kpe_blk, add_cols=nh * LANES, rope_add=True,
               out_dtype=BF16)
    q3, kv3 = q.reshape(b, s, -1), kv.reshape(b, s, -1)
    pairs = nh // 2
    o_d = _pflash(q3, kv3, kv3, name="flash_mla", mode="slots", dk=2 * LANES, batch=b, pairs=pairs, scale=hq ** -0.5,
                  q_map=lambda i, j: (i // pairs, j, i % pairs), k_map=lambda i, j: (i // pairs, 0, i % pairs),
                  v_map=lambda i, j: (i // pairs, 0, nh + i % pairs))

    return o_nsa, o_d.reshape(t, -1)


def kernel(x, positions, attn_norm, ffn_norm, final_norm, ffn_w_gate, ffn_w_up, ffn_w_down, ev_w_in, ev_w_out, diff_lambda_q1, diff_lambda_k1, diff_lambda_q2, diff_lambda_k2, diff_subln, od_w_in, od_w_out, nsa_gate_b, nsa_pe_k, nsa_pe_v, nsa_k_w1, nsa_k_w2, nsa_v_w1, nsa_v_w2, mla_q_norm, mla_w_uq, mla_kv_norm, mla_w_ukv):
    b, s, d = x.shape
    depth = attn_norm.shape[0]
    rope16 = _rope_tables(positions, ROPE_DIM, HEAD_DIM)
    rope32s = _rope_tables(positions, MLA_ROPE, LANES, offset=MLA_NOPE)
    h = x.reshape(b * s, d)
    for l in range(depth):
        i = l // 2
        if l % 2 == 0:
            oa, ob = _even_mixer(h, attn_norm[l], l, rope16, ev_w_in[i], diff_lambda_q1[i],
                                 diff_lambda_k1[i], diff_lambda_q2[i], diff_lambda_k2[i], diff_subln[i], b, s)
            w_out = ev_w_out[i]
        else:
            oa, ob = _odd_mixer(h, attn_norm[l], positions, rope16, rope32s, od_w_in[i], nsa_gate_b[i],
                                nsa_pe_k[i], nsa_pe_v[i], nsa_k_w1[i], nsa_k_w2[i], nsa_v_w1[i], nsa_v_w2[i],
                                mla_q_norm[i], mla_w_uq[i], mla_kv_norm[i], mla_w_ukv[i], b, s)
            w_out = od_w_out[i]
        h = _ffn(h, oa, ob, w_out.astype(BF16), ffn_norm[l], ffn_w_gate[l].astype(BF16), ffn_w_up[l].astype(BF16),
                 ffn_w_down[l].astype(BF16), final_norm, final_norm=(l == depth - 1))
    return h.reshape(b, s, d)
```

```python
import functools
import math

import jax
import jax.numpy as jnp
from jax import lax
from jax.experimental import pallas as pl
from jax.experimental.pallas import tpu as pltpu

F32 = jnp.float32
BF16 = jnp.bfloat16

D_MODEL = 1024
HEAD_DIM = 64
ROPE_THETA = 500000.0
ROPE_DIM = HEAD_DIM // 4
NORM_EPS = 1e-5
D_FF = 2816

MOBA_HEADS = 8
MOBA_BLOCK = 256
MOBA_TOPK = 3
DIFF_HEADS = 4
DIFF_V = 2 * HEAD_DIM
NSA_HEADS = 8
NSA_GROUPS = 2
NSA_REP = NSA_HEADS // NSA_GROUPS
NSA_CMP_LEN = 32
NSA_CMP_STRIDE = 16
NSA_CMP_HIDDEN = 256
NSA_SEL_BLOCK = 64
NSA_SEL_TOPK = 16
NSA_WINDOW = 512
NSA_FORCE_BONUS = 1e3
MLA_HEADS = 8
MLA_Q_RANK = 256
MLA_KV_RANK = 128
MLA_NOPE = 64
MLA_ROPE = 32
MLA_V = 64

LANES = 128
MOBA_BIAS_LANES = 32
NEG = -1e30
M_INIT = -1e37
LOG2E = 1.4426950408889634
UNROLL = 8
VMEM_LIMIT = 56 * 1024 * 1024

ROW_TILE = 512
FLASH_COLS = 1024
FLASH_TK = 512
SELECT_TQ = 1024

_NT = (((1,), (1,)), ((), ()))


def _cparams(sem):
    return pltpu.CompilerParams(dimension_semantics=sem, vmem_limit_bytes=VMEM_LIMIT)


def _rope_tables(positions, dim, period, offset=0):
    r = dim // 2
    inv = 1.0 / (ROPE_THETA ** (jnp.arange(0, dim, 2, dtype=F32) / dim))
    ang = positions.astype(F32)[..., None] * inv
    cos, sin = jnp.cos(ang), jnp.sin(ang)
    const = lambda n, val: jnp.full(ang.shape[:-1] + (n,), val, F32)
    rest = period - offset - 2 * r
    zr = jnp.zeros_like(sin)
    c = jnp.concatenate([const(offset, 1.0), cos, cos, const(rest, 1.0)], -1)
    s1 = jnp.concatenate([const(offset, 0.0), -sin, zr, const(rest, 0.0)], -1)
    s2 = jnp.concatenate([const(offset, 0.0), zr, sin, const(rest, 0.0)], -1)
    reps = LANES // period
    tile = lambda t: jnp.tile(t, (1,) * (t.ndim - 1) + (reps,)).reshape(-1, LANES)
    return tile(c), tile(s1), tile(s2)


def _apply_rope(y, c, s1, s2, shift):
    return y * c + pltpu.roll(y, LANES - shift, 1) * s1 + pltpu.roll(y, shift, 1) * s2


def _proj_body(*refs, has_rope, add_cols, rope_add, rope_cols, shift, chunk, split):
    it = iter(refs)
    x_ref, g_ref, w_ref = next(it), next(it), next(it)
    if has_rope:
        c_ref, s1_ref, s2_ref = next(it), next(it), next(it)
    add_ref = next(it) if add_cols else None
    outs = list(it)
    n = w_ref.shape[1]
    xf = x_ref[...].astype(F32)
    y = xf * lax.rsqrt(jnp.mean(xf * xf, axis=-1, keepdims=True) + NORM_EPS)
    xb = (y * g_ref[...]).astype(BF16)
    if add_cols:
        add = add_ref[...].astype(F32)
        if rope_add:
            add = _apply_rope(add, c_ref[...], s1_ref[...], s2_ref[...], shift)
    spans = [(0, n, outs[0])] if len(outs) == 1 else [(0, split, outs[0]), (split, n, outs[1])]
    for lo, hi, o_ref in spans:
        for c0 in range(lo, hi, chunk):
            cw = min(chunk, hi - c0)
            y = jnp.dot(xb, w_ref[:, c0:c0 + cw], preferred_element_type=F32)
            if (has_rope and c0 < rope_cols) or c0 < add_cols:
                for k0 in range(0, cw, LANES):
                    ys = y[:, k0:k0 + LANES]
                    if has_rope and c0 + k0 < rope_cols:
                        ys = _apply_rope(ys, c_ref[...], s1_ref[...], s2_ref[...], shift)
                    if c0 + k0 < add_cols:
                        ys = ys + add
                    o_ref[:, c0 - lo + k0:c0 - lo + k0 + LANES] = ys.astype(o_ref.dtype)
            else:
                o_ref[:, c0 - lo:c0 - lo + cw] = y.astype(o_ref.dtype)


def _proj(x, w, gain, *, name, x_blk=None, rope=None, rope_cols=0, shift=0, add=None, add_blk=0, add_cols=0,
          rope_add=False, out_dtype=F32, split=None, out2_dtype=None, tm=ROW_TILE, chunk=512):
    t = x.shape[0]
    k, xj = (x.shape[1], 0) if x_blk is None else x_blk
    n = w.shape[1]
    assert t % tm == 0 and n % LANES == 0 and rope_cols % LANES == 0 and add_cols % LANES == 0
    assert split is None or split % LANES == 0
    has_rope = rope is not None
    args = [x, gain.reshape(1, k).astype(F32), w]
    specs = [pl.BlockSpec((tm, k), lambda i: (i, xj)), pl.BlockSpec((1, k), lambda i: (0, 0)),
             pl.BlockSpec((k, n), lambda i: (0, 0))]
    if has_rope:
        for tb in rope:
            args.append(tb)
            specs.append(pl.BlockSpec((tm, LANES), lambda i: (i, 0)))
    if add_cols:
        args.append(add)
        specs.append(pl.BlockSpec((tm, LANES), lambda i: (i, add_blk)))
    widths = [(n, out_dtype)] if split is None else [(split, out_dtype), (n - split, out2_dtype)]
    body = functools.partial(_proj_body, has_rope=has_rope, add_cols=add_cols, rope_add=rope_add,
                             rope_cols=rope_cols, shift=shift, chunk=chunk, split=split)
    out = pl.pallas_call(
        body, name=name, grid=(t // tm,), in_specs=specs,
        out_specs=[pl.BlockSpec((tm, wd), lambda i: (i, 0)) for wd, _ in widths],
        out_shape=[jax.ShapeDtypeStruct((t, wd), dt) for wd, dt in widths],
        compiler_params=_cparams(("parallel",)),
    )(*args)
    return out[0] if split is None else out


def _ffn_body(x_ref, oa_ref, ob_ref, wo_ref, g_ref, wg_ref, wu_ref, wd_ref, fg_ref, o_ref, *, final_norm):
    ka = oa_ref.shape[1]
    h = (x_ref[...] + jnp.dot(oa_ref[...], wo_ref[0:ka, :], preferred_element_type=F32)
         + jnp.dot(ob_ref[...], wo_ref[ka:, :], preferred_element_type=F32))
    y = h * lax.rsqrt(jnp.mean(h * h, axis=-1, keepdims=True) + NORM_EPS)
    xn = (y * g_ref[...]).astype(BF16)
    g = jnp.dot(xn, wg_ref[...], preferred_element_type=F32)
    u = jnp.dot(xn, wu_ref[...], preferred_element_type=F32)
    a = (jax.nn.silu(g) * u).astype(BF16)
    h = h + jnp.dot(a, wd_ref[...], preferred_element_type=F32)
    if final_norm:
        y = h * lax.rsqrt(jnp.mean(h * h, axis=-1, keepdims=True) + NORM_EPS)
        h = y * fg_ref[...]
    o_ref[...] = h


def _ffn(x, oa, ob, w_out, gain, wg, wu, wd, final_gain, *, final_norm, tm=ROW_TILE):
    t, d = x.shape
    f = wg.shape[1]
    assert t % tm == 0 and oa.shape[1] + ob.shape[1] == w_out.shape[0]
    once = pl.Buffered(1)
    return pl.pallas_call(
        functools.partial(_ffn_body, final_norm=final_norm), name="ffn",
        grid=(t // tm,),
        in_specs=[
            pl.BlockSpec((tm, d), lambda i: (i, 0)),
            pl.BlockSpec((tm, oa.shape[1]), lambda i: (i, 0)),
            pl.BlockSpec((tm, ob.shape[1]), lambda i: (i, 0)),
            pl.BlockSpec(w_out.shape, lambda i: (0, 0), pipeline_mode=once),
            pl.BlockSpec((1, d), lambda i: (0, 0)),
            pl.BlockSpec((d, f), lambda i: (0, 0), pipeline_mode=once),
            pl.BlockSpec((d, f), lambda i: (0, 0), pipeline_mode=once),
            pl.BlockSpec((f, d), lambda i: (0, 0), pipeline_mode=once),
            pl.BlockSpec((1, d), lambda i: (0, 0)),
        ],
        out_specs=pl.BlockSpec((tm, d), lambda i: (i, 0)),
        out_shape=jax.ShapeDtypeStruct((t, d), F32),
        compiler_params=_cparams(("parallel",)),
    )(x, oa, ob, w_out, gain.reshape(1, d).astype(F32), wg, wu, wd, final_gain.reshape(1, d).astype(F32))


def _diag_masks(tq, tk):
    w = min(tq, tk)
    o = jnp.arange(max(tk // tq, 1))[:, None, None]
    k = jnp.arange(tk)[None, :, None]
    i = jnp.arange(w)[None, None, :]
    return jnp.where(k <= i + o * tq, 0.0, NEG).astype(F32)


def _sweep(q, k_get, vt_get, s_ref, mask_ref, *, q0, tq, tk, dv, scale):
    cols = q.shape[0]

    def fold8(x, op):
        return op(x.reshape(x.shape[0] // 8, 8, x.shape[1]), axis=0)

    def first_col(u, n, n_masked):
        d = u - (n - n_masked)
        return d * tk if (d > 0 and cols == tq) else 0

    def upd(full, c0, f):
        return f(full) if c0 == 0 else jnp.concatenate([full[:, :c0], f(full[:, c0:])], axis=1)

    def score_block(j0, n, mrun, n_masked):
        ks = pl.multiple_of(j0 * tk, tk)
        n_wide = sum(1 for u in range(n) if first_col(u, n, n_masked) == 0)
        s = lax.dot_general(k_get(ks, n_wide * tk), q, _NT, preferred_element_type=F32) * (scale * LOG2E)
        for u in range(n):
            c0 = first_col(u, n, n_masked)
            if c0 == 0:
                su = s[u * tk:(u + 1) * tk]
            else:
                ku = k_get(pl.multiple_of(ks + u * tk, tk), tk)
                su = lax.dot_general(ku, q[c0:], _NT, preferred_element_type=F32) * (scale * LOG2E)
            if u >= n - n_masked:
                pat = mask_ref[0]
                w = pat.shape[1]
                if cols == tq:
                    su = su + pat if su.shape[1] == w else jnp.concatenate([su[:, :w] + pat, su[:, w:]], axis=1)
                else:
                    su = jnp.concatenate([su[:, c:c + w] + pat for c in range(0, cols, w)], axis=1)
            s_ref[j0 + u, :, c0:cols] = su
            mrun = upd(mrun, c0, lambda t: jnp.maximum(t, fold8(su, jnp.max)))
        return mrun

    def pv_block(j0, n, carry, m, n_masked=0):
        l8, acc = carry
        for u in range(n):
            c0 = first_col(u, n, n_masked)
            p = jnp.exp2(s_ref[j0 + u, :, c0:cols] - m[:, c0:])
            l8 = upd(l8, c0, lambda t: t + fold8(p, jnp.sum))
            pv = jnp.dot(vt_get(j0 + u), p.astype(BF16), preferred_element_type=F32)
            acc = upd(acc, c0, lambda t: t + pv)
        return l8, acc

    m_init = jnp.full((8, cols), M_INIT, F32)
    acc_init = (jnp.zeros((8, cols), F32), jnp.zeros((dv, cols), F32))
    n_diag = max(tq // tk, 1)
    assert UNROLL % n_diag == 0
    n_full = q0 // tk
    groups = n_full // UNROLL
    rem = n_full - groups * UNROLL
    tail0 = groups * UNROLL
    tails = range(n_diag, UNROLL + n_diag, n_diag)

    mrun = lax.fori_loop(0, groups, lambda i, t: score_block(UNROLL * i, UNROLL, t, 0), m_init)
    mrun = lax.switch(rem // n_diag, [functools.partial(score_block, tail0, n, n_masked=n_diag) for n in tails],
                      mrun)
    m = jnp.max(mrun, axis=0, keepdims=True)
    carry = lax.fori_loop(0, groups, lambda i, t: pv_block(UNROLL * i, UNROLL, t, m), acc_init)
    l8, acc = lax.switch(rem // n_diag, [functools.partial(pv_block, tail0, n, m=m, n_masked=n_diag) for n in tails],
                         carry)
    l = jnp.sum(l8, axis=0, keepdims=True)
    return acc / jnp.maximum(l, 1e-30)


def _head_lanes(mode, v, lane):
    d = HEAD_DIM
    if mode == "diff":
        return jnp.logical_and(lane >= d * v, lane < d * (v + 1))
    if mode == "moba":
        lo = 2 * d + v * MOBA_BIAS_LANES
        return jnp.logical_or(jnp.logical_and(lane >= d * v, lane < d * (v + 1)),
                              jnp.logical_and(lane >= lo, lane < lo + MOBA_BIAS_LANES))
    return jnp.logical_and(lane >= LANES * v, lane < LANES * (v + 1))


def _pflash_body(*refs, mode, tq, tk, scale, lambda_init):
    if mode == "diff":
        q_ref, k_ref, v_ref, mask_ref, lq1_ref, lk1_ref, lq2_ref, lk2_ref, sg_ref, o_ref, s_ref, vt_ref = refs
    else:
        q_ref, k_ref, v_ref, mask_ref, o_ref, s_ref, vt_ref = refs
    qi = pl.program_id(1)
    n_chunks = v_ref.shape[1] // tk

    @pl.when(qi == 0)
    def _():
        for c in range(n_chunks):
            vt_ref[c] = v_ref[0, c * tk:(c + 1) * tk, :].astype(F32).T.astype(BF16)

    q = q_ref[0]
    lane = lax.broadcasted_iota(jnp.int32, q.shape, 1)
    d = HEAD_DIM
    outs = []
    for v in range(2):
        qv = jnp.where(_head_lanes(mode, v, lane), q, jnp.zeros_like(q))
        if mode == "diff":
            vt_get, dv = (lambda j: vt_ref[j]), LANES
        else:
            vt_get, dv = (lambda j, v=v: vt_ref[j, v * d:(v + 1) * d, :]), d
        outs.append(_sweep(qv, lambda ks, n: k_ref[0, pl.ds(ks, n), :], vt_get, s_ref, mask_ref,
                           q0=qi * tq, tq=tq, tk=tk, dv=dv, scale=scale))
    if mode == "diff":
        lam = (jnp.exp(jnp.sum(lq1_ref[...] * lk1_ref[...], axis=-1, keepdims=True))
               - jnp.exp(jnp.sum(lq2_ref[...] * lk2_ref[...], axis=-1, keepdims=True)) + lambda_init)
        d = outs[0] - lam * outs[1]
        y = d * lax.rsqrt(jnp.mean(d * d, axis=0, keepdims=True) + NORM_EPS)
        o_t = (y * sg_ref[...]) * (1.0 - lambda_init)
    else:
        o_t = jnp.concatenate(outs, axis=0)
    o_ref[0] = o_t.T.astype(o_ref.dtype)


def _pflash(q, k, v, *, name, mode, dk, q_map, k_map, v_map, batch, pairs, tq=FLASH_COLS, tk=FLASH_TK, scale=1.0,
            diff_params=None, lambda_init=0.0):
    s = v.shape[1]
    assert s % tk == 0 and tq % tk == 0 and s % tq == 0
    args = [q, k, v, _diag_masks(tq, tk)]
    in_specs = [pl.BlockSpec((1, tq, dk), q_map), pl.BlockSpec((1, s, dk), k_map),
                pl.BlockSpec((1, s, LANES), v_map), pl.BlockSpec((1, tk, tk), lambda i, j: (0, 0, 0))]
    if mode == "diff":
        lq1, lk1, lq2, lk2, subln = diff_params
        for a in (lq1, lk1, lq2, lk2):
            args.append(a.reshape(1, HEAD_DIM).astype(F32))
            in_specs.append(pl.BlockSpec((1, HEAD_DIM), lambda i, j: (0, 0)))
        args.append(subln.reshape(LANES, 1).astype(F32))
        in_specs.append(pl.BlockSpec((LANES, 1), lambda i, j: (0, 0)))
    body = functools.partial(_pflash_body, mode=mode, tq=tq, tk=tk, scale=scale, lambda_init=lambda_init)
    return pl.pallas_call(
        body, name=name, grid=(batch * pairs, s // tq), in_specs=in_specs,
        out_specs=pl.BlockSpec((1, tq, LANES), lambda i, j: (i // pairs, j, i % pairs)),
        out_shape=jax.ShapeDtypeStruct((batch, s, pairs * LANES), BF16),
        scratch_shapes=[pltpu.VMEM((s // tk, tk, tq), F32), pltpu.VMEM((s // tk, LANES, tk), BF16)],
        compiler_params=_cparams(("parallel", "arbitrary")),
    )(*args)


def _band_masks(tq, window):
    kl = jnp.arange(window + tq)[None, :, None]
    ql = jnp.arange(tq)[None, None, :]
    q0 = (jnp.arange(window // tq + 1) * tq)[:, None, None]
    k = jnp.maximum(q0 - window, 0) + kl
    t = q0 + ql
    return jnp.where(jnp.logical_and(k <= t, k > t - window), 0.0, NEG).astype(F32)


def _band(q, ka_ref, vt_ref, mask_ref, *, q0, tq, window):
    cols = q.shape[0]
    band = window + tq
    start = pl.multiple_of(jnp.maximum(q0 - window, 0), tq)
    s = lax.dot_general(ka_ref[pl.ds(start, band), :], q, _NT, preferred_element_type=F32) * LOG2E

    s = jnp.concatenate([s[:, c:c + tq] + mask_ref[0] for c in range(0, cols, tq)], axis=1)
    p = jnp.exp2(s - jnp.max(s, axis=0, keepdims=True))
    l = jnp.sum(p, axis=0, keepdims=True)
    pb = p.astype(BF16)
    acc = None
    for c in range(band // tq):
        part = jnp.dot(vt_ref[start // tq + c], pb[c * tq:(c + 1) * tq], preferred_element_type=F32)
        acc = part if acc is None else acc + part
    return acc / jnp.maximum(l, 1e-30)


def _gflash_body(*refs, tq, tk, rep, groups, window, select):
    if select:
        q_ref, k_ref, v_ref, mb_ref, mask_ref, o_ref, s_ref, vt_ref, ka_ref, qa_ref = refs
    else:
        q_ref, k_ref, v_ref, mask_ref, o_ref, vt_ref, ka_ref, qa_ref = refs
    d = HEAD_DIM
    tv = vt_ref.shape[2]
    g = pl.program_id(0) % groups
    qi = pl.program_id(1)
    n_chunks = v_ref.shape[1] // tk

    @pl.when(qi == 0)
    def _():
        move = _place(LANES, LANES, g * d)
        for c in range(n_chunks):
            rows = slice(c * tk, (c + 1) * tk)
            vt = v_ref[0, rows, :].astype(F32).T
            vt = jnp.where(g == 0, vt[0:d], vt[d:2 * d]).astype(BF16)
            for u in range(tk // tv):
                vt_ref[c * (tk // tv) + u] = vt[:, u * tv:(u + 1) * tv]
            kg = jnp.dot(k_ref[0, rows, :], move, preferred_element_type=F32).astype(BF16)
            if select:
                blk = (c * tk + lax.broadcasted_iota(jnp.int32, (tk, LANES), 0)) // NSA_SEL_BLOCK
                lane = lax.broadcasted_iota(jnp.int32, (tk, LANES), 1)
                ka_ref[rows, 0:LANES] = jnp.where(blk == lane, 1.0, 0.0).astype(BF16)
                ka_ref[rows, LANES:2 * LANES] = kg
            else:
                ka_ref[rows, :] = kg

    q = q_ref[0]
    for r in range(rep):
        qr = jnp.dot(q, _place(rep * d, LANES, r * d, d ** -0.5), preferred_element_type=F32).astype(BF16)
        rows = slice(r * tq, (r + 1) * tq)
        if select:
            qa_ref[rows, 0:LANES] = mb_ref[0]
            qa_ref[rows, LANES:2 * LANES] = qr
        else:
            qa_ref[rows, :] = qr
    if window is None:
        o_t = _sweep(qa_ref[...], lambda ks, n: ka_ref[pl.ds(ks, n), :], lambda j: vt_ref[j], s_ref, mask_ref,
                     q0=qi * tq, tq=tq, tk=tk, dv=d, scale=1.0)
    else:
        half = (rep // 2) * tq
        o_t = jnp.concatenate([_band(qa_ref[0:half, :], ka_ref, vt_ref, mask_ref, q0=qi * tq, tq=tq, window=window),
                               _band(qa_ref[half:, :], ka_ref, vt_ref, mask_ref, q0=qi * tq, tq=tq, window=window)],
                              axis=1)
    for r in range(rep):
        o_ref[0, :, r * d:(r + 1) * d] = o_t[:, r * tq:(r + 1) * tq].T.astype(o_ref.dtype)


def _gflash(yq, yk, yv, mbias, *, name, q_blk, k_blk, v_blk, tq=FLASH_COLS // NSA_REP, tk=FLASH_TK, window=None):
    b, s, _ = yq.shape
    rep, groups, d = NSA_REP, NSA_GROUPS, HEAD_DIM
    assert groups * d == LANES and s % tk == 0 and tk % tq == 0
    select = mbias is not None
    dk = 2 * LANES if select else LANES
    assert select == (window is None)
    assert window is None or (window % tq == 0 and s >= window + tq)
    tv = tk if window is None else tq
    scratch = [pltpu.VMEM((s // tv, d, tv), BF16), pltpu.VMEM((s, dk), BF16), pltpu.VMEM((rep * tq, dk), BF16)]
    if window is None:
        scratch.insert(0, pltpu.VMEM((s // tk, tk, rep * tq), F32))
    args = [yq, yk, yv]
    in_specs = [pl.BlockSpec((1, tq, rep * d), lambda i, j: (i // groups, j, q_blk + i % groups)),
                pl.BlockSpec((1, s, LANES), lambda i, j: (i // groups, 0, k_blk)),
                pl.BlockSpec((1, s, LANES), lambda i, j: (i // groups, 0, v_blk))]
    if select:
        args += [mbias, _diag_masks(tq, tk)]
        in_specs += [pl.BlockSpec((1, tq, LANES), lambda i, j: (i, j, 0)),
                     pl.BlockSpec((1, tk, tq), lambda i, j: (j % (tk // tq), 0, 0))]
    else:
        args.append(_band_masks(tq, window))
        in_specs.append(pl.BlockSpec((1, window + tq, tq), lambda i, j: (jnp.minimum(j, window // tq), 0, 0)))
    body = functools.partial(_gflash_body, tq=tq, tk=tk, rep=rep, groups=groups, window=window, select=select)
    return pl.pallas_call(
        body, name=name, grid=(b * groups, s // tq), in_specs=in_specs,
        out_specs=pl.BlockSpec((1, tq, rep * d), lambda i, j: (i // groups, j, i % groups)),
        out_shape=jax.ShapeDtypeStruct((b, s, groups * rep * d), F32),
        scratch_shapes=scratch,
        compiler_params=_cparams(("parallel", "arbitrary")),
    )(*args)


def _topk_mask(score, row, k):
    sel = None
    for _ in range(k):
        mx = jnp.max(score, axis=0, keepdims=True)
        idx = jnp.min(jnp.where(score == mx, row, LANES), axis=0, keepdims=True)
        hit = row == idx
        pick = jnp.logical_and(hit, mx > -jnp.inf)
        sel = pick if sel is None else jnp.logical_or(sel, pick)
        score = jnp.where(hit, -jnp.inf, score)
    return sel


def _moba_gate_body(q_ref, k_ref, qa_ref, ka_ref, kmean_ref, *, tq, seq):
    qi = pl.program_id(1)
    nbl = MOBA_BIAS_LANES

    @pl.when(qi == 0)
    def _():
        rowblk = lax.rem(lax.broadcasted_iota(jnp.int32, (LANES, seq), 0), nbl)
        colblk = lax.broadcasted_iota(jnp.int32, (LANES, seq), 1) // MOBA_BLOCK
        ind = jnp.where(rowblk == colblk, 1.0, 0.0).astype(BF16)
        ksum = jnp.dot(ind, k_ref[0], preferred_element_type=F32)
        rhead = lax.broadcasted_iota(jnp.int32, (LANES, LANES), 0) // nbl
        lhead = lax.broadcasted_iota(jnp.int32, (LANES, LANES), 1) // HEAD_DIM
        kmean_ref[...] = jnp.where(rhead == lhead, ksum * (1.0 / MOBA_BLOCK), 0.0)

    q = q_ref[0]
    q0 = pl.multiple_of(qi * tq, tq)
    km = kmean_ref[...]
    km_hi = km.astype(BF16)
    km_lo = (km - km_hi.astype(F32)).astype(BF16)
    gate = (lax.dot_general(km_hi, q, _NT, preferred_element_type=F32)
            + lax.dot_general(km_lo, q, _NT, preferred_element_type=F32))
    blk_t = lax.broadcasted_iota(jnp.int32, (nbl, tq), 0)
    own_t = (q0 + lax.broadcasted_iota(jnp.int32, (nbl, tq), 1)) // MOBA_BLOCK
    parts = []
    for v in range(2):
        g_v = jnp.where(blk_t < own_t, gate[v * nbl:(v + 1) * nbl], -jnp.inf)
        sel = jnp.logical_or(_topk_mask(g_v, blk_t, MOBA_TOPK), blk_t == own_t)
        parts.append(jnp.where(sel, 0.0, NEG))
    parts.append(jnp.zeros((LANES - 2 * nbl, tq), F32))
    bias = jnp.concatenate(parts, axis=0).T
    qa_ref[0, :, 0:LANES] = q * (HEAD_DIM ** -0.5)
    qa_ref[0, :, LANES:2 * LANES] = bias.astype(BF16)
    lane = lax.broadcasted_iota(jnp.int32, (tq, LANES), 1)
    own = (q0 + lax.broadcasted_iota(jnp.int32, (tq, LANES), 0)) // MOBA_BLOCK
    onehot = jnp.where(jnp.logical_and(lane < 2 * nbl, lax.rem(lane, nbl) == own), 1.0, 0.0)
    ka_ref[0, :, 0:LANES] = k_ref[0, pl.ds(q0, tq), :]
    ka_ref[0, :, LANES:2 * LANES] = onehot.astype(BF16)


def _moba_gate(y3, *, q_blk, k_blk, pairs, tq=SELECT_TQ):
    b, s, _ = y3.shape
    assert s % MOBA_BLOCK == 0 and s // MOBA_BLOCK <= MOBA_BIAS_LANES and s % tq == 0
    out = jax.ShapeDtypeStruct((b * pairs, s, 2 * LANES), BF16)
    return pl.pallas_call(
        functools.partial(_moba_gate_body, tq=tq, seq=s), name="moba_gate",
        grid=(b * pairs, s // tq),
        in_specs=[pl.BlockSpec((1, tq, LANES), lambda i, j: (i // pairs, j, q_blk + i % pairs)),
                  pl.BlockSpec((1, s, LANES), lambda i, j: (i // pairs, 0, k_blk + i % pairs))],
        out_specs=[pl.BlockSpec((1, tq, 2 * LANES), lambda i, j: (i, j, 0)),
                   pl.BlockSpec((1, tq, 2 * LANES), lambda i, j: (i, j, 0))],
        out_shape=[out, out],
        scratch_shapes=[pltpu.VMEM((LANES, LANES), F32)],
        compiler_params=_cparams(("parallel", "arbitrary")),
    )(y3, y3)


def _nsa_compress_body(x_ref, pe_ref, w1_ref, w2_ref, c_ref, s1_ref, s2_ref, o_ref, *, rope):
    half = w1_ref.shape[0] // 2
    x = x_ref[0].astype(F32)
    lo = (x + pe_ref[:, 0:half]).astype(BF16)
    hi = (x + pe_ref[:, half:2 * half]).astype(BF16)
    a = jnp.dot(lo, w1_ref[0:half, :], preferred_element_type=F32)
    b = jnp.dot(hi, w1_ref[half:2 * half, :], preferred_element_type=F32)
    n = a.shape[0]
    h1 = a + pltpu.roll(b, n - 1, 0)
    y = jnp.dot(jax.nn.gelu(h1).astype(BF16), w2_ref[...], preferred_element_type=F32)
    if rope:
        y = _apply_rope(y, c_ref[0], s1_ref[0], s2_ref[0], ROPE_DIM // 2)
    o_ref[0] = y[:, 0:HEAD_DIM].astype(o_ref.dtype)


def _nsa_compress(xr, pe, w1, w2, tables, *, rope, groups):
    g, n, w = xr.shape
    hid = w1.shape[1]
    w2p = jnp.pad(w2, ((0, 0), (0, LANES - w2.shape[1])))
    tspec = pl.BlockSpec((1, n, LANES), lambda i: (i // groups, 0, 0))
    return pl.pallas_call(
        functools.partial(_nsa_compress_body, rope=rope), name="nsa_compress",
        grid=(g,),
        in_specs=[pl.BlockSpec((1, n, w), lambda i: (i, 0, 0)),
                  pl.BlockSpec((1, 2 * w), lambda i: (0, 0)),
                  pl.BlockSpec((2 * w, hid), lambda i: (0, 0)),
                  pl.BlockSpec((hid, LANES), lambda i: (0, 0)),
                  tspec, tspec, tspec],
        out_specs=pl.BlockSpec((1, n, HEAD_DIM), lambda i: (i, 0, 0)),
        out_shape=jax.ShapeDtypeStruct((g, n, HEAD_DIM), BF16),
        compiler_params=_cparams(("parallel",)),
    )(xr, pe.reshape(1, 2 * w).astype(F32), w1, w2p, *tables)


def _place(n_src, n_dst, shift, value=1.0):
    src = lax.broadcasted_iota(jnp.int32, (n_src, n_dst), 0)
    dst = lax.broadcasted_iota(jnp.int32, (n_src, n_dst), 1)
    return jnp.where(jnp.logical_and(src == dst + shift, dst < HEAD_DIM), value, 0.0).astype(BF16)


def _nsa_cmp_body(q_ref, kc_ref, vc_ref, oc_ref, mb_ref, kcp_ref, *, tq, rep):
    qi = pl.program_id(1)
    q0 = qi * tq
    nc = kc_ref.shape[1]
    width = q_ref.shape[-1]

    @pl.when(qi == 0)
    def _():
        for r in range(rep):
            src = lax.broadcasted_iota(jnp.int32, (HEAD_DIM, width), 0)
            dst = lax.broadcasted_iota(jnp.int32, (HEAD_DIM, width), 1)
            spread = jnp.where(dst == src + r * HEAD_DIM, HEAD_DIM ** -0.5, 0.0).astype(BF16)
            kcp_ref[r] = jnp.dot(kc_ref[0], spread, preferred_element_type=F32).astype(BF16)

    q = q_ref[0]

    def branches(nr):
        def run():
            vct = vc_ref[0, :, 0:nr]
            tpos = q0 + lax.broadcasted_iota(jnp.int32, (nr, tq), 1)
            cend = lax.broadcasted_iota(jnp.int32, (nr, tq), 0) * NSA_CMP_STRIDE + (NSA_CMP_LEN - 1)
            ok = cend <= tpos
            psum = jnp.zeros((nr, tq), F32)
            for r in range(rep):
                s = lax.dot_general(kcp_ref[r, 0:nr, :], q, _NT, preferred_element_type=F32)
                s = jnp.where(ok, s, NEG)
                m = jnp.max(s, axis=0, keepdims=True)
                e = jnp.where(ok, jnp.exp(s - m), 0.0)
                p = e * (1.0 / jnp.maximum(jnp.sum(e, axis=0, keepdims=True), 1e-30))
                o_t = jnp.dot(vct, p.astype(BF16), preferred_element_type=F32)
                oc_ref[0, :, r * HEAD_DIM:(r + 1) * HEAD_DIM] = o_t.T.astype(oc_ref.dtype)
                psum = psum + p
            nb = nr * NSA_CMP_STRIDE // NSA_SEL_BLOCK
            sstart = lax.broadcasted_iota(jnp.int32, (nb, nr), 0) * NSA_SEL_BLOCK
            cstart = lax.broadcasted_iota(jnp.int32, (nb, nr), 1) * NSA_CMP_STRIDE
            ov = jnp.where(jnp.logical_and(cstart < sstart + NSA_SEL_BLOCK, cstart + NSA_CMP_LEN > sstart),
                           1.0, 0.0).astype(BF16)
            p_hi = psum.astype(BF16)
            r1 = psum - p_hi.astype(F32)
            p_mid = r1.astype(BF16)
            p_lo = (r1 - p_mid.astype(F32)).astype(BF16)
            imp = (jnp.dot(ov, p_hi, preferred_element_type=F32) + jnp.dot(ov, p_mid, preferred_element_type=F32)
                   + jnp.dot(ov, p_lo, preferred_element_type=F32))
            blk = lax.broadcasted_iota(jnp.int32, (nb, tq), 0)
            qb = (q0 + lax.broadcasted_iota(jnp.int32, (nb, tq), 1)) // NSA_SEL_BLOCK
            forced = jnp.logical_or(blk == 0, jnp.logical_or(blk == qb, blk == qb - 1))
            imp = jnp.where(forced, imp + NSA_FORCE_BONUS, imp)
            imp = jnp.where(blk <= qb, imp, -jnp.inf)
            bias = jnp.where(_topk_mask(imp, blk, NSA_SEL_TOPK), 0.0, NEG)
            if nb < LANES:
                bias = jnp.concatenate([bias, jnp.full((LANES - nb, tq), NEG, F32)], axis=0)
            return bias
        return run

    sizes = list(range(LANES, nc + 1, LANES))
    needed = (q0 + tq) // NSA_CMP_STRIDE
    bias_t = lax.switch(jnp.minimum((needed + LANES - 1) // LANES, len(sizes)) - 1, [branches(nr) for nr in sizes])
    mb_ref[0] = bias_t.T.astype(mb_ref.dtype)


def _nsa_cmp(y3, kc, vc, *, tq=SELECT_TQ):
    b, s, _ = y3.shape
    g, nc, d = kc.shape
    rep, n = NSA_REP, NSA_GROUPS
    assert s // NSA_SEL_BLOCK <= LANES and s % tq == 0
    return pl.pallas_call(
        functools.partial(_nsa_cmp_body, tq=tq, rep=rep), name="nsa_cmp",
        grid=(g, s // tq),
        in_specs=[pl.BlockSpec((1, tq, rep * d), lambda i, j: (i // n, j, i % n)),
                  pl.BlockSpec((1, nc, d), lambda i, j: (i, 0, 0)),
                  pl.BlockSpec((1, d, nc), lambda i, j: (i, 0, 0))],
        out_specs=[pl.BlockSpec((1, tq, rep * d), lambda i, j: (i // n, j, i % n)),
                   pl.BlockSpec((1, tq, LANES), lambda i, j: (i, j, 0))],
        out_shape=[jax.ShapeDtypeStruct((b, s, n * rep * d), F32),
                   jax.ShapeDtypeStruct((g, s, LANES), BF16)],
        scratch_shapes=[pltpu.VMEM((rep, nc, rep * d), BF16)],
        compiler_params=_cparams(("parallel", "arbitrary")),
    )(y3, kc, vc)


def _nsa_combine_body(oc_ref, os_ref, ow_ref, g_ref, b_ref, e_ref, o_ref):
    gs = jax.nn.sigmoid(g_ref[...] + b_ref[...])
    g_hi = gs.astype(BF16)
    g_lo = (gs - g_hi.astype(F32)).astype(BF16)
    out = None
    for i, ref in enumerate((oc_ref, os_ref, ow_ref)):
        w = (jnp.dot(g_hi, e_ref[i], preferred_element_type=F32)
             + jnp.dot(g_lo, e_ref[i], preferred_element_type=F32))
        term = w * ref[...]
        out = term if out is None else out + term
    o_ref[...] = out.astype(o_ref.dtype)


def _nsa_combine(oc, osel, ow, yg, g_blk, gate_b, *, tm=ROW_TILE):
    t, n = oc.shape
    nh = NSA_HEADS * 3
    bp = jnp.pad(gate_b.reshape(1, nh).astype(F32), ((0, 0), (0, LANES - nh)))
    row = jnp.arange(LANES)[:, None]
    col = jnp.arange(n)[None, :] // HEAD_DIM
    expand = jnp.stack([(row == col * 3 + i) for i in range(3)]).astype(BF16)
    tok = pl.BlockSpec((tm, n), lambda i: (i, 0))
    return pl.pallas_call(
        _nsa_combine_body, name="nsa_combine", grid=(t // tm,),
        in_specs=[tok, tok, tok, pl.BlockSpec((tm, LANES), lambda i: (i, g_blk)),
                  pl.BlockSpec((1, LANES), lambda i: (0, 0)),
                  pl.BlockSpec((3, LANES, n), lambda i: (0, 0, 0))],
        out_specs=tok,
        out_shape=jax.ShapeDtypeStruct((t, n), BF16),
        compiler_params=_cparams(("parallel",)),
    )(oc, osel, ow, yg, bp, expand)


def _to_heads(x, b, s, n):
    return x.reshape(b, s, n, -1).transpose(0, 2, 1, 3).reshape(b * n, s, -1)


def _even_mixer(h, gain, layer_idx, rope16, w_in, lq1, lk1, lq2, lk2, subln, b, s):
    na = MOBA_HEADS * HEAD_DIM
    nb = DIFF_HEADS * 2 * HEAD_DIM
    w_rope = jnp.concatenate([w_in[:, 0:2 * na], w_in[:, 3 * na:3 * na + 2 * nb]], axis=1)
    w_rest = jnp.concatenate([w_in[:, 2 * na:3 * na], w_in[:, 3 * na + 2 * nb:]], axis=1)
    w = jnp.concatenate([w_rope, w_rest], axis=1).astype(BF16)
    y = _proj(h, w, gain, name="proj_even_in", rope=rope16, rope_cols=w_rope.shape[1], shift=ROPE_DIM // 2, out_dtype=BF16)
    y3 = y.reshape(b, s, -1)
    blk = lambda off: off // LANES
    pairs = na // LANES
    qa, ka = _moba_gate(y3, q_blk=blk(0), k_blk=blk(na), pairs=pairs)
    o_a = _pflash(qa, ka, y3, name="flash_moba", mode="moba", dk=2 * LANES, batch=b, pairs=pairs,
                  q_map=lambda i, j: (i, j, 0), k_map=lambda i, j: (i, 0, 0),
                  v_map=lambda i, j: (i // pairs, 0, blk(2 * na + 2 * nb) + i % pairs))
    lambda_init = 0.8 - 0.6 * math.exp(-0.3 * layer_idx)
    nh = DIFF_HEADS
    o_b = _pflash(y3, y3, y3, name="flash_diff", mode="diff", dk=LANES, batch=b, pairs=nh, scale=HEAD_DIM ** -0.5,
                  q_map=lambda i, j: (i // nh, j, blk(2 * na) + i % nh),
                  k_map=lambda i, j: (i // nh, 0, blk(2 * na + nb) + i % nh),
                  v_map=lambda i, j: (i // nh, 0, blk(3 * na + 2 * nb) + i % nh),
                  diff_params=(lq1, lk1, lq2, lk2, subln), lambda_init=lambda_init)
    t = b * s
    return o_a.reshape(t, -1), o_b.reshape(t, -1)


def _odd_mixer(h, gain, positions, rope16, rope32s, w_in, gate_b, pe_k, pe_v, k_w1, k_w2, v_w1, v_w2,
               q_norm, w_uq, kv_norm, w_ukv, b, s):
    G, R, d = NSA_GROUPS, NSA_REP, HEAD_DIM
    sizes = [NSA_HEADS * d] + [G * d] * 6 + [NSA_HEADS * 3, MLA_Q_RANK, MLA_KV_RANK, MLA_ROPE]
    offs = [0]
    for z in sizes:
        offs.append(offs[-1] + z)
    col = lambda i: w_in[:, offs[i]:offs[i + 1]]
    w_r = jnp.concatenate([col(0), col(3), col(5)], axis=1)
    w_p = jnp.concatenate([col(1), col(2), col(4), col(6), col(8), col(9), col(7)], axis=1)
    w_p = jnp.pad(w_p, ((0, 0), (0, (-w_p.shape[1]) % LANES)))
    w_pe = jnp.pad(col(10), ((0, 0), (MLA_NOPE, LANES - MLA_NOPE - MLA_ROPE)))
    yr, yp = _proj(h, jnp.concatenate([w_r, w_p, w_pe], axis=1).astype(BF16), gain, name="proj_odd_in", rope=rope16,
                   rope_cols=w_r.shape[1], shift=ROPE_DIM // 2, out_dtype=BF16, split=w_r.shape[1], out2_dtype=F32)
    k_cmp, v_cmp = yp[:, 0:G * d], yp[:, G * d:2 * G * d]
    cq_blk = 4 * G * d // MLA_Q_RANK
    ckv_blk = (4 * G * d + MLA_Q_RANK) // MLA_KV_RANK
    gate_blk = (4 * G * d + MLA_Q_RANK + MLA_KV_RANK) // LANES
    kpe_blk = w_p.shape[1] // LANES

    yr3, yp3 = yr.reshape(b, s, -1), yp.reshape(b, s, -1)
    nc = s // NSA_CMP_STRIDE
    cpos = jnp.concatenate([positions[:, NSA_CMP_LEN - 1::NSA_CMP_STRIDE], positions[:, -1:]], axis=1)
    ctab = [t.reshape(b, nc, LANES) for t in _rope_tables(cpos, ROPE_DIM, HEAD_DIM)]
    xk = _to_heads(k_cmp, b, s, G).reshape(b * G, nc, NSA_CMP_STRIDE * d)
    xv = _to_heads(v_cmp, b, s, G).reshape(b * G, nc, NSA_CMP_STRIDE * d)
    kc = _nsa_compress(xk, pe_k, k_w1.astype(BF16), k_w2.astype(BF16), ctab, rope=True, groups=G)
    vc = _nsa_compress(xv, pe_v, v_w1.astype(BF16), v_w2.astype(BF16), ctab, rope=False, groups=G)
    o_c, mbias = _nsa_cmp(yr3, kc, vc.transpose(0, 2, 1))
    k_blk = NSA_HEADS * d // LANES
    vsw = yp[:, 2 * G * d:4 * G * d].astype(BF16).reshape(b, s, -1)
    o_s = _gflash(yr3, yr3, vsw, mbias, name="flash_sel", q_blk=0, k_blk=k_blk, v_blk=0)
    o_w = _gflash(yr3, yr3, vsw, None, name="flash_win", q_blk=0, k_blk=k_blk + 1, v_blk=1, window=NSA_WINDOW)
    t = b * s
    o_nsa = _nsa_combine(o_c.reshape(t, -1), o_s.reshape(t, -1), o_w.reshape(t, -1), yp, gate_blk, gate_b)

    hq = MLA_NOPE + MLA_ROPE
    nh = MLA_HEADS
    wq = jnp.pad(w_uq.reshape(MLA_Q_RANK, nh, hq), ((0, 0), (0, 0), (0, LANES - hq))).reshape(MLA_Q_RANK, nh * LANES)
    q = _proj(yp, wq.astype(BF16), q_norm, name="proj_mla_q", x_blk=(MLA_Q_RANK, cq_blk), rope=rope32s,
              rope_cols=nh * LANES, shift=MLA_ROPE // 2, out_dtype=BF16)
    wkv = w_ukv.reshape(MLA_KV_RANK, nh, MLA_NOPE + MLA_V)
    wk = jnp.pad(wkv[:, :, :MLA_NOPE], ((0, 0), (0, 0), (0, LANES - MLA_NOPE))).reshape(MLA_KV_RANK, nh * LANES)
    wkv = jnp.concatenate([wk, wkv[:, :, MLA_NOPE:].reshape(MLA_KV_RANK, nh * MLA_V)], axis=1).astype(BF16)
    kv = _proj(yp, wkv, kv_norm, name="proj_mla_kv", x_blk=(MLA_KV_RANK, ckv_blk), rope=rope32s,
               shift=MLA_ROPE // 2, add=yp, add_blk=kpe_blk, add_cols=nh * LANES, rope_add=True,
               out_dtype=BF16)
    q3, kv3 = q.reshape(b, s, -1), kv.reshape(b, s, -1)
    pairs = nh // 2
    o_d = _pflash(q3, kv3, kv3, name="flash_mla", mode="slots", dk=2 * LANES, batch=b, pairs=pairs, scale=hq ** -0.5,
                  q_map=lambda i, j: (i // pairs, j, i % pairs), k_map=lambda i, j: (i // pairs, 0, i % pairs),
                  v_map=lambda i, j: (i // pairs, 0, nh + i % pairs))

    return o_nsa, o_d.reshape(t, -1)


def kernel(x, positions, attn_norm, ffn_norm, final_norm, ffn_w_gate, ffn_w_up, ffn_w_down, ev_w_in, ev_w_out, diff_lambda_q1, diff_lambda_k1, diff_lambda_q2, diff_lambda_k2, diff_subln, od_w_in, od_w_out, nsa_gate_b, nsa_pe_k, nsa_pe_v, nsa_k_w1, nsa_k_w2, nsa_v_w1, nsa_v_w2, mla_q_norm, mla_w_uq, mla_kv_norm, mla_w_ukv):
    b, s, d = x.shape
    depth = attn_norm.shape[0]
    rope16 = _rope_tables(positions, ROPE_DIM, HEAD_DIM)
    rope32s = _rope_tables(positions, MLA_ROPE, LANES, offset=MLA_NOPE)
    h = x.reshape(b * s, d)
    for l in range(depth):
        i = l // 2
        if l % 2 == 0:
            oa, ob = _even_mixer(h, attn_norm[l], l, rope16, ev_w_in[i], diff_lambda_q1[i],
                                 diff_lambda_k1[i], diff_lambda_q2[i], diff_lambda_k2[i], diff_subln[i], b, s)
            w_out = ev_w_out[i]
        else:
            oa, ob = _odd_mixer(h, attn_norm[l], positions, rope16, rope32s, od_w_in[i], nsa_gate_b[i],
                                nsa_pe_k[i], nsa_pe_v[i], nsa_k_w1[i], nsa_k_w2[i], nsa_v_w1[i], nsa_v_w2[i],
                                mla_q_norm[i], mla_w_uq[i], mla_kv_norm[i], mla_w_ukv[i], b, s)
            w_out = od_w_out[i]
        h = _ffn(h, oa, ob, w_out.astype(BF16), ffn_norm[l], ffn_w_gate[l].astype(BF16), ffn_w_up[l].astype(BF16),
                 ffn_w_down[l].astype(BF16), final_norm, final_norm=(l == depth - 1))
    return h.reshape(b, s, d)
```

```python
import functools
import math

import jax
import jax.numpy as jnp
from jax import lax
from jax.experimental import pallas as pl
from jax.experimental.pallas import tpu as pltpu

F32 = jnp.float32
BF16 = jnp.bfloat16

D_MODEL = 1024
HEAD_DIM = 64
ROPE_THETA = 500000.0
ROPE_DIM = HEAD_DIM // 4
NORM_EPS = 1e-5
D_FF = 2816

MOBA_HEADS = 8
MOBA_BLOCK = 256
MOBA_TOPK = 3
DIFF_HEADS = 4
DIFF_V = 2 * HEAD_DIM
NSA_HEADS = 8
NSA_GROUPS = 2
NSA_REP = NSA_HEADS // NSA_GROUPS
NSA_CMP_LEN = 32
NSA_CMP_STRIDE = 16
NSA_CMP_HIDDEN = 256
NSA_SEL_BLOCK = 64
NSA_SEL_TOPK = 16
NSA_WINDOW = 512
NSA_FORCE_BONUS = 1e3
MLA_HEADS = 8
MLA_Q_RANK = 256
MLA_KV_RANK = 128
MLA_NOPE = 64
MLA_ROPE = 32
MLA_V = 64

LANES = 128
MOBA_BIAS_LANES = 32
NEG = -1e30
M_INIT = -1e37
LOG2E = 1.4426950408889634
UNROLL = 8
VMEM_LIMIT = 56 * 1024 * 1024

ROW_TILE = 512
FLASH_COLS = 1024
FLASH_TK = 512
SELECT_TQ = 2048
WINDOW_TQ = 512

_NT = (((1,), (1,)), ((), ()))


def _cparams(sem):
    return pltpu.CompilerParams(dimension_semantics=sem, vmem_limit_bytes=VMEM_LIMIT)


def _rope_tables(positions, dim, period, offset=0):
    r = dim // 2
    inv = 1.0 / (ROPE_THETA ** (jnp.arange(0, dim, 2, dtype=F32) / dim))
    ang = positions.astype(F32)[..., None] * inv
    cos, sin = jnp.cos(ang), jnp.sin(ang)
    const = lambda n, val: jnp.full(ang.shape[:-1] + (n,), val, F32)
    rest = period - offset - 2 * r
    zr = jnp.zeros_like(sin)
    c = jnp.concatenate([const(offset, 1.0), cos, cos, const(rest, 1.0)], -1)
    s1 = jnp.concatenate([const(offset, 0.0), -sin, zr, const(rest, 0.0)], -1)
    s2 = jnp.concatenate([const(offset, 0.0), zr, sin, const(rest, 0.0)], -1)
    reps = LANES // period
    tile = lambda t: jnp.tile(t, (1,) * (t.ndim - 1) + (reps,)).reshape(-1, LANES)
    return tile(c), tile(s1), tile(s2)


def _apply_rope(y, c, s1, s2, shift):
    return y * c + pltpu.roll(y, LANES - shift, 1) * s1 + pltpu.roll(y, shift, 1) * s2


def _proj_body(*refs, has_rope, add_cols, rope_add, rope_cols, shift, chunk, split):
    it = iter(refs)
    x_ref, g_ref, w_ref = next(it), next(it), next(it)
    if has_rope:
        c_ref, s1_ref, s2_ref = next(it), next(it), next(it)
    add_ref = next(it) if add_cols else None
    outs = list(it)
    n = w_ref.shape[1]
    xf = x_ref[...].astype(F32)
    y = xf * lax.rsqrt(jnp.mean(xf * xf, axis=-1, keepdims=True) + NORM_EPS)
    xb = (y * g_ref[...]).astype(BF16)
    if add_cols:
        add = add_ref[...].astype(F32)
        if rope_add:
            add = _apply_rope(add, c_ref[...], s1_ref[...], s2_ref[...], shift)
    spans = [(0, n, outs[0])] if len(outs) == 1 else [(0, split, outs[0]), (split, n, outs[1])]
    for lo, hi, o_ref in spans:
        for c0 in range(lo, hi, chunk):
            cw = min(chunk, hi - c0)
            y = jnp.dot(xb, w_ref[:, c0:c0 + cw], preferred_element_type=F32)
            if (has_rope and c0 < rope_cols) or c0 < add_cols:
                for k0 in range(0, cw, LANES):
                    ys = y[:, k0:k0 + LANES]
                    if has_rope and c0 + k0 < rope_cols:
                        ys = _apply_rope(ys, c_ref[...], s1_ref[...], s2_ref[...], shift)
                    if c0 + k0 < add_cols:
                        ys = ys + add
                    o_ref[:, c0 - lo + k0:c0 - lo + k0 + LANES] = ys.astype(o_ref.dtype)
            else:
                o_ref[:, c0 - lo:c0 - lo + cw] = y.astype(o_ref.dtype)


def _proj(x, w, gain, *, name, x_blk=None, rope=None, rope_cols=0, shift=0, add=None, add_blk=0, add_cols=0,
          rope_add=False, out_dtype=F32, split=None, out2_dtype=None, tm=ROW_TILE, chunk=512):
    t = x.shape[0]
    k, xj = (x.shape[1], 0) if x_blk is None else x_blk
    n = w.shape[1]
    assert t % tm == 0 and n % LANES == 0 and rope_cols % LANES == 0 and add_cols % LANES == 0
    assert split is None or split % LANES == 0
    has_rope = rope is not None
    args = [x, gain.reshape(1, k).astype(F32), w]
    specs = [pl.BlockSpec((tm, k), lambda i: (i, xj)), pl.BlockSpec((1, k), lambda i: (0, 0)),
             pl.BlockSpec((k, n), lambda i: (0, 0))]
    if has_rope:
        for tb in rope:
            args.append(tb)
            specs.append(pl.BlockSpec((tm, LANES), lambda i: (i, 0)))
    if add_cols:
        args.append(add)
        specs.append(pl.BlockSpec((tm, LANES), lambda i: (i, add_blk)))
    widths = [(n, out_dtype)] if split is None else [(split, out_dtype), (n - split, out2_dtype)]
    body = functools.partial(_proj_body, has_rope=has_rope, add_cols=add_cols, rope_add=rope_add,
                             rope_cols=rope_cols, shift=shift, chunk=chunk, split=split)
    out = pl.pallas_call(
        body, name=name, grid=(t // tm,), in_specs=specs,
        out_specs=[pl.BlockSpec((tm, wd), lambda i: (i, 0)) for wd, _ in widths],
        out_shape=[jax.ShapeDtypeStruct((t, wd), dt) for wd, dt in widths],
        compiler_params=_cparams(("parallel",)),
    )(*args)
    return out[0] if split is None else out


def _ffn_body(x_ref, oa_ref, ob_ref, wo_ref, g_ref, wg_ref, wu_ref, wd_ref, fg_ref, o_ref, *, final_norm):
    ka = oa_ref.shape[1]
    h = (x_ref[...] + jnp.dot(oa_ref[...], wo_ref[0:ka, :], preferred_element_type=F32)
         + jnp.dot(ob_ref[...], wo_ref[ka:, :], preferred_element_type=F32))
    y = h * lax.rsqrt(jnp.mean(h * h, axis=-1, keepdims=True) + NORM_EPS)
    xn = (y * g_ref[...]).astype(BF16)
    g = jnp.dot(xn, wg_ref[...], preferred_element_type=F32)
    u = jnp.dot(xn, wu_ref[...], preferred_element_type=F32)
    a = (jax.nn.silu(g) * u).astype(BF16)
    h = h + jnp.dot(a, wd_ref[...], preferred_element_type=F32)
    if final_norm:
        y = h * lax.rsqrt(jnp.mean(h * h, axis=-1, keepdims=True) + NORM_EPS)
        h = y * fg_ref[...]
    o_ref[...] = h


def _ffn(x, oa, ob, w_out, gain, wg, wu, wd, final_gain, *, final_norm, tm=ROW_TILE):
    t, d = x.shape
    f = wg.shape[1]
    assert t % tm == 0 and oa.shape[1] + ob.shape[1] == w_out.shape[0]
    once = pl.Buffered(1)
    return pl.pallas_call(
        functools.partial(_ffn_body, final_norm=final_norm), name="ffn",
        grid=(t // tm,),
        in_specs=[
            pl.BlockSpec((tm, d), lambda i: (i, 0)),
            pl.BlockSpec((tm, oa.shape[1]), lambda i: (i, 0)),
            pl.BlockSpec((tm, ob.shape[1]), lambda i: (i, 0)),
            pl.BlockSpec(w_out.shape, lambda i: (0, 0), pipeline_mode=once),
            pl.BlockSpec((1, d), lambda i: (0, 0)),
            pl.BlockSpec((d, f), lambda i: (0, 0), pipeline_mode=once),
            pl.BlockSpec((d, f), lambda i: (0, 0), pipeline_mode=once),
            pl.BlockSpec((f, d), lambda i: (0, 0), pipeline_mode=once),
            pl.BlockSpec((1, d), lambda i: (0, 0)),
        ],
        out_specs=pl.BlockSpec((tm, d), lambda i: (i, 0)),
        out_shape=jax.ShapeDtypeStruct((t, d), F32),
        compiler_params=_cparams(("parallel",)),
    )(x, oa, ob, w_out, gain.reshape(1, d).astype(F32), wg, wu, wd, final_gain.reshape(1, d).astype(F32))


def _diag_masks(tq, tk):
    w = min(tq, tk)
    o = jnp.arange(max(tk // tq, 1))[:, None, None]
    k = jnp.arange(tk)[None, :, None]
    i = jnp.arange(w)[None, None, :]
    return jnp.where(k <= i + o * tq, 0.0, NEG).astype(F32)


def _sweep(q, k_get, vt_get, s_ref, mask_ref, *, q0, tq, tk, dv, scale):
    cols = q.shape[0]

    def fold8(x, op):
        return op(x.reshape(x.shape[0] // 8, 8, x.shape[1]), axis=0)

    def first_col(u, n, n_masked):
        d = u - (n - n_masked)
        return d * tk if (d > 0 and cols == tq) else 0

    def upd(full, c0, f):
        return f(full) if c0 == 0 else jnp.concatenate([full[:, :c0], f(full[:, c0:])], axis=1)

    def score_block(j0, n, mrun, n_masked):
        ks = pl.multiple_of(j0 * tk, tk)
        n_wide = sum(1 for u in range(n) if first_col(u, n, n_masked) == 0)
        s = lax.dot_general(k_get(ks, n_wide * tk), q, _NT, preferred_element_type=F32) * (scale * LOG2E)
        for u in range(n):
            c0 = first_col(u, n, n_masked)
            if c0 == 0:
                su = s[u * tk:(u + 1) * tk]
            else:
                ku = k_get(pl.multiple_of(ks + u * tk, tk), tk)
                su = lax.dot_general(ku, q[c0:], _NT, preferred_element_type=F32) * (scale * LOG2E)
            if u >= n - n_masked:
                pat = mask_ref[0]
                w = pat.shape[1]
                if cols == tq:
                    su = su + pat if su.shape[1] == w else jnp.concatenate([su[:, :w] + pat, su[:, w:]], axis=1)
                else:
                    su = jnp.concatenate([su[:, c:c + w] + pat for c in range(0, cols, w)], axis=1)
            s_ref[j0 + u, :, c0:cols] = su
            mrun = upd(mrun, c0, lambda t: jnp.maximum(t, fold8(su, jnp.max)))
        return mrun

    def pv_block(j0, n, carry, m, n_masked=0):
        l8, acc = carry
        for u in range(n):
            c0 = first_col(u, n, n_masked)
            p = jnp.exp2(s_ref[j0 + u, :, c0:cols] - m[:, c0:])
            l8 = upd(l8, c0, lambda t: t + fold8(p, jnp.sum))
            pv = jnp.dot(vt_get(j0 + u), p.astype(BF16), preferred_element_type=F32)
            acc = upd(acc, c0, lambda t: t + pv)
        return l8, acc

    m_init = jnp.full((8, cols), M_INIT, F32)
    acc_init = (jnp.zeros((8, cols), F32), jnp.zeros((dv, cols), F32))
    n_diag = max(tq // tk, 1)
    assert UNROLL % n_diag == 0
    n_full = q0 // tk
    groups = n_full // UNROLL
    rem = n_full - groups * UNROLL
    tail0 = groups * UNROLL
    tails = range(n_diag, UNROLL + n_diag, n_diag)

    mrun = lax.fori_loop(0, groups, lambda i, t: score_block(UNROLL * i, UNROLL, t, 0), m_init)
    mrun = lax.switch(rem // n_diag, [functools.partial(score_block, tail0, n, n_masked=n_diag) for n in tails],
                      mrun)
    m = jnp.max(mrun, axis=0, keepdims=True)
    carry = lax.fori_loop(0, groups, lambda i, t: pv_block(UNROLL * i, UNROLL, t, m), acc_init)
    l8, acc = lax.switch(rem // n_diag, [functools.partial(pv_block, tail0, n, m=m, n_masked=n_diag) for n in tails],
                         carry)
    l = jnp.sum(l8, axis=0, keepdims=True)
    return acc / jnp.maximum(l, 1e-30)


def _head_lanes(mode, v, lane):
    d = HEAD_DIM
    if mode == "diff":
        return jnp.logical_and(lane >= d * v, lane < d * (v + 1))
    if mode == "moba":
        lo = 2 * d + v * MOBA_BIAS_LANES
        return jnp.logical_or(jnp.logical_and(lane >= d * v, lane < d * (v + 1)),
                              jnp.logical_and(lane >= lo, lane < lo + MOBA_BIAS_LANES))
    return jnp.logical_and(lane >= LANES * v, lane < LANES * (v + 1))


def _pflash_body(*refs, mode, tq, tk, scale, lambda_init):
    if mode == "diff":
        q_ref, k_ref, v_ref, mask_ref, lq1_ref, lk1_ref, lq2_ref, lk2_ref, sg_ref, o_ref, s_ref, vt_ref = refs
    else:
        q_ref, k_ref, v_ref, mask_ref, o_ref, s_ref, vt_ref = refs
    qi = pl.program_id(1)
    n_chunks = v_ref.shape[1] // tk

    @pl.when(qi == 0)
    def _():
        for c in range(n_chunks):
            vt_ref[c] = v_ref[0, c * tk:(c + 1) * tk, :].astype(F32).T.astype(BF16)

    q = q_ref[0]
    lane = lax.broadcasted_iota(jnp.int32, q.shape, 1)
    d = HEAD_DIM
    outs = []
    for v in range(2):
        qv = jnp.where(_head_lanes(mode, v, lane), q, jnp.zeros_like(q))
        if mode == "diff":
            vt_get, dv = (lambda j: vt_ref[j]), LANES
        else:
            vt_get, dv = (lambda j, v=v: vt_ref[j, v * d:(v + 1) * d, :]), d
        outs.append(_sweep(qv, lambda ks, n: k_ref[0, pl.ds(ks, n), :], vt_get, s_ref, mask_ref,
                           q0=qi * tq, tq=tq, tk=tk, dv=dv, scale=scale))
    if mode == "diff":
        lam = (jnp.exp(jnp.sum(lq1_ref[...] * lk1_ref[...], axis=-1, keepdims=True))
               - jnp.exp(jnp.sum(lq2_ref[...] * lk2_ref[...], axis=-1, keepdims=True)) + lambda_init)
        d = outs[0] - lam * outs[1]
        y = d * lax.rsqrt(jnp.mean(d * d, axis=0, keepdims=True) + NORM_EPS)
        o_t = (y * sg_ref[...]) * (1.0 - lambda_init)
    else:
        o_t = jnp.concatenate(outs, axis=0)
    o_ref[0] = o_t.T.astype(o_ref.dtype)


def _pflash(q, k, v, *, name, mode, dk, q_map, k_map, v_map, batch, pairs, tq=FLASH_COLS, tk=FLASH_TK, scale=1.0,
            diff_params=None, lambda_init=0.0):
    s = v.shape[1]
    assert s % tk == 0 and tq % tk == 0 and s % tq == 0
    args = [q, k, v, _diag_masks(tq, tk)]
    in_specs = [pl.BlockSpec((1, tq, dk), q_map), pl.BlockSpec((1, s, dk), k_map),
                pl.BlockSpec((1, s, LANES), v_map), pl.BlockSpec((1, tk, tk), lambda i, j: (0, 0, 0))]
    if mode == "diff":
        lq1, lk1, lq2, lk2, subln = diff_params
        for a in (lq1, lk1, lq2, lk2):
            args.append(a.reshape(1, HEAD_DIM).astype(F32))
            in_specs.append(pl.BlockSpec((1, HEAD_DIM), lambda i, j: (0, 0)))
        args.append(subln.reshape(LANES, 1).astype(F32))
        in_specs.append(pl.BlockSpec((LANES, 1), lambda i, j: (0, 0)))
    body = functools.partial(_pflash_body, mode=mode, tq=tq, tk=tk, scale=scale, lambda_init=lambda_init)
    return pl.pallas_call(
        body, name=name, grid=(batch * pairs, s // tq), in_specs=in_specs,
        out_specs=pl.BlockSpec((1, tq, LANES), lambda i, j: (i // pairs, j, i % pairs)),
        out_shape=jax.ShapeDtypeStruct((batch, s, pairs * LANES), BF16),
        scratch_shapes=[pltpu.VMEM((s // tk, tk, tq), F32), pltpu.VMEM((s // tk, LANES, tk), BF16)],
        compiler_params=_cparams(("parallel", "arbitrary")),
    )(*args)


def _band_masks(tq, window):
    kl = jnp.arange(window + tq)[None, :, None]
    ql = jnp.arange(tq)[None, None, :]
    q0 = (jnp.arange(window // tq + 1) * tq)[:, None, None]
    k = jnp.maximum(q0 - window, 0) + kl
    t = q0 + ql
    return jnp.where(jnp.logical_and(k <= t, k > t - window), 0.0, NEG).astype(F32)


def _band(q, ka_ref, vt_ref, mask_ref, *, q0, tq, window):
    cols = q.shape[0]
    band = window + tq
    start = pl.multiple_of(jnp.maximum(q0 - window, 0), tq)
    s = lax.dot_general(ka_ref[pl.ds(start, band), :], q, _NT, preferred_element_type=F32) * LOG2E

    s = jnp.concatenate([s[:, c:c + tq] + mask_ref[0] for c in range(0, cols, tq)], axis=1)
    p = jnp.exp2(s - jnp.max(s, axis=0, keepdims=True))
    l = jnp.sum(p, axis=0, keepdims=True)
    pb = p.astype(BF16)
    acc = None
    for c in range(band // tq):
        part = jnp.dot(vt_ref[start // tq + c], pb[c * tq:(c + 1) * tq], preferred_element_type=F32)
        acc = part if acc is None else acc + part
    return acc / jnp.maximum(l, 1e-30)


def _gflash_body(*refs, tq, tk, rep, groups, window, select):
    if select:
        q_ref, k_ref, v_ref, mb_ref, mask_ref, o_ref, s_ref, vt_ref, ka_ref, qa_ref = refs
    else:
        q_ref, k_ref, v_ref, mask_ref, o_ref, vt_ref, ka_ref, qa_ref = refs
    d = HEAD_DIM
    tv = vt_ref.shape[2]
    g = pl.program_id(0) % groups
    qi = pl.program_id(1)
    n_chunks = v_ref.shape[1] // tk

    @pl.when(qi == 0)
    def _():
        move = _place(LANES, LANES, g * d)
        for c in range(n_chunks):
            rows = slice(c * tk, (c + 1) * tk)
            vt = v_ref[0, rows, :].astype(F32).T
            vt = jnp.where(g == 0, vt[0:d], vt[d:2 * d]).astype(BF16)
            for u in range(tk // tv):
                vt_ref[c * (tk // tv) + u] = vt[:, u * tv:(u + 1) * tv]
            kg = jnp.dot(k_ref[0, rows, :], move, preferred_element_type=F32).astype(BF16)
            if select:
                blk = (c * tk + lax.broadcasted_iota(jnp.int32, (tk, LANES), 0)) // NSA_SEL_BLOCK
                lane = lax.broadcasted_iota(jnp.int32, (tk, LANES), 1)
                ka_ref[rows, 0:LANES] = jnp.where(blk == lane, 1.0, 0.0).astype(BF16)
                ka_ref[rows, LANES:2 * LANES] = kg
            else:
                ka_ref[rows, :] = kg

    q = q_ref[0]
    for r in range(rep):
        qr = jnp.dot(q, _place(rep * d, LANES, r * d, d ** -0.5), preferred_element_type=F32).astype(BF16)
        rows = slice(r * tq, (r + 1) * tq)
        if select:
            qa_ref[rows, 0:LANES] = mb_ref[0]
            qa_ref[rows, LANES:2 * LANES] = qr
        else:
            qa_ref[rows, :] = qr
    if window is None:
        o_t = _sweep(qa_ref[...], lambda ks, n: ka_ref[pl.ds(ks, n), :], lambda j: vt_ref[j], s_ref, mask_ref,
                     q0=qi * tq, tq=tq, tk=tk, dv=d, scale=1.0)
    else:
        half = (rep // 2) * tq
        o_t = jnp.concatenate([_band(qa_ref[0:half, :], ka_ref, vt_ref, mask_ref, q0=qi * tq, tq=tq, window=window),
                               _band(qa_ref[half:, :], ka_ref, vt_ref, mask_ref, q0=qi * tq, tq=tq, window=window)],
                              axis=1)
    for r in range(rep):
        o_ref[0, :, r * d:(r + 1) * d] = o_t[:, r * tq:(r + 1) * tq].T.astype(o_ref.dtype)


def _gflash(yq, yk, yv, mbias, *, name, q_blk, k_blk, v_blk, tq=FLASH_COLS // NSA_REP, tk=FLASH_TK, window=None):
    b, s, _ = yq.shape
    rep, groups, d = NSA_REP, NSA_GROUPS, HEAD_DIM
    assert groups * d == LANES and s % tk == 0 and tk % tq == 0
    select = mbias is not None
    dk = 2 * LANES if select else LANES
    assert select == (window is None)
    assert window is None or (window % tq == 0 and s >= window + tq)
    tv = tk if window is None else tq
    scratch = [pltpu.VMEM((s // tv, d, tv), BF16), pltpu.VMEM((s, dk), BF16), pltpu.VMEM((rep * tq, dk), BF16)]
    if window is None:
        scratch.insert(0, pltpu.VMEM((s // tk, tk, rep * tq), F32))
    args = [yq, yk, yv]
    in_specs = [pl.BlockSpec((1, tq, rep * d), lambda i, j: (i // groups, j, q_blk + i % groups)),
                pl.BlockSpec((1, s, LANES), lambda i, j: (i // groups, 0, k_blk)),
                pl.BlockSpec((1, s, LANES), lambda i, j: (i // groups, 0, v_blk))]
    if select:
        args += [mbias, _diag_masks(tq, tk)]
        in_specs += [pl.BlockSpec((1, tq, LANES), lambda i, j: (i, j, 0)),
                     pl.BlockSpec((1, tk, tq), lambda i, j: (j % (tk // tq), 0, 0))]
    else:
        args.append(_band_masks(tq, window))
        in_specs.append(pl.BlockSpec((1, window + tq, tq), lambda i, j: (jnp.minimum(j, window // tq), 0, 0)))
    body = functools.partial(_gflash_body, tq=tq, tk=tk, rep=rep, groups=groups, window=window, select=select)
    return pl.pallas_call(
        body, name=name, grid=(b * groups, s // tq), in_specs=in_specs,
        out_specs=pl.BlockSpec((1, tq, rep * d), lambda i, j: (i // groups, j, i % groups)),
        out_shape=jax.ShapeDtypeStruct((b, s, groups * rep * d), F32),
        scratch_shapes=scratch,
        compiler_params=_cparams(("parallel", "arbitrary")),
    )(*args)


def _topk_mask(score, row, k):
    sel = None
    for _ in range(k):
        mx = jnp.max(score, axis=0, keepdims=True)
        idx = jnp.min(jnp.where(score == mx, row, LANES), axis=0, keepdims=True)
        hit = row == idx
        pick = jnp.logical_and(hit, mx > -jnp.inf)
        sel = pick if sel is None else jnp.logical_or(sel, pick)
        score = jnp.where(hit, -jnp.inf, score)
    return sel


def _moba_gate_body(q_ref, k_ref, qa_ref, ka_ref, kmean_ref, *, tq, seq):
    qi = pl.program_id(1)
    nbl = MOBA_BIAS_LANES

    @pl.when(qi == 0)
    def _():
        rowblk = lax.rem(lax.broadcasted_iota(jnp.int32, (LANES, seq), 0), nbl)
        colblk = lax.broadcasted_iota(jnp.int32, (LANES, seq), 1) // MOBA_BLOCK
        ind = jnp.where(rowblk == colblk, 1.0, 0.0).astype(BF16)
        ksum = jnp.dot(ind, k_ref[0], preferred_element_type=F32)
        rhead = lax.broadcasted_iota(jnp.int32, (LANES, LANES), 0) // nbl
        lhead = lax.broadcasted_iota(jnp.int32, (LANES, LANES), 1) // HEAD_DIM
        kmean_ref[...] = jnp.where(rhead == lhead, ksum * (1.0 / MOBA_BLOCK), 0.0)

    q = q_ref[0]
    q0 = pl.multiple_of(qi * tq, tq)
    km = kmean_ref[...]
    km_hi = km.astype(BF16)
    km_lo = (km - km_hi.astype(F32)).astype(BF16)
    gate = (lax.dot_general(km_hi, q, _NT, preferred_element_type=F32)
            + lax.dot_general(km_lo, q, _NT, preferred_element_type=F32))
    blk_t = lax.broadcasted_iota(jnp.int32, (nbl, tq), 0)
    own_t = (q0 + lax.broadcasted_iota(jnp.int32, (nbl, tq), 1)) // MOBA_BLOCK
    parts = []
    for v in range(2):
        g_v = jnp.where(blk_t < own_t, gate[v * nbl:(v + 1) * nbl], -jnp.inf)
        sel = jnp.logical_or(_topk_mask(g_v, blk_t, MOBA_TOPK), blk_t == own_t)
        parts.append(jnp.where(sel, 0.0, NEG))
    parts.append(jnp.zeros((LANES - 2 * nbl, tq), F32))
    bias = jnp.concatenate(parts, axis=0).T
    qa_ref[0, :, 0:LANES] = q * (HEAD_DIM ** -0.5)
    qa_ref[0, :, LANES:2 * LANES] = bias.astype(BF16)
    lane = lax.broadcasted_iota(jnp.int32, (tq, LANES), 1)
    own = (q0 + lax.broadcasted_iota(jnp.int32, (tq, LANES), 0)) // MOBA_BLOCK
    onehot = jnp.where(jnp.logical_and(lane < 2 * nbl, lax.rem(lane, nbl) == own), 1.0, 0.0)
    ka_ref[0, :, 0:LANES] = k_ref[0, pl.ds(q0, tq), :]
    ka_ref[0, :, LANES:2 * LANES] = onehot.astype(BF16)


def _moba_gate(y3, *, q_blk, k_blk, pairs, tq=SELECT_TQ):
    b, s, _ = y3.shape
    assert s % MOBA_BLOCK == 0 and s // MOBA_BLOCK <= MOBA_BIAS_LANES and s % tq == 0
    out = jax.ShapeDtypeStruct((b * pairs, s, 2 * LANES), BF16)
    return pl.pallas_call(
        functools.partial(_moba_gate_body, tq=tq, seq=s), name="moba_gate",
        grid=(b * pairs, s // tq),
        in_specs=[pl.BlockSpec((1, tq, LANES), lambda i, j: (i // pairs, j, q_blk + i % pairs)),
                  pl.BlockSpec((1, s, LANES), lambda i, j: (i // pairs, 0, k_blk + i % pairs))],
        out_specs=[pl.BlockSpec((1, tq, 2 * LANES), lambda i, j: (i, j, 0)),
                   pl.BlockSpec((1, tq, 2 * LANES), lambda i, j: (i, j, 0))],
        out_shape=[out, out],
        scratch_shapes=[pltpu.VMEM((LANES, LANES), F32)],
        compiler_params=_cparams(("parallel", "arbitrary")),
    )(y3, y3)


def _nsa_compress_body(x_ref, pe_ref, w1_ref, w2_ref, c_ref, s1_ref, s2_ref, o_ref, *, rope):
    half = w1_ref.shape[0] // 2
    x = x_ref[0].astype(F32)
    lo = (x + pe_ref[:, 0:half]).astype(BF16)
    hi = (x + pe_ref[:, half:2 * half]).astype(BF16)
    a = jnp.dot(lo, w1_ref[0:half, :], preferred_element_type=F32)
    b = jnp.dot(hi, w1_ref[half:2 * half, :], preferred_element_type=F32)
    n = a.shape[0]
    h1 = a + pltpu.roll(b, n - 1, 0)
    y = jnp.dot(jax.nn.gelu(h1).astype(BF16), w2_ref[...], preferred_element_type=F32)
    if rope:
        y = _apply_rope(y, c_ref[0], s1_ref[0], s2_ref[0], ROPE_DIM // 2)
    o_ref[0] = y[:, 0:HEAD_DIM].astype(o_ref.dtype)


def _nsa_compress(xr, pe, w1, w2, tables, *, rope, groups):
    g, n, w = xr.shape
    hid = w1.shape[1]
    w2p = jnp.pad(w2, ((0, 0), (0, LANES - w2.shape[1])))
    tspec = pl.BlockSpec((1, n, LANES), lambda i: (i // groups, 0, 0))
    return pl.pallas_call(
        functools.partial(_nsa_compress_body, rope=rope), name="nsa_compress",
        grid=(g,),
        in_specs=[pl.BlockSpec((1, n, w), lambda i: (i, 0, 0)),
                  pl.BlockSpec((1, 2 * w), lambda i: (0, 0)),
                  pl.BlockSpec((2 * w, hid), lambda i: (0, 0)),
                  pl.BlockSpec((hid, LANES), lambda i: (0, 0)),
                  tspec, tspec, tspec],
        out_specs=pl.BlockSpec((1, n, HEAD_DIM), lambda i: (i, 0, 0)),
        out_shape=jax.ShapeDtypeStruct((g, n, HEAD_DIM), BF16),
        compiler_params=_cparams(("parallel",)),
    )(xr, pe.reshape(1, 2 * w).astype(F32), w1, w2p, *tables)


def _place(n_src, n_dst, shift, value=1.0):
    src = lax.broadcasted_iota(jnp.int32, (n_src, n_dst), 0)
    dst = lax.broadcasted_iota(jnp.int32, (n_src, n_dst), 1)
    return jnp.where(jnp.logical_and(src == dst + shift, dst < HEAD_DIM), value, 0.0).astype(BF16)


def _nsa_cmp_body(q_ref, kc_ref, vc_ref, oc_ref, mb_ref, kcp_ref, *, tq, rep):
    qi = pl.program_id(1)
    q0 = qi * tq
    nc = kc_ref.shape[1]
    width = q_ref.shape[-1]

    @pl.when(qi == 0)
    def _():
        for r in range(rep):
            src = lax.broadcasted_iota(jnp.int32, (HEAD_DIM, width), 0)
            dst = lax.broadcasted_iota(jnp.int32, (HEAD_DIM, width), 1)
            spread = jnp.where(dst == src + r * HEAD_DIM, HEAD_DIM ** -0.5, 0.0).astype(BF16)
            kcp_ref[r] = jnp.dot(kc_ref[0], spread, preferred_element_type=F32).astype(BF16)

    q = q_ref[0]

    def branches(nr):
        def run():
            vct = vc_ref[0, :, 0:nr]
            tpos = q0 + lax.broadcasted_iota(jnp.int32, (nr, tq), 1)
            cend = lax.broadcasted_iota(jnp.int32, (nr, tq), 0) * NSA_CMP_STRIDE + (NSA_CMP_LEN - 1)
            ok = cend <= tpos
            psum = jnp.zeros((nr, tq), F32)
            for r in range(rep):
                s = lax.dot_general(kcp_ref[r, 0:nr, :], q, _NT, preferred_element_type=F32)
                s = jnp.where(ok, s, NEG)
                m = jnp.max(s, axis=0, keepdims=True)
                e = jnp.where(ok, jnp.exp(s - m), 0.0)
                p = e * (1.0 / jnp.maximum(jnp.sum(e, axis=0, keepdims=True), 1e-30))
                o_t = jnp.dot(vct, p.astype(BF16), preferred_element_type=F32)
                oc_ref[0, :, r * HEAD_DIM:(r + 1) * HEAD_DIM] = o_t.T.astype(oc_ref.dtype)
                psum = psum + p
            nb = nr * NSA_CMP_STRIDE // NSA_SEL_BLOCK
            sstart = lax.broadcasted_iota(jnp.int32, (nb, nr), 0) * NSA_SEL_BLOCK
            cstart = lax.broadcasted_iota(jnp.int32, (nb, nr), 1) * NSA_CMP_STRIDE
            ov = jnp.where(jnp.logical_and(cstart < sstart + NSA_SEL_BLOCK, cstart + NSA_CMP_LEN > sstart),
                           1.0, 0.0).astype(BF16)
            p_hi = psum.astype(BF16)
            r1 = psum - p_hi.astype(F32)
            p_mid = r1.astype(BF16)
            p_lo = (r1 - p_mid.astype(F32)).astype(BF16)
            imp = (jnp.dot(ov, p_hi, preferred_element_type=F32) + jnp.dot(ov, p_mid, preferred_element_type=F32)
                   + jnp.dot(ov, p_lo, preferred_element_type=F32))
            blk = lax.broadcasted_iota(jnp.int32, (nb, tq), 0)
            qb = (q0 + lax.broadcasted_iota(jnp.int32, (nb, tq), 1)) // NSA_SEL_BLOCK
            forced = jnp.logical_or(blk == 0, jnp.logical_or(blk == qb, blk == qb - 1))
            imp = jnp.where(forced, imp + NSA_FORCE_BONUS, imp)
            imp = jnp.where(blk <= qb, imp, -jnp.inf)
            bias = jnp.where(_topk_mask(imp, blk, NSA_SEL_TOPK), 0.0, NEG)
            if nb < LANES:
                bias = jnp.concatenate([bias, jnp.full((LANES - nb, tq), NEG, F32)], axis=0)
            return bias
        return run

    sizes = list(range(LANES, nc + 1, LANES))
    needed = (q0 + tq) // NSA_CMP_STRIDE
    bias_t = lax.switch(jnp.minimum((needed + LANES - 1) // LANES, len(sizes)) - 1, [branches(nr) for nr in sizes])
    mb_ref[0] = bias_t.T.astype(mb_ref.dtype)


def _nsa_cmp(y3, kc, vc, *, tq=SELECT_TQ):
    b, s, _ = y3.shape
    g, nc, d = kc.shape
    rep, n = NSA_REP, NSA_GROUPS
    assert s // NSA_SEL_BLOCK <= LANES and s % tq == 0
    return pl.pallas_call(
        functools.partial(_nsa_cmp_body, tq=tq, rep=rep), name="nsa_cmp",
        grid=(g, s // tq),
        in_specs=[pl.BlockSpec((1, tq, rep * d), lambda i, j: (i // n, j, i % n)),
                  pl.BlockSpec((1, nc, d), lambda i, j: (i, 0, 0)),
                  pl.BlockSpec((1, d, nc), lambda i, j: (i, 0, 0))],
        out_specs=[pl.BlockSpec((1, tq, rep * d), lambda i, j: (i // n, j, i % n)),
                   pl.BlockSpec((1, tq, LANES), lambda i, j: (i, j, 0))],
        out_shape=[jax.ShapeDtypeStruct((b, s, n * rep * d), F32),
                   jax.ShapeDtypeStruct((g, s, LANES), BF16)],
        scratch_shapes=[pltpu.VMEM((rep, nc, rep * d), BF16)],
        compiler_params=_cparams(("parallel", "arbitrary")),
    )(y3, kc, vc)


def _nsa_combine_body(oc_ref, os_ref, ow_ref, g_ref, b_ref, e_ref, o_ref):
    gs = jax.nn.sigmoid(g_ref[...] + b_ref[...])
    g_hi = gs.astype(BF16)
    g_lo = (gs - g_hi.astype(F32)).astype(BF16)
    out = None
    for i, ref in enumerate((oc_ref, os_ref, ow_ref)):
        w = (jnp.dot(g_hi, e_ref[i], preferred_element_type=F32)
             + jnp.dot(g_lo, e_ref[i], preferred_element_type=F32))
        term = w * ref[...]
        out = term if out is None else out + term
    o_ref[...] = out.astype(o_ref.dtype)


def _nsa_combine(oc, osel, ow, yg, g_blk, gate_b, *, tm=ROW_TILE):
    t, n = oc.shape
    nh = NSA_HEADS * 3
    bp = jnp.pad(gate_b.reshape(1, nh).astype(F32), ((0, 0), (0, LANES - nh)))
    row = jnp.arange(LANES)[:, None]
    col = jnp.arange(n)[None, :] // HEAD_DIM
    expand = jnp.stack([(row == col * 3 + i) for i in range(3)]).astype(BF16)
    tok = pl.BlockSpec((tm, n), lambda i: (i, 0))
    return pl.pallas_call(
        _nsa_combine_body, name="nsa_combine", grid=(t // tm,),
        in_specs=[tok, tok, tok, pl.BlockSpec((tm, LANES), lambda i: (i, g_blk)),
                  pl.BlockSpec((1, LANES), lambda i: (0, 0)),
                  pl.BlockSpec((3, LANES, n), lambda i: (0, 0, 0))],
        out_specs=tok,
        out_shape=jax.ShapeDtypeStruct((t, n), BF16),
        compiler_params=_cparams(("parallel",)),
    )(oc, osel, ow, yg, bp, expand)


def _to_heads(x, b, s, n):
    return x.reshape(b, s, n, -1).transpose(0, 2, 1, 3).reshape(b * n, s, -1)


def _even_mixer(h, gain, layer_idx, rope16, w_in, lq1, lk1, lq2, lk2, subln, b, s):
    na = MOBA_HEADS * HEAD_DIM
    nb = DIFF_HEADS * 2 * HEAD_DIM
    w_rope = jnp.concatenate([w_in[:, 0:2 * na], w_in[:, 3 * na:3 * na + 2 * nb]], axis=1)
    w_rest = jnp.concatenate([w_in[:, 2 * na:3 * na], w_in[:, 3 * na + 2 * nb:]], axis=1)
    w = jnp.concatenate([w_rope, w_rest], axis=1).astype(BF16)
    y = _proj(h, w, gain, name="proj_even_in", rope=rope16, rope_cols=w_rope.shape[1], shift=ROPE_DIM // 2, out_dtype=BF16)
    y3 = y.reshape(b, s, -1)
    blk = lambda off: off // LANES
    pairs = na // LANES
    qa, ka = _moba_gate(y3, q_blk=blk(0), k_blk=blk(na), pairs=pairs)
    o_a = _pflash(qa, ka, y3, name="flash_moba", mode="moba", dk=2 * LANES, batch=b, pairs=pairs,
                  q_map=lambda i, j: (i, j, 0), k_map=lambda i, j: (i, 0, 0),
                  v_map=lambda i, j: (i // pairs, 0, blk(2 * na + 2 * nb) + i % pairs))
    lambda_init = 0.8 - 0.6 * math.exp(-0.3 * layer_idx)
    nh = DIFF_HEADS
    o_b = _pflash(y3, y3, y3, name="flash_diff", mode="diff", dk=LANES, batch=b, pairs=nh, scale=HEAD_DIM ** -0.5,
                  q_map=lambda i, j: (i // nh, j, blk(2 * na) + i % nh),
                  k_map=lambda i, j: (i // nh, 0, blk(2 * na + nb) + i % nh),
                  v_map=lambda i, j: (i // nh, 0, blk(3 * na + 2 * nb) + i % nh),
                  diff_params=(lq1, lk1, lq2, lk2, subln), lambda_init=lambda_init)
    t = b * s
    return o_a.reshape(t, -1), o_b.reshape(t, -1)


def _odd_mixer(h, gain, positions, rope16, rope32s, w_in, gate_b, pe_k, pe_v, k_w1, k_w2, v_w1, v_w2,
               q_norm, w_uq, kv_norm, w_ukv, b, s):
    G, R, d = NSA_GROUPS, NSA_REP, HEAD_DIM
    sizes = [NSA_HEADS * d] + [G * d] * 6 + [NSA_HEADS * 3, MLA_Q_RANK, MLA_KV_RANK, MLA_ROPE]
    offs = [0]
    for z in sizes:
        offs.append(offs[-1] + z)
    col = lambda i: w_in[:, offs[i]:offs[i + 1]]
    w_r = jnp.concatenate([col(0), col(3), col(5)], axis=1)
    w_p = jnp.concatenate([col(1), col(2), col(4), col(6), col(8), col(9), col(7)], axis=1)
    w_p = jnp.pad(w_p, ((0, 0), (0, (-w_p.shape[1]) % LANES)))
    w_pe = jnp.pad(col(10), ((0, 0), (MLA_NOPE, LANES - MLA_NOPE - MLA_ROPE)))
    yr, yp = _proj(h, jnp.concatenate([w_r, w_p, w_pe], axis=1).astype(BF16), gain, name="proj_odd_in", rope=rope16,
                   rope_cols=w_r.shape[1], shift=ROPE_DIM // 2, out_dtype=BF16, split=w_r.shape[1], out2_dtype=F32)
    k_cmp, v_cmp = yp[:, 0:G * d], yp[:, G * d:2 * G * d]
    cq_blk = 4 * G * d // MLA_Q_RANK
    ckv_blk = (4 * G * d + MLA_Q_RANK) // MLA_KV_RANK
    gate_blk = (4 * G * d + MLA_Q_RANK + MLA_KV_RANK) // LANES
    kpe_blk = w_p.shape[1] // LANES

    yr3, yp3 = yr.reshape(b, s, -1), yp.reshape(b, s, -1)
    nc = s // NSA_CMP_STRIDE
    cpos = jnp.concatenate([positions[:, NSA_CMP_LEN - 1::NSA_CMP_STRIDE], positions[:, -1:]], axis=1)
    ctab = [t.reshape(b, nc, LANES) for t in _rope_tables(cpos, ROPE_DIM, HEAD_DIM)]
    xk = _to_heads(k_cmp, b, s, G).reshape(b * G, nc, NSA_CMP_STRIDE * d)
    xv = _to_heads(v_cmp, b, s, G).reshape(b * G, nc, NSA_CMP_STRIDE * d)
    kc = _nsa_compress(xk, pe_k, k_w1.astype(BF16), k_w2.astype(BF16), ctab, rope=True, groups=G)
    vc = _nsa_compress(xv, pe_v, v_w1.astype(BF16), v_w2.astype(BF16), ctab, rope=False, groups=G)
    o_c, mbias = _nsa_cmp(yr3, kc, vc.transpose(0, 2, 1))
    k_blk = NSA_HEADS * d // LANES
    vsw = yp[:, 2 * G * d:4 * G * d].astype(BF16).reshape(b, s, -1)
    o_s = _gflash(yr3, yr3, vsw, mbias, name="flash_sel", q_blk=0, k_blk=k_blk, v_blk=0)
    o_w = _gflash(yr3, yr3, vsw, None, name="flash_win", q_blk=0, k_blk=k_blk + 1, v_blk=1, tq=WINDOW_TQ,
                  window=NSA_WINDOW)
    t = b * s
    o_nsa = _nsa_combine(o_c.reshape(t, -1), o_s.reshape(t, -1), o_w.reshape(t, -1), yp, gate_blk, gate_b)

    hq = MLA_NOPE + MLA_ROPE
    nh = MLA_HEADS
    wq = jnp.pad(w_uq.reshape(MLA_Q_RANK, nh, hq), ((0, 0), (0, 0), (0, LANES - hq))).reshape(MLA_Q_RANK, nh * LANES)
    q = _proj(yp, wq.astype(BF16), q_norm, name="proj_mla_q", x_blk=(MLA_Q_RANK, cq_blk), rope=rope32s,
              rope_cols=nh * LANES, shift=MLA_ROPE // 2, out_dtype=BF16)
    wkv = w_ukv.reshape(MLA_KV_RANK, nh, MLA_NOPE + MLA_V)
    wk = jnp.pad(wkv[:, :, :MLA_NOPE], ((0, 0), (0, 0), (0, LANES - MLA_NOPE))).reshape(MLA_KV_RANK, nh * LANES)
    wkv = jnp.concatenate([wk, wkv[:, :, MLA_NOPE:].reshape(MLA_KV_RANK, nh * MLA_V)], axis=1).astype(BF16)
    kv = _proj(yp, wkv, kv_norm, name="proj_mla_kv", x_blk=(MLA_KV_RANK, ckv_blk), rope=rope32s,
               shift=MLA_ROPE // 2, add=yp, add_blk=kpe_blk, add_cols=nh * LANES, rope_add=True,
               out_dtype=BF16)
    q3, kv3 = q.reshape(b, s, -1), kv.reshape(b, s, -1)
    pairs = nh // 2
    o_d = _pflash(q3, kv3, kv3, name="flash_mla", mode="slots", dk=2 * LANES, batch=b, pairs=pairs, scale=hq ** -0.5,
                  q_map=lambda i, j: (i // pairs, j, i % pairs), k_map=lambda i, j: (i // pairs, 0, i % pairs),
                  v_map=lambda i, j: (i // pairs, 0, nh + i % pairs))

    return o_nsa, o_d.reshape(t, -1)


def kernel(x, positions, attn_norm, ffn_norm, final_norm, ffn_w_gate, ffn_w_up, ffn_w_down, ev_w_in, ev_w_out, diff_lambda_q1, diff_lambda_k1, diff_lambda_q2, diff_lambda_k2, diff_subln, od_w_in, od_w_out, nsa_gate_b, nsa_pe_k, nsa_pe_v, nsa_k_w1, nsa_k_w2, nsa_v_w1, nsa_v_w2, mla_q_norm, mla_w_uq, mla_kv_norm, mla_w_ukv):
    b, s, d = x.shape
    depth = attn_norm.shape[0]
    rope16 = _rope_tables(positions, ROPE_DIM, HEAD_DIM)
    rope32s = _rope_tables(positions, MLA_ROPE, LANES, offset=MLA_NOPE)
    h = x.reshape(b * s, d)
    for l in range(depth):
        i = l // 2
        if l % 2 == 0:
            oa, ob = _even_mixer(h, attn_norm[l], l, rope16, ev_w_in[i], diff_lambda_q1[i],
                                 diff_lambda_k1[i], diff_lambda_q2[i], diff_lambda_k2[i], diff_subln[i], b, s)
            w_out = ev_w_out[i]
        else:
            oa, ob = _odd_mixer(h, attn_norm[l], positions, rope16, rope32s, od_w_in[i], nsa_gate_b[i],
                                nsa_pe_k[i], nsa_pe_v[i], nsa_k_w1[i], nsa_k_w2[i], nsa_v_w1[i], nsa_v_w2[i],
                                mla_q_norm[i], mla_w_uq[i], mla_kv_norm[i], mla_w_ukv[i], b, s)
            w_out = od_w_out[i]
        h = _ffn(h, oa, ob, w_out.astype(BF16), ffn_norm[l], ffn_w_gate[l].astype(BF16), ffn_w_up[l].astype(BF16),
                 ffn_w_down[l].astype(BF16), final_norm, final_norm=(l == depth - 1))
    return h.reshape(b, s, d)
```

```python
import functools
import math

import jax
import jax.numpy as jnp
from jax import lax
from jax.experimental import pallas as pl
from jax.experimental.pallas import tpu as pltpu

F32 = jnp.float32
BF16 = jnp.bfloat16

D_MODEL = 1024
HEAD_DIM = 64
ROPE_THETA = 500000.0
ROPE_DIM = HEAD_DIM // 4
NORM_EPS = 1e-5
D_FF = 2816

MOBA_HEADS = 8
MOBA_BLOCK = 256
MOBA_TOPK = 3
DIFF_HEADS = 4
DIFF_V = 2 * HEAD_DIM
NSA_HEADS = 8
NSA_GROUPS = 2
NSA_REP = NSA_HEADS // NSA_GROUPS
NSA_CMP_LEN = 32
NSA_CMP_STRIDE = 16
NSA_CMP_HIDDEN = 256
NSA_SEL_BLOCK = 64
NSA_SEL_TOPK = 16
NSA_WINDOW = 512
NSA_FORCE_BONUS = 1e3
MLA_HEADS = 8
MLA_Q_RANK = 256
MLA_KV_RANK = 128
MLA_NOPE = 64
MLA_ROPE = 32
MLA_V = 64

LANES = 128
MOBA_BIAS_LANES = 32
NEG = -1e30
M_INIT = -1e37
LOG2E = 1.4426950408889634
UNROLL = 8
VMEM_LIMIT = 56 * 1024 * 1024

ROW_TILE = 512
FLASH_COLS = 1024
FLASH_TK = 512
GATE_TQ = 2048
CMP_TQ = 1024
WINDOW_TQ = 512

_NT = (((1,), (1,)), ((), ()))


def _cparams(sem):
    return pltpu.CompilerParams(dimension_semantics=sem, vmem_limit_bytes=VMEM_LIMIT)


def _rope_tables(positions, dim, period, offset=0):
    r = dim // 2
    inv = 1.0 / (ROPE_THETA ** (jnp.arange(0, dim, 2, dtype=F32) / dim))
    ang = positions.astype(F32)[..., None] * inv
    cos, sin = jnp.cos(ang), jnp.sin(ang)
    const = lambda n, val: jnp.full(ang.shape[:-1] + (n,), val, F32)
    rest = period - offset - 2 * r
    zr = jnp.zeros_like(sin)
    c = jnp.concatenate([const(offset, 1.0), cos, cos, const(rest, 1.0)], -1)
    s1 = jnp.concatenate([const(offset, 0.0), -sin, zr, const(rest, 0.0)], -1)
    s2 = jnp.concatenate([const(offset, 0.0), zr, sin, const(rest, 0.0)], -1)
    reps = LANES // period
    tile = lambda t: jnp.tile(t, (1,) * (t.ndim - 1) + (reps,)).reshape(-1, LANES)
    return tile(c), tile(s1), tile(s2)


def _apply_rope(y, c, s1, s2, shift):
    return y * c + pltpu.roll(y, LANES - shift, 1) * s1 + pltpu.roll(y, shift, 1) * s2


def _proj_body(*refs, has_rope, add_cols, rope_add, rope_cols, shift, chunk, split):
    it = iter(refs)
    x_ref, g_ref, w_ref = next(it), next(it), next(it)
    if has_rope:
        c_ref, s1_ref, s2_ref = next(it), next(it), next(it)
    add_ref = next(it) if add_cols else None
    outs = list(it)
    n = w_ref.shape[1]
    xf = x_ref[...].astype(F32)
    y = xf * lax.rsqrt(jnp.mean(xf * xf, axis=-1, keepdims=True) + NORM_EPS)
    xb = (y * g_ref[...]).astype(BF16)
    if add_cols:
        add = add_ref[...].astype(F32)
        if rope_add:
            add = _apply_rope(add, c_ref[...], s1_ref[...], s2_ref[...], shift)
    spans = [(0, n, outs[0])] if len(outs) == 1 else [(0, split, outs[0]), (split, n, outs[1])]
    for lo, hi, o_ref in spans:
        for c0 in range(lo, hi, chunk):
            cw = min(chunk, hi - c0)
            y = jnp.dot(xb, w_ref[:, c0:c0 + cw], preferred_element_type=F32)
            if (has_rope and c0 < rope_cols) or c0 < add_cols:
                for k0 in range(0, cw, LANES):
                    ys = y[:, k0:k0 + LANES]
                    if has_rope and c0 + k0 < rope_cols:
                        ys = _apply_rope(ys, c_ref[...], s1_ref[...], s2_ref[...], shift)
                    if c0 + k0 < add_cols:
                        ys = ys + add
                    o_ref[:, c0 - lo + k0:c0 - lo + k0 + LANES] = ys.astype(o_ref.dtype)
            else:
                o_ref[:, c0 - lo:c0 - lo + cw] = y.astype(o_ref.dtype)


def _proj(x, w, gain, *, name, x_blk=None, rope=None, rope_cols=0, shift=0, add=None, add_blk=0, add_cols=0,
          rope_add=False, out_dtype=F32, split=None, out2_dtype=None, tm=ROW_TILE, chunk=512):
    t = x.shape[0]
    k, xj = (x.shape[1], 0) if x_blk is None else x_blk
    n = w.shape[1]
    assert t % tm == 0 and n % LANES == 0 and rope_cols % LANES == 0 and add_cols % LANES == 0
    assert split is None or split % LANES == 0
    has_rope = rope is not None
    args = [x, gain.reshape(1, k).astype(F32), w]
    specs = [pl.BlockSpec((tm, k), lambda i: (i, xj)), pl.BlockSpec((1, k), lambda i: (0, 0)),
             pl.BlockSpec((k, n), lambda i: (0, 0))]
    if has_rope:
        for tb in rope:
            args.append(tb)
            specs.append(pl.BlockSpec((tm, LANES), lambda i: (i, 0)))
    if add_cols:
        args.append(add)
        specs.append(pl.BlockSpec((tm, LANES), lambda i: (i, add_blk)))
    widths = [(n, out_dtype)] if split is None else [(split, out_dtype), (n - split, out2_dtype)]
    body = functools.partial(_proj_body, has_rope=has_rope, add_cols=add_cols, rope_add=rope_add,
                             rope_cols=rope_cols, shift=shift, chunk=chunk, split=split)
    out = pl.pallas_call(
        body, name=name, grid=(t // tm,), in_specs=specs,
        out_specs=[pl.BlockSpec((tm, wd), lambda i: (i, 0)) for wd, _ in widths],
        out_shape=[jax.ShapeDtypeStruct((t, wd), dt) for wd, dt in widths],
        compiler_params=_cparams(("parallel",)),
    )(*args)
    return out[0] if split is None else out


def _ffn_body(x_ref, oa_ref, ob_ref, wo_ref, g_ref, wg_ref, wu_ref, wd_ref, fg_ref, o_ref, *, final_norm):
    ka = oa_ref.shape[1]
    h = (x_ref[...] + jnp.dot(oa_ref[...], wo_ref[0:ka, :], preferred_element_type=F32)
         + jnp.dot(ob_ref[...], wo_ref[ka:, :], preferred_element_type=F32))
    y = h * lax.rsqrt(jnp.mean(h * h, axis=-1, keepdims=True) + NORM_EPS)
    xn = (y * g_ref[...]).astype(BF16)
    g = jnp.dot(xn, wg_ref[...], preferred_element_type=F32)
    u = jnp.dot(xn, wu_ref[...], preferred_element_type=F32)
    a = (jax.nn.silu(g) * u).astype(BF16)
    h = h + jnp.dot(a, wd_ref[...], preferred_element_type=F32)
    if final_norm:
        y = h * lax.rsqrt(jnp.mean(h * h, axis=-1, keepdims=True) + NORM_EPS)
        h = y * fg_ref[...]
    o_ref[...] = h


def _ffn(x, oa, ob, w_out, gain, wg, wu, wd, final_gain, *, final_norm, tm=ROW_TILE):
    t, d = x.shape
    f = wg.shape[1]
    assert t % tm == 0 and oa.shape[1] + ob.shape[1] == w_out.shape[0]
    once = pl.Buffered(1)
    return pl.pallas_call(
        functools.partial(_ffn_body, final_norm=final_norm), name="ffn",
        grid=(t // tm,),
        in_specs=[
            pl.BlockSpec((tm, d), lambda i: (i, 0)),
            pl.BlockSpec((tm, oa.shape[1]), lambda i: (i, 0)),
            pl.BlockSpec((tm, ob.shape[1]), lambda i: (i, 0)),
            pl.BlockSpec(w_out.shape, lambda i: (0, 0), pipeline_mode=once),
            pl.BlockSpec((1, d), lambda i: (0, 0)),
            pl.BlockSpec((d, f), lambda i: (0, 0), pipeline_mode=once),
            pl.BlockSpec((d, f), lambda i: (0, 0), pipeline_mode=once),
            pl.BlockSpec((f, d), lambda i: (0, 0), pipeline_mode=once),
            pl.BlockSpec((1, d), lambda i: (0, 0)),
        ],
        out_specs=pl.BlockSpec((tm, d), lambda i: (i, 0)),
        out_shape=jax.ShapeDtypeStruct((t, d), F32),
        compiler_params=_cparams(("parallel",)),
    )(x, oa, ob, w_out, gain.reshape(1, d).astype(F32), wg, wu, wd, final_gain.reshape(1, d).astype(F32))


def _diag_masks(tq, tk):
    w = min(tq, tk)
    o = jnp.arange(max(tk // tq, 1))[:, None, None]
    k = jnp.arange(tk)[None, :, None]
    i = jnp.arange(w)[None, None, :]
    return jnp.where(k <= i + o * tq, 0.0, NEG).astype(F32)


def _sweep(q, k_get, vt_get, s_ref, mask_ref, *, q0, tq, tk, dv, scale):
    cols = q.shape[0]

    def fold8(x, op):
        return op(x.reshape(x.shape[0] // 8, 8, x.shape[1]), axis=0)

    def first_col(u, n, n_masked):
        d = u - (n - n_masked)
        return d * tk if (d > 0 and cols == tq) else 0

    def upd(full, c0, f):
        return f(full) if c0 == 0 else jnp.concatenate([full[:, :c0], f(full[:, c0:])], axis=1)

    def score_block(j0, n, mrun, n_masked):
        ks = pl.multiple_of(j0 * tk, tk)
        n_wide = sum(1 for u in range(n) if first_col(u, n, n_masked) == 0)
        s = lax.dot_general(k_get(ks, n_wide * tk), q, _NT, preferred_element_type=F32) * (scale * LOG2E)
        for u in range(n):
            c0 = first_col(u, n, n_masked)
            if c0 == 0:
                su = s[u * tk:(u + 1) * tk]
            else:
                ku = k_get(pl.multiple_of(ks + u * tk, tk), tk)
                su = lax.dot_general(ku, q[c0:], _NT, preferred_element_type=F32) * (scale * LOG2E)
            if u >= n - n_masked:
                pat = mask_ref[0]
                w = pat.shape[1]
                if cols == tq:
                    su = su + pat if su.shape[1] == w else jnp.concatenate([su[:, :w] + pat, su[:, w:]], axis=1)
                else:
                    su = jnp.concatenate([su[:, c:c + w] + pat for c in range(0, cols, w)], axis=1)
            s_ref[j0 + u, :, c0:cols] = su
            mrun = upd(mrun, c0, lambda t: jnp.maximum(t, fold8(su, jnp.max)))
        return mrun

    def pv_block(j0, n, carry, m, n_masked=0):
        l8, acc = carry
        for u in range(n):
            c0 = first_col(u, n, n_masked)
            p = jnp.exp2(s_ref[j0 + u, :, c0:cols] - m[:, c0:])
            l8 = upd(l8, c0, lambda t: t + fold8(p, jnp.sum))
            pv = jnp.dot(vt_get(j0 + u), p.astype(BF16), preferred_element_type=F32)
            acc = upd(acc, c0, lambda t: t + pv)
        return l8, acc

    m_init = jnp.full((8, cols), M_INIT, F32)
    acc_init = (jnp.zeros((8, cols), F32), jnp.zeros((dv, cols), F32))
    n_diag = max(tq // tk, 1)
    assert UNROLL % n_diag == 0
    n_full = q0 // tk
    groups = n_full // UNROLL
    rem = n_full - groups * UNROLL
    tail0 = groups * UNROLL
    tails = range(n_diag, UNROLL + n_diag, n_diag)

    mrun = lax.fori_loop(0, groups, lambda i, t: score_block(UNROLL * i, UNROLL, t, 0), m_init)
    mrun = lax.switch(rem // n_diag, [functools.partial(score_block, tail0, n, n_masked=n_diag) for n in tails],
                      mrun)
    m = jnp.max(mrun, axis=0, keepdims=True)
    carry = lax.fori_loop(0, groups, lambda i, t: pv_block(UNROLL * i, UNROLL, t, m), acc_init)
    l8, acc = lax.switch(rem // n_diag, [functools.partial(pv_block, tail0, n, m=m, n_masked=n_diag) for n in tails],
                         carry)
    l = jnp.sum(l8, axis=0, keepdims=True)
    return acc / jnp.maximum(l, 1e-30)


def _head_lanes(mode, v, lane):
    d = HEAD_DIM
    if mode == "diff":
        return jnp.logical_and(lane >= d * v, lane < d * (v + 1))
    if mode == "moba":
        lo = 2 * d + v * MOBA_BIAS_LANES
        return jnp.logical_or(jnp.logical_and(lane >= d * v, lane < d * (v + 1)),
                              jnp.logical_and(lane >= lo, lane < lo + MOBA_BIAS_LANES))
    return jnp.logical_and(lane >= LANES * v, lane < LANES * (v + 1))


def _pflash_body(*refs, mode, tq, tk, scale, lambda_init):
    if mode == "diff":
        q_ref, k_ref, v_ref, mask_ref, lq1_ref, lk1_ref, lq2_ref, lk2_ref, sg_ref, o_ref, s_ref, vt_ref = refs
    else:
        q_ref, k_ref, v_ref, mask_ref, o_ref, s_ref, vt_ref = refs
    qi = pl.program_id(1)
    n_chunks = v_ref.shape[1] // tk

    @pl.when(qi == 0)
    def _():
        for c in range(n_chunks):
            vt_ref[c] = v_ref[0, c * tk:(c + 1) * tk, :].astype(F32).T.astype(BF16)

    q = q_ref[0]
    lane = lax.broadcasted_iota(jnp.int32, q.shape, 1)
    d = HEAD_DIM
    outs = []
    for v in range(2):
        qv = jnp.where(_head_lanes(mode, v, lane), q, jnp.zeros_like(q))
        if mode == "diff":
            vt_get, dv = (lambda j: vt_ref[j]), LANES
        else:
            vt_get, dv = (lambda j, v=v: vt_ref[j, v * d:(v + 1) * d, :]), d
        outs.append(_sweep(qv, lambda ks, n: k_ref[0, pl.ds(ks, n), :], vt_get, s_ref, mask_ref,
                           q0=qi * tq, tq=tq, tk=tk, dv=dv, scale=scale))
    if mode == "diff":
        lam = (jnp.exp(jnp.sum(lq1_ref[...] * lk1_ref[...], axis=-1, keepdims=True))
               - jnp.exp(jnp.sum(lq2_ref[...] * lk2_ref[...], axis=-1, keepdims=True)) + lambda_init)
        d = outs[0] - lam * outs[1]
        y = d * lax.rsqrt(jnp.mean(d * d, axis=0, keepdims=True) + NORM_EPS)
        o_t = (y * sg_ref[...]) * (1.0 - lambda_init)
    else:
        o_t = jnp.concatenate(outs, axis=0)
    o_ref[0] = o_t.T.astype(o_ref.dtype)


def _pflash(q, k, v, *, name, mode, dk, q_map, k_map, v_map, batch, pairs, tq=FLASH_COLS, tk=FLASH_TK, scale=1.0,
            diff_params=None, lambda_init=0.0):
    s = v.shape[1]
    assert s % tk == 0 and tq % tk == 0 and s % tq == 0
    args = [q, k, v, _diag_masks(tq, tk)]
    in_specs = [pl.BlockSpec((1, tq, dk), q_map), pl.BlockSpec((1, s, dk), k_map),
                pl.BlockSpec((1, s, LANES), v_map), pl.BlockSpec((1, tk, tk), lambda i, j: (0, 0, 0))]
    if mode == "diff":
        lq1, lk1, lq2, lk2, subln = diff_params
        for a in (lq1, lk1, lq2, lk2):
            args.append(a.reshape(1, HEAD_DIM).astype(F32))
            in_specs.append(pl.BlockSpec((1, HEAD_DIM), lambda i, j: (0, 0)))
        args.append(subln.reshape(LANES, 1).astype(F32))
        in_specs.append(pl.BlockSpec((LANES, 1), lambda i, j: (0, 0)))
    body = functools.partial(_pflash_body, mode=mode, tq=tq, tk=tk, scale=scale, lambda_init=lambda_init)
    return pl.pallas_call(
        body, name=name, grid=(batch * pairs, s // tq), in_specs=in_specs,
        out_specs=pl.BlockSpec((1, tq, LANES), lambda i, j: (i // pairs, j, i % pairs)),
        out_shape=jax.ShapeDtypeStruct((batch, s, pairs * LANES), BF16),
        scratch_shapes=[pltpu.VMEM((s // tk, tk, tq), F32), pltpu.VMEM((s // tk, LANES, tk), BF16)],
        compiler_params=_cparams(("parallel", "arbitrary")),
    )(*args)


def _band_masks(tq, window):
    kl = jnp.arange(window + tq)[None, :, None]
    ql = jnp.arange(tq)[None, None, :]
    q0 = (jnp.arange(window // tq + 1) * tq)[:, None, None]
    k = jnp.maximum(q0 - window, 0) + kl
    t = q0 + ql
    return jnp.where(jnp.logical_and(k <= t, k > t - window), 0.0, NEG).astype(F32)


def _band(q, ka_ref, vt_ref, mask_ref, *, q0, tq, window):
    cols = q.shape[0]
    band = window + tq
    start = pl.multiple_of(jnp.maximum(q0 - window, 0), tq)
    s = lax.dot_general(ka_ref[pl.ds(start, band), :], q, _NT, preferred_element_type=F32) * LOG2E

    s = jnp.concatenate([s[:, c:c + tq] + mask_ref[0] for c in range(0, cols, tq)], axis=1)
    p = jnp.exp2(s - jnp.max(s, axis=0, keepdims=True))
    l = jnp.sum(p, axis=0, keepdims=True)
    pb = p.astype(BF16)
    acc = None
    for c in range(band // tq):
        part = jnp.dot(vt_ref[start // tq + c], pb[c * tq:(c + 1) * tq], preferred_element_type=F32)
        acc = part if acc is None else acc + part
    return acc / jnp.maximum(l, 1e-30)


def _gflash_body(*refs, tq, tk, rep, groups, window, select):
    if select:
        q_ref, k_ref, v_ref, mb_ref, mask_ref, o_ref, s_ref, vt_ref, ka_ref, qa_ref = refs
    else:
        q_ref, k_ref, v_ref, mask_ref, o_ref, vt_ref, ka_ref, qa_ref = refs
    d = HEAD_DIM
    tv = vt_ref.shape[2]
    g = pl.program_id(0) % groups
    qi = pl.program_id(1)
    n_chunks = v_ref.shape[1] // tk

    @pl.when(qi == 0)
    def _():
        move = _place(LANES, LANES, g * d)
        for c in range(n_chunks):
            rows = slice(c * tk, (c + 1) * tk)
            vt = v_ref[0, rows, :].astype(F32).T
            vt = jnp.where(g == 0, vt[0:d], vt[d:2 * d]).astype(BF16)
            for u in range(tk // tv):
                vt_ref[c * (tk // tv) + u] = vt[:, u * tv:(u + 1) * tv]
            kg = jnp.dot(k_ref[0, rows, :], move, preferred_element_type=F32).astype(BF16)
            if select:
                blk = (c * tk + lax.broadcasted_iota(jnp.int32, (tk, LANES), 0)) // NSA_SEL_BLOCK
                lane = lax.broadcasted_iota(jnp.int32, (tk, LANES), 1)
                ka_ref[rows, 0:LANES] = jnp.where(blk == lane, 1.0, 0.0).astype(BF16)
                ka_ref[rows, LANES:2 * LANES] = kg
            else:
                ka_ref[rows, :] = kg

    q = q_ref[0]
    for r in range(rep):
        qr = jnp.dot(q, _place(rep * d, LANES, r * d, d ** -0.5), preferred_element_type=F32).astype(BF16)
        rows = slice(r * tq, (r + 1) * tq)
        if select:
            qa_ref[rows, 0:LANES] = mb_ref[0]
            qa_ref[rows, LANES:2 * LANES] = qr
        else:
            qa_ref[rows, :] = qr
    if window is None:
        o_t = _sweep(qa_ref[...], lambda ks, n: ka_ref[pl.ds(ks, n), :], lambda j: vt_ref[j], s_ref, mask_ref,
                     q0=qi * tq, tq=tq, tk=tk, dv=d, scale=1.0)
    else:
        half = (rep // 2) * tq
        o_t = jnp.concatenate([_band(qa_ref[0:half, :], ka_ref, vt_ref, mask_ref, q0=qi * tq, tq=tq, window=window),
                               _band(qa_ref[half:, :], ka_ref, vt_ref, mask_ref, q0=qi * tq, tq=tq, window=window)],
                              axis=1)
    for r in range(rep):
        o_ref[0, :, r * d:(r + 1) * d] = o_t[:, r * tq:(r + 1) * tq].T.astype(o_ref.dtype)


def _gflash(yq, yk, yv, mbias, *, name, q_blk, k_blk, v_blk, tq=FLASH_COLS // NSA_REP, tk=FLASH_TK, window=None):
    b, s, _ = yq.shape
    rep, groups, d = NSA_REP, NSA_GROUPS, HEAD_DIM
    assert groups * d == LANES and s % tk == 0 and tk % tq == 0
    select = mbias is not None
    dk = 2 * LANES if select else LANES
    assert select == (window is None)
    assert window is None or (window % tq == 0 and s >= window + tq)
    tv = tk if window is None else tq
    scratch = [pltpu.VMEM((s // tv, d, tv), BF16), pltpu.VMEM((s, dk), BF16), pltpu.VMEM((rep * tq, dk), BF16)]
    if window is None:
        scratch.insert(0, pltpu.VMEM((s // tk, tk, rep * tq), F32))
    args = [yq, yk, yv]
    in_specs = [pl.BlockSpec((1, tq, rep * d), lambda i, j: (i // groups, j, q_blk + i % groups)),
                pl.BlockSpec((1, s, LANES), lambda i, j: (i // groups, 0, k_blk)),
                pl.BlockSpec((1, s, LANES), lambda i, j: (i // groups, 0, v_blk))]
    if select:
        args += [mbias, _diag_masks(tq, tk)]
        in_specs += [pl.BlockSpec((1, tq, LANES), lambda i, j: (i, j, 0)),
                     pl.BlockSpec((1, tk, tq), lambda i, j: (j % (tk // tq), 0, 0))]
    else:
        args.append(_band_masks(tq, window))
        in_specs.append(pl.BlockSpec((1, window + tq, tq), lambda i, j: (jnp.minimum(j, window // tq), 0, 0)))
    body = functools.partial(_gflash_body, tq=tq, tk=tk, rep=rep, groups=groups, window=window, select=select)
    return pl.pallas_call(
        body, name=name, grid=(b * groups, s // tq), in_specs=in_specs,
        out_specs=pl.BlockSpec((1, tq, rep * d), lambda i, j: (i // groups, j, i % groups)),
        out_shape=jax.ShapeDtypeStruct((b, s, groups * rep * d), F32),
        scratch_shapes=scratch,
        compiler_params=_cparams(("parallel", "arbitrary")),
    )(*args)


def _topk_mask(score, row, k):
    sel = None
    for _ in range(k):
        mx = jnp.max(score, axis=0, keepdims=True)
        idx = jnp.min(jnp.where(score == mx, row, LANES), axis=0, keepdims=True)
        hit = row == idx
        pick = jnp.logical_and(hit, mx > -jnp.inf)
        sel = pick if sel is None else jnp.logical_or(sel, pick)
        score = jnp.where(hit, -jnp.inf, score)
    return sel


def _moba_gate_body(q_ref, k_ref, qa_ref, ka_ref, kmean_ref, *, tq, seq):
    qi = pl.program_id(1)
    nbl = MOBA_BIAS_LANES

    @pl.when(qi == 0)
    def _():
        rowblk = lax.rem(lax.broadcasted_iota(jnp.int32, (LANES, seq), 0), nbl)
        colblk = lax.broadcasted_iota(jnp.int32, (LANES, seq), 1) // MOBA_BLOCK
        ind = jnp.where(rowblk == colblk, 1.0, 0.0).astype(BF16)
        ksum = jnp.dot(ind, k_ref[0], preferred_element_type=F32)
        rhead = lax.broadcasted_iota(jnp.int32, (LANES, LANES), 0) // nbl
        lhead = lax.broadcasted_iota(jnp.int32, (LANES, LANES), 1) // HEAD_DIM
        kmean_ref[...] = jnp.where(rhead == lhead, ksum * (1.0 / MOBA_BLOCK), 0.0)

    q = q_ref[0]
    q0 = pl.multiple_of(qi * tq, tq)
    km = kmean_ref[...]
    km_hi = km.astype(BF16)
    km_lo = (km - km_hi.astype(F32)).astype(BF16)
    gate = (lax.dot_general(km_hi, q, _NT, preferred_element_type=F32)
            + lax.dot_general(km_lo, q, _NT, preferred_element_type=F32))
    blk_t = lax.broadcasted_iota(jnp.int32, (nbl, tq), 0)
    own_t = (q0 + lax.broadcasted_iota(jnp.int32, (nbl, tq), 1)) // MOBA_BLOCK
    parts = []
    for v in range(2):
        g_v = jnp.where(blk_t < own_t, gate[v * nbl:(v + 1) * nbl], -jnp.inf)
        sel = jnp.logical_or(_topk_mask(g_v, blk_t, MOBA_TOPK), blk_t == own_t)
        parts.append(jnp.where(sel, 0.0, NEG))
    parts.append(jnp.zeros((LANES - 2 * nbl, tq), F32))
    bias = jnp.concatenate(parts, axis=0).T
    qa_ref[0, :, 0:LANES] = q * (HEAD_DIM ** -0.5)
    qa_ref[0, :, LANES:2 * LANES] = bias.astype(BF16)
    lane = lax.broadcasted_iota(jnp.int32, (tq, LANES), 1)
    own = (q0 + lax.broadcasted_iota(jnp.int32, (tq, LANES), 0)) // MOBA_BLOCK
    onehot = jnp.where(jnp.logical_and(lane < 2 * nbl, lax.rem(lane, nbl) == own), 1.0, 0.0)
    ka_ref[0, :, 0:LANES] = k_ref[0, pl.ds(q0, tq), :]
    ka_ref[0, :, LANES:2 * LANES] = onehot.astype(BF16)


def _moba_gate(y3, *, q_blk, k_blk, pairs, tq=GATE_TQ):
    b, s, _ = y3.shape
    assert s % MOBA_BLOCK == 0 and s // MOBA_BLOCK <= MOBA_BIAS_LANES and s % tq == 0
    out = jax.ShapeDtypeStruct((b * pairs, s, 2 * LANES), BF16)
    return pl.pallas_call(
        functools.partial(_moba_gate_body, tq=tq, seq=s), name="moba_gate",
        grid=(b * pairs, s // tq),
        in_specs=[pl.BlockSpec((1, tq, LANES), lambda i, j: (i // pairs, j, q_blk + i % pairs)),
                  pl.BlockSpec((1, s, LANES), lambda i, j: (i // pairs, 0, k_blk + i % pairs))],
        out_specs=[pl.BlockSpec((1, tq, 2 * LANES), lambda i, j: (i, j, 0)),
                   pl.BlockSpec((1, tq, 2 * LANES), lambda i, j: (i, j, 0))],
        out_shape=[out, out],
        scratch_shapes=[pltpu.VMEM((LANES, LANES), F32)],
        compiler_params=_cparams(("parallel", "arbitrary")),
    )(y3, y3)


def _nsa_compress_body(x_ref, pe_ref, w1_ref, w2_ref, c_ref, s1_ref, s2_ref, o_ref, *, rope):
    half = w1_ref.shape[0] // 2
    x = x_ref[0].astype(F32)
    lo = (x + pe_ref[:, 0:half]).astype(BF16)
    hi = (x + pe_ref[:, half:2 * half]).astype(BF16)
    a = jnp.dot(lo, w1_ref[0:half, :], preferred_element_type=F32)
    b = jnp.dot(hi, w1_ref[half:2 * half, :], preferred_element_type=F32)
    n = a.shape[0]
    h1 = a + pltpu.roll(b, n - 1, 0)
    y = jnp.dot(jax.nn.gelu(h1).astype(BF16), w2_ref[...], preferred_element_type=F32)
    if rope:
        y = _apply_rope(y, c_ref[0], s1_ref[0], s2_ref[0], ROPE_DIM // 2)
    o_ref[0] = y[:, 0:HEAD_DIM].astype(o_ref.dtype)


def _nsa_compress(xr, pe, w1, w2, tables, *, rope, groups):
    g, n, w = xr.shape
    hid = w1.shape[1]
    w2p = jnp.pad(w2, ((0, 0), (0, LANES - w2.shape[1])))
    tspec = pl.BlockSpec((1, n, LANES), lambda i: (i // groups, 0, 0))
    return pl.pallas_call(
        functools.partial(_nsa_compress_body, rope=rope), name="nsa_compress",
        grid=(g,),
        in_specs=[pl.BlockSpec((1, n, w), lambda i: (i, 0, 0)),
                  pl.BlockSpec((1, 2 * w), lambda i: (0, 0)),
                  pl.BlockSpec((2 * w, hid), lambda i: (0, 0)),
                  pl.BlockSpec((hid, LANES), lambda i: (0, 0)),
                  tspec, tspec, tspec],
        out_specs=pl.BlockSpec((1, n, HEAD_DIM), lambda i: (i, 0, 0)),
        out_shape=jax.ShapeDtypeStruct((g, n, HEAD_DIM), BF16),
        compiler_params=_cparams(("parallel",)),
    )(xr, pe.reshape(1, 2 * w).astype(F32), w1, w2p, *tables)


def _place(n_src, n_dst, shift, value=1.0):
    src = lax.broadcasted_iota(jnp.int32, (n_src, n_dst), 0)
    dst = lax.broadcasted_iota(jnp.int32, (n_src, n_dst), 1)
    return jnp.where(jnp.logical_and(src == dst + shift, dst < HEAD_DIM), value, 0.0).astype(BF16)


def _nsa_cmp_body(q_ref, kc_ref, vc_ref, oc_ref, mb_ref, kcp_ref, *, tq, rep):
    qi = pl.program_id(1)
    q0 = qi * tq
    nc = kc_ref.shape[1]
    width = q_ref.shape[-1]

    @pl.when(qi == 0)
    def _():
        for r in range(rep):
            src = lax.broadcasted_iota(jnp.int32, (HEAD_DIM, width), 0)
            dst = lax.broadcasted_iota(jnp.int32, (HEAD_DIM, width), 1)
            spread = jnp.where(dst == src + r * HEAD_DIM, HEAD_DIM ** -0.5, 0.0).astype(BF16)
            kcp_ref[r] = jnp.dot(kc_ref[0], spread, preferred_element_type=F32).astype(BF16)

    q = q_ref[0]

    def branches(nr):
        def run():
            vct = vc_ref[0, :, 0:nr]
            tpos = q0 + lax.broadcasted_iota(jnp.int32, (nr, tq), 1)
            cend = lax.broadcasted_iota(jnp.int32, (nr, tq), 0) * NSA_CMP_STRIDE + (NSA_CMP_LEN - 1)
            ok = cend <= tpos
            psum = jnp.zeros((nr, tq), F32)
            for r in range(rep):
                s = lax.dot_general(kcp_ref[r, 0:nr, :], q, _NT, preferred_element_type=F32)
                s = jnp.where(ok, s, NEG)
                m = jnp.max(s, axis=0, keepdims=True)
                e = jnp.where(ok, jnp.exp(s - m), 0.0)
                p = e * (1.0 / jnp.maximum(jnp.sum(e, axis=0, keepdims=True), 1e-30))
                o_t = jnp.dot(vct, p.astype(BF16), preferred_element_type=F32)
                oc_ref[0, :, r * HEAD_DIM:(r + 1) * HEAD_DIM] = o_t.T.astype(oc_ref.dtype)
                psum = psum + p
            nb = nr * NSA_CMP_STRIDE // NSA_SEL_BLOCK
            sstart = lax.broadcasted_iota(jnp.int32, (nb, nr), 0) * NSA_SEL_BLOCK
            cstart = lax.broadcasted_iota(jnp.int32, (nb, nr), 1) * NSA_CMP_STRIDE
            ov = jnp.where(jnp.logical_and(cstart < sstart + NSA_SEL_BLOCK, cstart + NSA_CMP_LEN > sstart),
                           1.0, 0.0).astype(BF16)
            p_hi = psum.astype(BF16)
            r1 = psum - p_hi.astype(F32)
            p_mid = r1.astype(BF16)
            p_lo = (r1 - p_mid.astype(F32)).astype(BF16)
            imp = (jnp.dot(ov, p_hi, preferred_element_type=F32) + jnp.dot(ov, p_mid, preferred_element_type=F32)
                   + jnp.dot(ov, p_lo, preferred_element_type=F32))
            blk = lax.broadcasted_iota(jnp.int32, (nb, tq), 0)
            qb = (q0 + lax.broadcasted_iota(jnp.int32, (nb, tq), 1)) // NSA_SEL_BLOCK
            forced = jnp.logical_or(blk == 0, jnp.logical_or(blk == qb, blk == qb - 1))
            imp = jnp.where(forced, imp + NSA_FORCE_BONUS, imp)
            imp = jnp.where(blk <= qb, imp, -jnp.inf)
            bias = jnp.where(_topk_mask(imp, blk, NSA_SEL_TOPK), 0.0, NEG)
            if nb < LANES:
                bias = jnp.concatenate([bias, jnp.full((LANES - nb, tq), NEG, F32)], axis=0)
            return bias
        return run

    sizes = list(range(LANES, nc + 1, LANES))
    needed = (q0 + tq) // NSA_CMP_STRIDE
    bias_t = lax.switch(jnp.minimum((needed + LANES - 1) // LANES, len(sizes)) - 1, [branches(nr) for nr in sizes])
    mb_ref[0] = bias_t.T.astype(mb_ref.dtype)


def _nsa_cmp(y3, kc, vc, *, tq=CMP_TQ):
    b, s, _ = y3.shape
    g, nc, d = kc.shape
    rep, n = NSA_REP, NSA_GROUPS
    assert s // NSA_SEL_BLOCK <= LANES and s % tq == 0
    return pl.pallas_call(
        functools.partial(_nsa_cmp_body, tq=tq, rep=rep), name="nsa_cmp",
        grid=(g, s // tq),
        in_specs=[pl.BlockSpec((1, tq, rep * d), lambda i, j: (i // n, j, i % n)),
                  pl.BlockSpec((1, nc, d), lambda i, j: (i, 0, 0)),
                  pl.BlockSpec((1, d, nc), lambda i, j: (i, 0, 0))],
        out_specs=[pl.BlockSpec((1, tq, rep * d), lambda i, j: (i // n, j, i % n)),
                   pl.BlockSpec((1, tq, LANES), lambda i, j: (i, j, 0))],
        out_shape=[jax.ShapeDtypeStruct((b, s, n * rep * d), F32),
                   jax.ShapeDtypeStruct((g, s, LANES), BF16)],
        scratch_shapes=[pltpu.VMEM((rep, nc, rep * d), BF16)],
        compiler_params=_cparams(("parallel", "arbitrary")),
    )(y3, kc, vc)


def _nsa_combine_body(oc_ref, os_ref, ow_ref, g_ref, b_ref, e_ref, o_ref):
    gs = jax.nn.sigmoid(g_ref[...] + b_ref[...])
    g_hi = gs.astype(BF16)
    g_lo = (gs - g_hi.astype(F32)).astype(BF16)
    out = None
    for i, ref in enumerate((oc_ref, os_ref, ow_ref)):
        w = (jnp.dot(g_hi, e_ref[i], preferred_element_type=F32)
             + jnp.dot(g_lo, e_ref[i], preferred_element_type=F32))
        term = w * ref[...]
        out = term if out is None else out + term
    o_ref[...] = out.astype(o_ref.dtype)


def _nsa_combine(oc, osel, ow, yg, g_blk, gate_b, *, tm=ROW_TILE):
    t, n = oc.shape
    nh = NSA_HEADS * 3
    bp = jnp.pad(gate_b.reshape(1, nh).astype(F32), ((0, 0), (0, LANES - nh)))
    row = jnp.arange(LANES)[:, None]
    col = jnp.arange(n)[None, :] // HEAD_DIM
    expand = jnp.stack([(row == col * 3 + i) for i in range(3)]).astype(BF16)
    tok = pl.BlockSpec((tm, n), lambda i: (i, 0))
    return pl.pallas_call(
        _nsa_combine_body, name="nsa_combine", grid=(t // tm,),
        in_specs=[tok, tok, tok, pl.BlockSpec((tm, LANES), lambda i: (i, g_blk)),
                  pl.BlockSpec((1, LANES), lambda i: (0, 0)),
                  pl.BlockSpec((3, LANES, n), lambda i: (0, 0, 0))],
        out_specs=tok,
        out_shape=jax.ShapeDtypeStruct((t, n), BF16),
        compiler_params=_cparams(("parallel",)),
    )(oc, osel, ow, yg, bp, expand)


def _to_heads(x, b, s, n):
    return x.reshape(b, s, n, -1).transpose(0, 2, 1, 3).reshape(b * n, s, -1)


def _even_mixer(h, gain, layer_idx, rope16, w_in, lq1, lk1, lq2, lk2, subln, b, s):
    na = MOBA_HEADS * HEAD_DIM
    nb = DIFF_HEADS * 2 * HEAD_DIM
    w_rope = jnp.concatenate([w_in[:, 0:2 * na], w_in[:, 3 * na:3 * na + 2 * nb]], axis=1)
    w_rest = jnp.concatenate([w_in[:, 2 * na:3 * na], w_in[:, 3 * na + 2 * nb:]], axis=1)
    w = jnp.concatenate([w_rope, w_rest], axis=1).astype(BF16)
    y = _proj(h, w, gain, name="proj_even_in", rope=rope16, rope_cols=w_rope.shape[1], shift=ROPE_DIM // 2, out_dtype=BF16)
    y3 = y.reshape(b, s, -1)
    blk = lambda off: off // LANES
    pairs = na // LANES
    qa, ka = _moba_gate(y3, q_blk=blk(0), k_blk=blk(na), pairs=pairs)
    o_a = _pflash(qa, ka, y3, name="flash_moba", mode="moba", dk=2 * LANES, batch=b, pairs=pairs,
                  q_map=lambda i, j: (i, j, 0), k_map=lambda i, j: (i, 0, 0),
                  v_map=lambda i, j: (i // pairs, 0, blk(2 * na + 2 * nb) + i % pairs))
    lambda_init = 0.8 - 0.6 * math.exp(-0.3 * layer_idx)
    nh = DIFF_HEADS
    o_b = _pflash(y3, y3, y3, name="flash_diff", mode="diff", dk=LANES, batch=b, pairs=nh, scale=HEAD_DIM ** -0.5,
                  q_map=lambda i, j: (i // nh, j, blk(2 * na) + i % nh),
                  k_map=lambda i, j: (i // nh, 0, blk(2 * na + nb) + i % nh),
                  v_map=lambda i, j: (i // nh, 0, blk(3 * na + 2 * nb) + i % nh),
                  diff_params=(lq1, lk1, lq2, lk2, subln), lambda_init=lambda_init)
    t = b * s
    return o_a.reshape(t, -1), o_b.reshape(t, -1)


def _odd_mixer(h, gain, positions, rope16, rope32s, w_in, gate_b, pe_k, pe_v, k_w1, k_w2, v_w1, v_w2,
               q_norm, w_uq, kv_norm, w_ukv, b, s):
    G, R, d = NSA_GROUPS, NSA_REP, HEAD_DIM
    sizes = [NSA_HEADS * d] + [G * d] * 6 + [NSA_HEADS * 3, MLA_Q_RANK, MLA_KV_RANK, MLA_ROPE]
    offs = [0]
    for z in sizes:
        offs.append(offs[-1] + z)
    col = lambda i: w_in[:, offs[i]:offs[i + 1]]
    w_r = jnp.concatenate([col(0), col(3), col(5)], axis=1)
    w_p = jnp.concatenate([col(1), col(2), col(4), col(6), col(8), col(9), col(7)], axis=1)
    w_p = jnp.pad(w_p, ((0, 0), (0, (-w_p.shape[1]) % LANES)))
    w_pe = jnp.pad(col(10), ((0, 0), (MLA_NOPE, LANES - MLA_NOPE - MLA_ROPE)))
    yr, yp = _proj(h, jnp.concatenate([w_r, w_p, w_pe], axis=1).astype(BF16), gain, name="proj_odd_in", rope=rope16,
                   rope_cols=w_r.shape[1], shift=ROPE_DIM // 2, out_dtype=BF16, split=w_r.shape[1], out2_dtype=F32)
    k_cmp, v_cmp = yp[:, 0:G * d], yp[:, G * d:2 * G * d]
    cq_blk = 4 * G * d // MLA_Q_RANK
    ckv_blk = (4 * G * d + MLA_Q_RANK) // MLA_KV_RANK
    gate_blk = (4 * G * d + MLA_Q_RANK + MLA_KV_RANK) // LANES
    kpe_blk = w_p.shape[1] // LANES

    yr3, yp3 = yr.reshape(b, s, -1), yp.reshape(b, s, -1)
    nc = s // NSA_CMP_STRIDE
    cpos = jnp.concatenate([positions[:, NSA_CMP_LEN - 1::NSA_CMP_STRIDE], positions[:, -1:]], axis=1)
    ctab = [t.reshape(b, nc, LANES) for t in _rope_tables(cpos, ROPE_DIM, HEAD_DIM)]
    xk = _to_heads(k_cmp, b, s, G).reshape(b * G, nc, NSA_CMP_STRIDE * d)
    xv = _to_heads(v_cmp, b, s, G).reshape(b * G, nc, NSA_CMP_STRIDE * d)
    kc = _nsa_compress(xk, pe_k, k_w1.astype(BF16), k_w2.astype(BF16), ctab, rope=True, groups=G)
    vc = _nsa_compress(xv, pe_v, v_w1.astype(BF16), v_w2.astype(BF16), ctab, rope=False, groups=G)
    o_c, mbias = _nsa_cmp(yr3, kc, vc.transpose(0, 2, 1))
    k_blk = NSA_HEADS * d // LANES
    vsw = yp[:, 2 * G * d:4 * G * d].astype(BF16).reshape(b, s, -1)
    o_s = _gflash(yr3, yr3, vsw, mbias, name="flash_sel", q_blk=0, k_blk=k_blk, v_blk=0)
    o_w = _gflash(yr3, yr3, vsw, None, name="flash_win", q_blk=0, k_blk=k_blk + 1, v_blk=1, tq=WINDOW_TQ,
                  window=NSA_WINDOW)
    t = b * s
    o_nsa = _nsa_combine(o_c.reshape(t, -1), o_s.reshape(t, -1), o_w.reshape(t, -1), yp, gate_blk, gate_b)

    hq = MLA_NOPE + MLA_ROPE
    nh = MLA_HEADS
    wq = jnp.pad(w_uq.reshape(MLA_Q_RANK, nh, hq), ((0, 0), (0, 0), (0, LANES - hq))).reshape(MLA_Q_RANK, nh * LANES)
    q = _proj(yp, wq.astype(BF16), q_norm, name="proj_mla_q", x_blk=(MLA_Q_RANK, cq_blk), rope=rope32s,
              rope_cols=nh * LANES, shift=MLA_ROPE // 2, out_dtype=BF16)
    wkv = w_ukv.reshape(MLA_KV_RANK, nh, MLA_NOPE + MLA_V)
    wk = jnp.pad(wkv[:, :, :MLA_NOPE], ((0, 0), (0, 0), (0, LANES - MLA_NOPE))).reshape(MLA_KV_RANK, nh * LANES)
    wkv = jnp.concatenate([wk, wkv[:, :, MLA_NOPE:].reshape(MLA_KV_RANK, nh * MLA_V)], axis=1).astype(BF16)
    kv = _proj(yp, wkv, kv_norm, name="proj_mla_kv", x_blk=(MLA_KV_RANK, ckv_blk), rope=rope32s,
               shift=MLA_ROPE // 2, add=yp, add_blk=kpe_blk, add_cols=nh * LANES, rope_add=True,
               out_dtype=BF16)
    q3, kv3 = q.reshape(b, s, -1), kv.reshape(b, s, -1)
    pairs = nh // 2
    o_d = _pflash(q3, kv3, kv3, name="flash_mla", mode="slots", dk=2 * LANES, batch=b, pairs=pairs, scale=hq ** -0.5,
                  q_map=lambda i, j: (i // pairs, j, i % pairs), k_map=lambda i, j: (i // pairs, 0, i % pairs),
                  v_map=lambda i, j: (i // pairs, 0, nh + i % pairs))

    return o_nsa, o_d.reshape(t, -1)


def kernel(x, positions, attn_norm, ffn_norm, final_norm, ffn_w_gate, ffn_w_up, ffn_w_down, ev_w_in, ev_w_out, diff_lambda_q1, diff_lambda_k1, diff_lambda_q2, diff_lambda_k2, diff_subln, od_w_in, od_w_out, nsa_gate_b, nsa_pe_k, nsa_pe_v, nsa_k_w1, nsa_k_w2, nsa_v_w1, nsa_v_w2, mla_q_norm, mla_w_uq, mla_kv_norm, mla_w_ukv):
    b, s, d = x.shape
    depth = attn_norm.shape[0]
    rope16 = _rope_tables(positions, ROPE_DIM, HEAD_DIM)
    rope32s = _rope_tables(positions, MLA_ROPE, LANES, offset=MLA_NOPE)
    h = x.reshape(b * s, d)
    for l in range(depth):
        i = l // 2
        if l % 2 == 0:
            oa, ob = _even_mixer(h, attn_norm[l], l, rope16, ev_w_in[i], diff_lambda_q1[i],
                                 diff_lambda_k1[i], diff_lambda_q2[i], diff_lambda_k2[i], diff_subln[i], b, s)
            w_out = ev_w_out[i]
        else:
            oa, ob = _odd_mixer(h, attn_norm[l], positions, rope16, rope32s, od_w_in[i], nsa_gate_b[i],
                                nsa_pe_k[i], nsa_pe_v[i], nsa_k_w1[i], nsa_k_w2[i], nsa_v_w1[i], nsa_v_w2[i],
                                mla_q_norm[i], mla_w_uq[i], mla_kv_norm[i], mla_w_ukv[i], b, s)
            w_out = od_w_out[i]
        h = _ffn(h, oa, ob, w_out.astype(BF16), ffn_norm[l], ffn_w_gate[l].astype(BF16), ffn_w_up[l].astype(BF16),
                 ffn_w_down[l].astype(BF16), final_norm, final_norm=(l == depth - 1))
    return h.reshape(b, s, d)
```
